```python
import jax, jax.numpy as jnp
from jax import lax
import numpy as np

D_MODEL = 2048
BATCH = 8
SEQ = 4096
DEPTH = 4

LRU_WIDTH = D_MODEL // 2
LRU_HEADS = 8
LRU_HEAD_DIM = LRU_WIDTH // LRU_HEADS
LRU_CONV = 4
LRU_C = 8.0
HGRN_HEADS = 8
HGRN_DK = (D_MODEL // 2) // HGRN_HEADS
HGRN_DV = (D_MODEL // 2) // HGRN_HEADS
HGRN_WIDTH = HGRN_HEADS * HGRN_DK
HGRN_CHUNK = 64
SC_WIDTH = D_MODEL // 2
SC_CONV = 3
CF_WIDTH = D_MODEL // 2
CF_CONV = 31
FFN_HIDDEN = -(-8 * D_MODEL // (3 * 256)) * 256
N_EVEN = (DEPTH + 1) // 2
N_ODD = DEPTH // 2
EV_IN = 2 * LRU_WIDTH + 4 * HGRN_WIDTH
EV_OUT = LRU_WIDTH + HGRN_HEADS * HGRN_DV
OD_IN = 3 * SC_WIDTH + 2 * CF_WIDTH
OD_OUT = SC_WIDTH + CF_WIDTH
EPS = 1e-6
F_FLOOR = 1e-30

kernel_name = "hybrid_rglru_hgrn2_shortconv_conformer_trunk"


def rmsnorm(x, g):
    xf = x.astype(jnp.float32)
    y = xf * lax.rsqrt(jnp.mean(xf * xf, axis=-1, keepdims=True) + EPS)
    return (y * g.astype(jnp.float32)).astype(x.dtype)


def layernorm(x, g, b):
    xf = x.astype(jnp.float32)
    mu = jnp.mean(xf, axis=-1, keepdims=True)
    xc = xf - mu
    y = xc * lax.rsqrt(jnp.mean(xc * xc, axis=-1, keepdims=True) + EPS)
    return (y * g.astype(jnp.float32) + b.astype(jnp.float32)).astype(x.dtype)


def split_cols(t, sizes):
    outs, off = [], 0
    for s in sizes:
        outs.append(t[..., off:off + s])
        off += s
    return outs


def causal_dwconv(x, w, b=None):
    K, C = w.shape
    y = lax.conv_general_dilated(
        x, w[:, None, :].astype(x.dtype), window_strides=(1,), padding=[(K - 1, 0)],
        dimension_numbers=("NWC", "WIO", "NWC"), feature_group_count=C)
    if b is not None:
        y = y + b.astype(x.dtype)
    return y


def rg_lru(x, wa, ba, wx, bx, lam):
    Bsz, S, W = x.shape
    xh = x.reshape(Bsz, S, LRU_HEADS, LRU_HEAD_DIM)
    r = jax.nn.sigmoid((jnp.einsum('bshd,hde->bshe', xh, wa) + ba).astype(jnp.float32)).reshape(Bsz, S, W)
    i = jax.nn.sigmoid((jnp.einsum('bshd,hde->bshe', xh, wx) + bx).astype(jnp.float32)).reshape(Bsz, S, W)
    log_a = -LRU_C * r * jax.nn.softplus(-lam.astype(jnp.float32))
    a = jnp.exp(log_a)
    mult = jnp.sqrt(jnp.maximum(-jnp.expm1(2.0 * log_a), 0.0))
    mult = mult.at[:, 0].set(1.0)
    u = mult * i * x.astype(jnp.float32)

    def combine(left, right):
        a_l, b_l = left
        a_r, b_r = right
        return a_l * a_r, a_r * b_l + b_r

    _, h = lax.associative_scan(combine, (a, u), axis=1)
    return h


def hgrn2_chunked(q, k, v, log_f):
    Bsz, S, H, DK = q.shape
    DV = v.shape[-1]
    C = HGRN_CHUNK
    NC = S // C

    def to_chunks(t):
        return t.reshape(Bsz, NC, C, H, t.shape[-1]).transpose(1, 0, 3, 2, 4)

    qc, kc, vc, gc = to_chunks(q), to_chunks(k), to_chunks(v), to_chunks(log_f)
    bc = jnp.cumsum(gc, axis=3)
    causal = jnp.tril(jnp.ones((C, C), dtype=bool))[:, :, None]

    def step(state, inp):
        q_, k_, v_, b_ = inp
        inter = jnp.einsum('bhtk,bhkv->bhtv', q_ * jnp.exp(b_), state)
        diff = b_[:, :, :, None, :] - b_[:, :, None, :, :]
        decay = jnp.where(causal, jnp.exp(jnp.where(causal, diff, 0.0)), 0.0)
        scores = jnp.einsum('bhtk,bhtsk,bhsk->bhts', q_, decay, k_)
        intra = jnp.einsum('bhts,bhsv->bhtv', scores, v_)
        b_last = b_[:, :, -1:, :]
        new_state = (jnp.exp(b_last[:, :, 0, :])[..., None] * state
                     + jnp.einsum('bhsk,bhsv->bhkv', k_ * jnp.exp(b_last - b_), v_))
        return new_state, inter + intra

    state0 = jnp.zeros((Bsz, H, DK, DV), jnp.float32)
    _, o = lax.scan(step, state0, (qc, kc, vc, bc))
    return o.transpose(1, 0, 3, 2, 4).reshape(Bsz, S, H, DV)


def even_mixer(h, w_in, b_in, conv_w, conv_b, wa, ba, wx, bx, lam, lb, norm_g, w_out):
    Bsz, S, _ = h.shape
    proj = h @ w_in + b_in
    x_a, gate_a, q, f, i, g = split_cols(
        proj, [LRU_WIDTH, LRU_WIDTH, HGRN_WIDTH, HGRN_WIDTH, HGRN_HEADS * HGRN_DV, HGRN_HEADS * HGRN_DV])
    x_a = causal_dwconv(x_a, conv_w, conv_b)
    h_a = rg_lru(x_a, wa, ba, wx, bx, lam)
    y_a = (h_a * jax.nn.gelu(gate_a.astype(jnp.float32))).astype(h.dtype)
    z = f.astype(jnp.float32).reshape(Bsz, S, HGRN_HEADS, HGRN_DK)
    lb = lb.astype(jnp.float32).reshape(HGRN_HEADS, HGRN_DK)
    sig = jax.nn.sigmoid(z)
    f_gate = lb + (1.0 - lb) * sig
    log_f = jnp.log(jnp.maximum(f_gate, F_FLOOR))
    k = (1.0 - lb) * (1.0 - sig)
    qf = jax.nn.silu(q.astype(jnp.float32)).reshape(Bsz, S, HGRN_HEADS, HGRN_DK)
    vf = i.astype(jnp.float32).reshape(Bsz, S, HGRN_HEADS, HGRN_DV)
    o = hgrn2_chunked(qf, k, vf, log_f)
    o = o * lax.rsqrt(jnp.mean(o * o, axis=-1, keepdims=True) + EPS)
    o = o.reshape(Bsz, S, HGRN_HEADS * HGRN_DV) * norm_g.astype(jnp.float32)
    y_b = (o * jax.nn.silu(g.astype(jnp.float32))).astype(h.dtype)
    return jnp.concatenate([y_a, y_b], axis=-1) @ w_out


def odd_mixer(h, w_in, b_in, sc_w, cf_w, cf_b, cf_g, cf_beta, w_out):
    proj = h @ w_in + b_in
    sb, sc, sv, cu, cg = split_cols(proj, [SC_WIDTH, SC_WIDTH, SC_WIDTH, CF_WIDTH, CF_WIDTH])
    y_c = sb * causal_dwconv(sc * sv, sc_w)
    glu = cu * jax.nn.sigmoid(cg)
    d = causal_dwconv(glu, cf_w, cf_b)
    y_d = jax.nn.silu(layernorm(d, cf_g, cf_beta))
    return jnp.concatenate([y_c, y_d], axis=-1) @ w_out


def swiglu(h, wg, wu, wd):
    return (jax.nn.silu(h @ wg) * (h @ wu)) @ wd


def _fwd_setup_inputs(seed: int = 0) -> dict:
    key = jax.random.key(seed)
    ks = iter(jax.random.split(key, 40))
    f32 = jnp.float32

    def nrm(shape, scale):
        return jax.random.normal(next(ks), shape, f32) * scale

    def gain(shape):
        return 1.0 + 0.02 * jax.random.normal(next(ks), shape, f32)

    u = jax.random.uniform(next(ks), (N_EVEN, LRU_WIDTH), f32, 0.9, 0.999)
    a_base = u ** (1.0 / LRU_C)
    lru_lambda = jnp.log(a_base) - jnp.log1p(-a_base)
    return {
        "x": jax.random.normal(next(ks), (BATCH, SEQ, D_MODEL), f32),
        "ln_mix_g": gain((DEPTH, D_MODEL)),
        "ln_ffn_g": gain((DEPTH, D_MODEL)),
        "ln_final_g": gain((D_MODEL,)),
        "ev_w_in": nrm((N_EVEN, D_MODEL, EV_IN), D_MODEL ** -0.5),
        "ev_b_in": nrm((N_EVEN, EV_IN), 0.01),
        "lru_conv_w": nrm((N_EVEN, LRU_CONV, LRU_WIDTH), LRU_CONV ** -0.5),
        "lru_conv_b": nrm((N_EVEN, LRU_WIDTH), 0.01),
        "lru_wa": nrm((N_EVEN, LRU_HEADS, LRU_HEAD_DIM, LRU_HEAD_DIM), LRU_HEAD_DIM ** -0.5),
        "lru_ba": nrm((N_EVEN, LRU_HEADS, LRU_HEAD_DIM), 0.01),
        "lru_wx": nrm((N_EVEN, LRU_HEADS, LRU_HEAD_DIM, LRU_HEAD_DIM), LRU_HEAD_DIM ** -0.5),
        "lru_bx": nrm((N_EVEN, LRU_HEADS, LRU_HEAD_DIM), 0.01),
        "lru_lambda": lru_lambda,
        "hgrn_lb_logits": nrm((N_EVEN, HGRN_WIDTH), 0.1),
        "hgrn_norm_g": gain((N_EVEN, HGRN_HEADS * HGRN_DV)),
        "ev_w_out": nrm((N_EVEN, EV_OUT, D_MODEL), EV_OUT ** -0.5),
        "od_w_in": nrm((N_ODD, D_MODEL, OD_IN), D_MODEL ** -0.5),
        "od_b_in": nrm((N_ODD, OD_IN), 0.01),
        "sc_conv_w": nrm((N_ODD, SC_CONV, SC_WIDTH), SC_CONV ** -0.5),
        "cf_conv_w": nrm((N_ODD, CF_CONV, CF_WIDTH), CF_CONV ** -0.5),
        "cf_conv_b": nrm((N_ODD, CF_WIDTH), 0.01),
        "cf_ln_g": gain((N_ODD, CF_WIDTH)),
        "cf_ln_b": nrm((N_ODD, CF_WIDTH), 0.01),
        "od_w_out": nrm((N_ODD, OD_OUT, D_MODEL), OD_OUT ** -0.5),
        "ffn_w_gate": nrm((DEPTH, D_MODEL, FFN_HIDDEN), D_MODEL ** -0.5),
        "ffn_w_up": nrm((DEPTH, D_MODEL, FFN_HIDDEN), D_MODEL ** -0.5),
        "ffn_w_down": nrm((DEPTH, FFN_HIDDEN, D_MODEL), FFN_HIDDEN ** -0.5),
    }


def _fwd_reference(x, ln_mix_g, ln_ffn_g, ln_final_g, ev_w_in, ev_b_in, lru_conv_w, lru_conv_b,
              lru_wa, lru_ba, lru_wx, lru_bx, lru_lambda, hgrn_lb_logits, hgrn_norm_g, ev_w_out,
              od_w_in, od_b_in, sc_conv_w, cf_conv_w, cf_conv_b, cf_ln_g, cf_ln_b, od_w_out,
              ffn_w_gate, ffn_w_up, ffn_w_down):
    sm = jax.nn.softmax(hgrn_lb_logits.astype(jnp.float32), axis=0)
    lower_bounds = jnp.cumsum(sm, axis=0) - sm[0]
    for layer in range(DEPTH):
        h = rmsnorm(x, ln_mix_g[layer])
        if layer % 2 == 0:
            j = layer // 2
            x = x + even_mixer(h, ev_w_in[j], ev_b_in[j], lru_conv_w[j], lru_conv_b[j],
                               lru_wa[j], lru_ba[j], lru_wx[j], lru_bx[j], lru_lambda[j],
                               lower_bounds[j], hgrn_norm_g[j], ev_w_out[j])
        else:
            j = layer // 2
            x = x + odd_mixer(h, od_w_in[j], od_b_in[j], sc_conv_w[j], cf_conv_w[j], cf_conv_b[j],
                              cf_ln_g[j], cf_ln_b[j], od_w_out[j])
        h = rmsnorm(x, ln_ffn_g[layer])
        x = x + swiglu(h, ffn_w_gate[layer], ffn_w_up[layer], ffn_w_down[layer])
    return rmsnorm(x, ln_final_g)


import jax as _jax
import jax.numpy as _jnp

TWIN_FORMAT = 'train_step'
FWD_PARAMS = ['x', 'ln_mix_g', 'ln_ffn_g', 'ln_final_g', 'ev_w_in', 'ev_b_in', 'lru_conv_w', 'lru_conv_b', 'lru_wa', 'lru_ba', 'lru_wx', 'lru_bx', 'lru_lambda', 'hgrn_lb_logits', 'hgrn_norm_g', 'ev_w_out', 'od_w_in', 'od_b_in', 'sc_conv_w', 'cf_conv_w', 'cf_conv_b', 'cf_ln_g', 'cf_ln_b', 'od_w_out', 'ffn_w_gate', 'ffn_w_up', 'ffn_w_down']
TWIN_WEIGHTS = ['ln_mix_g', 'ln_ffn_g', 'ln_final_g', 'ev_w_in', 'ev_b_in', 'lru_conv_w', 'lru_conv_b', 'lru_wa', 'lru_ba', 'lru_wx', 'lru_bx', 'lru_lambda', 'hgrn_lb_logits', 'hgrn_norm_g', 'ev_w_out', 'od_w_in', 'od_b_in', 'sc_conv_w', 'cf_conv_w', 'cf_conv_b', 'cf_ln_g', 'cf_ln_b', 'od_w_out', 'ffn_w_gate', 'ffn_w_up', 'ffn_w_down']
TWIN_DIFF_INPUT = 'x'
TWIN_INPUTS = ['x', 'ln_mix_g', 'ln_ffn_g', 'ln_final_g', 'ev_w_in', 'ev_b_in', 'lru_conv_w', 'lru_conv_b', 'lru_wa', 'lru_ba', 'lru_wx', 'lru_bx', 'lru_lambda', 'hgrn_lb_logits', 'hgrn_norm_g', 'ev_w_out', 'od_w_in', 'od_b_in', 'sc_conv_w', 'cf_conv_w', 'cf_conv_b', 'cf_ln_g', 'cf_ln_b', 'od_w_out', 'ffn_w_gate', 'ffn_w_up', 'ffn_w_down', 'loss_target', 'm_ln_mix_g', 'm_ln_ffn_g', 'm_ln_final_g', 'm_ev_w_in', 'm_ev_b_in', 'm_lru_conv_w', 'm_lru_conv_b', 'm_lru_wa', 'm_lru_ba', 'm_lru_wx', 'm_lru_bx', 'm_lru_lambda', 'm_hgrn_lb_logits', 'm_hgrn_norm_g', 'm_ev_w_out', 'm_od_w_in', 'm_od_b_in', 'm_sc_conv_w', 'm_cf_conv_w', 'm_cf_conv_b', 'm_cf_ln_g', 'm_cf_ln_b', 'm_od_w_out', 'm_ffn_w_gate', 'm_ffn_w_up', 'm_ffn_w_down', 'v_ln_mix_g', 'v_ln_ffn_g', 'v_ln_final_g', 'v_ev_w_in', 'v_ev_b_in', 'v_lru_conv_w', 'v_lru_conv_b', 'v_lru_wa', 'v_lru_ba', 'v_lru_wx', 'v_lru_bx', 'v_lru_lambda', 'v_hgrn_lb_logits', 'v_hgrn_norm_g', 'v_ev_w_out', 'v_od_w_in', 'v_od_b_in', 'v_sc_conv_w', 'v_cf_conv_w', 'v_cf_conv_b', 'v_cf_ln_g', 'v_cf_ln_b', 'v_od_w_out', 'v_ffn_w_gate', 'v_ffn_w_up', 'v_ffn_w_down']
TWIN_OUTPUTS = ['loss', 'grad_x', 'grad_ln_mix_g', 'grad_ln_ffn_g', 'grad_ln_final_g', 'grad_ev_w_in', 'grad_ev_b_in', 'grad_lru_conv_w', 'grad_lru_conv_b', 'grad_lru_wa', 'grad_lru_ba', 'grad_lru_wx', 'grad_lru_bx', 'grad_lru_lambda', 'grad_hgrn_lb_logits', 'grad_hgrn_norm_g', 'grad_ev_w_out', 'grad_od_w_in', 'grad_od_b_in', 'grad_sc_conv_w', 'grad_cf_conv_w', 'grad_cf_conv_b', 'grad_cf_ln_g', 'grad_cf_ln_b', 'grad_od_w_out', 'grad_ffn_w_gate', 'grad_ffn_w_up', 'grad_ffn_w_down', 'delta_ln_mix_g', 'delta_ln_ffn_g', 'delta_ln_final_g', 'delta_ev_w_in', 'delta_ev_b_in', 'delta_lru_conv_w', 'delta_lru_conv_b', 'delta_lru_wa', 'delta_lru_ba', 'delta_lru_wx', 'delta_lru_bx', 'delta_lru_lambda', 'delta_hgrn_lb_logits', 'delta_hgrn_norm_g', 'delta_ev_w_out', 'delta_od_w_in', 'delta_od_b_in', 'delta_sc_conv_w', 'delta_cf_conv_w', 'delta_cf_conv_b', 'delta_cf_ln_g', 'delta_cf_ln_b', 'delta_od_w_out', 'delta_ffn_w_gate', 'delta_ffn_w_up', 'delta_ffn_w_down', 'new_m_ln_mix_g', 'new_m_ln_ffn_g', 'new_m_ln_final_g', 'new_m_ev_w_in', 'new_m_ev_b_in', 'new_m_lru_conv_w', 'new_m_lru_conv_b', 'new_m_lru_wa', 'new_m_lru_ba', 'new_m_lru_wx', 'new_m_lru_bx', 'new_m_lru_lambda', 'new_m_hgrn_lb_logits', 'new_m_hgrn_norm_g', 'new_m_ev_w_out', 'new_m_od_w_in', 'new_m_od_b_in', 'new_m_sc_conv_w', 'new_m_cf_conv_w', 'new_m_cf_conv_b', 'new_m_cf_ln_g', 'new_m_cf_ln_b', 'new_m_od_w_out', 'new_m_ffn_w_gate', 'new_m_ffn_w_up', 'new_m_ffn_w_down', 'new_v_ln_mix_g', 'new_v_ln_ffn_g', 'new_v_ln_final_g', 'new_v_ev_w_in', 'new_v_ev_b_in', 'new_v_lru_conv_w', 'new_v_lru_conv_b', 'new_v_lru_wa', 'new_v_lru_ba', 'new_v_lru_wx', 'new_v_lru_bx', 'new_v_lru_lambda', 'new_v_hgrn_lb_logits', 'new_v_hgrn_norm_g', 'new_v_ev_w_out', 'new_v_od_w_in', 'new_v_od_b_in', 'new_v_sc_conv_w', 'new_v_cf_conv_w', 'new_v_cf_conv_b', 'new_v_cf_ln_g', 'new_v_cf_ln_b', 'new_v_od_w_out', 'new_v_ffn_w_gate', 'new_v_ffn_w_up', 'new_v_ffn_w_down']
TWIN_LEAF_KINDS = {'loss': 'loss', 'grad_x': 'grad_x', 'grad_ln_mix_g': 'grad_w', 'grad_ln_ffn_g': 'grad_w', 'grad_ln_final_g': 'grad_w', 'grad_ev_w_in': 'grad_w', 'grad_ev_b_in': 'grad_w', 'grad_lru_conv_w': 'grad_w', 'grad_lru_conv_b': 'grad_w', 'grad_lru_wa': 'grad_w', 'grad_lru_ba': 'grad_w', 'grad_lru_wx': 'grad_w', 'grad_lru_bx': 'grad_w', 'grad_lru_lambda': 'grad_w', 'grad_hgrn_lb_logits': 'grad_w', 'grad_hgrn_norm_g': 'grad_w', 'grad_ev_w_out': 'grad_w', 'grad_od_w_in': 'grad_w', 'grad_od_b_in': 'grad_w', 'grad_sc_conv_w': 'grad_w', 'grad_cf_conv_w': 'grad_w', 'grad_cf_conv_b': 'grad_w', 'grad_cf_ln_g': 'grad_w', 'grad_cf_ln_b': 'grad_w', 'grad_od_w_out': 'grad_w', 'grad_ffn_w_gate': 'grad_w', 'grad_ffn_w_up': 'grad_w', 'grad_ffn_w_down': 'grad_w', 'delta_ln_mix_g': 'delta_w', 'delta_ln_ffn_g': 'delta_w', 'delta_ln_final_g': 'delta_w', 'delta_ev_w_in': 'delta_w', 'delta_ev_b_in': 'delta_w', 'delta_lru_conv_w': 'delta_w', 'delta_lru_conv_b': 'delta_w', 'delta_lru_wa': 'delta_w', 'delta_lru_ba': 'delta_w', 'delta_lru_wx': 'delta_w', 'delta_lru_bx': 'delta_w', 'delta_lru_lambda': 'delta_w', 'delta_hgrn_lb_logits': 'delta_w', 'delta_hgrn_norm_g': 'delta_w', 'delta_ev_w_out': 'delta_w', 'delta_od_w_in': 'delta_w', 'delta_od_b_in': 'delta_w', 'delta_sc_conv_w': 'delta_w', 'delta_cf_conv_w': 'delta_w', 'delta_cf_conv_b': 'delta_w', 'delta_cf_ln_g': 'delta_w', 'delta_cf_ln_b': 'delta_w', 'delta_od_w_out': 'delta_w', 'delta_ffn_w_gate': 'delta_w', 'delta_ffn_w_up': 'delta_w', 'delta_ffn_w_down': 'delta_w', 'new_m_ln_mix_g': 'new_m', 'new_m_ln_ffn_g': 'new_m', 'new_m_ln_final_g': 'new_m', 'new_m_ev_w_in': 'new_m', 'new_m_ev_b_in': 'new_m', 'new_m_lru_conv_w': 'new_m', 'new_m_lru_conv_b': 'new_m', 'new_m_lru_wa': 'new_m', 'new_m_lru_ba': 'new_m', 'new_m_lru_wx': 'new_m', 'new_m_lru_bx': 'new_m', 'new_m_lru_lambda': 'new_m', 'new_m_hgrn_lb_logits': 'new_m', 'new_m_hgrn_norm_g': 'new_m', 'new_m_ev_w_out': 'new_m', 'new_m_od_w_in': 'new_m', 'new_m_od_b_in': 'new_m', 'new_m_sc_conv_w': 'new_m', 'new_m_cf_conv_w': 'new_m', 'new_m_cf_conv_b': 'new_m', 'new_m_cf_ln_g': 'new_m', 'new_m_cf_ln_b': 'new_m', 'new_m_od_w_out': 'new_m', 'new_m_ffn_w_gate': 'new_m', 'new_m_ffn_w_up': 'new_m', 'new_m_ffn_w_down': 'new_m', 'new_v_ln_mix_g': 'new_v', 'new_v_ln_ffn_g': 'new_v', 'new_v_ln_final_g': 'new_v', 'new_v_ev_w_in': 'new_v', 'new_v_ev_b_in': 'new_v', 'new_v_lru_conv_w': 'new_v', 'new_v_lru_conv_b': 'new_v', 'new_v_lru_wa': 'new_v', 'new_v_lru_ba': 'new_v', 'new_v_lru_wx': 'new_v', 'new_v_lru_bx': 'new_v', 'new_v_lru_lambda': 'new_v', 'new_v_hgrn_lb_logits': 'new_v', 'new_v_hgrn_norm_g': 'new_v', 'new_v_ev_w_out': 'new_v', 'new_v_od_w_in': 'new_v', 'new_v_od_b_in': 'new_v', 'new_v_sc_conv_w': 'new_v', 'new_v_cf_conv_w': 'new_v', 'new_v_cf_conv_b': 'new_v', 'new_v_cf_ln_g': 'new_v', 'new_v_cf_ln_b': 'new_v', 'new_v_od_w_out': 'new_v', 'new_v_ffn_w_gate': 'new_v', 'new_v_ffn_w_up': 'new_v', 'new_v_ffn_w_down': 'new_v'}


def _forward(args):
    return _fwd_reference(*[args[k] for k in FWD_PARAMS])


def _output_shape():
    def fwd():
        inp = _fwd_setup_inputs(0)
        return _fwd_reference(*[inp[k] for k in FWD_PARAMS])
    out = _jax.eval_shape(fwd)
    return out.shape, out.dtype

N_MICROBATCH = 1
ADAM_LR = 0.001
ADAM_B1 = 0.9
ADAM_B2 = 0.999
ADAM_EPS = 1e-08
ADAM_WD = 0.01
ADAM_STEP = 10
PER_EXAMPLE_BATCH_AXIS = {'x': 0, 'loss_target': 0}
SHARED_INPUTS = []
_WEIGHT_DTYPES = {'ln_mix_g': _jnp.float32, 'ln_ffn_g': _jnp.float32, 'ln_final_g': _jnp.float32, 'ev_w_in': _jnp.float32, 'ev_b_in': _jnp.float32, 'lru_conv_w': _jnp.float32, 'lru_conv_b': _jnp.float32, 'lru_wa': _jnp.float32, 'lru_ba': _jnp.float32, 'lru_wx': _jnp.float32, 'lru_bx': _jnp.float32, 'lru_lambda': _jnp.float32, 'hgrn_lb_logits': _jnp.float32, 'hgrn_norm_g': _jnp.float32, 'ev_w_out': _jnp.float32, 'od_w_in': _jnp.float32, 'od_b_in': _jnp.float32, 'sc_conv_w': _jnp.float32, 'cf_conv_w': _jnp.float32, 'cf_conv_b': _jnp.float32, 'cf_ln_g': _jnp.float32, 'cf_ln_b': _jnp.float32, 'od_w_out': _jnp.float32, 'ffn_w_gate': _jnp.float32, 'ffn_w_up': _jnp.float32, 'ffn_w_down': _jnp.float32}
MOMENT_SCALE = {'ln_mix_g': 8.905067e-02, 'ln_ffn_g': 6.528646e-02, 'ln_final_g': 1.599757e+01, 'ev_w_in': 4.919494e-02, 'ev_b_in': 2.037310e-01, 'lru_conv_w': 5.389063e-02, 'lru_conv_b': 5.587807e-01, 'lru_wa': 1.615341e-02, 'lru_ba': 1.449397e-02, 'lru_wx': 2.879838e-02, 'lru_bx': 1.771955e-02, 'lru_lambda': 2.800056e-02, 'hgrn_lb_logits': 3.537642e-03, 'hgrn_norm_g': 6.683587e-02, 'ev_w_out': 5.940873e-02, 'od_w_in': 5.466180e-02, 'od_b_in': 5.746973e-02, 'sc_conv_w': 6.787662e-02, 'cf_conv_w': 4.141755e-02, 'cf_conv_b': 1.051903e-01, 'cf_ln_g': 5.655662e-02, 'cf_ln_b': 6.165775e-02, 'od_w_out': 5.571378e-02, 'ffn_w_gate': 2.855851e-02, 'ffn_w_up': 2.768169e-02, 'ffn_w_down': 4.591519e-02}


def _to_microbatches(a, axis):
    t = _jnp.moveaxis(a, axis, 0)
    t = t.reshape((N_MICROBATCH, t.shape[0] // N_MICROBATCH) + t.shape[1:])
    return _jnp.moveaxis(t, 1, axis + 1)


def setup_inputs(seed: int = 0) -> dict:
    inp = _fwd_setup_inputs(seed)
    key = _jax.random.fold_in(_jax.random.key(seed), 7919)
    shape, _ = _output_shape()
    out = dict(inp)
    out["loss_target"] = _jax.random.normal(_jax.random.fold_in(key, 0), shape, _jnp.float32)
    for i, name in enumerate(TWIN_WEIGHTS):
        w = inp[name].astype(_jnp.float32)
        if MOMENT_SCALE is None:
            s = _jnp.sqrt(_jnp.mean(_jnp.square(w)) + 1e-30)
        else:
            s = MOMENT_SCALE[name]
        km, kv = _jax.random.split(_jax.random.fold_in(key, i + 1))
        out[name] = w
        out["m_" + name] = s * _jax.random.normal(km, w.shape, _jnp.float32)
        out["v_" + name] = (s * s) * _jax.random.uniform(kv, w.shape, _jnp.float32, 0.5, 1.5)
    if N_MICROBATCH > 1:
        for name, axis in PER_EXAMPLE_BATCH_AXIS.items():
            out[name] = _to_microbatches(out[name], axis)
    return {'x': out['x'], 'ln_mix_g': out['ln_mix_g'], 'ln_ffn_g': out['ln_ffn_g'], 'ln_final_g': out['ln_final_g'], 'ev_w_in': out['ev_w_in'], 'ev_b_in': out['ev_b_in'], 'lru_conv_w': out['lru_conv_w'], 'lru_conv_b': out['lru_conv_b'], 'lru_wa': out['lru_wa'], 'lru_ba': out['lru_ba'], 'lru_wx': out['lru_wx'], 'lru_bx': out['lru_bx'], 'lru_lambda': out['lru_lambda'], 'hgrn_lb_logits': out['hgrn_lb_logits'], 'hgrn_norm_g': out['hgrn_norm_g'], 'ev_w_out': out['ev_w_out'], 'od_w_in': out['od_w_in'], 'od_b_in': out['od_b_in'], 'sc_conv_w': out['sc_conv_w'], 'cf_conv_w': out['cf_conv_w'], 'cf_conv_b': out['cf_conv_b'], 'cf_ln_g': out['cf_ln_g'], 'cf_ln_b': out['cf_ln_b'], 'od_w_out': out['od_w_out'], 'ffn_w_gate': out['ffn_w_gate'], 'ffn_w_up': out['ffn_w_up'], 'ffn_w_down': out['ffn_w_down'], 'loss_target': out['loss_target'], 'm_ln_mix_g': out['m_ln_mix_g'], 'm_ln_ffn_g': out['m_ln_ffn_g'], 'm_ln_final_g': out['m_ln_final_g'], 'm_ev_w_in': out['m_ev_w_in'], 'm_ev_b_in': out['m_ev_b_in'], 'm_lru_conv_w': out['m_lru_conv_w'], 'm_lru_conv_b': out['m_lru_conv_b'], 'm_lru_wa': out['m_lru_wa'], 'm_lru_ba': out['m_lru_ba'], 'm_lru_wx': out['m_lru_wx'], 'm_lru_bx': out['m_lru_bx'], 'm_lru_lambda': out['m_lru_lambda'], 'm_hgrn_lb_logits': out['m_hgrn_lb_logits'], 'm_hgrn_norm_g': out['m_hgrn_norm_g'], 'm_ev_w_out': out['m_ev_w_out'], 'm_od_w_in': out['m_od_w_in'], 'm_od_b_in': out['m_od_b_in'], 'm_sc_conv_w': out['m_sc_conv_w'], 'm_cf_conv_w': out['m_cf_conv_w'], 'm_cf_conv_b': out['m_cf_conv_b'], 'm_cf_ln_g': out['m_cf_ln_g'], 'm_cf_ln_b': out['m_cf_ln_b'], 'm_od_w_out': out['m_od_w_out'], 'm_ffn_w_gate': out['m_ffn_w_gate'], 'm_ffn_w_up': out['m_ffn_w_up'], 'm_ffn_w_down': out['m_ffn_w_down'], 'v_ln_mix_g': out['v_ln_mix_g'], 'v_ln_ffn_g': out['v_ln_ffn_g'], 'v_ln_final_g': out['v_ln_final_g'], 'v_ev_w_in': out['v_ev_w_in'], 'v_ev_b_in': out['v_ev_b_in'], 'v_lru_conv_w': out['v_lru_conv_w'], 'v_lru_conv_b': out['v_lru_conv_b'], 'v_lru_wa': out['v_lru_wa'], 'v_lru_ba': out['v_lru_ba'], 'v_lru_wx': out['v_lru_wx'], 'v_lru_bx': out['v_lru_bx'], 'v_lru_lambda': out['v_lru_lambda'], 'v_hgrn_lb_logits': out['v_hgrn_lb_logits'], 'v_hgrn_norm_g': out['v_hgrn_norm_g'], 'v_ev_w_out': out['v_ev_w_out'], 'v_od_w_in': out['v_od_w_in'], 'v_od_b_in': out['v_od_b_in'], 'v_sc_conv_w': out['v_sc_conv_w'], 'v_cf_conv_w': out['v_cf_conv_w'], 'v_cf_conv_b': out['v_cf_conv_b'], 'v_cf_ln_g': out['v_cf_ln_g'], 'v_cf_ln_b': out['v_cf_ln_b'], 'v_od_w_out': out['v_od_w_out'], 'v_ffn_w_gate': out['v_ffn_w_gate'], 'v_ffn_w_up': out['v_ffn_w_up'], 'v_ffn_w_down': out['v_ffn_w_down']}


def _loss(weights, diff, rest, loss_target):
    with _jax.named_scope("forward"):
        args = {**rest, TWIN_DIFF_INPUT: diff, **{k: w.astype(_WEIGHT_DTYPES[k]) for k, w in weights.items()}}
        y = _forward(args)
    with _jax.named_scope("loss_head"):
        err = _jnp.square(y.astype(_jnp.float32) - loss_target)
        return 0.5 * _jnp.sum(_jnp.mean(err, axis=-1)) if err.ndim else 0.5 * err


def _adamw(w, g, m, v):
    m = ADAM_B1 * m + (1.0 - ADAM_B1) * g
    v = ADAM_B2 * v + (1.0 - ADAM_B2) * _jnp.square(g)
    m_hat = m / (1.0 - ADAM_B1 ** ADAM_STEP)
    v_hat = v / (1.0 - ADAM_B2 ** ADAM_STEP)
    delta = -ADAM_LR * (m_hat / (_jnp.sqrt(v_hat) + ADAM_EPS) + ADAM_WD * w)
    return delta, m, v


def reference(x, ln_mix_g, ln_ffn_g, ln_final_g, ev_w_in, ev_b_in, lru_conv_w, lru_conv_b, lru_wa, lru_ba, lru_wx, lru_bx, lru_lambda, hgrn_lb_logits, hgrn_norm_g, ev_w_out, od_w_in, od_b_in, sc_conv_w, cf_conv_w, cf_conv_b, cf_ln_g, cf_ln_b, od_w_out, ffn_w_gate, ffn_w_up, ffn_w_down, loss_target, m_ln_mix_g, m_ln_ffn_g, m_ln_final_g, m_ev_w_in, m_ev_b_in, m_lru_conv_w, m_lru_conv_b, m_lru_wa, m_lru_ba, m_lru_wx, m_lru_bx, m_lru_lambda, m_hgrn_lb_logits, m_hgrn_norm_g, m_ev_w_out, m_od_w_in, m_od_b_in, m_sc_conv_w, m_cf_conv_w, m_cf_conv_b, m_cf_ln_g, m_cf_ln_b, m_od_w_out, m_ffn_w_gate, m_ffn_w_up, m_ffn_w_down, v_ln_mix_g, v_ln_ffn_g, v_ln_final_g, v_ev_w_in, v_ev_b_in, v_lru_conv_w, v_lru_conv_b, v_lru_wa, v_lru_ba, v_lru_wx, v_lru_bx, v_lru_lambda, v_hgrn_lb_logits, v_hgrn_norm_g, v_ev_w_out, v_od_w_in, v_od_b_in, v_sc_conv_w, v_cf_conv_w, v_cf_conv_b, v_cf_ln_g, v_cf_ln_b, v_od_w_out, v_ffn_w_gate, v_ffn_w_up, v_ffn_w_down):
    given = dict(x=x, ln_mix_g=ln_mix_g, ln_ffn_g=ln_ffn_g, ln_final_g=ln_final_g, ev_w_in=ev_w_in, ev_b_in=ev_b_in, lru_conv_w=lru_conv_w, lru_conv_b=lru_conv_b, lru_wa=lru_wa, lru_ba=lru_ba, lru_wx=lru_wx, lru_bx=lru_bx, lru_lambda=lru_lambda, hgrn_lb_logits=hgrn_lb_logits, hgrn_norm_g=hgrn_norm_g, ev_w_out=ev_w_out, od_w_in=od_w_in, od_b_in=od_b_in, sc_conv_w=sc_conv_w, cf_conv_w=cf_conv_w, cf_conv_b=cf_conv_b, cf_ln_g=cf_ln_g, cf_ln_b=cf_ln_b, od_w_out=od_w_out, ffn_w_gate=ffn_w_gate, ffn_w_up=ffn_w_up, ffn_w_down=ffn_w_down, loss_target=loss_target, m_ln_mix_g=m_ln_mix_g, m_ln_ffn_g=m_ln_ffn_g, m_ln_final_g=m_ln_final_g, m_ev_w_in=m_ev_w_in, m_ev_b_in=m_ev_b_in, m_lru_conv_w=m_lru_conv_w, m_lru_conv_b=m_lru_conv_b, m_lru_wa=m_lru_wa, m_lru_ba=m_lru_ba, m_lru_wx=m_lru_wx, m_lru_bx=m_lru_bx, m_lru_lambda=m_lru_lambda, m_hgrn_lb_logits=m_hgrn_lb_logits, m_hgrn_norm_g=m_hgrn_norm_g, m_ev_w_out=m_ev_w_out, m_od_w_in=m_od_w_in, m_od_b_in=m_od_b_in, m_sc_conv_w=m_sc_conv_w, m_cf_conv_w=m_cf_conv_w, m_cf_conv_b=m_cf_conv_b, m_cf_ln_g=m_cf_ln_g, m_cf_ln_b=m_cf_ln_b, m_od_w_out=m_od_w_out, m_ffn_w_gate=m_ffn_w_gate, m_ffn_w_up=m_ffn_w_up, m_ffn_w_down=m_ffn_w_down, v_ln_mix_g=v_ln_mix_g, v_ln_ffn_g=v_ln_ffn_g, v_ln_final_g=v_ln_final_g, v_ev_w_in=v_ev_w_in, v_ev_b_in=v_ev_b_in, v_lru_conv_w=v_lru_conv_w, v_lru_conv_b=v_lru_conv_b, v_lru_wa=v_lru_wa, v_lru_ba=v_lru_ba, v_lru_wx=v_lru_wx, v_lru_bx=v_lru_bx, v_lru_lambda=v_lru_lambda, v_hgrn_lb_logits=v_hgrn_lb_logits, v_hgrn_norm_g=v_hgrn_norm_g, v_ev_w_out=v_ev_w_out, v_od_w_in=v_od_w_in, v_od_b_in=v_od_b_in, v_sc_conv_w=v_sc_conv_w, v_cf_conv_w=v_cf_conv_w, v_cf_conv_b=v_cf_conv_b, v_cf_ln_g=v_cf_ln_g, v_cf_ln_b=v_cf_ln_b, v_od_w_out=v_od_w_out, v_ffn_w_gate=v_ffn_w_gate, v_ffn_w_up=v_ffn_w_up, v_ffn_w_down=v_ffn_w_down)
    weights = {n: given[n] for n in TWIN_WEIGHTS}
    shared = {n: given[n] for n in SHARED_INPUTS}
    per_example = {n: given[n] for n in ['x']}
    grad_fn = _jax.value_and_grad(_loss, argnums=(0, 1))

    def one_microbatch(ex, loss_target):
        ex = dict(ex)
        diff = ex.pop(TWIN_DIFF_INPUT)
        return grad_fn(weights, diff, {**shared, **ex}, loss_target)

    if N_MICROBATCH == 1:
        loss, (grad_w, grad_x) = one_microbatch(per_example, given["loss_target"])
    else:
        def body(carry, xs):
            loss_sum, grad_sum = carry
            l_k, (gw_k, gx_k) = one_microbatch(xs[0], xs[1])
            with _jax.named_scope("update"):
                return (loss_sum + l_k, _jax.tree.map(_jnp.add, grad_sum, gw_k)), gx_k

        init = (_jnp.zeros((), _jnp.float32), _jax.tree.map(_jnp.zeros_like, weights))
        (loss, grad_w), grad_x = _jax.lax.scan(body, init, (per_example, given["loss_target"]))
    with _jax.named_scope("update"):
        delta_w, new_m, new_v = {}, {}, {}
        for n in TWIN_WEIGHTS:
            delta_w[n], new_m[n], new_v[n] = _adamw(weights[n], grad_w[n], given["m_" + n], given["v_" + n])
    return (loss, grad_x, *[grad_w[n] for n in TWIN_WEIGHTS], *[delta_w[n] for n in TWIN_WEIGHTS],
            *[new_m[n] for n in TWIN_WEIGHTS], *[new_v[n] for n in TWIN_WEIGHTS])
```

```python
import functools

import jax
import jax.numpy as jnp
from jax import lax
from jax.experimental import pallas as pl
from jax.experimental.pallas import tpu as pltpu

f32 = jnp.float32
_MXU = jnp.bfloat16
_WIRE = jnp.bfloat16

N_HEADS = 8
LRU_C = 8.0
EPS = 1e-6
F_FLOOR = 1e-30
SUB = 16
ADAM_LR, ADAM_B1, ADAM_B2, ADAM_EPS, ADAM_WD, ADAM_STEP = 0.001, 0.9, 0.999, 1e-08, 0.01, 10
V7X_VMEM_LIMIT = 48 * 1024 * 1024
LANE = 128
MESH = pl.DeviceIdType.MESH

_IN_NAMES = ['x', 'ln_mix_g', 'ln_ffn_g', 'ln_final_g', 'ev_w_in', 'ev_b_in', 'lru_conv_w', 'lru_conv_b', 'lru_wa', 'lru_ba',
             'lru_wx', 'lru_bx', 'lru_lambda', 'hgrn_lb_logits', 'hgrn_norm_g', 'ev_w_out', 'od_w_in', 'od_b_in', 'sc_conv_w',
             'cf_conv_w', 'cf_conv_b', 'cf_ln_g', 'cf_ln_b', 'od_w_out', 'ffn_w_gate', 'ffn_w_up', 'ffn_w_down']
_BIG = ['ev_w_in', 'ev_w_out', 'od_w_in', 'od_w_out', 'ffn_w_gate', 'ffn_w_up', 'ffn_w_down']
_BIG_COL = {'ev_w_in': True, 'ev_w_out': False, 'od_w_in': True, 'od_w_out': False, 'ffn_w_gate': True, 'ffn_w_up': True,
            'ffn_w_down': False}
_SMALL_SHARDED = ['lru_conv_w', 'od_b_in', 'sc_conv_w', 'cf_conv_w', 'cf_conv_b', 'cf_ln_g', 'cf_ln_b']
_SMALL_REPL = ['ln_mix_g', 'ln_ffn_g', 'ln_final_g', 'ev_b_in', 'lru_conv_b', 'lru_wa', 'lru_ba', 'lru_wx', 'lru_bx',
               'lru_lambda', 'hgrn_lb_logits', 'hgrn_norm_g']


def _tile(n, pref, align):
    if n <= pref:
        return n
    t = (pref // align) * align
    while t >= align:
        if n % t == 0:
            return t
        t -= align
    return n


def _params(sem):
    return pltpu.CompilerParams(dimension_semantics=sem, vmem_limit_bytes=V7X_VMEM_LIMIT)


def _rows(shape):
    return lax.broadcasted_iota(jnp.int32, shape, 0)


def _mxdot(a, b, dims=(((1,), (0,)), ((), ()))):
    return lax.dot_general(a.astype(_MXU), b.astype(_MXU), dims, preferred_element_type=f32)


_NN = (((1,), (0,)), ((), ()))
_NT = (((1,), (1,)), ((), ()))
_TN = (((0,), (0,)), ((), ()))


def _matmul(a, b, mode, out_dtype, name, bias=None, add=None, tm=1024, tn=512, tk=2048):
    if mode == 'nn':
        (M, K), (K2, N) = a.shape, b.shape
    elif mode == 'nt':
        (M, K), (N, K2) = a.shape, b.shape
    else:
        (K, M), (K2, N) = a.shape, b.shape
    assert K == K2, (name, a.shape, b.shape)
    tm, tn, tk = _tile(M, tm, LANE), _tile(N, tn, LANE), _tile(K, tk, LANE)
    nk = K // tk
    dims = {'nn': _NN, 'nt': _NT, 'tn': _TN}[mode]
    has_bias, has_add = bias is not None, add is not None

    def body(*refs):
        a_ref, b_ref = refs[0], refs[1]
        pos = 2
        bias_ref = add_ref = None
        if has_bias:
            bias_ref = refs[pos]
            pos += 1
        if has_add:
            add_ref = refs[pos]
            pos += 1
        o_ref, acc_ref = refs[pos], refs[pos + 1]
        k = pl.program_id(2)

        @pl.when(k == 0)
        def _():
            acc_ref[...] = jnp.zeros_like(acc_ref)

        acc_ref[...] += _mxdot(a_ref[...], b_ref[...], dims)

        @pl.when(k == nk - 1)
        def _():
            r = acc_ref[...]
            if has_bias:
                r = r + bias_ref[...]
            if has_add:
                r = r + add_ref[...]
            o_ref[...] = r.astype(o_ref.dtype)

    if mode == 'tn':
        a_spec = pl.BlockSpec((tk, tm), lambda i, j, k: (k, i))
    else:
        a_spec = pl.BlockSpec((tm, tk), lambda i, j, k: (i, k))
    if mode == 'nt':
        b_spec = pl.BlockSpec((tn, tk), lambda i, j, k: (j, k))
    else:
        b_spec = pl.BlockSpec((tk, tn), lambda i, j, k: (k, j))
    in_specs, args = [a_spec, b_spec], [a, b]
    if has_bias:
        in_specs.append(pl.BlockSpec((1, tn), lambda i, j, k: (0, j)))
        args.append(bias)
    if has_add:
        in_specs.append(pl.BlockSpec((tm, tn), lambda i, j, k: (i, j)))
        args.append(add)
    return pl.pallas_call(
        body, name=name, grid=(M // tm, N // tn, nk), in_specs=in_specs,
        out_specs=pl.BlockSpec((tm, tn), lambda i, j, k: (i, j)),
        out_shape=jax.ShapeDtypeStruct((M, N), out_dtype),
        scratch_shapes=[pltpu.VMEM((tm, tn), f32)],
        compiler_params=_params(("parallel", "parallel", "arbitrary")),
    )(*args)


def _pw(fn, slabs, out_dtypes, width, name, consts=()):
    T = slabs[0][0].shape[0]
    tt, cb = _tile(T, 512, 8), _tile(width, 512, LANE)
    nin, ncst = len(slabs), len(consts)

    def body(*refs):
        res = fn(*[r[...] for r in refs[:nin + ncst]])
        if not isinstance(res, (tuple, list)):
            res = (res,)
        for r, o in zip(res, refs[nin + ncst:], strict=True):
            o[...] = r.astype(o.dtype)

    in_specs, args = [], []
    for arr, col0 in slabs:
        assert col0 % cb == 0
        in_specs.append(pl.BlockSpec((tt, cb), functools.partial(lambda t, c, c0: (t, c0 + c), c0=col0 // cb)))
        args.append(arr)
    for cst in consts:
        in_specs.append(pl.BlockSpec((1, cb), lambda t, c: (0, c)))
        args.append(cst)
    outs = pl.pallas_call(
        body, name=name, grid=(T // tt, width // cb), in_specs=in_specs,
        out_specs=[pl.BlockSpec((tt, cb), lambda t, c: (t, c)) for _ in out_dtypes],
        out_shape=[jax.ShapeDtypeStruct((T, width), d) for d in out_dtypes],
        compiler_params=_params(("parallel", "parallel")),
    )(*args)
    return outs[0] if len(out_dtypes) == 1 else outs


def _rmsnorm_fwd(x, g, name):
    T, D = x.shape
    tt = _tile(T, 256, 8)

    def body(x_ref, g_ref, o_ref):
        xv = x_ref[...]
        r = lax.rsqrt(jnp.mean(xv * xv, axis=-1, keepdims=True) + EPS)
        o_ref[...] = (xv * r * g_ref[...]).astype(o_ref.dtype)

    return pl.pallas_call(
        body, name=name, grid=(T // tt,),
        in_specs=[pl.BlockSpec((tt, D), lambda t: (t, 0)), pl.BlockSpec((1, D), lambda t: (0, 0))],
        out_specs=pl.BlockSpec((tt, D), lambda t: (t, 0)),
        out_shape=jax.ShapeDtypeStruct((T, D), _MXU), compiler_params=_params(("parallel",)),
    )(x, g)


def _rmsnorm_bwd(dh, x, g, dx_in, name):
    T, D = x.shape
    tt = _tile(T, 256, 8)

    def body(dh_ref, x_ref, g_ref, dxi_ref, dx_ref, dg_ref):
        t = pl.program_id(0)
        xv, d = x_ref[...], dh_ref[...]
        r = lax.rsqrt(jnp.mean(xv * xv, axis=-1, keepdims=True) + EPS)
        n = xv * r
        dn = d * g_ref[...]
        dx_ref[...] = dxi_ref[...] + r * (dn - n * jnp.mean(dn * n, axis=-1, keepdims=True))

        @pl.when(t == 0)
        def _():
            dg_ref[...] = jnp.zeros_like(dg_ref)

        dg_ref[...] += jnp.sum(d * n, axis=0, keepdims=True)

    return pl.pallas_call(
        body, name=name, grid=(T // tt,),
        in_specs=[pl.BlockSpec((tt, D), lambda t: (t, 0)), pl.BlockSpec((tt, D), lambda t: (t, 0)),
                  pl.BlockSpec((1, D), lambda t: (0, 0)), pl.BlockSpec((tt, D), lambda t: (t, 0))],
        out_specs=[pl.BlockSpec((tt, D), lambda t: (t, 0)), pl.BlockSpec((1, D), lambda t: (0, 0))],
        out_shape=[jax.ShapeDtypeStruct((T, D), f32), jax.ShapeDtypeStruct((1, D), f32)],
        compiler_params=_params(("arbitrary",)),
    )(dh, x, g, dx_in)


def _final_loss(x, g, tgt, name):
    T, D = x.shape
    tt = _tile(T, 256, 8)

    def body(x_ref, g_ref, t_ref, l_ref, dx_ref, dg_ref):
        t = pl.program_id(0)
        xv = x_ref[...]
        r = lax.rsqrt(jnp.mean(xv * xv, axis=-1, keepdims=True) + EPS)
        n = xv * r
        e = n * g_ref[...] - t_ref[...]
        part = 0.5 * jnp.sum(jnp.mean(e * e, axis=-1, keepdims=True), axis=0, keepdims=True)
        dy = e * (1.0 / D)
        dn = dy * g_ref[...]
        dx_ref[...] = r * (dn - n * jnp.mean(dn * n, axis=-1, keepdims=True))

        @pl.when(t == 0)
        def _():
            dg_ref[...] = jnp.zeros_like(dg_ref)
            l_ref[...] = jnp.zeros_like(l_ref)

        dg_ref[...] += jnp.sum(dy * n, axis=0, keepdims=True)
        l_ref[...] += jnp.broadcast_to(part, l_ref.shape)

    return pl.pallas_call(
        body, name=name, grid=(T // tt,),
        in_specs=[pl.BlockSpec((tt, D), lambda t: (t, 0)), pl.BlockSpec((1, D), lambda t: (0, 0)),
                  pl.BlockSpec((tt, D), lambda t: (t, 0))],
        out_specs=[pl.BlockSpec((1, LANE), lambda t: (0, 0)), pl.BlockSpec((tt, D), lambda t: (t, 0)),
                   pl.BlockSpec((1, D), lambda t: (0, 0))],
        out_shape=[jax.ShapeDtypeStruct((1, LANE), f32), jax.ShapeDtypeStruct((T, D), f32),
                   jax.ShapeDtypeStruct((1, D), f32)],
        compiler_params=_params(("arbitrary",)),
    )(x, g, tgt)


def _colsum(x, name):
    T, N = x.shape
    tt, cb = _tile(T, 512, 8), _tile(N, 512, LANE)

    def body(x_ref, o_ref):
        @pl.when(pl.program_id(1) == 0)
        def _():
            o_ref[...] = jnp.zeros_like(o_ref)

        o_ref[...] += jnp.sum(x_ref[...].astype(f32), axis=0, keepdims=True)

    return pl.pallas_call(
        body, name=name, grid=(N // cb, T // tt), in_specs=[pl.BlockSpec((tt, cb), lambda c, t: (t, c))],
        out_specs=pl.BlockSpec((1, cb), lambda c, t: (0, c)), out_shape=jax.ShapeDtypeStruct((1, N), f32),
        compiler_params=_params(("parallel", "arbitrary")),
    )(x)


def _shift_down(cur, prev, j):
    if j == 0:
        return cur
    n = cur.shape[0]
    return jnp.where(_rows(cur.shape) < j, pltpu.roll(prev, j, 0), pltpu.roll(cur, j, 0))


def _shift_up(cur, nxt, j):
    if j == 0:
        return cur
    n = cur.shape[0]
    return jnp.where(_rows(cur.shape) >= n - j, pltpu.roll(nxt, n - j, 0), pltpu.roll(cur, n - j, 0))


def _conv_tiles(T, C, K):
    tt, cb = _tile(T, 256, 8), _tile(C, 256, LANE)
    assert tt >= K, (tt, K)
    return tt, cb


def _conv_fwd(x, col0, w, b, C, name):
    T, K = x.shape[0], w.shape[0]
    tt, cb = _conv_tiles(T, C, K)
    c0 = col0 // cb
    assert col0 % cb == 0
    has_b = b is not None

    def body(*refs):
        cur_ref, prev_ref, w_ref = refs[:3]
        o_ref = refs[-1]
        t = pl.program_id(1)
        cur = cur_ref[...]
        prev = jnp.where(t > 0, prev_ref[...], 0.0)
        wv = w_ref[...]
        acc = jnp.zeros_like(cur)
        for k in range(K):
            acc = acc + wv[k:k + 1, :] * _shift_down(cur, prev, K - 1 - k)
        if has_b:
            acc = acc + refs[3][...]
        o_ref[...] = acc

    in_specs = [pl.BlockSpec((tt, cb), lambda c, t: (t, c0 + c)),
                pl.BlockSpec((tt, cb), lambda c, t: (jnp.maximum(t - 1, 0), c0 + c)),
                pl.BlockSpec((K, cb), lambda c, t: (0, c))]
    args = [x, x, w]
    if has_b:
        in_specs.append(pl.BlockSpec((1, cb), lambda c, t: (0, c)))
        args.append(b)
    return pl.pallas_call(
        body, name=name, grid=(C // cb, T // tt), in_specs=in_specs,
        out_specs=pl.BlockSpec((tt, cb), lambda c, t: (t, c)), out_shape=jax.ShapeDtypeStruct((T, C), f32),
        compiler_params=_params(("parallel", "parallel")),
    )(*args)


def _conv_bwd_dx(dy, w, name):
    T, C = dy.shape
    K = w.shape[0]
    tt, cb = _conv_tiles(T, C, K)
    nt = T // tt

    def body(cur_ref, nxt_ref, w_ref, o_ref):
        t = pl.program_id(1)
        cur = cur_ref[...]
        nxt = jnp.where(t < nt - 1, nxt_ref[...], 0.0)
        wv = w_ref[...]
        acc = jnp.zeros_like(cur)
        for k in range(K):
            acc = acc + wv[k:k + 1, :] * _shift_up(cur, nxt, K - 1 - k)
        o_ref[...] = acc

    return pl.pallas_call(
        body, name=name, grid=(C // cb, nt),
        in_specs=[pl.BlockSpec((tt, cb), lambda c, t: (t, c)),
                  pl.BlockSpec((tt, cb), lambda c, t: (jnp.minimum(t + 1, nt - 1), c)),
                  pl.BlockSpec((K, cb), lambda c, t: (0, c))],
        out_specs=pl.BlockSpec((tt, cb), lambda c, t: (t, c)), out_shape=jax.ShapeDtypeStruct((T, C), f32),
        compiler_params=_params(("parallel", "parallel")),
    )(dy, dy, w)


def _conv_bwd_dw(dy, x, col0, K, name):
    T, C = dy.shape
    tt, cb = _conv_tiles(T, C, K)
    c0 = col0 // cb
    assert col0 % cb == 0

    def body(dy_ref, cur_ref, prev_ref, dw_ref, db_ref):
        t = pl.program_id(1)

        @pl.when(t == 0)
        def _():
            dw_ref[...] = jnp.zeros_like(dw_ref)
            db_ref[...] = jnp.zeros_like(db_ref)

        d = dy_ref[...]
        cur = cur_ref[...]
        prev = jnp.where(t > 0, prev_ref[...], 0.0)
        for k in range(K):
            row = jnp.sum(d * _shift_down(cur, prev, K - 1 - k), axis=0, keepdims=True)
            dw_ref[pl.ds(k, 1), :] = dw_ref[pl.ds(k, 1), :] + row
        db_ref[...] += jnp.sum(d, axis=0, keepdims=True)

    return pl.pallas_call(
        body, name=name, grid=(C // cb, T // tt),
        in_specs=[pl.BlockSpec((tt, cb), lambda c, t: (t, c)),
                  pl.BlockSpec((tt, cb), lambda c, t: (t, c0 + c)),
                  pl.BlockSpec((tt, cb), lambda c, t: (jnp.maximum(t - 1, 0), c0 + c))],
        out_specs=[pl.BlockSpec((K, cb), lambda c, t: (0, c)), pl.BlockSpec((1, cb), lambda c, t: (0, c))],
        out_shape=[jax.ShapeDtypeStruct((K, C), f32), jax.ShapeDtypeStruct((1, C), f32)],
        compiler_params=_params(("parallel", "arbitrary")),
    )(dy, x, x)


def _expm1(z):
    poly = z * (1.0 + z * (0.5 + z * (1.0 / 6.0 + z * (1.0 / 24.0 + z * (1.0 / 120.0)))))
    return jnp.where(jnp.abs(z) < 0.1, poly, jnp.exp(z) - 1.0)


def _lru_pt(xc, rp, ip, lam, first):
    r = jax.nn.sigmoid(rp)
    i = jax.nn.sigmoid(ip)
    log_a = -LRU_C * r * jax.nn.softplus(-lam)
    a = jnp.exp(log_a)
    mult = jnp.sqrt(jnp.maximum(-_expm1(2.0 * log_a), 0.0))
    mult = jnp.where(first, 1.0, mult)
    return a, mult * i * xc


def _first_mask(shape, t):
    return jnp.logical_and(_rows(shape) == 0, t == 0)


def _lru_gates_fwd(xc, wa, ba, wx, bx, lam, name):
    T, W = xc.shape
    hd = W // N_HEADS
    tt = _tile(T, 512, 8)

    def body(xc_ref, wa_ref, ba_ref, wx_ref, bx_ref, lam_ref, a_ref, u_ref):
        t = pl.program_id(1)
        x = xc_ref[...]
        rp = _mxdot(x, wa_ref[...]) + ba_ref[...]
        ip = _mxdot(x, wx_ref[...]) + bx_ref[...]
        a, u = _lru_pt(x, rp, ip, lam_ref[...], _first_mask(x.shape, t))
        a_ref[...] = a
        u_ref[...] = u

    wspec = pl.BlockSpec((None, hd, hd), lambda h, t: (h, 0, 0))
    bspec = pl.BlockSpec((None, 1, hd), lambda h, t: (h, 0, 0))
    tspec = pl.BlockSpec((tt, hd), lambda h, t: (t, h))
    return pl.pallas_call(
        body, name=name, grid=(N_HEADS, T // tt),
        in_specs=[tspec, wspec, bspec, wspec, bspec, pl.BlockSpec((1, hd), lambda h, t: (0, h))],
        out_specs=[tspec, tspec], out_shape=[jax.ShapeDtypeStruct((T, W), f32)] * 2,
        compiler_params=_params(("parallel", "parallel")),
    )(xc, wa, ba, wx, bx, lam)


def _lru_scan_fwd(a, u, gate, gcol0, name):
    T, W = a.shape
    tt, cb = _tile(T, 256, 8), _tile(W, 512, LANE)
    g0 = gcol0 // cb
    assert gcol0 % cb == 0

    def body(a_ref, u_ref, g_ref, h_ref, y_ref, carry_ref):
        t = pl.program_id(1)

        @pl.when(t == 0)
        def _():
            carry_ref[...] = jnp.zeros_like(carry_ref)

        def step(i, h):
            base = pl.multiple_of(i * 8, 8)
            a8, u8 = a_ref[pl.ds(base, 8), :], u_ref[pl.ds(base, 8), :]
            rows = []
            for j in range(8):
                h = a8[j:j + 1, :] * h + u8[j:j + 1, :]
                rows.append(h)
            h_ref[pl.ds(base, 8), :] = jnp.concatenate(rows, axis=0)
            return h

        h_last = lax.fori_loop(0, tt // 8, step, carry_ref[0:1, :])
        carry_ref[...] = jnp.broadcast_to(h_last, carry_ref.shape)
        y_ref[...] = (h_ref[...] * jax.nn.gelu(g_ref[...])).astype(y_ref.dtype)

    tspec = pl.BlockSpec((tt, cb), lambda c, t: (t, c))
    return pl.pallas_call(
        body, name=name, grid=(W // cb, T // tt),
        in_specs=[tspec, tspec, pl.BlockSpec((tt, cb), lambda c, t: (t, g0 + c))],
        out_specs=[tspec, tspec],
        out_shape=[jax.ShapeDtypeStruct((T, W), f32), jax.ShapeDtypeStruct((T, W), _MXU)],
        scratch_shapes=[pltpu.VMEM((8, cb), f32)],
        compiler_params=_params(("parallel", "arbitrary")),
    )(a, u, gate)


def _lru_scan_bwd(dy, dcol0, gate, gcol0, h, a, name):
    T, W = a.shape
    tt, cb = _tile(T, 256, 8), _tile(W, 512, LANE)
    nt = T // tt
    d0, g0 = dcol0 // cb, gcol0 // cb
    assert dcol0 % cb == 0 and gcol0 % cb == 0

    def body(dy_ref, g_ref, h_ref, a_ref, lam_ref, dg_ref, carry_ref, dh_ref):
        t = pl.program_id(1)

        @pl.when(t == 0)
        def _():
            carry_ref[...] = jnp.zeros_like(carry_ref)

        _, vjp = jax.vjp(lambda hh, gg: hh * jax.nn.gelu(gg), h_ref[...], g_ref[...])
        dh, dg = vjp(dy_ref[...])
        dg_ref[...] = dg
        dh_ref[...] = dh

        def step(i, c):
            base = pl.multiple_of((tt // 8 - 1 - i) * 8, 8)
            a8, d8 = a_ref[pl.ds(base, 8), :], dh_ref[pl.ds(base, 8), :]
            rows = [None] * 8
            for j in range(7, -1, -1):
                lam = d8[j:j + 1, :] + c
                c = a8[j:j + 1, :] * lam
                rows[j] = lam
            lam_ref[pl.ds(base, 8), :] = jnp.concatenate(rows, axis=0)
            return c

        c_last = lax.fori_loop(0, tt // 8, step, carry_ref[0:1, :])
        carry_ref[...] = jnp.broadcast_to(c_last, carry_ref.shape)

    rev = lambda c, t: (nt - 1 - t, c)
    tspec = pl.BlockSpec((tt, cb), rev)
    return pl.pallas_call(
        body, name=name, grid=(W // cb, nt),
        in_specs=[pl.BlockSpec((tt, cb), lambda c, t: (nt - 1 - t, d0 + c)),
                  pl.BlockSpec((tt, cb), lambda c, t: (nt - 1 - t, g0 + c)), tspec, tspec],
        out_specs=[tspec, tspec], out_shape=[jax.ShapeDtypeStruct((T, W), f32)] * 2,
        scratch_shapes=[pltpu.VMEM((8, cb), f32), pltpu.VMEM((tt, cb), f32)],
        compiler_params=_params(("parallel", "arbitrary")),
    )(dy, gate, h, a)


def _lru_gates_bwd(lam_g, h, xc, wa, ba, wx, bx, lam, name):
    T, W = xc.shape
    hd = W // N_HEADS
    tt = _tile(T, 512, 8)

    def body(lg_ref, h_ref, hp_ref, xc_ref, wa_ref, ba_ref, wx_ref, bx_ref, lam_ref,
             dxc_ref, dwa_ref, dba_ref, dwx_ref, dbx_ref, dlam_ref):
        t = pl.program_id(1)

        @pl.when(t == 0)
        def _():
            for r in (dwa_ref, dba_ref, dwx_ref, dbx_ref, dlam_ref):
                r[...] = jnp.zeros_like(r)

        x = xc_ref[...]
        lg = lg_ref[...]
        h_prev = _shift_down(h_ref[...], jnp.where(t > 0, hp_ref[...], 0.0), 1)
        rp = _mxdot(x, wa_ref[...]) + ba_ref[...]
        ip = _mxdot(x, wx_ref[...]) + bx_ref[...]
        first = _first_mask(x.shape, t)
        _, vjp = jax.vjp(lambda xx, r_, i_, l_: _lru_pt(xx, r_, i_, l_, first), x, rp, ip, lam_ref[...])
        dx, drp, dip, dl = vjp((lg * h_prev, lg))
        dxc_ref[...] = dx + _mxdot(drp, wa_ref[...], _NT) + _mxdot(dip, wx_ref[...], _NT)
        dwa_ref[...] += _mxdot(x, drp, _TN)
        dwx_ref[...] += _mxdot(x, dip, _TN)
        dba_ref[...] += jnp.sum(drp, axis=0, keepdims=True)
        dbx_ref[...] += jnp.sum(dip, axis=0, keepdims=True)
        dlam_ref[...] += dl

    wspec = pl.BlockSpec((None, hd, hd), lambda h_, t: (h_, 0, 0))
    bspec = pl.BlockSpec((None, 1, hd), lambda h_, t: (h_, 0, 0))
    tspec = pl.BlockSpec((tt, hd), lambda h_, t: (t, h_))
    pspec = pl.BlockSpec((tt, hd), lambda h_, t: (jnp.maximum(t - 1, 0), h_))
    lspec = pl.BlockSpec((1, hd), lambda h_, t: (0, h_))
    return pl.pallas_call(
        body, name=name, grid=(N_HEADS, T // tt),
        in_specs=[tspec, tspec, pspec, tspec, wspec, bspec, wspec, bspec, lspec],
        out_specs=[tspec, wspec, bspec, wspec, bspec, lspec],
        out_shape=[jax.ShapeDtypeStruct((T, W), f32), jax.ShapeDtypeStruct((N_HEADS, hd, hd), f32),
                   jax.ShapeDtypeStruct((N_HEADS, 1, hd), f32), jax.ShapeDtypeStruct((N_HEADS, hd, hd), f32),
                   jax.ShapeDtypeStruct((N_HEADS, 1, hd), f32), jax.ShapeDtypeStruct((1, W), f32)],
        compiler_params=_params(("parallel", "arbitrary")),
    )(lam_g, h, h, xc, wa, ba, wx, bx, lam)


def _hgrn_pt(z, qp, lb):
    sig = jax.nn.sigmoid(z)
    fg = lb + (1.0 - lb) * sig
    logf = jnp.log(jnp.maximum(fg, F_FLOOR))
    k = (1.0 - lb) * (1.0 - sig)
    return logf, k, jax.nn.silu(qp)


def _hgrn_out(o, gp, ng):
    on = o * lax.rsqrt(jnp.mean(o * o, axis=-1, keepdims=True) + EPS)
    return on * ng * jax.nn.silu(gp)


def _cumsum_rows(x):
    n, row, sh = x.shape[0], _rows(x.shape), 1
    while sh < n:
        x = x + jnp.where(row >= sh, pltpu.roll(x, sh, 0), 0.0)
        sh *= 2
    return x


def _rev_cumsum_rows(x):
    n, row, sh = x.shape[0], _rows(x.shape), 1
    while sh < n:
        x = x + jnp.where(row < n - sh, pltpu.roll(x, n - sh, 0), 0.0)
        sh *= 2
    return x


def _hgrn_fwd(proj, qcol, fcol, vcol, gcol, lb, ng, name):
    T = proj.shape[0]
    W = lb.shape[1]
    hd = W // N_HEADS
    tt = _tile(T, 256, SUB)
    ns = tt // SUB
    q0, f0, v0, g0 = qcol // hd, fcol // hd, vcol // hd, gcol // hd

    def body(q_ref, f_ref, v_ref, g_ref, lb_ref, ng_ref, y_ref, o_ref, st_ref, s_ref):
        t = pl.program_id(1)

        @pl.when(t == 0)
        def _():
            s_ref[...] = jnp.zeros_like(s_ref)

        lbv = lb_ref[...]
        row = _rows((SUB, hd))

        def sub(j, carry):
            rs = pl.ds(pl.multiple_of(j * SUB, SUB), SUB)
            logf, k, qf = _hgrn_pt(f_ref[rs, :], q_ref[rs, :], lbv)
            v = v_ref[rs, :]
            b = _cumsum_rows(logf)
            b_last = b[SUB - 1:SUB, :]
            S = s_ref[...]
            st_ref[j] = S
            intra = jnp.zeros((SUB, hd), f32)
            for r in range(SUB):
                e = jnp.exp(jnp.minimum(b[r:r + 1, :] - b, 0.0))
                m = jnp.where(row <= r, qf[r:r + 1, :] * k * e, 0.0)
                p = jnp.sum(m, axis=1, keepdims=True)
                intra = jnp.where(row == r, jnp.sum(p * v, axis=0, keepdims=True), intra)
            o_ref[rs, :] = _mxdot(qf * jnp.exp(b), S, _NT) + intra
            s_ref[...] = S * jnp.exp(b_last) + _mxdot(v, k * jnp.exp(b_last - b), _TN)
            return carry

        lax.fori_loop(0, ns, sub, 0)
        y_ref[...] = _hgrn_out(o_ref[...], g_ref[...], ng_ref[...]).astype(y_ref.dtype)

    def slab(c0):
        return pl.BlockSpec((tt, hd), functools.partial(lambda h, t, c0: (t, c0 + h), c0=c0))

    hspec = pl.BlockSpec((tt, hd), lambda h, t: (t, h))
    cspec = pl.BlockSpec((1, hd), lambda h, t: (0, h))
    return pl.pallas_call(
        body, name=name, grid=(N_HEADS, T // tt),
        in_specs=[slab(q0), slab(f0), slab(v0), slab(g0), cspec, cspec],
        out_specs=[hspec, hspec, pl.BlockSpec((None, ns, hd, hd), lambda h, t: (h, t, 0, 0))],
        out_shape=[jax.ShapeDtypeStruct((T, W), _MXU), jax.ShapeDtypeStruct((T, W), f32),
                   jax.ShapeDtypeStruct((N_HEADS, T // SUB, hd, hd), f32)],
        scratch_shapes=[pltpu.VMEM((hd, hd), f32)],
        compiler_params=_params(("parallel", "arbitrary")),
    )(proj, proj, proj, proj, lb, ng)


def _hgrn_bwd(dy, dcol, proj, qcol, fcol, vcol, gcol, lb, ng, o, states, name):
    T = proj.shape[0]
    W = lb.shape[1]
    hd = W // N_HEADS
    tt = _tile(T, 256, SUB)
    ns, nt = tt // SUB, T // tt
    q0, f0, v0, g0, d0 = qcol // hd, fcol // hd, vcol // hd, gcol // hd, dcol // hd

    def body(dy_ref, q_ref, f_ref, v_ref, g_ref, lb_ref, ng_ref, o_ref, st_ref,
             dq_ref, df_ref, dv_ref, dg_ref, dlb_ref, dng_ref, ds_ref, do_ref):
        t = pl.program_id(1)

        @pl.when(t == 0)
        def _():
            ds_ref[...] = jnp.zeros_like(ds_ref)
            dlb_ref[...] = jnp.zeros_like(dlb_ref)
            dng_ref[...] = jnp.zeros_like(dng_ref)

        _, vjp_out = jax.vjp(_hgrn_out, o_ref[...], g_ref[...], ng_ref[...])
        do, dgp, dng = vjp_out(dy_ref[...])
        do_ref[...] = do
        dg_ref[...] = dgp
        dng_ref[...] += dng
        lbv = lb_ref[...]
        row = _rows((SUB, hd))

        def sub(jj, carry):
            j = ns - 1 - jj
            rs = pl.ds(pl.multiple_of(j * SUB, SUB), SUB)
            z, qp = f_ref[rs, :], q_ref[rs, :]
            (logf, k, qf), vjp_pt = jax.vjp(_hgrn_pt, z, qp, lbv)
            v = v_ref[rs, :]
            dO = do_ref[rs, :]
            b = _cumsum_rows(logf)
            b_last = b[SUB - 1:SUB, :]
            S = st_ref[j]
            dS = ds_ref[...]
            eb = jnp.exp(b)
            kd = jnp.exp(b_last - b)
            d = jnp.exp(b_last)
            qe, ke = qf * eb, k * kd
            dqe = _mxdot(dO, S, _NN)
            dke = _mxdot(v, dS, _NN)
            dv = _mxdot(ke, dS, _NT)
            dd = jnp.sum(dS * S, axis=0, keepdims=True)
            ds_ref[...] = dS * d + _mxdot(dO, qe, _TN)
            dq_i = jnp.zeros((SUB, hd), f32)
            dk_i = jnp.zeros((SUB, hd), f32)
            for r in range(SUB):
                em = jnp.where(row <= r, jnp.exp(jnp.minimum(b[r:r + 1, :] - b, 0.0)), 0.0)
                ke_r = k * em
                qr, dor = qf[r:r + 1, :], dO[r:r + 1, :]
                p = jnp.sum(qr * ke_r, axis=1, keepdims=True)
                dv = dv + p * dor
                dp = jnp.sum(dor * v, axis=1, keepdims=True)
                dq_i = jnp.where(row == r, jnp.sum(dp * ke_r, axis=0, keepdims=True), dq_i)
                dk_i = dk_i + dp * (qr * em)
            dqf = dqe * eb + dq_i
            dk = dke * kd + dk_i
            dke_ke = dke * ke
            db = dqe * qe - dke_ke + qf * dq_i - k * dk_i
            db_last = jnp.sum(dke_ke, axis=0, keepdims=True) + dd * d
            db = db + jnp.where(row == SUB - 1, db_last, 0.0)
            dz, dqp, dlb = vjp_pt((_rev_cumsum_rows(db), dk, dqf))
            dq_ref[rs, :] = dqp
            df_ref[rs, :] = dz
            dv_ref[rs, :] = dv
            dlb_ref[...] += dlb
            return carry

        lax.fori_loop(0, ns, sub, 0)

    def slab(c0):
        return pl.BlockSpec((tt, hd), functools.partial(lambda h, t, c0: (nt - 1 - t, c0 + h), c0=c0))

    hspec = pl.BlockSpec((tt, hd), lambda h, t: (nt - 1 - t, h))
    cspec = pl.BlockSpec((1, hd), lambda h, t: (0, h))
    return pl.pallas_call(
        body, name=name, grid=(N_HEADS, nt),
        in_specs=[slab(d0), slab(q0), slab(f0), slab(v0), slab(g0), cspec, cspec, hspec,
                  pl.BlockSpec((None, ns, hd, hd), lambda h, t: (h, nt - 1 - t, 0, 0))],
        out_specs=[hspec, hspec, hspec, hspec, cspec, cspec],
        out_shape=[jax.ShapeDtypeStruct((T, W), f32)] * 4 + [jax.ShapeDtypeStruct((1, W), f32)] * 2,
        scratch_shapes=[pltpu.VMEM((hd, hd), f32), pltpu.VMEM((tt, hd), f32)],
        compiler_params=_params(("parallel", "arbitrary")),
    )(dy, proj, proj, proj, proj, lb, ng, o, states)


def _lower_bounds(logits, name):
    def fn(lg):
        sm = jax.nn.softmax(lg, axis=0)
        run, rows_ = None, []
        for j in range(lg.shape[0]):
            run = sm[j:j + 1, :] if run is None else run + sm[j:j + 1, :]
            rows_.append(run - sm[0:1, :])
        return jnp.concatenate(rows_, axis=0)
    return fn


def _lb_fwd(logits, name):
    fn = _lower_bounds(logits, name)

    def body(l_ref, o_ref):
        o_ref[...] = fn(l_ref[...])

    return pl.pallas_call(body, name=name, out_shape=jax.ShapeDtypeStruct(logits.shape, f32))(logits)


def _lb_bwd(logits, dlb, name):
    fn = _lower_bounds(logits, name)

    def body(l_ref, d_ref, o_ref):
        _, vjp = jax.vjp(fn, l_ref[...])
        o_ref[...] = vjp(d_ref[...])[0]

    return pl.pallas_call(body, name=name, out_shape=jax.ShapeDtypeStruct(logits.shape, f32))(logits, dlb)


def _ln_silu(d, g, b):
    mu = jnp.mean(d, axis=-1, keepdims=True)
    xc = d - mu
    y = xc * lax.rsqrt(jnp.mean(xc * xc, axis=-1, keepdims=True) + EPS)
    return jax.nn.silu(y * g + b)


def _ln_fwd(d, g, b, name):
    T, W = d.shape
    tt = _tile(T, 256, 8)

    def body(d_ref, g_ref, b_ref, o_ref):
        o_ref[...] = _ln_silu(d_ref[...], g_ref[...], b_ref[...]).astype(o_ref.dtype)

    return pl.pallas_call(
        body, name=name, grid=(T // tt,),
        in_specs=[pl.BlockSpec((tt, W), lambda t: (t, 0))] + [pl.BlockSpec((1, W), lambda t: (0, 0))] * 2,
        out_specs=pl.BlockSpec((tt, W), lambda t: (t, 0)), out_shape=jax.ShapeDtypeStruct((T, W), _MXU),
        compiler_params=_params(("parallel",)),
    )(d, g, b)


def _ln_bwd(dy, dcol0, d, g, b, name):
    T, W = d.shape
    tt = _tile(T, 256, 8)
    c0 = dcol0 // W
    assert dcol0 % W == 0

    def body(dy_ref, d_ref, g_ref, b_ref, dd_ref, dg_ref, db_ref):
        @pl.when(pl.program_id(0) == 0)
        def _():
            dg_ref[...] = jnp.zeros_like(dg_ref)
            db_ref[...] = jnp.zeros_like(db_ref)

        _, vjp = jax.vjp(_ln_silu, d_ref[...], g_ref[...], b_ref[...])
        dd, dg, db = vjp(dy_ref[...])
        dd_ref[...] = dd
        dg_ref[...] += dg
        db_ref[...] += db

    cspec = pl.BlockSpec((1, W), lambda t: (0, 0))
    return pl.pallas_call(
        body, name=name, grid=(T // tt,),
        in_specs=[pl.BlockSpec((tt, W), lambda t: (t, c0)), pl.BlockSpec((tt, W), lambda t: (t, 0)), cspec, cspec],
        out_specs=[pl.BlockSpec((tt, W), lambda t: (t, 0)), cspec, cspec],
        out_shape=[jax.ShapeDtypeStruct((T, W), f32), jax.ShapeDtypeStruct((1, W), f32), jax.ShapeDtypeStruct((1, W), f32)],
        compiler_params=_params(("arbitrary",)),
    )(dy, d, g, b)


def _adamw(w, m, v, parts, name):
    R, C = w.shape
    tr, tc = _tile(R, 128, 8), _tile(C, 2048, LANE)
    npart = len(parts)

    def body(*refs):
        w_ref, m_ref, v_ref = refs[:3]
        g_ref, d_ref, mo_ref, vo_ref = refs[3 + npart:]
        g = refs[3][...].astype(f32)
        for p_ref in refs[4:3 + npart]:
            g = g + p_ref[...].astype(f32)
        mm = ADAM_B1 * m_ref[...] + (1.0 - ADAM_B1) * g
        vv = ADAM_B2 * v_ref[...] + (1.0 - ADAM_B2) * jnp.square(g)
        m_hat = mm / (1.0 - ADAM_B1 ** ADAM_STEP)
        v_hat = vv / (1.0 - ADAM_B2 ** ADAM_STEP)
        g_ref[...] = g
        d_ref[...] = -ADAM_LR * (m_hat / (jnp.sqrt(v_hat) + ADAM_EPS) + ADAM_WD * w_ref[...])
        mo_ref[...] = mm
        vo_ref[...] = vv

    spec = pl.BlockSpec((tr, tc), lambda i, j: (i, j))
    return pl.pallas_call(
        body, name=name, grid=(R // tr, C // tc), in_specs=[spec] * (3 + npart), out_specs=[spec] * 4,
        out_shape=[jax.ShapeDtypeStruct((R, C), f32)] * 4, compiler_params=_params(("parallel", "parallel")),
    )(w, m, v, *parts)


def _sum_slots(recv, name):
    S, R, C = recv.shape
    tr, tc = _tile(R, 256, 8), _tile(C, 2048, LANE)

    def body(r_ref, o_ref):
        acc = r_ref[0].astype(f32)
        for s in range(1, S):
            acc = acc + r_ref[s].astype(f32)
        o_ref[...] = acc

    return pl.pallas_call(
        body, name=name, grid=(R // tr, C // tc), in_specs=[pl.BlockSpec((S, tr, tc), lambda i, j: (0, i, j))],
        out_specs=pl.BlockSpec((tr, tc), lambda i, j: (i, j)), out_shape=jax.ShapeDtypeStruct((R, C), f32),
        compiler_params=_params(("parallel", "parallel")),
    )(recv)


_CHIP_FLIPS = ((1, 0), (0, 1), (1, 1))
_ANY = pl.BlockSpec(memory_space=pl.ANY)


def _me():
    return lax.axis_index("x"), lax.axis_index("y"), lax.axis_index("c")


def _allgather_chips(local, col, name):
    L, R, C = local.shape
    out_shape = (L, R, 4 * C) if col else (L, 4 * R, C)

    def body(loc_ref, out_ref, send_sems, recv_sems, local_sem):
        x, y, c = _me()

        def block(px, py):
            s = 2 * px + py
            if col:
                return out_ref.at[:, :, pl.ds(s * C, C)]
            return out_ref.at[:, pl.ds(s * R, R), :]

        mine = pltpu.make_async_copy(loc_ref, block(x, y), local_sem)
        mine.start()
        sends = []
        for j, (fx, fy) in enumerate(_CHIP_FLIPS):
            cp = pltpu.make_async_remote_copy(src_ref=loc_ref, dst_ref=block(x, y), send_sem=send_sems.at[j],
                                              recv_sem=recv_sems.at[j], device_id=(x ^ fx, y ^ fy, c), device_id_type=MESH)
            cp.start()
            sends.append(cp)
        for j, (fx, fy) in enumerate(_CHIP_FLIPS):
            pltpu.make_async_remote_copy(src_ref=loc_ref, dst_ref=block(x ^ fx, y ^ fy), send_sem=send_sems.at[j],
                                         recv_sem=recv_sems.at[j], device_id=(x ^ fx, y ^ fy, c),
                                         device_id_type=MESH).wait_recv()
        for cp in sends:
            cp.wait_send()
        mine.wait()

    return pl.pallas_call(
        body, name=name, in_specs=[_ANY], out_specs=_ANY, out_shape=jax.ShapeDtypeStruct(out_shape, local.dtype),
        scratch_shapes=[pltpu.SemaphoreType.DMA((3,)), pltpu.SemaphoreType.DMA((3,)), pltpu.SemaphoreType.DMA],
    )(local)


def _scatter_chips(full, col, name):
    L = full.shape[0]
    R, C = (full.shape[1], full.shape[2] // 4) if col else (full.shape[1] // 4, full.shape[2])

    def body(full_ref, out_ref, send_sems, recv_sems, local_sem):
        x, y, c = _me()

        def block(px, py):
            s = 2 * px + py
            if col:
                return full_ref.at[:, :, pl.ds(s * C, C)]
            return full_ref.at[:, pl.ds(s * R, R), :]

        me_slot = out_ref.at[2 * x + y]
        mine = pltpu.make_async_copy(block(x, y), me_slot, local_sem)
        mine.start()
        sends = []
        for j, (fx, fy) in enumerate(_CHIP_FLIPS):
            cp = pltpu.make_async_remote_copy(src_ref=block(x ^ fx, y ^ fy), dst_ref=me_slot, send_sem=send_sems.at[j],
                                              recv_sem=recv_sems.at[j], device_id=(x ^ fx, y ^ fy, c), device_id_type=MESH)
            cp.start()
            sends.append(cp)
        for j, (fx, fy) in enumerate(_CHIP_FLIPS):
            pltpu.make_async_remote_copy(src_ref=block(x, y), dst_ref=out_ref.at[2 * (x ^ fx) + (y ^ fy)],
                                         send_sem=send_sems.at[j], recv_sem=recv_sems.at[j],
                                         device_id=(x ^ fx, y ^ fy, c), device_id_type=MESH).wait_recv()
        for cp in sends:
            cp.wait_send()
        mine.wait()

    return pl.pallas_call(
        body, name=name, in_specs=[_ANY], out_specs=_ANY, out_shape=jax.ShapeDtypeStruct((4, L, R, C), full.dtype),
        scratch_shapes=[pltpu.SemaphoreType.DMA((3,)), pltpu.SemaphoreType.DMA((3,)), pltpu.SemaphoreType.DMA],
    )(full)


def _swap_cores(v, name):
    def body(v_ref, out_ref, send_sem, recv_sem):
        x, y, c = _me()
        cp = pltpu.make_async_remote_copy(src_ref=v_ref, dst_ref=out_ref, send_sem=send_sem, recv_sem=recv_sem,
                                          device_id=(x, y, 1 - c), device_id_type=MESH)
        cp.start()
        cp.wait()

    return pl.pallas_call(
        body, name=name, in_specs=[_ANY], out_specs=_ANY, out_shape=jax.ShapeDtypeStruct(v.shape, v.dtype),
        scratch_shapes=[pltpu.SemaphoreType.DMA, pltpu.SemaphoreType.DMA],
    )(v)


def _gather_all(v, name):
    def body(v_ref, out_ref, send_sems, recv_sems, local_sem):
        x, y, c = _me()
        me_slot = out_ref.at[4 * x + 2 * y + c]
        mine = pltpu.make_async_copy(v_ref, me_slot, local_sem)
        mine.start()
        sends = []
        for m in range(1, 8):
            peer = (x ^ (m >> 2), y ^ ((m >> 1) & 1), c ^ (m & 1))
            cp = pltpu.make_async_remote_copy(src_ref=v_ref, dst_ref=me_slot, send_sem=send_sems.at[m - 1],
                                              recv_sem=recv_sems.at[m - 1], device_id=peer, device_id_type=MESH)
            cp.start()
            sends.append(cp)
        for m in range(1, 8):
            peer = (x ^ (m >> 2), y ^ ((m >> 1) & 1), c ^ (m & 1))
            pltpu.make_async_remote_copy(src_ref=v_ref, dst_ref=out_ref.at[4 * peer[0] + 2 * peer[1] + peer[2]],
                                         send_sem=send_sems.at[m - 1], recv_sem=recv_sems.at[m - 1], device_id=peer,
                                         device_id_type=MESH).wait_recv()
        for cp in sends:
            cp.wait_send()
        mine.wait()

    return pl.pallas_call(
        body, name=name, in_specs=[_ANY], out_specs=_ANY, out_shape=jax.ShapeDtypeStruct((8,) + v.shape, v.dtype),
        scratch_shapes=[pltpu.SemaphoreType.DMA((7,)), pltpu.SemaphoreType.DMA((7,)), pltpu.SemaphoreType.DMA],
    )(v)


def _pack(arrs):
    flat = jnp.concatenate([a.reshape(-1).astype(f32) for a in arrs])
    n = flat.shape[0]
    rows = -(-n // (8 * LANE)) * 8
    return jnp.pad(flat, (0, rows * LANE - n)).reshape(rows, LANE)


def _unpack(buf, shapes):
    flat, outs, off = buf.reshape(-1), [], 0
    for s in shapes:
        n = 1
        for d_ in s:
            n *= d_
        outs.append(flat[off:off + n].reshape(s))
        off += n
    return outs


def _step(p):
    x0 = p['x'][0]
    tgt = p['loss_target'][0]
    T, D = x0.shape
    W = D // 2
    depth = p['ln_mix_g'].shape[0]
    chip = 2 * lax.axis_index("x") + lax.axis_index("y")

    full = {}
    for n in _BIG:
        full[n] = _allgather_chips(p[n].astype(_MXU), _BIG_COL[n], name=f"ag_{n}")
    small_rows = []
    cl = W // 4
    for n in _SMALL_SHARDED:
        small_rows.append(p[n].reshape(-1, cl))
    srows = [a.shape[0] for a in small_rows]
    spack = jnp.concatenate(small_rows, axis=0)
    spad = -(-spack.shape[0] // 8) * 8
    spack = jnp.pad(spack, ((0, spad - spack.shape[0]), (0, 0)))
    sfull = _allgather_chips(spack[None], False, name="ag_small")[0].reshape(4, spad, cl)
    small = {}
    off = 0
    for n, r in zip(_SMALL_SHARDED, srows):
        blk = sfull[:, off:off + r, :]
        lead = p[n].shape[:-1]
        q = p[n].shape[-1] // cl
        blk = blk.reshape((4,) + lead + (q, cl))
        blk = jnp.moveaxis(blk, 0, len(lead))
        small[n] = blk.reshape(lead + (4 * q * cl,))
        off += r

    lbs = _lb_fwd(p['hgrn_lb_logits'], name="lb_fwd")

    def row(a):
        return a.reshape(1, -1)

    saved = []
    x = x0
    for layer in range(depth):
        j = layer // 2
        s = {'x_in': x}
        h = _rmsnorm_fwd(x, row(p['ln_mix_g'][layer]), name="rms_fwd")
        s['h'] = h
        if layer % 2 == 0:
            proj = _matmul(h, full['ev_w_in'][j], 'nn', f32, "mm_ev_in", bias=row(p['ev_b_in'][j]))
            xc = _conv_fwd(proj, 0, small['lru_conv_w'][j], row(p['lru_conv_b'][j]), W, name="lru_conv_fwd")
            ba, bx = p['lru_ba'][j][:, None, :], p['lru_bx'][j][:, None, :]
            a, u = _lru_gates_fwd(xc, p['lru_wa'][j], ba, p['lru_wx'][j], bx, row(p['lru_lambda'][j]), name="lru_gates_fwd")
            hl, y_a = _lru_scan_fwd(a, u, proj, W, name="lru_scan_fwd")
            y_b, o, states = _hgrn_fwd(proj, 2 * W, 3 * W, 4 * W, 5 * W, row(lbs[j]), row(p['hgrn_norm_g'][j]),
                                       name="hgrn_fwd")
            s.update(proj=proj, xc=xc, a=a, hl=hl, o=o, states=states)
            ycat = jnp.concatenate([y_a, y_b], axis=1)
            w_out = full['ev_w_out'][j]
        else:
            proj = _matmul(h, full['od_w_in'][j], 'nn', f32, "mm_od_in", bias=row(small['od_b_in'][j]))
            pp = _pw(lambda a_, b_: a_ * b_, [(proj, W), (proj, 2 * W)], [f32], W, name="sc_mul")
            cp = _conv_fwd(pp, 0, small['sc_conv_w'][j], None, W, name="sc_conv_fwd")
            y_c = _pw(lambda a_, b_: a_ * b_, [(proj, 0), (cp, 0)], [_MXU], W, name="sc_out")
            glu = _pw(lambda a_, b_: a_ * jax.nn.sigmoid(b_), [(proj, 3 * W), (proj, 4 * W)], [f32], W, name="cf_glu")
            dcv = _conv_fwd(glu, 0, small['cf_conv_w'][j], row(small['cf_conv_b'][j]), W, name="cf_conv_fwd")
            y_d = _ln_fwd(dcv, row(small['cf_ln_g'][j]), row(small['cf_ln_b'][j]), name="cf_ln_fwd")
            s.update(proj=proj, pp=pp, cp=cp, glu=glu, dcv=dcv)
            ycat = jnp.concatenate([y_c, y_d], axis=1)
            w_out = full['od_w_out'][j]
        s['ycat'] = ycat
        x = _matmul(ycat, w_out, 'nn', f32, "mm_mix_out", add=x)
        s['x_mid'] = x
        h2 = _rmsnorm_fwd(x, row(p['ln_ffn_g'][layer]), name="rms_fwd")
        gate = _matmul(h2, full['ffn_w_gate'][layer], 'nn', f32, "mm_ffn_in")
        up = _matmul(h2, full['ffn_w_up'][layer], 'nn', f32, "mm_ffn_in")
        act = _pw(lambda g_, u_: jax.nn.silu(g_) * u_, [(gate, 0), (up, 0)], [_MXU], gate.shape[1], name="swiglu")
        x = _matmul(act, full['ffn_w_down'][layer], 'nn', f32, "mm_ffn_out", add=x, tk=1408)
        s.update(h2=h2, gate=gate, up=up, act=act)
        saved.append(s)

    loss_b, dx, dg_final = _final_loss(x, row(p['ln_final_g']), tgt, name="final_loss")

    gfull = {n: [None] * p[n].shape[0] for n in _BIG}
    gs = {n: [None] * p[n].shape[0] for n in _IN_NAMES[1:] if n not in _BIG and n != 'ln_final_g'}
    for layer in reversed(range(depth)):
        j = layer // 2
        s = saved[layer]
        F = s['gate'].shape[1]
        da = _matmul(dx, full['ffn_w_down'][layer], 'nt', f32, "mm_ffn_dact")
        gfull['ffn_w_down'][layer] = _matmul(s['act'], dx, 'tn', _WIRE, "mm_dw_down", tk=1024)

        def swiglu_bwd(da_, g_, u_):
            _, vjp = jax.vjp(lambda gg, uu: jax.nn.silu(gg) * uu, g_, u_)
            return vjp(da_)

        dgate, dup = _pw(swiglu_bwd, [(da, 0), (s['gate'], 0), (s['up'], 0)], [_MXU, _MXU], F, name="swiglu_bwd")
        gfull['ffn_w_gate'][layer] = _matmul(s['h2'], dgate, 'tn', _WIRE, "mm_dw_in", tk=1024)
        gfull['ffn_w_up'][layer] = _matmul(s['h2'], dup, 'tn', _WIRE, "mm_dw_in", tk=1024)
        dh2 = _matmul(dgate, full['ffn_w_gate'][layer], 'nt', f32, "mm_ffn_dh", tk=1408)
        dh2 = _matmul(dup, full['ffn_w_up'][layer], 'nt', f32, "mm_ffn_dh_acc", add=dh2, tk=1408)
        dx, gs['ln_ffn_g'][layer] = _rmsnorm_bwd(dh2, s['x_mid'], row(p['ln_ffn_g'][layer]), dx, name="rms_bwd")
        if layer % 2 == 0:
            w_out, w_in, n_out, n_in = full['ev_w_out'][j], full['ev_w_in'][j], 'ev_w_out', 'ev_w_in'
        else:
            w_out, w_in, n_out, n_in = full['od_w_out'][j], full['od_w_in'][j], 'od_w_out', 'od_w_in'
        dycat = _matmul(dx, w_out, 'nt', f32, "mm_mix_dy")
        gfull[n_out][j] = _matmul(s['ycat'], dx, 'tn', _WIRE, "mm_dw_out", tk=1024)
        proj = s['proj']
        if layer % 2 == 0:
            ba, bx = p['lru_ba'][j][:, None, :], p['lru_bx'][j][:, None, :]
            lam_g, dgate_a = _lru_scan_bwd(dycat, 0, proj, W, s['hl'], s['a'], name="lru_scan_bwd")
            dxc, dwa, dba, dwx, dbx, dlam = _lru_gates_bwd(lam_g, s['hl'], s['xc'], p['lru_wa'][j], ba, p['lru_wx'][j], bx,
                                                           row(p['lru_lambda'][j]), name="lru_gates_bwd")
            dxa = _conv_bwd_dx(dxc, small['lru_conv_w'][j], name="lru_conv_dx")
            dcw, dcb = _conv_bwd_dw(dxc, proj, 0, small['lru_conv_w'][j].shape[0], name="lru_conv_dw")
            dq, df, dv, dgp, dlb, dng = _hgrn_bwd(dycat, W, proj, 2 * W, 3 * W, 4 * W, 5 * W, row(lbs[j]),
                                                  row(p['hgrn_norm_g'][j]), s['o'], s['states'], name="hgrn_bwd")
            gs['lru_wa'][j], gs['lru_ba'][j], gs['lru_wx'][j], gs['lru_bx'][j] = dwa, dba[:, 0, :], dwx, dbx[:, 0, :]
            gs['lru_lambda'][j], gs['lru_conv_w'][j], gs['lru_conv_b'][j] = dlam[0], dcw, dcb[0]
            gs['hgrn_lb_logits'][j], gs['hgrn_norm_g'][j] = dlb[0], dng[0]
            dproj = jnp.concatenate([dxa, dgate_a, dq, df, dv, dgp], axis=1)
        else:
            def sc_bwd1(dy_, cp_, sb_):
                return dy_ * cp_, dy_ * sb_

            dsb, dcp = _pw(sc_bwd1, [(dycat, 0), (s['cp'], 0), (proj, 0)], [f32, f32], W, name="sc_bwd1")
            dpp = _conv_bwd_dx(dcp, small['sc_conv_w'][j], name="sc_conv_dx")
            dscw, _ = _conv_bwd_dw(dcp, s['pp'], 0, small['sc_conv_w'][j].shape[0], name="sc_conv_dw")

            def sc_bwd2(dp_, sc_, sv_):
                return dp_ * sv_, dp_ * sc_

            dsc, dsv = _pw(sc_bwd2, [(dpp, 0), (proj, W), (proj, 2 * W)], [f32, f32], W, name="sc_bwd2")
            dd, dlg, dlbeta = _ln_bwd(dycat, W, s['dcv'], row(small['cf_ln_g'][j]), row(small['cf_ln_b'][j]),
                                      name="cf_ln_bwd")
            dglu = _conv_bwd_dx(dd, small['cf_conv_w'][j], name="cf_conv_dx")
            dcfw, dcfb = _conv_bwd_dw(dd, s['glu'], 0, small['cf_conv_w'][j].shape[0], name="cf_conv_dw")

            def glu_bwd(dg_, cu_, cg_):
                _, vjp = jax.vjp(lambda a_, b_: a_ * jax.nn.sigmoid(b_), cu_, cg_)
                return vjp(dg_)

            dcu, dcg = _pw(glu_bwd, [(dglu, 0), (proj, 3 * W), (proj, 4 * W)], [f32, f32], W, name="cf_glu_bwd")
            gs['sc_conv_w'][j], gs['cf_conv_w'][j], gs['cf_conv_b'][j] = dscw, dcfw, dcfb[0]
            gs['cf_ln_g'][j], gs['cf_ln_b'][j] = dlg[0], dlbeta[0]
            dproj = jnp.concatenate([dsb, dsc, dsv, dcu, dcg], axis=1)
        bname = 'ev_b_in' if layer % 2 == 0 else 'od_b_in'
        gs[bname][j] = _colsum(dproj, name="colsum_" + n_in)[0]
        gfull[n_in][j] = _matmul(s['h'], dproj, 'tn', _WIRE, "mm_dw_" + n_in, tk=1024)
        dh = _matmul(dproj, w_in, 'nt', f32, "mm_dh_" + n_in)
        dx, gs['ln_mix_g'][layer] = _rmsnorm_bwd(dh, s['x_in'], row(p['ln_mix_g'][layer]), dx, name="rms_bwd")

    g_small = {n: jnp.stack(v_) for n, v_ in gs.items()}
    g_small['ln_mix_g'] = g_small['ln_mix_g'][:, 0, :]
    g_small['ln_ffn_g'] = g_small['ln_ffn_g'][:, 0, :]
    g_small['ln_final_g'] = dg_final[0]
    g_small['hgrn_lb_logits'] = _lb_bwd(p['hgrn_lb_logits'], g_small['hgrn_lb_logits'], name="lb_bwd")
    small_names = _SMALL_REPL + _SMALL_SHARDED
    pack = _pack([g_small[n] for n in small_names])
    tot = _sum_slots(_gather_all(pack, name="gather_small_grads"), name="sum_small_grads")
    g_tot = dict(zip(small_names, _unpack(tot, [g_small[n].shape for n in small_names])))
    for n in _SMALL_SHARDED:
        lead = p[n].shape[:-1]
        q = p[n].shape[-1] // cl
        blk = g_tot[n].reshape(lead + (4, q * cl))
        g_tot[n] = lax.dynamic_index_in_dim(blk, chip, axis=len(lead), keepdims=False)

    outs = {}
    shapes = [p[n].shape for n in small_names]
    res = _adamw(_pack([p[n] for n in small_names]), _pack([p['m_' + n] for n in small_names]),
                 _pack([p['v_' + n] for n in small_names]), [_pack([g_tot[n] for n in small_names])], name="adamw_small")
    for kind, buf in zip(('grad', 'delta', 'new_m', 'new_v'), res):
        for n, a in zip(small_names, _unpack(buf, shapes)):
            outs[kind + '_' + n] = a

    for n in _BIG:
        g = jnp.stack(gfull[n])
        recv = _scatter_chips(g, _BIG_COL[n], name=f"rs_{n}")
        L, R, C = recv.shape[1:]
        part = _sum_slots(recv.reshape(4, L * R, C), name=f"sum_{n}")
        other = _swap_cores(part, name=f"swap_{n}")
        res = _adamw(p[n].reshape(L * R, C), p['m_' + n].reshape(L * R, C), p['v_' + n].reshape(L * R, C), [part, other],
                     name=f"adamw_{n}")
        for kind, buf in zip(('grad', 'delta', 'new_m', 'new_v'), res):
            outs[kind + '_' + n] = buf.reshape(L, R, C)

    loss = lax.psum(loss_b[0, 0], ("x", "y", "c"))
    weights = _IN_NAMES[1:]
    return (loss, dx[None], *[outs['grad_' + n] for n in weights], *[outs['delta_' + n] for n in weights],
            *[outs['new_m_' + n] for n in weights], *[outs['new_v_' + n] for n in weights])


def kernel(x, ln_mix_g, ln_ffn_g, ln_final_g, ev_w_in, ev_b_in, lru_conv_w, lru_conv_b, lru_wa, lru_ba, lru_wx, lru_bx, lru_lambda, hgrn_lb_logits, hgrn_norm_g, ev_w_out, od_w_in, od_b_in, sc_conv_w, cf_conv_w, cf_conv_b, cf_ln_g, cf_ln_b, od_w_out, ffn_w_gate, ffn_w_up, ffn_w_down, loss_target, m_ln_mix_g, m_ln_ffn_g, m_ln_final_g, m_ev_w_in, m_ev_b_in, m_lru_conv_w, m_lru_conv_b, m_lru_wa, m_lru_ba, m_lru_wx, m_lru_bx, m_lru_lambda, m_hgrn_lb_logits, m_hgrn_norm_g, m_ev_w_out, m_od_w_in, m_od_b_in, m_sc_conv_w, m_cf_conv_w, m_cf_conv_b, m_cf_ln_g, m_cf_ln_b, m_od_w_out, m_ffn_w_gate, m_ffn_w_up, m_ffn_w_down, v_ln_mix_g, v_ln_ffn_g, v_ln_final_g, v_ev_w_in, v_ev_b_in, v_lru_conv_w, v_lru_conv_b, v_lru_wa, v_lru_ba, v_lru_wx, v_lru_bx, v_lru_lambda, v_hgrn_lb_logits, v_hgrn_norm_g, v_ev_w_out, v_od_w_in, v_od_b_in, v_sc_conv_w, v_cf_conv_w, v_cf_conv_b, v_cf_ln_g, v_cf_ln_b, v_od_w_out, v_ffn_w_gate, v_ffn_w_up, v_ffn_w_down):
    vals = locals()
    p = {n: vals[n] for n in _IN_NAMES + ['loss_target']}
    for n in _IN_NAMES[1:]:
        p['m_' + n] = vals['m_' + n]
        p['v_' + n] = vals['v_' + n]
    return _step(p)
```

```python
import functools

import jax
import jax.numpy as jnp
from jax import lax
from jax.experimental import pallas as pl
from jax.experimental.pallas import tpu as pltpu

f32 = jnp.float32
_MXU = jnp.bfloat16
_WIRE = jnp.bfloat16

N_HEADS = 8
LRU_C = 8.0
EPS = 1e-6
F_FLOOR = 1e-30
SUB = 16
ADAM_LR, ADAM_B1, ADAM_B2, ADAM_EPS, ADAM_WD, ADAM_STEP = 0.001, 0.9, 0.999, 1e-08, 0.01, 10
V7X_VMEM_LIMIT = 48 * 1024 * 1024
LANE = 128
MESH = pl.DeviceIdType.MESH

_IN_NAMES = ['x', 'ln_mix_g', 'ln_ffn_g', 'ln_final_g', 'ev_w_in', 'ev_b_in', 'lru_conv_w', 'lru_conv_b', 'lru_wa', 'lru_ba',
             'lru_wx', 'lru_bx', 'lru_lambda', 'hgrn_lb_logits', 'hgrn_norm_g', 'ev_w_out', 'od_w_in', 'od_b_in', 'sc_conv_w',
             'cf_conv_w', 'cf_conv_b', 'cf_ln_g', 'cf_ln_b', 'od_w_out', 'ffn_w_gate', 'ffn_w_up', 'ffn_w_down']
_BIG = ['ev_w_in', 'ev_w_out', 'od_w_in', 'od_w_out', 'ffn_w_gate', 'ffn_w_up', 'ffn_w_down']
_BIG_COL = {'ev_w_in': True, 'ev_w_out': False, 'od_w_in': True, 'od_w_out': False, 'ffn_w_gate': True, 'ffn_w_up': True,
            'ffn_w_down': False}
_SMALL_SHARDED = ['lru_conv_w', 'od_b_in', 'sc_conv_w', 'cf_conv_w', 'cf_conv_b', 'cf_ln_g', 'cf_ln_b']
_SMALL_REPL = ['ln_mix_g', 'ln_ffn_g', 'ln_final_g', 'ev_b_in', 'lru_conv_b', 'lru_wa', 'lru_ba', 'lru_wx', 'lru_bx',
               'lru_lambda', 'hgrn_lb_logits', 'hgrn_norm_g']


def _tile(n, pref, align):
    if n <= pref:
        return n
    t = (pref // align) * align
    while t >= align:
        if n % t == 0:
            return t
        t -= align
    return n


def _params(sem):
    return pltpu.CompilerParams(dimension_semantics=sem, vmem_limit_bytes=V7X_VMEM_LIMIT)


def _rows(shape):
    return lax.broadcasted_iota(jnp.int32, shape, 0)


def _mxdot(a, b, dims=(((1,), (0,)), ((), ()))):
    return lax.dot_general(a.astype(_MXU), b.astype(_MXU), dims, preferred_element_type=f32)


_NN = (((1,), (0,)), ((), ()))
_NT = (((1,), (1,)), ((), ()))
_TN = (((0,), (0,)), ((), ()))


def _matmul(a, b, mode, out_dtype, name, bias=None, add=None, tm=1024, tn=512, tk=2048):
    if mode == 'nn':
        (M, K), (K2, N) = a.shape, b.shape
    elif mode == 'nt':
        (M, K), (N, K2) = a.shape, b.shape
    else:
        (K, M), (K2, N) = a.shape, b.shape
    assert K == K2, (name, a.shape, b.shape)
    tm, tn, tk = _tile(M, tm, LANE), _tile(N, tn, LANE), _tile(K, tk, LANE)
    nk = K // tk
    dims = {'nn': _NN, 'nt': _NT, 'tn': _TN}[mode]
    has_bias, has_add = bias is not None, add is not None

    def body(*refs):
        a_ref, b_ref = refs[0], refs[1]
        pos = 2
        bias_ref = add_ref = None
        if has_bias:
            bias_ref = refs[pos]
            pos += 1
        if has_add:
            add_ref = refs[pos]
            pos += 1
        o_ref, acc_ref = refs[pos], refs[pos + 1]
        k = pl.program_id(2)

        @pl.when(k == 0)
        def _():
            acc_ref[...] = jnp.zeros_like(acc_ref)

        acc_ref[...] += _mxdot(a_ref[...], b_ref[...], dims)

        @pl.when(k == nk - 1)
        def _():
            r = acc_ref[...]
            if has_bias:
                r = r + bias_ref[...]
            if has_add:
                r = r + add_ref[...]
            o_ref[...] = r.astype(o_ref.dtype)

    if mode == 'tn':
        a_spec = pl.BlockSpec((tk, tm), lambda i, j, k: (k, i))
    else:
        a_spec = pl.BlockSpec((tm, tk), lambda i, j, k: (i, k))
    if mode == 'nt':
        b_spec = pl.BlockSpec((tn, tk), lambda i, j, k: (j, k))
    else:
        b_spec = pl.BlockSpec((tk, tn), lambda i, j, k: (k, j))
    in_specs, args = [a_spec, b_spec], [a, b]
    if has_bias:
        in_specs.append(pl.BlockSpec((1, tn), lambda i, j, k: (0, j)))
        args.append(bias)
    if has_add:
        in_specs.append(pl.BlockSpec((tm, tn), lambda i, j, k: (i, j)))
        args.append(add)
    return pl.pallas_call(
        body, name=name, grid=(M // tm, N // tn, nk), in_specs=in_specs,
        out_specs=pl.BlockSpec((tm, tn), lambda i, j, k: (i, j)),
        out_shape=jax.ShapeDtypeStruct((M, N), out_dtype),
        scratch_shapes=[pltpu.VMEM((tm, tn), f32)],
        compiler_params=_params(("parallel", "parallel", "arbitrary")),
    )(*args)


def _pw(fn, slabs, out_dtypes, width, name, consts=()):
    T = slabs[0][0].shape[0]
    tt, cb = _tile(T, 512, 8), _tile(width, 512, LANE)
    nin, ncst = len(slabs), len(consts)

    def body(*refs):
        res = fn(*[r[...] for r in refs[:nin + ncst]])
        if not isinstance(res, (tuple, list)):
            res = (res,)
        for r, o in zip(res, refs[nin + ncst:], strict=True):
            o[...] = r.astype(o.dtype)

    in_specs, args = [], []
    for arr, col0 in slabs:
        assert col0 % cb == 0
        in_specs.append(pl.BlockSpec((tt, cb), functools.partial(lambda t, c, c0: (t, c0 + c), c0=col0 // cb)))
        args.append(arr)
    for cst in consts:
        in_specs.append(pl.BlockSpec((1, cb), lambda t, c: (0, c)))
        args.append(cst)
    outs = pl.pallas_call(
        body, name=name, grid=(T // tt, width // cb), in_specs=in_specs,
        out_specs=[pl.BlockSpec((tt, cb), lambda t, c: (t, c)) for _ in out_dtypes],
        out_shape=[jax.ShapeDtypeStruct((T, width), d) for d in out_dtypes],
        compiler_params=_params(("parallel", "parallel")),
    )(*args)
    return outs[0] if len(out_dtypes) == 1 else outs


def _rmsnorm_fwd(x, g, name):
    T, D = x.shape
    tt = _tile(T, 256, 8)

    def body(x_ref, g_ref, o_ref):
        xv = x_ref[...]
        r = lax.rsqrt(jnp.mean(xv * xv, axis=-1, keepdims=True) + EPS)
        o_ref[...] = (xv * r * g_ref[...]).astype(o_ref.dtype)

    return pl.pallas_call(
        body, name=name, grid=(T // tt,),
        in_specs=[pl.BlockSpec((tt, D), lambda t: (t, 0)), pl.BlockSpec((1, D), lambda t: (0, 0))],
        out_specs=pl.BlockSpec((tt, D), lambda t: (t, 0)),
        out_shape=jax.ShapeDtypeStruct((T, D), _MXU), compiler_params=_params(("parallel",)),
    )(x, g)


def _rmsnorm_bwd(dh, x, g, dx_in, name):
    T, D = x.shape
    tt = _tile(T, 256, 8)

    def body(dh_ref, x_ref, g_ref, dxi_ref, dx_ref, dg_ref):
        t = pl.program_id(0)
        xv, d = x_ref[...], dh_ref[...]
        r = lax.rsqrt(jnp.mean(xv * xv, axis=-1, keepdims=True) + EPS)
        n = xv * r
        dn = d * g_ref[...]
        dx_ref[...] = dxi_ref[...] + r * (dn - n * jnp.mean(dn * n, axis=-1, keepdims=True))

        @pl.when(t == 0)
        def _():
            dg_ref[...] = jnp.zeros_like(dg_ref)

        dg_ref[...] += jnp.sum(d * n, axis=0, keepdims=True)

    return pl.pallas_call(
        body, name=name, grid=(T // tt,),
        in_specs=[pl.BlockSpec((tt, D), lambda t: (t, 0)), pl.BlockSpec((tt, D), lambda t: (t, 0)),
                  pl.BlockSpec((1, D), lambda t: (0, 0)), pl.BlockSpec((tt, D), lambda t: (t, 0))],
        out_specs=[pl.BlockSpec((tt, D), lambda t: (t, 0)), pl.BlockSpec((1, D), lambda t: (0, 0))],
        out_shape=[jax.ShapeDtypeStruct((T, D), f32), jax.ShapeDtypeStruct((1, D), f32)],
        compiler_params=_params(("arbitrary",)),
    )(dh, x, g, dx_in)


def _final_loss(x, g, tgt, name):
    T, D = x.shape
    tt = _tile(T, 256, 8)

    def body(x_ref, g_ref, t_ref, l_ref, dx_ref, dg_ref):
        t = pl.program_id(0)
        xv = x_ref[...]
        r = lax.rsqrt(jnp.mean(xv * xv, axis=-1, keepdims=True) + EPS)
        n = xv * r
        e = n * g_ref[...] - t_ref[...]
        part = 0.5 * jnp.sum(jnp.mean(e * e, axis=-1, keepdims=True), axis=0, keepdims=True)
        dy = e * (1.0 / D)
        dn = dy * g_ref[...]
        dx_ref[...] = r * (dn - n * jnp.mean(dn * n, axis=-1, keepdims=True))

        @pl.when(t == 0)
        def _():
            dg_ref[...] = jnp.zeros_like(dg_ref)
            l_ref[...] = jnp.zeros_like(l_ref)

        dg_ref[...] += jnp.sum(dy * n, axis=0, keepdims=True)
        l_ref[...] += jnp.broadcast_to(part, l_ref.shape)

    return pl.pallas_call(
        body, name=name, grid=(T // tt,),
        in_specs=[pl.BlockSpec((tt, D), lambda t: (t, 0)), pl.BlockSpec((1, D), lambda t: (0, 0)),
                  pl.BlockSpec((tt, D), lambda t: (t, 0))],
        out_specs=[pl.BlockSpec((1, LANE), lambda t: (0, 0)), pl.BlockSpec((tt, D), lambda t: (t, 0)),
                   pl.BlockSpec((1, D), lambda t: (0, 0))],
        out_shape=[jax.ShapeDtypeStruct((1, LANE), f32), jax.ShapeDtypeStruct((T, D), f32),
                   jax.ShapeDtypeStruct((1, D), f32)],
        compiler_params=_params(("arbitrary",)),
    )(x, g, tgt)


def _colsum(x, name):
    T, N = x.shape
    tt, cb = _tile(T, 512, 8), _tile(N, 512, LANE)

    def body(x_ref, o_ref):
        @pl.when(pl.program_id(1) == 0)
        def _():
            o_ref[...] = jnp.zeros_like(o_ref)

        o_ref[...] += jnp.sum(x_ref[...].astype(f32), axis=0, keepdims=True)

    return pl.pallas_call(
        body, name=name, grid=(N // cb, T // tt), in_specs=[pl.BlockSpec((tt, cb), lambda c, t: (t, c))],
        out_specs=pl.BlockSpec((1, cb), lambda c, t: (0, c)), out_shape=jax.ShapeDtypeStruct((1, N), f32),
        compiler_params=_params(("parallel", "arbitrary")),
    )(x)


def _shift_down(cur, prev, j):
    if j == 0:
        return cur
    n = cur.shape[0]
    return jnp.where(_rows(cur.shape) < j, pltpu.roll(prev, j, 0), pltpu.roll(cur, j, 0))


def _shift_up(cur, nxt, j):
    if j == 0:
        return cur
    n = cur.shape[0]
    return jnp.where(_rows(cur.shape) >= n - j, pltpu.roll(nxt, n - j, 0), pltpu.roll(cur, n - j, 0))


def _conv_tiles(T, C, K):
    tt, cb = _tile(T, 256, 8), _tile(C, 256, LANE)
    assert tt >= K, (tt, K)
    return tt, cb


def _conv_fwd(x, col0, w, b, C, name):
    T, K = x.shape[0], w.shape[0]
    tt, cb = _conv_tiles(T, C, K)
    c0 = col0 // cb
    assert col0 % cb == 0
    has_b = b is not None

    def body(*refs):
        cur_ref, prev_ref, w_ref = refs[:3]
        o_ref = refs[-1]
        t = pl.program_id(1)
        cur = cur_ref[...]
        prev = jnp.where(t > 0, prev_ref[...], 0.0)
        wv = w_ref[...]
        acc = jnp.zeros_like(cur)
        for k in range(K):
            acc = acc + wv[k:k + 1, :] * _shift_down(cur, prev, K - 1 - k)
        if has_b:
            acc = acc + refs[3][...]
        o_ref[...] = acc

    in_specs = [pl.BlockSpec((tt, cb), lambda c, t: (t, c0 + c)),
                pl.BlockSpec((tt, cb), lambda c, t: (jnp.maximum(t - 1, 0), c0 + c)),
                pl.BlockSpec((K, cb), lambda c, t: (0, c))]
    args = [x, x, w]
    if has_b:
        in_specs.append(pl.BlockSpec((1, cb), lambda c, t: (0, c)))
        args.append(b)
    return pl.pallas_call(
        body, name=name, grid=(C // cb, T // tt), in_specs=in_specs,
        out_specs=pl.BlockSpec((tt, cb), lambda c, t: (t, c)), out_shape=jax.ShapeDtypeStruct((T, C), f32),
        compiler_params=_params(("parallel", "parallel")),
    )(*args)


def _conv_bwd_dx(dy, w, name):
    T, C = dy.shape
    K = w.shape[0]
    tt, cb = _conv_tiles(T, C, K)
    nt = T // tt

    def body(cur_ref, nxt_ref, w_ref, o_ref):
        t = pl.program_id(1)
        cur = cur_ref[...]
        nxt = jnp.where(t < nt - 1, nxt_ref[...], 0.0)
        wv = w_ref[...]
        acc = jnp.zeros_like(cur)
        for k in range(K):
            acc = acc + wv[k:k + 1, :] * _shift_up(cur, nxt, K - 1 - k)
        o_ref[...] = acc

    return pl.pallas_call(
        body, name=name, grid=(C // cb, nt),
        in_specs=[pl.BlockSpec((tt, cb), lambda c, t: (t, c)),
                  pl.BlockSpec((tt, cb), lambda c, t: (jnp.minimum(t + 1, nt - 1), c)),
                  pl.BlockSpec((K, cb), lambda c, t: (0, c))],
        out_specs=pl.BlockSpec((tt, cb), lambda c, t: (t, c)), out_shape=jax.ShapeDtypeStruct((T, C), f32),
        compiler_params=_params(("parallel", "parallel")),
    )(dy, dy, w)


def _conv_bwd_dw(dy, x, col0, K, name):
    T, C = dy.shape
    tt, cb = _conv_tiles(T, C, K)
    c0 = col0 // cb
    assert col0 % cb == 0

    def body(dy_ref, cur_ref, prev_ref, dw_ref, db_ref):
        t = pl.program_id(1)

        @pl.when(t == 0)
        def _():
            dw_ref[...] = jnp.zeros_like(dw_ref)
            db_ref[...] = jnp.zeros_like(db_ref)

        d = dy_ref[...]
        cur = cur_ref[...]
        prev = jnp.where(t > 0, prev_ref[...], 0.0)
        for k in range(K):
            row = jnp.sum(d * _shift_down(cur, prev, K - 1 - k), axis=0, keepdims=True)
            dw_ref[pl.ds(k, 1), :] = dw_ref[pl.ds(k, 1), :] + row
        db_ref[...] += jnp.sum(d, axis=0, keepdims=True)

    return pl.pallas_call(
        body, name=name, grid=(C // cb, T // tt),
        in_specs=[pl.BlockSpec((tt, cb), lambda c, t: (t, c)),
                  pl.BlockSpec((tt, cb), lambda c, t: (t, c0 + c)),
                  pl.BlockSpec((tt, cb), lambda c, t: (jnp.maximum(t - 1, 0), c0 + c))],
        out_specs=[pl.BlockSpec((K, cb), lambda c, t: (0, c)), pl.BlockSpec((1, cb), lambda c, t: (0, c))],
        out_shape=[jax.ShapeDtypeStruct((K, C), f32), jax.ShapeDtypeStruct((1, C), f32)],
        compiler_params=_params(("parallel", "arbitrary")),
    )(dy, x, x)


def _expm1(z):
    poly = z * (1.0 + z * (0.5 + z * (1.0 / 6.0 + z * (1.0 / 24.0 + z * (1.0 / 120.0)))))
    return jnp.where(jnp.abs(z) < 0.1, poly, jnp.exp(z) - 1.0)


def _lru_pt(xc, rp, ip, lam, first):
    r = jax.nn.sigmoid(rp)
    i = jax.nn.sigmoid(ip)
    log_a = -LRU_C * r * jax.nn.softplus(-lam)
    a = jnp.exp(log_a)
    mult = jnp.sqrt(jnp.maximum(-_expm1(2.0 * log_a), 0.0))
    mult = jnp.where(first, 1.0, mult)
    return a, mult * i * xc


def _first_mask(shape, t):
    return jnp.logical_and(_rows(shape) == 0, t == 0)


def _lru_gates_fwd(xc, wa, ba, wx, bx, lam, name):
    T, W = xc.shape
    hd = W // N_HEADS
    tt = _tile(T, 512, 8)

    def body(xc_ref, wa_ref, ba_ref, wx_ref, bx_ref, lam_ref, a_ref, u_ref):
        t = pl.program_id(1)
        x = xc_ref[...]
        rp = _mxdot(x, wa_ref[...]) + ba_ref[...]
        ip = _mxdot(x, wx_ref[...]) + bx_ref[...]
        a, u = _lru_pt(x, rp, ip, lam_ref[...], _first_mask(x.shape, t))
        a_ref[...] = a
        u_ref[...] = u

    wspec = pl.BlockSpec((None, hd, hd), lambda h, t: (h, 0, 0))
    bspec = pl.BlockSpec((None, 1, hd), lambda h, t: (h, 0, 0))
    tspec = pl.BlockSpec((tt, hd), lambda h, t: (t, h))
    return pl.pallas_call(
        body, name=name, grid=(N_HEADS, T // tt),
        in_specs=[tspec, wspec, bspec, wspec, bspec, pl.BlockSpec((1, hd), lambda h, t: (0, h))],
        out_specs=[tspec, tspec], out_shape=[jax.ShapeDtypeStruct((T, W), f32)] * 2,
        compiler_params=_params(("parallel", "parallel")),
    )(xc, wa, ba, wx, bx, lam)


def _lru_scan_fwd(a, u, gate, gcol0, name):
    T, W = a.shape
    tt, cb = _tile(T, 256, 8), _tile(W, 512, LANE)
    g0 = gcol0 // cb
    assert gcol0 % cb == 0

    def body(a_ref, u_ref, g_ref, h_ref, y_ref, carry_ref):
        t = pl.program_id(1)

        @pl.when(t == 0)
        def _():
            carry_ref[...] = jnp.zeros_like(carry_ref)

        def step(i, h):
            base = pl.multiple_of(i * 8, 8)
            a8, u8 = a_ref[pl.ds(base, 8), :], u_ref[pl.ds(base, 8), :]
            rows = []
            for j in range(8):
                h = a8[j:j + 1, :] * h + u8[j:j + 1, :]
                rows.append(h)
            h_ref[pl.ds(base, 8), :] = jnp.concatenate(rows, axis=0)
            return h

        h_last = lax.fori_loop(0, tt // 8, step, carry_ref[0:1, :])
        carry_ref[...] = jnp.broadcast_to(h_last, carry_ref.shape)
        y_ref[...] = (h_ref[...] * jax.nn.gelu(g_ref[...])).astype(y_ref.dtype)

    tspec = pl.BlockSpec((tt, cb), lambda c, t: (t, c))
    return pl.pallas_call(
        body, name=name, grid=(W // cb, T // tt),
        in_specs=[tspec, tspec, pl.BlockSpec((tt, cb), lambda c, t: (t, g0 + c))],
        out_specs=[tspec, tspec],
        out_shape=[jax.ShapeDtypeStruct((T, W), f32), jax.ShapeDtypeStruct((T, W), _MXU)],
        scratch_shapes=[pltpu.VMEM((8, cb), f32)],
        compiler_params=_params(("parallel", "arbitrary")),
    )(a, u, gate)


def _lru_scan_bwd(dy, dcol0, gate, gcol0, h, a, name):
    T, W = a.shape
    tt, cb = _tile(T, 256, 8), _tile(W, 512, LANE)
    nt = T // tt
    d0, g0 = dcol0 // cb, gcol0 // cb
    assert dcol0 % cb == 0 and gcol0 % cb == 0

    def body(dy_ref, g_ref, h_ref, a_ref, lam_ref, dg_ref, carry_ref, dh_ref):
        t = pl.program_id(1)

        @pl.when(t == 0)
        def _():
            carry_ref[...] = jnp.zeros_like(carry_ref)

        _, vjp = jax.vjp(lambda hh, gg: hh * jax.nn.gelu(gg), h_ref[...], g_ref[...])
        dh, dg = vjp(dy_ref[...])
        dg_ref[...] = dg
        dh_ref[...] = dh

        def step(i, c):
            base = pl.multiple_of((tt // 8 - 1 - i) * 8, 8)
            a8, d8 = a_ref[pl.ds(base, 8), :], dh_ref[pl.ds(base, 8), :]
            rows = [None] * 8
            for j in range(7, -1, -1):
                lam = d8[j:j + 1, :] + c
                c = a8[j:j + 1, :] * lam
                rows[j] = lam
            lam_ref[pl.ds(base, 8), :] = jnp.concatenate(rows, axis=0)
            return c

        c_last = lax.fori_loop(0, tt // 8, step, carry_ref[0:1, :])
        carry_ref[...] = jnp.broadcast_to(c_last, carry_ref.shape)

    rev = lambda c, t: (nt - 1 - t, c)
    tspec = pl.BlockSpec((tt, cb), rev)
    return pl.pallas_call(
        body, name=name, grid=(W // cb, nt),
        in_specs=[pl.BlockSpec((tt, cb), lambda c, t: (nt - 1 - t, d0 + c)),
                  pl.BlockSpec((tt, cb), lambda c, t: (nt - 1 - t, g0 + c)), tspec, tspec],
        out_specs=[tspec, tspec], out_shape=[jax.ShapeDtypeStruct((T, W), f32)] * 2,
        scratch_shapes=[pltpu.VMEM((8, cb), f32), pltpu.VMEM((tt, cb), f32)],
        compiler_params=_params(("parallel", "arbitrary")),
    )(dy, gate, h, a)


def _lru_gates_bwd(lam_g, h, xc, wa, ba, wx, bx, lam, name):
    T, W = xc.shape
    hd = W // N_HEADS
    tt = _tile(T, 512, 8)

    def body(lg_ref, h_ref, hp_ref, xc_ref, wa_ref, ba_ref, wx_ref, bx_ref, lam_ref,
             dxc_ref, dwa_ref, dba_ref, dwx_ref, dbx_ref, dlam_ref):
        t = pl.program_id(1)

        @pl.when(t == 0)
        def _():
            for r in (dwa_ref, dba_ref, dwx_ref, dbx_ref, dlam_ref):
                r[...] = jnp.zeros_like(r)

        x = xc_ref[...]
        lg = lg_ref[...]
        h_prev = _shift_down(h_ref[...], jnp.where(t > 0, hp_ref[...], 0.0), 1)
        rp = _mxdot(x, wa_ref[...]) + ba_ref[...]
        ip = _mxdot(x, wx_ref[...]) + bx_ref[...]
        first = _first_mask(x.shape, t)
        _, vjp = jax.vjp(lambda xx, r_, i_, l_: _lru_pt(xx, r_, i_, l_, first), x, rp, ip, lam_ref[...])
        dx, drp, dip, dl = vjp((lg * h_prev, lg))
        dxc_ref[...] = dx + _mxdot(drp, wa_ref[...], _NT) + _mxdot(dip, wx_ref[...], _NT)
        dwa_ref[...] += _mxdot(x, drp, _TN)
        dwx_ref[...] += _mxdot(x, dip, _TN)
        dba_ref[...] += jnp.sum(drp, axis=0, keepdims=True)
        dbx_ref[...] += jnp.sum(dip, axis=0, keepdims=True)
        dlam_ref[...] += dl

    wspec = pl.BlockSpec((None, hd, hd), lambda h_, t: (h_, 0, 0))
    bspec = pl.BlockSpec((None, 1, hd), lambda h_, t: (h_, 0, 0))
    tspec = pl.BlockSpec((tt, hd), lambda h_, t: (t, h_))
    pspec = pl.BlockSpec((tt, hd), lambda h_, t: (jnp.maximum(t - 1, 0), h_))
    lspec = pl.BlockSpec((1, hd), lambda h_, t: (0, h_))
    return pl.pallas_call(
        body, name=name, grid=(N_HEADS, T // tt),
        in_specs=[tspec, tspec, pspec, tspec, wspec, bspec, wspec, bspec, lspec],
        out_specs=[tspec, wspec, bspec, wspec, bspec, lspec],
        out_shape=[jax.ShapeDtypeStruct((T, W), f32), jax.ShapeDtypeStruct((N_HEADS, hd, hd), f32),
                   jax.ShapeDtypeStruct((N_HEADS, 1, hd), f32), jax.ShapeDtypeStruct((N_HEADS, hd, hd), f32),
                   jax.ShapeDtypeStruct((N_HEADS, 1, hd), f32), jax.ShapeDtypeStruct((1, W), f32)],
        compiler_params=_params(("parallel", "arbitrary")),
    )(lam_g, h, h, xc, wa, ba, wx, bx, lam)


def _hgrn_pt(z, qp, lb):
    sig = jax.nn.sigmoid(z)
    fg = lb + (1.0 - lb) * sig
    logf = jnp.log(jnp.maximum(fg, F_FLOOR))
    k = (1.0 - lb) * (1.0 - sig)
    return logf, k, jax.nn.silu(qp)


def _hgrn_out(o, gp, ng):
    on = o * lax.rsqrt(jnp.mean(o * o, axis=-1, keepdims=True) + EPS)
    return on * ng * jax.nn.silu(gp)


def _cumsum_rows(x):
    n, row, sh = x.shape[0], _rows(x.shape), 1
    while sh < n:
        x = x + jnp.where(row >= sh, pltpu.roll(x, sh, 0), 0.0)
        sh *= 2
    return x


def _rev_cumsum_rows(x):
    n, row, sh = x.shape[0], _rows(x.shape), 1
    while sh < n:
        x = x + jnp.where(row < n - sh, pltpu.roll(x, n - sh, 0), 0.0)
        sh *= 2
    return x


def _hgrn_fwd(proj, qcol, fcol, vcol, gcol, lb, ng, name):
    T = proj.shape[0]
    W = lb.shape[1]
    hd = W // N_HEADS
    tt = _tile(T, 256, SUB)
    ns = tt // SUB
    q0, f0, v0, g0 = qcol // hd, fcol // hd, vcol // hd, gcol // hd

    def body(q_ref, f_ref, v_ref, g_ref, lb_ref, ng_ref, y_ref, o_ref, st_ref, s_ref):
        t = pl.program_id(1)

        @pl.when(t == 0)
        def _():
            s_ref[...] = jnp.zeros_like(s_ref)

        lbv = lb_ref[...]
        row = _rows((SUB, hd))

        def sub(j, carry):
            rs = pl.ds(pl.multiple_of(j * SUB, SUB), SUB)
            logf, k, qf = _hgrn_pt(f_ref[rs, :], q_ref[rs, :], lbv)
            v = v_ref[rs, :]
            b = _cumsum_rows(logf)
            b_last = b[SUB - 1:SUB, :]
            S = s_ref[...]
            st_ref[j] = S
            intra = jnp.zeros((SUB, hd), f32)
            for r in range(SUB):
                e = jnp.exp(jnp.minimum(b[r:r + 1, :] - b, 0.0))
                m = jnp.where(row <= r, qf[r:r + 1, :] * k * e, 0.0)
                p = jnp.sum(m, axis=1, keepdims=True)
                intra = jnp.where(row == r, jnp.sum(p * v, axis=0, keepdims=True), intra)
            o_ref[rs, :] = _mxdot(qf * jnp.exp(b), S, _NT) + intra
            s_ref[...] = S * jnp.exp(b_last) + _mxdot(v, k * jnp.exp(b_last - b), _TN)
            return carry

        lax.fori_loop(0, ns, sub, 0)
        y_ref[...] = _hgrn_out(o_ref[...], g_ref[...], ng_ref[...]).astype(y_ref.dtype)

    def slab(c0):
        return pl.BlockSpec((tt, hd), functools.partial(lambda h, t, c0: (t, c0 + h), c0=c0))

    hspec = pl.BlockSpec((tt, hd), lambda h, t: (t, h))
    cspec = pl.BlockSpec((1, hd), lambda h, t: (0, h))
    return pl.pallas_call(
        body, name=name, grid=(N_HEADS, T // tt),
        in_specs=[slab(q0), slab(f0), slab(v0), slab(g0), cspec, cspec],
        out_specs=[hspec, hspec, pl.BlockSpec((None, ns, hd, hd), lambda h, t: (h, t, 0, 0))],
        out_shape=[jax.ShapeDtypeStruct((T, W), _MXU), jax.ShapeDtypeStruct((T, W), f32),
                   jax.ShapeDtypeStruct((N_HEADS, T // SUB, hd, hd), f32)],
        scratch_shapes=[pltpu.VMEM((hd, hd), f32)],
        compiler_params=_params(("parallel", "arbitrary")),
    )(proj, proj, proj, proj, lb, ng)


def _hgrn_bwd(dy, dcol, proj, qcol, fcol, vcol, gcol, lb, ng, o, states, name):
    T = proj.shape[0]
    W = lb.shape[1]
    hd = W // N_HEADS
    tt = _tile(T, 256, SUB)
    ns, nt = tt // SUB, T // tt
    q0, f0, v0, g0, d0 = qcol // hd, fcol // hd, vcol // hd, gcol // hd, dcol // hd

    def body(dy_ref, q_ref, f_ref, v_ref, g_ref, lb_ref, ng_ref, o_ref, st_ref,
             dq_ref, df_ref, dv_ref, dg_ref, dlb_ref, dng_ref, ds_ref, do_ref):
        t = pl.program_id(1)

        @pl.when(t == 0)
        def _():
            ds_ref[...] = jnp.zeros_like(ds_ref)
            dlb_ref[...] = jnp.zeros_like(dlb_ref)
            dng_ref[...] = jnp.zeros_like(dng_ref)

        _, vjp_out = jax.vjp(_hgrn_out, o_ref[...], g_ref[...], ng_ref[...])
        do, dgp, dng = vjp_out(dy_ref[...])
        do_ref[...] = do
        dg_ref[...] = dgp
        dng_ref[...] += dng
        lbv = lb_ref[...]
        row = _rows((SUB, hd))

        def sub(jj, carry):
            j = ns - 1 - jj
            rs = pl.ds(pl.multiple_of(j * SUB, SUB), SUB)
            z, qp = f_ref[rs, :], q_ref[rs, :]
            (logf, k, qf), vjp_pt = jax.vjp(_hgrn_pt, z, qp, lbv)
            v = v_ref[rs, :]
            dO = do_ref[rs, :]
            b = _cumsum_rows(logf)
            b_last = b[SUB - 1:SUB, :]
            S = st_ref[j]
            dS = ds_ref[...]
            eb = jnp.exp(b)
            kd = jnp.exp(b_last - b)
            d = jnp.exp(b_last)
            qe, ke = qf * eb, k * kd
            dqe = _mxdot(dO, S, _NN)
            dke = _mxdot(v, dS, _NN)
            dv = _mxdot(ke, dS, _NT)
            dd = jnp.sum(dS * S, axis=0, keepdims=True)
            ds_ref[...] = dS * d + _mxdot(dO, qe, _TN)
            dq_i = jnp.zeros((SUB, hd), f32)
            dk_i = jnp.zeros((SUB, hd), f32)
            for r in range(SUB):
                em = jnp.where(row <= r, jnp.exp(jnp.minimum(b[r:r + 1, :] - b, 0.0)), 0.0)
                ke_r = k * em
                qr, dor = qf[r:r + 1, :], dO[r:r + 1, :]
                p = jnp.sum(qr * ke_r, axis=1, keepdims=True)
                dv = dv + p * dor
                dp = jnp.sum(dor * v, axis=1, keepdims=True)
                dq_i = jnp.where(row == r, jnp.sum(dp * ke_r, axis=0, keepdims=True), dq_i)
                dk_i = dk_i + dp * (qr * em)
            dqf = dqe * eb + dq_i
            dk = dke * kd + dk_i
            dke_ke = dke * ke
            db = dqe * qe - dke_ke + qf * dq_i - k * dk_i
            db_last = jnp.sum(dke_ke, axis=0, keepdims=True) + dd * d
            db = db + jnp.where(row == SUB - 1, db_last, 0.0)
            dz, dqp, dlb = vjp_pt((_rev_cumsum_rows(db), dk, dqf))
            dq_ref[rs, :] = dqp
            df_ref[rs, :] = dz
            dv_ref[rs, :] = dv
            dlb_ref[...] += dlb
            return carry

        lax.fori_loop(0, ns, sub, 0)

    def slab(c0):
        return pl.BlockSpec((tt, hd), functools.partial(lambda h, t, c0: (nt - 1 - t, c0 + h), c0=c0))

    hspec = pl.BlockSpec((tt, hd), lambda h, t: (nt - 1 - t, h))
    cspec = pl.BlockSpec((1, hd), lambda h, t: (0, h))
    return pl.pallas_call(
        body, name=name, grid=(N_HEADS, nt),
        in_specs=[slab(d0), slab(q0), slab(f0), slab(v0), slab(g0), cspec, cspec, hspec,
                  pl.BlockSpec((None, ns, hd, hd), lambda h, t: (h, nt - 1 - t, 0, 0))],
        out_specs=[hspec, hspec, hspec, hspec, cspec, cspec],
        out_shape=[jax.ShapeDtypeStruct((T, W), f32)] * 4 + [jax.ShapeDtypeStruct((1, W), f32)] * 2,
        scratch_shapes=[pltpu.VMEM((hd, hd), f32), pltpu.VMEM((tt, hd), f32)],
        compiler_params=_params(("parallel", "arbitrary")),
    )(dy, proj, proj, proj, proj, lb, ng, o, states)


def _lower_bounds(logits, name):
    def fn(lg):
        sm = jax.nn.softmax(lg, axis=0)
        run, rows_ = None, []
        for j in range(lg.shape[0]):
            run = sm[j:j + 1, :] if run is None else run + sm[j:j + 1, :]
            rows_.append(run - sm[0:1, :])
        return jnp.concatenate(rows_, axis=0)
    return fn


def _lb_fwd(logits, name):
    fn = _lower_bounds(logits, name)

    def body(l_ref, o_ref):
        o_ref[...] = fn(l_ref[...])

    return pl.pallas_call(body, name=name, out_shape=jax.ShapeDtypeStruct(logits.shape, f32))(logits)


def _lb_bwd(logits, dlb, name):
    fn = _lower_bounds(logits, name)

    def body(l_ref, d_ref, o_ref):
        _, vjp = jax.vjp(fn, l_ref[...])
        o_ref[...] = vjp(d_ref[...])[0]

    return pl.pallas_call(body, name=name, out_shape=jax.ShapeDtypeStruct(logits.shape, f32))(logits, dlb)


def _ln_silu(d, g, b):
    mu = jnp.mean(d, axis=-1, keepdims=True)
    xc = d - mu
    y = xc * lax.rsqrt(jnp.mean(xc * xc, axis=-1, keepdims=True) + EPS)
    return jax.nn.silu(y * g + b)


def _ln_fwd(d, g, b, name):
    T, W = d.shape
    tt = _tile(T, 256, 8)

    def body(d_ref, g_ref, b_ref, o_ref):
        o_ref[...] = _ln_silu(d_ref[...], g_ref[...], b_ref[...]).astype(o_ref.dtype)

    return pl.pallas_call(
        body, name=name, grid=(T // tt,),
        in_specs=[pl.BlockSpec((tt, W), lambda t: (t, 0))] + [pl.BlockSpec((1, W), lambda t: (0, 0))] * 2,
        out_specs=pl.BlockSpec((tt, W), lambda t: (t, 0)), out_shape=jax.ShapeDtypeStruct((T, W), _MXU),
        compiler_params=_params(("parallel",)),
    )(d, g, b)


def _ln_bwd(dy, dcol0, d, g, b, name):
    T, W = d.shape
    tt = _tile(T, 256, 8)
    c0 = dcol0 // W
    assert dcol0 % W == 0

    def body(dy_ref, d_ref, g_ref, b_ref, dd_ref, dg_ref, db_ref):
        @pl.when(pl.program_id(0) == 0)
        def _():
            dg_ref[...] = jnp.zeros_like(dg_ref)
            db_ref[...] = jnp.zeros_like(db_ref)

        _, vjp = jax.vjp(_ln_silu, d_ref[...], g_ref[...], b_ref[...])
        dd, dg, db = vjp(dy_ref[...])
        dd_ref[...] = dd
        dg_ref[...] += dg
        db_ref[...] += db

    cspec = pl.BlockSpec((1, W), lambda t: (0, 0))
    return pl.pallas_call(
        body, name=name, grid=(T // tt,),
        in_specs=[pl.BlockSpec((tt, W), lambda t: (t, c0)), pl.BlockSpec((tt, W), lambda t: (t, 0)), cspec, cspec],
        out_specs=[pl.BlockSpec((tt, W), lambda t: (t, 0)), cspec, cspec],
        out_shape=[jax.ShapeDtypeStruct((T, W), f32), jax.ShapeDtypeStruct((1, W), f32), jax.ShapeDtypeStruct((1, W), f32)],
        compiler_params=_params(("arbitrary",)),
    )(dy, d, g, b)


def _adamw(w, m, v, parts, name):
    R, C = w.shape
    tr, tc = _tile(R, 128, 8), _tile(C, 2048, LANE)
    npart = len(parts)

    def body(*refs):
        w_ref, m_ref, v_ref = refs[:3]
        g_ref, d_ref, mo_ref, vo_ref = refs[3 + npart:]
        g = refs[3][...].astype(f32)
        for p_ref in refs[4:3 + npart]:
            g = g + p_ref[...].astype(f32)
        mm = ADAM_B1 * m_ref[...] + (1.0 - ADAM_B1) * g
        vv = ADAM_B2 * v_ref[...] + (1.0 - ADAM_B2) * jnp.square(g)
        m_hat = mm / (1.0 - ADAM_B1 ** ADAM_STEP)
        v_hat = vv / (1.0 - ADAM_B2 ** ADAM_STEP)
        g_ref[...] = g
        d_ref[...] = -ADAM_LR * (m_hat / (jnp.sqrt(v_hat) + ADAM_EPS) + ADAM_WD * w_ref[...])
        mo_ref[...] = mm
        vo_ref[...] = vv

    spec = pl.BlockSpec((tr, tc), lambda i, j: (i, j))
    return pl.pallas_call(
        body, name=name, grid=(R // tr, C // tc), in_specs=[spec] * (3 + npart), out_specs=[spec] * 4,
        out_shape=[jax.ShapeDtypeStruct((R, C), f32)] * 4, compiler_params=_params(("parallel", "parallel")),
    )(w, m, v, *parts)


def _adamw_layer(w3, m3, v3, layer, parts, prev, name):
    L, R, C = w3.shape
    tr, tc = _tile(R, 128, 8), _tile(C, 2048, LANE)
    npart = len(parts)

    def body(*refs):
        w_ref, m_ref, v_ref = refs[:3]
        g_ref, d_ref, mo_ref, vo_ref = refs[3 + npart + 4:]
        g = refs[3][...].astype(f32)
        for p_ref in refs[4:3 + npart]:
            g = g + p_ref[...].astype(f32)
        mm = ADAM_B1 * m_ref[...] + (1.0 - ADAM_B1) * g
        vv = ADAM_B2 * v_ref[...] + (1.0 - ADAM_B2) * jnp.square(g)
        m_hat = mm / (1.0 - ADAM_B1 ** ADAM_STEP)
        v_hat = vv / (1.0 - ADAM_B2 ** ADAM_STEP)
        g_ref[...] = g
        d_ref[...] = -ADAM_LR * (m_hat / (jnp.sqrt(v_hat) + ADAM_EPS) + ADAM_WD * w_ref[...])
        mo_ref[...] = mm
        vo_ref[...] = vv

    spec3 = pl.BlockSpec((None, tr, tc), lambda i, j: (layer, i, j))
    spec2 = pl.BlockSpec((tr, tc), lambda i, j: (i, j))
    return pl.pallas_call(
        body, name=name, grid=(R // tr, C // tc), in_specs=[spec3] * 3 + [spec2] * npart + [_ANY] * 4,
        out_specs=[spec3] * 4, out_shape=[jax.ShapeDtypeStruct((L, R, C), f32)] * 4,
        input_output_aliases={3 + npart + k: k for k in range(4)},
        compiler_params=_params(("parallel", "parallel")),
    )(w3, m3, v3, *parts, *prev)


def _sum_own_recv(own, recv, name):
    S, R, C = recv.shape
    tr, tc = _tile(R, 256, 8), _tile(C, 2048, LANE)

    def body(o_ref, r_ref, out_ref):
        acc = o_ref[...].astype(f32)
        for s in range(S):
            acc = acc + r_ref[s].astype(f32)
        out_ref[...] = acc

    return pl.pallas_call(
        body, name=name, grid=(R // tr, C // tc),
        in_specs=[pl.BlockSpec((tr, tc), lambda i, j: (i, j)), pl.BlockSpec((S, tr, tc), lambda i, j: (0, i, j))],
        out_specs=pl.BlockSpec((tr, tc), lambda i, j: (i, j)), out_shape=jax.ShapeDtypeStruct((R, C), f32),
        compiler_params=_params(("parallel", "parallel")),
    )(own, recv)


def _sum_slots(recv, name):
    S, R, C = recv.shape
    tr, tc = _tile(R, 256, 8), _tile(C, 2048, LANE)

    def body(r_ref, o_ref):
        acc = r_ref[0].astype(f32)
        for s in range(1, S):
            acc = acc + r_ref[s].astype(f32)
        o_ref[...] = acc

    return pl.pallas_call(
        body, name=name, grid=(R // tr, C // tc), in_specs=[pl.BlockSpec((S, tr, tc), lambda i, j: (0, i, j))],
        out_specs=pl.BlockSpec((tr, tc), lambda i, j: (i, j)), out_shape=jax.ShapeDtypeStruct((R, C), f32),
        compiler_params=_params(("parallel", "parallel")),
    )(recv)


_CHIP_FLIPS = ((1, 0), (0, 1), (1, 1))
_ANY = pl.BlockSpec(memory_space=pl.ANY)


def _me():
    return lax.axis_index("x"), lax.axis_index("y"), lax.axis_index("c")


def _allgather_chips(local, col, name):
    L, R, C = local.shape
    out_shape = (L, R, 4 * C) if col else (L, 4 * R, C)

    def body(loc_ref, out_ref, send_sems, recv_sems, local_sem):
        x, y, c = _me()

        def block(px, py):
            s = 2 * px + py
            if col:
                return out_ref.at[:, :, pl.ds(s * C, C)]
            return out_ref.at[:, pl.ds(s * R, R), :]

        mine = pltpu.make_async_copy(loc_ref, block(x, y), local_sem)
        mine.start()
        sends = []
        for j, (fx, fy) in enumerate(_CHIP_FLIPS):
            cp = pltpu.make_async_remote_copy(src_ref=loc_ref, dst_ref=block(x, y), send_sem=send_sems.at[j],
                                              recv_sem=recv_sems.at[j], device_id=(x ^ fx, y ^ fy, c), device_id_type=MESH)
            cp.start()
            sends.append(cp)
        for j, (fx, fy) in enumerate(_CHIP_FLIPS):
            pltpu.make_async_remote_copy(src_ref=loc_ref, dst_ref=block(x ^ fx, y ^ fy), send_sem=send_sems.at[j],
                                         recv_sem=recv_sems.at[j], device_id=(x ^ fx, y ^ fy, c),
                                         device_id_type=MESH).wait_recv()
        for cp in sends:
            cp.wait_send()
        mine.wait()

    return pl.pallas_call(
        body, name=name, in_specs=[_ANY], out_specs=_ANY, out_shape=jax.ShapeDtypeStruct(out_shape, local.dtype),
        scratch_shapes=[pltpu.SemaphoreType.DMA((3,)), pltpu.SemaphoreType.DMA((3,)), pltpu.SemaphoreType.DMA],
    )(local)


def _scatter_chips(full, col, name):
    L = full.shape[0]
    R, C = (full.shape[1], full.shape[2] // 4) if col else (full.shape[1] // 4, full.shape[2])

    def body(full_ref, out_ref, send_sems, recv_sems, local_sem):
        x, y, c = _me()

        def block(px, py):
            s = 2 * px + py
            if col:
                return full_ref.at[:, :, pl.ds(s * C, C)]
            return full_ref.at[:, pl.ds(s * R, R), :]

        me_slot = out_ref.at[2 * x + y]
        mine = pltpu.make_async_copy(block(x, y), me_slot, local_sem)
        mine.start()
        sends = []
        for j, (fx, fy) in enumerate(_CHIP_FLIPS):
            cp = pltpu.make_async_remote_copy(src_ref=block(x ^ fx, y ^ fy), dst_ref=me_slot, send_sem=send_sems.at[j],
                                              recv_sem=recv_sems.at[j], device_id=(x ^ fx, y ^ fy, c), device_id_type=MESH)
            cp.start()
            sends.append(cp)
        for j, (fx, fy) in enumerate(_CHIP_FLIPS):
            pltpu.make_async_remote_copy(src_ref=block(x, y), dst_ref=out_ref.at[2 * (x ^ fx) + (y ^ fy)],
                                         send_sem=send_sems.at[j], recv_sem=recv_sems.at[j],
                                         device_id=(x ^ fx, y ^ fy, c), device_id_type=MESH).wait_recv()
        for cp in sends:
            cp.wait_send()
        mine.wait()

    return pl.pallas_call(
        body, name=name, in_specs=[_ANY], out_specs=_ANY, out_shape=jax.ShapeDtypeStruct((4, L, R, C), full.dtype),
        scratch_shapes=[pltpu.SemaphoreType.DMA((3,)), pltpu.SemaphoreType.DMA((3,)), pltpu.SemaphoreType.DMA],
    )(full)


def _swap_cores(v, name):
    def body(v_ref, out_ref, send_sem, recv_sem):
        x, y, c = _me()
        cp = pltpu.make_async_remote_copy(src_ref=v_ref, dst_ref=out_ref, send_sem=send_sem, recv_sem=recv_sem,
                                          device_id=(x, y, 1 - c), device_id_type=MESH)
        cp.start()
        cp.wait()

    return pl.pallas_call(
        body, name=name, in_specs=[_ANY], out_specs=_ANY, out_shape=jax.ShapeDtypeStruct(v.shape, v.dtype),
        scratch_shapes=[pltpu.SemaphoreType.DMA, pltpu.SemaphoreType.DMA],
    )(v)


def _gather_all(v, name):
    def body(v_ref, out_ref, send_sems, recv_sems, local_sem):
        x, y, c = _me()
        me_slot = out_ref.at[4 * x + 2 * y + c]
        mine = pltpu.make_async_copy(v_ref, me_slot, local_sem)
        mine.start()
        sends = []
        for m in range(1, 8):
            peer = (x ^ (m >> 2), y ^ ((m >> 1) & 1), c ^ (m & 1))
            cp = pltpu.make_async_remote_copy(src_ref=v_ref, dst_ref=me_slot, send_sem=send_sems.at[m - 1],
                                              recv_sem=recv_sems.at[m - 1], device_id=peer, device_id_type=MESH)
            cp.start()
            sends.append(cp)
        for m in range(1, 8):
            peer = (x ^ (m >> 2), y ^ ((m >> 1) & 1), c ^ (m & 1))
            pltpu.make_async_remote_copy(src_ref=v_ref, dst_ref=out_ref.at[4 * peer[0] + 2 * peer[1] + peer[2]],
                                         send_sem=send_sems.at[m - 1], recv_sem=recv_sems.at[m - 1], device_id=peer,
                                         device_id_type=MESH).wait_recv()
        for cp in sends:
            cp.wait_send()
        mine.wait()

    return pl.pallas_call(
        body, name=name, in_specs=[_ANY], out_specs=_ANY, out_shape=jax.ShapeDtypeStruct((8,) + v.shape, v.dtype),
        scratch_shapes=[pltpu.SemaphoreType.DMA((7,)), pltpu.SemaphoreType.DMA((7,)), pltpu.SemaphoreType.DMA],
    )(v)


def _swap_cores_multi(vs, name):
    n = len(vs)

    def body(*refs):
        v_refs, out_refs, send_sems, recv_sems = refs[:n], refs[n:2 * n], refs[2 * n], refs[2 * n + 1]
        x, y, c = _me()
        cps = [pltpu.make_async_remote_copy(src_ref=v_refs[i], dst_ref=out_refs[i], send_sem=send_sems.at[i],
                                            recv_sem=recv_sems.at[i], device_id=(x, y, 1 - c), device_id_type=MESH)
               for i in range(n)]
        for cp in cps:
            cp.start()
        for cp in cps:
            cp.wait()

    return pl.pallas_call(
        body, name=name, in_specs=[_ANY] * n, out_specs=[_ANY] * n,
        out_shape=[jax.ShapeDtypeStruct(v.shape, v.dtype) for v in vs],
        scratch_shapes=[pltpu.SemaphoreType.DMA((n,)), pltpu.SemaphoreType.DMA((n,))],
    )(*vs)


_HBM = pl.BlockSpec(memory_space=pltpu.HBM)
_SEM = pl.BlockSpec(memory_space=pltpu.SEMAPHORE)
_EFFECT = pltpu.SideEffectType.DATAFLOW_SIDE_EFFECTING


def _chip_block(ref, s, col, n):
    return ref.at[:, pl.ds(s * n, n)] if col else ref.at[pl.ds(s * n, n), :]


def _xchg_copies(kind, src_ref, land_ref, col, sems):
    x, y, c = _me()
    cps = []
    for j, (fx, fy) in enumerate(_CHIP_FLIPS):
        px, py = x ^ fx, y ^ fy
        if kind == 'gather':
            n = src_ref.shape[1] if col else src_ref.shape[0]
            src, dst = src_ref, _chip_block(land_ref, 2 * x + y, col, n)
        else:
            n = land_ref.shape[2] if col else land_ref.shape[1]
            src, dst = _chip_block(src_ref, 2 * px + py, col, n), land_ref.at[j]
        cps.append(pltpu.make_async_remote_copy(src_ref=src, dst_ref=dst, send_sem=sems[2 * j], recv_sem=sems[2 * j + 1],
                                                device_id=(px, py, c), device_id_type=MESH))
    return cps


def _xchg_start(kind, srcs, lands, cols, after, name):
    n = len(srcs)
    nsem = 6 * n

    def body(*refs):
        src_refs, land_refs = refs[:n], refs[n:2 * n]
        sems = refs[2 * n + 1:2 * n + 1 + nsem]
        token = refs[-1]
        for i in range(n):
            for cp in _xchg_copies(kind, src_refs[i], land_refs[i], cols[i], sems[6 * i:6 * i + 6]):
                cp.start()
        token[...] = jnp.zeros_like(token)

    hbm = lambda a: pltpu.HBM(a.shape, a.dtype)
    outs = pl.pallas_call(
        body, name=name,
        out_shape=tuple([pltpu.SemaphoreType.DMA(())] * nsem + [hbm(a) for a in srcs] + [hbm(a) for a in lands]
                        + [jax.ShapeDtypeStruct((8, LANE), f32)]),
        in_specs=tuple([_HBM] * (2 * n) + [_ANY]),
        out_specs=tuple([_SEM] * nsem + [_HBM] * (2 * n) + [pl.BlockSpec(memory_space=pltpu.VMEM)]),
        input_output_aliases={i: nsem + i for i in range(2 * n)},
        compiler_params=pltpu.CompilerParams(has_side_effects=_EFFECT),
    )(*[pltpu.with_memory_space_constraint(a, pltpu.HBM) for a in list(srcs) + list(lands)], after)
    return outs[:nsem], outs[nsem:nsem + n], outs[nsem + n:nsem + 2 * n], outs[-1]


def _xchg_wait(kind, started, cols, after, name):
    sems, srcs, lands, _ = started
    n = len(srcs)
    nsem = 6 * n

    def body(*refs):
        src_refs, land_refs = refs[:n], refs[n:2 * n]
        sem_refs = refs[2 * n:2 * n + nsem]
        for i in range(n):
            for cp in _xchg_copies(kind, src_refs[i], land_refs[i], cols[i], sem_refs[6 * i:6 * i + 6]):
                cp.wait_send()
                cp.wait_recv()

    hbm = lambda a: pltpu.HBM(a.shape, a.dtype)
    outs = pl.pallas_call(
        body, name=name, out_shape=tuple([hbm(a) for a in srcs] + [hbm(a) for a in lands]),
        in_specs=tuple([_HBM] * (2 * n) + [_SEM] * nsem + [_ANY]), out_specs=tuple([_HBM] * (2 * n)),
        input_output_aliases={i: i for i in range(2 * n)},
        compiler_params=pltpu.CompilerParams(has_side_effects=_EFFECT),
    )(*srcs, *lands, *sems, after)
    return outs[:n], outs[n:]


def _pack(arrs):
    flat = jnp.concatenate([a.reshape(-1).astype(f32) for a in arrs])
    n = flat.shape[0]
    rows = -(-n // (8 * LANE)) * 8
    return jnp.pad(flat, (0, rows * LANE - n)).reshape(rows, LANE)


def _unpack(buf, shapes):
    flat, outs, off = buf.reshape(-1), [], 0
    for s in shapes:
        n = 1
        for d_ in s:
            n *= d_
        outs.append(flat[off:off + n].reshape(s))
        off += n
    return outs


def _step(p):
    x0 = p['x'][0]
    tgt = p['loss_target'][0]
    T, D = x0.shape
    W = D // 2
    depth = p['ln_mix_g'].shape[0]
    chip = 2 * lax.axis_index("x") + lax.axis_index("y")

    def group_names(layer, grp):
        if grp == 'ffn':
            return [('ffn_w_gate', layer), ('ffn_w_up', layer), ('ffn_w_down', layer)]
        pre = 'ev' if layer % 2 == 0 else 'od'
        return [(pre + '_w_in', layer // 2), (pre + '_w_out', layer // 2)]

    def gather_start(layer, grp, after):
        srcs, lands, cols = [], [], []
        for n, l in group_names(layer, grp):
            loc = p[n][l].astype(_MXU)
            R, C = loc.shape
            col = _BIG_COL[n]
            land = lax.empty((R, 4 * C) if col else (4 * R, C), _MXU)
            zero = jnp.zeros((), jnp.int32)
            land = lax.dynamic_update_slice(land, loc, (zero, chip * C) if col else (chip * R, zero))
            srcs.append(loc)
            lands.append(land)
            cols.append(col)
        return _xchg_start('gather', srcs, lands, cols, after, name=f"ag_start_{grp}{layer}"), cols

    def gather_wait(started, layer, grp, after):
        st, cols = started
        _, lands = _xchg_wait('gather', st, cols, after, name=f"ag_wait_{grp}{layer}")
        return dict(zip([n for n, _ in group_names(layer, grp)], lands))

    groups = [(layer, grp) for layer in range(depth) for grp in ('mix', 'ffn')]
    pending = {groups[0]: gather_start(*groups[0], x0)}
    pending[groups[1]] = gather_start(*groups[1], pending[groups[0]][0][3])
    loose = [pending[groups[1]][0][3]]

    def take_weights(gi, after):
        wts = gather_wait(pending.pop(groups[gi]), *groups[gi], after)
        if gi + 2 < len(groups):
            pending[groups[gi + 2]] = gather_start(*groups[gi + 2], next(iter(wts.values())))
            loose.append(pending[groups[gi + 2]][0][3])
        tok = None
        while loose:
            t = loose.pop()
            tok = t if tok is None else tok + t
        return wts, tok

    small_rows = []
    cl = W // 4
    for n in _SMALL_SHARDED:
        small_rows.append(p[n].reshape(-1, cl))
    srows = [a.shape[0] for a in small_rows]
    spack = jnp.concatenate(small_rows, axis=0)
    spad = -(-spack.shape[0] // 8) * 8
    spack = jnp.pad(spack, ((0, spad - spack.shape[0]), (0, 0)))
    sfull = _allgather_chips(spack[None], False, name="ag_small")[0].reshape(4, spad, cl)
    small = {}
    off = 0
    for n, r in zip(_SMALL_SHARDED, srows):
        blk = sfull[:, off:off + r, :]
        lead = p[n].shape[:-1]
        q = p[n].shape[-1] // cl
        blk = blk.reshape((4,) + lead + (q, cl))
        blk = jnp.moveaxis(blk, 0, len(lead))
        small[n] = blk.reshape(lead + (4 * q * cl,))
        off += r

    lbs = _lb_fwd(p['hgrn_lb_logits'], name="lb_fwd")

    def row(a, tok=None):
        a = a.reshape(1, -1)
        return a if tok is None else a + tok[0:1, 0:1]

    saved = []
    full = {}
    x = x0
    for layer in range(depth):
        j = layer // 2
        s = {'x_in': x}
        wts, tok = take_weights(2 * layer, x)
        full[layer] = wts
        h = _rmsnorm_fwd(x, row(p['ln_mix_g'][layer], tok), name="rms_fwd")
        s['h'] = h
        if layer % 2 == 0:
            proj = _matmul(h, wts['ev_w_in'], 'nn', f32, "mm_ev_in", bias=row(p['ev_b_in'][j]))
            xc = _conv_fwd(proj, 0, small['lru_conv_w'][j], row(p['lru_conv_b'][j]), W, name="lru_conv_fwd")
            ba, bx = p['lru_ba'][j][:, None, :], p['lru_bx'][j][:, None, :]
            a, u = _lru_gates_fwd(xc, p['lru_wa'][j], ba, p['lru_wx'][j], bx, row(p['lru_lambda'][j]), name="lru_gates_fwd")
            hl, y_a = _lru_scan_fwd(a, u, proj, W, name="lru_scan_fwd")
            y_b, o, states = _hgrn_fwd(proj, 2 * W, 3 * W, 4 * W, 5 * W, row(lbs[j]), row(p['hgrn_norm_g'][j]),
                                       name="hgrn_fwd")
            s.update(proj=proj, xc=xc, a=a, hl=hl, o=o, states=states)
            ycat = jnp.concatenate([y_a, y_b], axis=1)
            w_out = wts['ev_w_out']
        else:
            proj = _matmul(h, wts['od_w_in'], 'nn', f32, "mm_od_in", bias=row(small['od_b_in'][j]))
            pp = _pw(lambda a_, b_: a_ * b_, [(proj, W), (proj, 2 * W)], [f32], W, name="sc_mul")
            cp = _conv_fwd(pp, 0, small['sc_conv_w'][j], None, W, name="sc_conv_fwd")
            y_c = _pw(lambda a_, b_: a_ * b_, [(proj, 0), (cp, 0)], [_MXU], W, name="sc_out")
            glu = _pw(lambda a_, b_: a_ * jax.nn.sigmoid(b_), [(proj, 3 * W), (proj, 4 * W)], [f32], W, name="cf_glu")
            dcv = _conv_fwd(glu, 0, small['cf_conv_w'][j], row(small['cf_conv_b'][j]), W, name="cf_conv_fwd")
            y_d = _ln_fwd(dcv, row(small['cf_ln_g'][j]), row(small['cf_ln_b'][j]), name="cf_ln_fwd")
            s.update(proj=proj, pp=pp, cp=cp, glu=glu, dcv=dcv)
            ycat = jnp.concatenate([y_c, y_d], axis=1)
            w_out = wts['od_w_out']
        s['ycat'] = ycat
        x = _matmul(ycat, w_out, 'nn', f32, "mm_mix_out", add=x)
        s['x_mid'] = x
        wts, tok = take_weights(2 * layer + 1, x)
        full[layer].update(wts)
        h2 = _rmsnorm_fwd(x, row(p['ln_ffn_g'][layer], tok), name="rms_fwd")
        gate = _matmul(h2, wts['ffn_w_gate'], 'nn', f32, "mm_ffn_in")
        up = _matmul(h2, wts['ffn_w_up'], 'nn', f32, "mm_ffn_in")
        act = _pw(lambda g_, u_: jax.nn.silu(g_) * u_, [(gate, 0), (up, 0)], [_MXU], gate.shape[1], name="swiglu")
        x = _matmul(act, wts['ffn_w_down'], 'nn', f32, "mm_ffn_out", add=x, tk=1408)
        s.update(h2=h2, gate=gate, up=up, act=act)
        saved.append(s)

    loss_b, dx, dg_final = _final_loss(x, row(p['ln_final_g']), tgt, name="final_loss")

    scat = {}
    tok = None

    def scatter_start(layer, grp, grads):
        srcs, lands, cols = [], [], []
        for (n, l), g in zip(group_names(layer, grp), grads, strict=True):
            R, C = p[n].shape[1:]
            srcs.append(g)
            lands.append(lax.empty((3, R, C), _WIRE))
            cols.append(_BIG_COL[n])
        st = _xchg_start('scatter', srcs, lands, cols, grads[0], name=f"rs_start_{grp}{layer}")
        scat[(layer, grp)] = (st, cols)
        return st[3]

    gs = {n: [None] * p[n].shape[0] for n in _IN_NAMES[1:] if n not in _BIG and n != 'ln_final_g'}
    for layer in reversed(range(depth)):
        j = layer // 2
        s = saved[layer]
        F = s['gate'].shape[1]
        wts = full[layer]
        da = _matmul(dx, wts['ffn_w_down'], 'nt', f32, "mm_ffn_dact")
        g_down = _matmul(s['act'], dx, 'tn', _WIRE, "mm_dw_down", tm=1408, tn=1024, tk=512)

        def swiglu_bwd(da_, g_, u_):
            _, vjp = jax.vjp(lambda gg, uu: jax.nn.silu(gg) * uu, g_, u_)
            return vjp(da_)

        dgate, dup = _pw(swiglu_bwd, [(da, 0), (s['gate'], 0), (s['up'], 0)], [_MXU, _MXU], F, name="swiglu_bwd")
        g_gate = _matmul(s['h2'], dgate, 'tn', _WIRE, "mm_dw_in", tm=1024, tn=1408, tk=512)
        g_up = _matmul(s['h2'], dup, 'tn', _WIRE, "mm_dw_in", tm=1024, tn=1408, tk=512)
        tok = scatter_start(layer, 'ffn', [g_gate, g_up, g_down])
        dh2 = _matmul(dgate, wts['ffn_w_gate'], 'nt', f32, "mm_ffn_dh", tk=1408)
        dh2 = _matmul(dup, wts['ffn_w_up'], 'nt', f32, "mm_ffn_dh_acc", add=dh2, tk=1408)
        dx, gs['ln_ffn_g'][layer] = _rmsnorm_bwd(dh2, s['x_mid'], row(p['ln_ffn_g'][layer], tok), dx, name="rms_bwd")
        if layer % 2 == 0:
            w_out, w_in, n_out, n_in = wts['ev_w_out'], wts['ev_w_in'], 'ev_w_out', 'ev_w_in'
        else:
            w_out, w_in, n_out, n_in = wts['od_w_out'], wts['od_w_in'], 'od_w_out', 'od_w_in'
        dycat = _matmul(dx, w_out, 'nt', f32, "mm_mix_dy")
        g_out = _matmul(s['ycat'], dx, 'tn', _WIRE, "mm_dw_out", tm=1024, tn=1024, tk=512)
        proj = s['proj']
        if layer % 2 == 0:
            ba, bx = p['lru_ba'][j][:, None, :], p['lru_bx'][j][:, None, :]
            lam_g, dgate_a = _lru_scan_bwd(dycat, 0, proj, W, s['hl'], s['a'], name="lru_scan_bwd")
            dxc, dwa, dba, dwx, dbx, dlam = _lru_gates_bwd(lam_g, s['hl'], s['xc'], p['lru_wa'][j], ba, p['lru_wx'][j], bx,
                                                           row(p['lru_lambda'][j]), name="lru_gates_bwd")
            dxa = _conv_bwd_dx(dxc, small['lru_conv_w'][j], name="lru_conv_dx")
            dcw, dcb = _conv_bwd_dw(dxc, proj, 0, small['lru_conv_w'][j].shape[0], name="lru_conv_dw")
            dq, df, dv, dgp, dlb, dng = _hgrn_bwd(dycat, W, proj, 2 * W, 3 * W, 4 * W, 5 * W, row(lbs[j]),
                                                  row(p['hgrn_norm_g'][j]), s['o'], s['states'], name="hgrn_bwd")
            gs['lru_wa'][j], gs['lru_ba'][j], gs['lru_wx'][j], gs['lru_bx'][j] = dwa, dba[:, 0, :], dwx, dbx[:, 0, :]
            gs['lru_lambda'][j], gs['lru_conv_w'][j], gs['lru_conv_b'][j] = dlam[0], dcw, dcb[0]
            gs['hgrn_lb_logits'][j], gs['hgrn_norm_g'][j] = dlb[0], dng[0]
            dproj = jnp.concatenate([dxa, dgate_a, dq, df, dv, dgp], axis=1)
        else:
            def sc_bwd1(dy_, cp_, sb_):
                return dy_ * cp_, dy_ * sb_

            dsb, dcp = _pw(sc_bwd1, [(dycat, 0), (s['cp'], 0), (proj, 0)], [f32, f32], W, name="sc_bwd1")
            dpp = _conv_bwd_dx(dcp, small['sc_conv_w'][j], name="sc_conv_dx")
            dscw, _ = _conv_bwd_dw(dcp, s['pp'], 0, small['sc_conv_w'][j].shape[0], name="sc_conv_dw")

            def sc_bwd2(dp_, sc_, sv_):
                return dp_ * sv_, dp_ * sc_

            dsc, dsv = _pw(sc_bwd2, [(dpp, 0), (proj, W), (proj, 2 * W)], [f32, f32], W, name="sc_bwd2")
            dd, dlg, dlbeta = _ln_bwd(dycat, W, s['dcv'], row(small['cf_ln_g'][j]), row(small['cf_ln_b'][j]),
                                      name="cf_ln_bwd")
            dglu = _conv_bwd_dx(dd, small['cf_conv_w'][j], name="cf_conv_dx")
            dcfw, dcfb = _conv_bwd_dw(dd, s['glu'], 0, small['cf_conv_w'][j].shape[0], name="cf_conv_dw")

            def glu_bwd(dg_, cu_, cg_):
                _, vjp = jax.vjp(lambda a_, b_: a_ * jax.nn.sigmoid(b_), cu_, cg_)
                return vjp(dg_)

            dcu, dcg = _pw(glu_bwd, [(dglu, 0), (proj, 3 * W), (proj, 4 * W)], [f32, f32], W, name="cf_glu_bwd")
            gs['sc_conv_w'][j], gs['cf_conv_w'][j], gs['cf_conv_b'][j] = dscw, dcfw, dcfb[0]
            gs['cf_ln_g'][j], gs['cf_ln_b'][j] = dlg[0], dlbeta[0]
            dproj = jnp.concatenate([dsb, dsc, dsv, dcu, dcg], axis=1)
        bname = 'ev_b_in' if layer % 2 == 0 else 'od_b_in'
        gs[bname][j] = _colsum(dproj, name="colsum_" + n_in)[0]
        g_in = _matmul(s['h'], dproj, 'tn', _WIRE, "mm_dw_" + n_in, tm=1024, tn=1536, tk=512)
        tok = scatter_start(layer, 'mix', [g_in, g_out])
        dh = _matmul(dproj, w_in, 'nt', f32, "mm_dh_" + n_in)
        dx, gs['ln_mix_g'][layer] = _rmsnorm_bwd(dh, s['x_in'], row(p['ln_mix_g'][layer], tok), dx, name="rms_bwd")

    g_small = {n: jnp.stack(v_) for n, v_ in gs.items()}
    g_small['ln_mix_g'] = g_small['ln_mix_g'][:, 0, :]
    g_small['ln_ffn_g'] = g_small['ln_ffn_g'][:, 0, :]
    g_small['ln_final_g'] = dg_final[0]
    g_small['hgrn_lb_logits'] = _lb_bwd(p['hgrn_lb_logits'], g_small['hgrn_lb_logits'], name="lb_bwd")
    small_names = _SMALL_REPL + _SMALL_SHARDED
    pack = _pack([g_small[n] for n in small_names])
    tot = _sum_slots(_gather_all(pack, name="gather_small_grads"), name="sum_small_grads")
    g_tot = dict(zip(small_names, _unpack(tot, [g_small[n].shape for n in small_names])))
    for n in _SMALL_SHARDED:
        lead = p[n].shape[:-1]
        q = p[n].shape[-1] // cl
        blk = g_tot[n].reshape(lead + (4, q * cl))
        g_tot[n] = lax.dynamic_index_in_dim(blk, chip, axis=len(lead), keepdims=False)

    outs = {}
    shapes = [p[n].shape for n in small_names]
    res = _adamw(_pack([p[n] for n in small_names]), _pack([p['m_' + n] for n in small_names]),
                 _pack([p['v_' + n] for n in small_names]), [_pack([g_tot[n] for n in small_names])], name="adamw_small")
    for kind, buf in zip(('grad', 'delta', 'new_m', 'new_v'), res):
        for n, a in zip(small_names, _unpack(buf, shapes)):
            outs[kind + '_' + n] = a

    acc = {n: tuple(lax.empty(p[n].shape, f32) for _ in range(4)) for n in _BIG}
    zero = jnp.zeros((), jnp.int32)
    for layer, grp in reversed(groups):
        st, cols = scat.pop((layer, grp))
        srcs, recvs = _xchg_wait('scatter', st, cols, dx, name=f"rs_wait_{grp}{layer}")
        names = group_names(layer, grp)
        parts = []
        for (n, l), g, recv, col in zip(names, srcs, recvs, cols, strict=True):
            R, C = p[n].shape[1:]
            own = lax.dynamic_slice(g, (zero, chip * C) if col else (chip * R, zero), (R, C))
            parts.append(_sum_own_recv(own, recv, name=f"sum_{n}"))
        others = _swap_cores_multi(parts, name=f"swap_{grp}{layer}")
        for (n, l), part, other in zip(names, parts, others, strict=True):
            acc[n] = tuple(_adamw_layer(p[n], p['m_' + n], p['v_' + n], l, [part, other], acc[n], name=f"adamw_{n}"))
    for n in _BIG:
        for kind, buf in zip(('grad', 'delta', 'new_m', 'new_v'), acc[n]):
            outs[kind + '_' + n] = buf

    loss = lax.psum(loss_b[0, 0], ("x", "y", "c"))
    weights = _IN_NAMES[1:]
    return (loss, dx[None], *[outs['grad_' + n] for n in weights], *[outs['delta_' + n] for n in weights],
            *[outs['new_m_' + n] for n in weights], *[outs['new_v_' + n] for n in weights])


def kernel(x, ln_mix_g, ln_ffn_g, ln_final_g, ev_w_in, ev_b_in, lru_conv_w, lru_conv_b, lru_wa, lru_ba, lru_wx, lru_bx, lru_lambda, hgrn_lb_logits, hgrn_norm_g, ev_w_out, od_w_in, od_b_in, sc_conv_w, cf_conv_w, cf_conv_b, cf_ln_g, cf_ln_b, od_w_out, ffn_w_gate, ffn_w_up, ffn_w_down, loss_target, m_ln_mix_g, m_ln_ffn_g, m_ln_final_g, m_ev_w_in, m_ev_b_in, m_lru_conv_w, m_lru_conv_b, m_lru_wa, m_lru_ba, m_lru_wx, m_lru_bx, m_lru_lambda, m_hgrn_lb_logits, m_hgrn_norm_g, m_ev_w_out, m_od_w_in, m_od_b_in, m_sc_conv_w, m_cf_conv_w, m_cf_conv_b, m_cf_ln_g, m_cf_ln_b, m_od_w_out, m_ffn_w_gate, m_ffn_w_up, m_ffn_w_down, v_ln_mix_g, v_ln_ffn_g, v_ln_final_g, v_ev_w_in, v_ev_b_in, v_lru_conv_w, v_lru_conv_b, v_lru_wa, v_lru_ba, v_lru_wx, v_lru_bx, v_lru_lambda, v_hgrn_lb_logits, v_hgrn_norm_g, v_ev_w_out, v_od_w_in, v_od_b_in, v_sc_conv_w, v_cf_conv_w, v_cf_conv_b, v_cf_ln_g, v_cf_ln_b, v_od_w_out, v_ffn_w_gate, v_ffn_w_up, v_ffn_w_down):
    vals = locals()
    p = {n: vals[n] for n in _IN_NAMES + ['loss_target']}
    for n in _IN_NAMES[1:]:
        p['m_' + n] = vals['m_' + n]
        p['v_' + n] = vals['v_' + n]
    return _step(p)
```

```python
import functools

import jax
import jax.numpy as jnp
from jax import lax
from jax.experimental import pallas as pl
from jax.experimental.pallas import tpu as pltpu

f32 = jnp.float32
_MXU = jnp.bfloat16
_WIRE = jnp.bfloat16

N_HEADS = 8
LRU_C = 8.0
EPS = 1e-6
F_FLOOR = 1e-30
SUB = 16
ADAM_LR, ADAM_B1, ADAM_B2, ADAM_EPS, ADAM_WD, ADAM_STEP = 0.001, 0.9, 0.999, 1e-08, 0.01, 10
V7X_VMEM_LIMIT = 48 * 1024 * 1024
LANE = 128
MESH = pl.DeviceIdType.MESH

_IN_NAMES = ['x', 'ln_mix_g', 'ln_ffn_g', 'ln_final_g', 'ev_w_in', 'ev_b_in', 'lru_conv_w', 'lru_conv_b', 'lru_wa', 'lru_ba',
             'lru_wx', 'lru_bx', 'lru_lambda', 'hgrn_lb_logits', 'hgrn_norm_g', 'ev_w_out', 'od_w_in', 'od_b_in', 'sc_conv_w',
             'cf_conv_w', 'cf_conv_b', 'cf_ln_g', 'cf_ln_b', 'od_w_out', 'ffn_w_gate', 'ffn_w_up', 'ffn_w_down']
_BIG = ['ev_w_in', 'ev_w_out', 'od_w_in', 'od_w_out', 'ffn_w_gate', 'ffn_w_up', 'ffn_w_down']
_BIG_COL = {'ev_w_in': True, 'ev_w_out': False, 'od_w_in': True, 'od_w_out': False, 'ffn_w_gate': True, 'ffn_w_up': True,
            'ffn_w_down': False}
_SMALL_SHARDED = ['lru_conv_w', 'od_b_in', 'sc_conv_w', 'cf_conv_w', 'cf_conv_b', 'cf_ln_g', 'cf_ln_b']
_SMALL_REPL = ['ln_mix_g', 'ln_ffn_g', 'ln_final_g', 'ev_b_in', 'lru_conv_b', 'lru_wa', 'lru_ba', 'lru_wx', 'lru_bx',
               'lru_lambda', 'hgrn_lb_logits', 'hgrn_norm_g']


def _tile(n, pref, align):
    if n <= pref:
        return n
    t = (pref // align) * align
    while t >= align:
        if n % t == 0:
            return t
        t -= align
    return n


def _params(sem):
    return pltpu.CompilerParams(dimension_semantics=sem, vmem_limit_bytes=V7X_VMEM_LIMIT)


def _rows(shape):
    return lax.broadcasted_iota(jnp.int32, shape, 0)


def _mxdot(a, b, dims=(((1,), (0,)), ((), ()))):
    return lax.dot_general(a.astype(_MXU), b.astype(_MXU), dims, preferred_element_type=f32)


_NN = (((1,), (0,)), ((), ()))
_NT = (((1,), (1,)), ((), ()))
_TN = (((0,), (0,)), ((), ()))


def _matmul(a, b, mode, out_dtype, name, bias=None, add=None, tm=1024, tn=512, tk=2048):
    if mode == 'nn':
        (M, K), (K2, N) = a.shape, b.shape
    elif mode == 'nt':
        (M, K), (N, K2) = a.shape, b.shape
    else:
        (K, M), (K2, N) = a.shape, b.shape
    assert K == K2, (name, a.shape, b.shape)
    tm, tn, tk = _tile(M, tm, LANE), _tile(N, tn, LANE), _tile(K, tk, LANE)
    nk = K // tk
    dims = {'nn': _NN, 'nt': _NT, 'tn': _TN}[mode]
    has_bias, has_add = bias is not None, add is not None

    def body(*refs):
        a_ref, b_ref = refs[0], refs[1]
        pos = 2
        bias_ref = add_ref = None
        if has_bias:
            bias_ref = refs[pos]
            pos += 1
        if has_add:
            add_ref = refs[pos]
            pos += 1
        o_ref, acc_ref = refs[pos], refs[pos + 1]
        k = pl.program_id(2)

        @pl.when(k == 0)
        def _():
            acc_ref[...] = jnp.zeros_like(acc_ref)

        acc_ref[...] += _mxdot(a_ref[...], b_ref[...], dims)

        @pl.when(k == nk - 1)
        def _():
            r = acc_ref[...]
            if has_bias:
                r = r + bias_ref[...]
            if has_add:
                r = r + add_ref[...]
            o_ref[...] = r.astype(o_ref.dtype)

    if mode == 'tn':
        a_spec = pl.BlockSpec((tk, tm), lambda i, j, k: (k, i))
    else:
        a_spec = pl.BlockSpec((tm, tk), lambda i, j, k: (i, k))
    if mode == 'nt':
        b_spec = pl.BlockSpec((tn, tk), lambda i, j, k: (j, k))
    else:
        b_spec = pl.BlockSpec((tk, tn), lambda i, j, k: (k, j))
    in_specs, args = [a_spec, b_spec], [a, b]
    if has_bias:
        in_specs.append(pl.BlockSpec((1, tn), lambda i, j, k: (0, j)))
        args.append(bias)
    if has_add:
        in_specs.append(pl.BlockSpec((tm, tn), lambda i, j, k: (i, j)))
        args.append(add)
    return pl.pallas_call(
        body, name=name, grid=(M // tm, N // tn, nk), in_specs=in_specs,
        out_specs=pl.BlockSpec((tm, tn), lambda i, j, k: (i, j)),
        out_shape=jax.ShapeDtypeStruct((M, N), out_dtype),
        scratch_shapes=[pltpu.VMEM((tm, tn), f32)],
        compiler_params=_params(("parallel", "parallel", "arbitrary")),
    )(*args)


def _ffn_in(h2, wg, wu, name):
    (M, K), N = h2.shape, wg.shape[1]
    tm, tn = _tile(M, 1024, LANE), _tile(N, 512, LANE)

    def body(a_ref, g_ref, u_ref, go_ref, uo_ref, act_ref):
        a = a_ref[...]
        g, u = _mxdot(a, g_ref[...]), _mxdot(a, u_ref[...])
        go_ref[...] = g.astype(go_ref.dtype)
        uo_ref[...] = u.astype(uo_ref.dtype)
        act_ref[...] = (jax.nn.silu(g) * u).astype(act_ref.dtype)

    wspec = pl.BlockSpec((K, tn), lambda i, j: (0, j))
    ospec = pl.BlockSpec((tm, tn), lambda i, j: (i, j))
    return pl.pallas_call(
        body, name=name, grid=(M // tm, N // tn), in_specs=[pl.BlockSpec((tm, K), lambda i, j: (i, 0)), wspec, wspec],
        out_specs=[ospec] * 3, out_shape=[jax.ShapeDtypeStruct((M, N), _MXU)] * 3,
        compiler_params=_params(("parallel", "parallel")),
    )(h2, wg, wu)


def _ffn_dact(dx, wd, gate, up, name):
    (M, K), N = dx.shape, wd.shape[0]
    tm, tn = _tile(M, 512, LANE), _tile(N, 512, LANE)

    def body(a_ref, w_ref, g_ref, u_ref, dg_ref, du_ref):
        da = _mxdot(a_ref[...], w_ref[...], _NT)
        _, vjp = jax.vjp(lambda gg, uu: jax.nn.silu(gg) * uu, g_ref[...].astype(f32), u_ref[...].astype(f32))
        dg, du = vjp(da)
        dg_ref[...] = dg.astype(dg_ref.dtype)
        du_ref[...] = du.astype(du_ref.dtype)

    ospec = pl.BlockSpec((tm, tn), lambda i, j: (i, j))
    return pl.pallas_call(
        body, name=name, grid=(M // tm, N // tn),
        in_specs=[pl.BlockSpec((tm, K), lambda i, j: (i, 0)), pl.BlockSpec((tn, K), lambda i, j: (j, 0)), ospec, ospec],
        out_specs=[ospec] * 2, out_shape=[jax.ShapeDtypeStruct((M, N), _MXU)] * 2,
        compiler_params=_params(("parallel", "parallel")),
    )(dx, wd, gate, up)


def _pw(fn, slabs, out_dtypes, width, name, consts=()):
    T = slabs[0][0].shape[0]
    tt, cb = _tile(T, 512, 8), _tile(width, 512, LANE)
    nin, ncst = len(slabs), len(consts)

    def body(*refs):
        res = fn(*[r[...] for r in refs[:nin + ncst]])
        if not isinstance(res, (tuple, list)):
            res = (res,)
        for r, o in zip(res, refs[nin + ncst:], strict=True):
            o[...] = r.astype(o.dtype)

    in_specs, args = [], []
    for arr, col0 in slabs:
        assert col0 % cb == 0
        in_specs.append(pl.BlockSpec((tt, cb), functools.partial(lambda t, c, c0: (t, c0 + c), c0=col0 // cb)))
        args.append(arr)
    for cst in consts:
        in_specs.append(pl.BlockSpec((1, cb), lambda t, c: (0, c)))
        args.append(cst)
    outs = pl.pallas_call(
        body, name=name, grid=(T // tt, width // cb), in_specs=in_specs,
        out_specs=[pl.BlockSpec((tt, cb), lambda t, c: (t, c)) for _ in out_dtypes],
        out_shape=[jax.ShapeDtypeStruct((T, width), d) for d in out_dtypes],
        compiler_params=_params(("parallel", "parallel")),
    )(*args)
    return outs[0] if len(out_dtypes) == 1 else outs


def _rmsnorm_fwd(x, g, name):
    T, D = x.shape
    tt = _tile(T, 256, 8)

    def body(x_ref, g_ref, o_ref):
        xv = x_ref[...]
        r = lax.rsqrt(jnp.mean(xv * xv, axis=-1, keepdims=True) + EPS)
        o_ref[...] = (xv * r * g_ref[...]).astype(o_ref.dtype)

    return pl.pallas_call(
        body, name=name, grid=(T // tt,),
        in_specs=[pl.BlockSpec((tt, D), lambda t: (t, 0)), pl.BlockSpec((1, D), lambda t: (0, 0))],
        out_specs=pl.BlockSpec((tt, D), lambda t: (t, 0)),
        out_shape=jax.ShapeDtypeStruct((T, D), _MXU), compiler_params=_params(("parallel",)),
    )(x, g)


def _rmsnorm_bwd(dh, x, g, dx_in, name):
    T, D = x.shape
    tt = _tile(T, 256, 8)

    def body(dh_ref, x_ref, g_ref, dxi_ref, dx_ref, dg_ref):
        t = pl.program_id(0)
        xv, d = x_ref[...], dh_ref[...]
        r = lax.rsqrt(jnp.mean(xv * xv, axis=-1, keepdims=True) + EPS)
        n = xv * r
        dn = d * g_ref[...]
        dx_ref[...] = dxi_ref[...] + r * (dn - n * jnp.mean(dn * n, axis=-1, keepdims=True))

        @pl.when(t == 0)
        def _():
            dg_ref[...] = jnp.zeros_like(dg_ref)

        dg_ref[...] += jnp.sum(d * n, axis=0, keepdims=True)

    return pl.pallas_call(
        body, name=name, grid=(T // tt,),
        in_specs=[pl.BlockSpec((tt, D), lambda t: (t, 0)), pl.BlockSpec((tt, D), lambda t: (t, 0)),
                  pl.BlockSpec((1, D), lambda t: (0, 0)), pl.BlockSpec((tt, D), lambda t: (t, 0))],
        out_specs=[pl.BlockSpec((tt, D), lambda t: (t, 0)), pl.BlockSpec((1, D), lambda t: (0, 0))],
        out_shape=[jax.ShapeDtypeStruct((T, D), f32), jax.ShapeDtypeStruct((1, D), f32)],
        compiler_params=_params(("arbitrary",)),
    )(dh, x, g, dx_in)


def _final_loss(x, g, tgt, name):
    T, D = x.shape
    tt = _tile(T, 256, 8)

    def body(x_ref, g_ref, t_ref, l_ref, dx_ref, dg_ref):
        t = pl.program_id(0)
        xv = x_ref[...]
        r = lax.rsqrt(jnp.mean(xv * xv, axis=-1, keepdims=True) + EPS)
        n = xv * r
        e = n * g_ref[...] - t_ref[...]
        part = 0.5 * jnp.sum(jnp.mean(e * e, axis=-1, keepdims=True), axis=0, keepdims=True)
        dy = e * (1.0 / D)
        dn = dy * g_ref[...]
        dx_ref[...] = r * (dn - n * jnp.mean(dn * n, axis=-1, keepdims=True))

        @pl.when(t == 0)
        def _():
            dg_ref[...] = jnp.zeros_like(dg_ref)
            l_ref[...] = jnp.zeros_like(l_ref)

        dg_ref[...] += jnp.sum(dy * n, axis=0, keepdims=True)
        l_ref[...] += jnp.broadcast_to(part, l_ref.shape)

    return pl.pallas_call(
        body, name=name, grid=(T // tt,),
        in_specs=[pl.BlockSpec((tt, D), lambda t: (t, 0)), pl.BlockSpec((1, D), lambda t: (0, 0)),
                  pl.BlockSpec((tt, D), lambda t: (t, 0))],
        out_specs=[pl.BlockSpec((1, LANE), lambda t: (0, 0)), pl.BlockSpec((tt, D), lambda t: (t, 0)),
                   pl.BlockSpec((1, D), lambda t: (0, 0))],
        out_shape=[jax.ShapeDtypeStruct((1, LANE), f32), jax.ShapeDtypeStruct((T, D), f32),
                   jax.ShapeDtypeStruct((1, D), f32)],
        compiler_params=_params(("arbitrary",)),
    )(x, g, tgt)


def _colsum(x, name):
    T, N = x.shape
    tt, cb = _tile(T, 512, 8), _tile(N, 512, LANE)

    def body(x_ref, o_ref):
        @pl.when(pl.program_id(1) == 0)
        def _():
            o_ref[...] = jnp.zeros_like(o_ref)

        o_ref[...] += jnp.sum(x_ref[...].astype(f32), axis=0, keepdims=True)

    return pl.pallas_call(
        body, name=name, grid=(N // cb, T // tt), in_specs=[pl.BlockSpec((tt, cb), lambda c, t: (t, c))],
        out_specs=pl.BlockSpec((1, cb), lambda c, t: (0, c)), out_shape=jax.ShapeDtypeStruct((1, N), f32),
        compiler_params=_params(("parallel", "arbitrary")),
    )(x)


def _shift_down(cur, prev, j):
    if j == 0:
        return cur
    n = cur.shape[0]
    return jnp.where(_rows(cur.shape) < j, pltpu.roll(prev, j, 0), pltpu.roll(cur, j, 0))


def _shift_up(cur, nxt, j):
    if j == 0:
        return cur
    n = cur.shape[0]
    return jnp.where(_rows(cur.shape) >= n - j, pltpu.roll(nxt, n - j, 0), pltpu.roll(cur, n - j, 0))


def _conv_tiles(T, C, K):
    tt, cb = _tile(T, 256, 8), _tile(C, 256, LANE)
    assert tt >= K, (tt, K)
    return tt, cb


def _conv_fwd(x, col0, w, b, C, name):
    T, K = x.shape[0], w.shape[0]
    tt, cb = _conv_tiles(T, C, K)
    c0 = col0 // cb
    assert col0 % cb == 0
    has_b = b is not None

    def body(*refs):
        cur_ref, prev_ref, w_ref = refs[:3]
        o_ref = refs[-1]
        t = pl.program_id(1)
        cur = cur_ref[...]
        prev = jnp.where(t > 0, prev_ref[...], 0.0)
        wv = w_ref[...]
        acc = jnp.zeros_like(cur)
        for k in range(K):
            acc = acc + wv[k:k + 1, :] * _shift_down(cur, prev, K - 1 - k)
        if has_b:
            acc = acc + refs[3][...]
        o_ref[...] = acc

    in_specs = [pl.BlockSpec((tt, cb), lambda c, t: (t, c0 + c)),
                pl.BlockSpec((tt, cb), lambda c, t: (jnp.maximum(t - 1, 0), c0 + c)),
                pl.BlockSpec((K, cb), lambda c, t: (0, c))]
    args = [x, x, w]
    if has_b:
        in_specs.append(pl.BlockSpec((1, cb), lambda c, t: (0, c)))
        args.append(b)
    return pl.pallas_call(
        body, name=name, grid=(C // cb, T // tt), in_specs=in_specs,
        out_specs=pl.BlockSpec((tt, cb), lambda c, t: (t, c)), out_shape=jax.ShapeDtypeStruct((T, C), f32),
        compiler_params=_params(("parallel", "parallel")),
    )(*args)


def _conv_bwd_dx(dy, w, name):
    T, C = dy.shape
    K = w.shape[0]
    tt, cb = _conv_tiles(T, C, K)
    nt = T // tt

    def body(cur_ref, nxt_ref, w_ref, o_ref):
        t = pl.program_id(1)
        cur = cur_ref[...]
        nxt = jnp.where(t < nt - 1, nxt_ref[...], 0.0)
        wv = w_ref[...]
        acc = jnp.zeros_like(cur)
        for k in range(K):
            acc = acc + wv[k:k + 1, :] * _shift_up(cur, nxt, K - 1 - k)
        o_ref[...] = acc

    return pl.pallas_call(
        body, name=name, grid=(C // cb, nt),
        in_specs=[pl.BlockSpec((tt, cb), lambda c, t: (t, c)),
                  pl.BlockSpec((tt, cb), lambda c, t: (jnp.minimum(t + 1, nt - 1), c)),
                  pl.BlockSpec((K, cb), lambda c, t: (0, c))],
        out_specs=pl.BlockSpec((tt, cb), lambda c, t: (t, c)), out_shape=jax.ShapeDtypeStruct((T, C), f32),
        compiler_params=_params(("parallel", "parallel")),
    )(dy, dy, w)


def _conv_bwd_dw(dy, x, col0, K, name):
    T, C = dy.shape
    tt, cb = _conv_tiles(T, C, K)
    c0 = col0 // cb
    assert col0 % cb == 0

    def body(dy_ref, cur_ref, prev_ref, dw_ref, db_ref):
        t = pl.program_id(1)

        @pl.when(t == 0)
        def _():
            dw_ref[...] = jnp.zeros_like(dw_ref)
            db_ref[...] = jnp.zeros_like(db_ref)

        d = dy_ref[...]
        cur = cur_ref[...]
        prev = jnp.where(t > 0, prev_ref[...], 0.0)
        for k in range(K):
            row = jnp.sum(d * _shift_down(cur, prev, K - 1 - k), axis=0, keepdims=True)
            dw_ref[pl.ds(k, 1), :] = dw_ref[pl.ds(k, 1), :] + row
        db_ref[...] += jnp.sum(d, axis=0, keepdims=True)

    return pl.pallas_call(
        body, name=name, grid=(C // cb, T // tt),
        in_specs=[pl.BlockSpec((tt, cb), lambda c, t: (t, c)),
                  pl.BlockSpec((tt, cb), lambda c, t: (t, c0 + c)),
                  pl.BlockSpec((tt, cb), lambda c, t: (jnp.maximum(t - 1, 0), c0 + c))],
        out_specs=[pl.BlockSpec((K, cb), lambda c, t: (0, c)), pl.BlockSpec((1, cb), lambda c, t: (0, c))],
        out_shape=[jax.ShapeDtypeStruct((K, C), f32), jax.ShapeDtypeStruct((1, C), f32)],
        compiler_params=_params(("parallel", "arbitrary")),
    )(dy, x, x)


def _expm1(z):
    poly = z * (1.0 + z * (0.5 + z * (1.0 / 6.0 + z * (1.0 / 24.0 + z * (1.0 / 120.0)))))
    return jnp.where(jnp.abs(z) < 0.1, poly, jnp.exp(z) - 1.0)


def _lru_pt(xc, rp, ip, lam, first):
    r = jax.nn.sigmoid(rp)
    i = jax.nn.sigmoid(ip)
    log_a = -LRU_C * r * jax.nn.softplus(-lam)
    a = jnp.exp(log_a)
    mult = jnp.sqrt(jnp.maximum(-_expm1(2.0 * log_a), 0.0))
    mult = jnp.where(first, 1.0, mult)
    return a, mult * i * xc


def _first_mask(shape, t):
    return jnp.logical_and(_rows(shape) == 0, t == 0)


def _lru_gates_fwd(xc, wa, ba, wx, bx, lam, name):
    T, W = xc.shape
    hd = W // N_HEADS
    tt = _tile(T, 512, 8)

    def body(xc_ref, wa_ref, ba_ref, wx_ref, bx_ref, lam_ref, a_ref, u_ref):
        t = pl.program_id(1)
        x = xc_ref[...]
        rp = _mxdot(x, wa_ref[...]) + ba_ref[...]
        ip = _mxdot(x, wx_ref[...]) + bx_ref[...]
        a, u = _lru_pt(x, rp, ip, lam_ref[...], _first_mask(x.shape, t))
        a_ref[...] = a
        u_ref[...] = u

    wspec = pl.BlockSpec((None, hd, hd), lambda h, t: (h, 0, 0))
    bspec = pl.BlockSpec((None, 1, hd), lambda h, t: (h, 0, 0))
    tspec = pl.BlockSpec((tt, hd), lambda h, t: (t, h))
    return pl.pallas_call(
        body, name=name, grid=(N_HEADS, T // tt),
        in_specs=[tspec, wspec, bspec, wspec, bspec, pl.BlockSpec((1, hd), lambda h, t: (0, h))],
        out_specs=[tspec, tspec], out_shape=[jax.ShapeDtypeStruct((T, W), f32)] * 2,
        compiler_params=_params(("parallel", "parallel")),
    )(xc, wa, ba, wx, bx, lam)


def _lru_scan_fwd(a, u, gate, gcol0, name):
    T, W = a.shape
    tt, cb = _tile(T, 256, 8), _tile(W, 512, LANE)
    g0 = gcol0 // cb
    assert gcol0 % cb == 0

    def body(a_ref, u_ref, g_ref, h_ref, y_ref, carry_ref):
        t = pl.program_id(1)

        @pl.when(t == 0)
        def _():
            carry_ref[...] = jnp.zeros_like(carry_ref)

        def step(i, h):
            base = pl.multiple_of(i * 8, 8)
            a8, u8 = a_ref[pl.ds(base, 8), :], u_ref[pl.ds(base, 8), :]
            rows = []
            for j in range(8):
                h = a8[j:j + 1, :] * h + u8[j:j + 1, :]
                rows.append(h)
            h_ref[pl.ds(base, 8), :] = jnp.concatenate(rows, axis=0)
            return h

        h_last = lax.fori_loop(0, tt // 8, step, carry_ref[0:1, :])
        carry_ref[...] = jnp.broadcast_to(h_last, carry_ref.shape)
        y_ref[...] = (h_ref[...] * jax.nn.gelu(g_ref[...])).astype(y_ref.dtype)

    tspec = pl.BlockSpec((tt, cb), lambda c, t: (t, c))
    return pl.pallas_call(
        body, name=name, grid=(W // cb, T // tt),
        in_specs=[tspec, tspec, pl.BlockSpec((tt, cb), lambda c, t: (t, g0 + c))],
        out_specs=[tspec, tspec],
        out_shape=[jax.ShapeDtypeStruct((T, W), f32), jax.ShapeDtypeStruct((T, W), _MXU)],
        scratch_shapes=[pltpu.VMEM((8, cb), f32)],
        compiler_params=_params(("parallel", "arbitrary")),
    )(a, u, gate)


def _lru_scan_bwd(dy, dcol0, gate, gcol0, h, a, name):
    T, W = a.shape
    tt, cb = _tile(T, 256, 8), _tile(W, 512, LANE)
    nt = T // tt
    d0, g0 = dcol0 // cb, gcol0 // cb
    assert dcol0 % cb == 0 and gcol0 % cb == 0

    def body(dy_ref, g_ref, h_ref, a_ref, lam_ref, dg_ref, carry_ref, dh_ref):
        t = pl.program_id(1)

        @pl.when(t == 0)
        def _():
            carry_ref[...] = jnp.zeros_like(carry_ref)

        _, vjp = jax.vjp(lambda hh, gg: hh * jax.nn.gelu(gg), h_ref[...], g_ref[...])
        dh, dg = vjp(dy_ref[...])
        dg_ref[...] = dg
        dh_ref[...] = dh

        def step(i, c):
            base = pl.multiple_of((tt // 8 - 1 - i) * 8, 8)
            a8, d8 = a_ref[pl.ds(base, 8), :], dh_ref[pl.ds(base, 8), :]
            rows = [None] * 8
            for j in range(7, -1, -1):
                lam = d8[j:j + 1, :] + c
                c = a8[j:j + 1, :] * lam
                rows[j] = lam
            lam_ref[pl.ds(base, 8), :] = jnp.concatenate(rows, axis=0)
            return c

        c_last = lax.fori_loop(0, tt // 8, step, carry_ref[0:1, :])
        carry_ref[...] = jnp.broadcast_to(c_last, carry_ref.shape)

    rev = lambda c, t: (nt - 1 - t, c)
    tspec = pl.BlockSpec((tt, cb), rev)
    return pl.pallas_call(
        body, name=name, grid=(W // cb, nt),
        in_specs=[pl.BlockSpec((tt, cb), lambda c, t: (nt - 1 - t, d0 + c)),
                  pl.BlockSpec((tt, cb), lambda c, t: (nt - 1 - t, g0 + c)), tspec, tspec],
        out_specs=[tspec, tspec], out_shape=[jax.ShapeDtypeStruct((T, W), f32)] * 2,
        scratch_shapes=[pltpu.VMEM((8, cb), f32), pltpu.VMEM((tt, cb), f32)],
        compiler_params=_params(("parallel", "arbitrary")),
    )(dy, gate, h, a)


def _lru_gates_bwd(lam_g, h, xc, wa, ba, wx, bx, lam, name):
    T, W = xc.shape
    hd = W // N_HEADS
    tt = _tile(T, 512, 8)

    def body(lg_ref, h_ref, hp_ref, xc_ref, wa_ref, ba_ref, wx_ref, bx_ref, lam_ref,
             dxc_ref, dwa_ref, dba_ref, dwx_ref, dbx_ref, dlam_ref):
        t = pl.program_id(1)

        @pl.when(t == 0)
        def _():
            for r in (dwa_ref, dba_ref, dwx_ref, dbx_ref, dlam_ref):
                r[...] = jnp.zeros_like(r)

        x = xc_ref[...]
        lg = lg_ref[...]
        h_prev = _shift_down(h_ref[...], jnp.where(t > 0, hp_ref[...], 0.0), 1)
        rp = _mxdot(x, wa_ref[...]) + ba_ref[...]
        ip = _mxdot(x, wx_ref[...]) + bx_ref[...]
        first = _first_mask(x.shape, t)
        _, vjp = jax.vjp(lambda xx, r_, i_, l_: _lru_pt(xx, r_, i_, l_, first), x, rp, ip, lam_ref[...])
        dx, drp, dip, dl = vjp((lg * h_prev, lg))
        dxc_ref[...] = dx + _mxdot(drp, wa_ref[...], _NT) + _mxdot(dip, wx_ref[...], _NT)
        dwa_ref[...] += _mxdot(x, drp, _TN)
        dwx_ref[...] += _mxdot(x, dip, _TN)
        dba_ref[...] += jnp.sum(drp, axis=0, keepdims=True)
        dbx_ref[...] += jnp.sum(dip, axis=0, keepdims=True)
        dlam_ref[...] += dl

    wspec = pl.BlockSpec((None, hd, hd), lambda h_, t: (h_, 0, 0))
    bspec = pl.BlockSpec((None, 1, hd), lambda h_, t: (h_, 0, 0))
    tspec = pl.BlockSpec((tt, hd), lambda h_, t: (t, h_))
    pspec = pl.BlockSpec((tt, hd), lambda h_, t: (jnp.maximum(t - 1, 0), h_))
    lspec = pl.BlockSpec((1, hd), lambda h_, t: (0, h_))
    return pl.pallas_call(
        body, name=name, grid=(N_HEADS, T // tt),
        in_specs=[tspec, tspec, pspec, tspec, wspec, bspec, wspec, bspec, lspec],
        out_specs=[tspec, wspec, bspec, wspec, bspec, lspec],
        out_shape=[jax.ShapeDtypeStruct((T, W), f32), jax.ShapeDtypeStruct((N_HEADS, hd, hd), f32),
                   jax.ShapeDtypeStruct((N_HEADS, 1, hd), f32), jax.ShapeDtypeStruct((N_HEADS, hd, hd), f32),
                   jax.ShapeDtypeStruct((N_HEADS, 1, hd), f32), jax.ShapeDtypeStruct((1, W), f32)],
        compiler_params=_params(("parallel", "arbitrary")),
    )(lam_g, h, h, xc, wa, ba, wx, bx, lam)


def _hgrn_pt(z, qp, lb):
    sig = jax.nn.sigmoid(z)
    fg = lb + (1.0 - lb) * sig
    logf = jnp.log(jnp.maximum(fg, F_FLOOR))
    k = (1.0 - lb) * (1.0 - sig)
    return logf, k, jax.nn.silu(qp)


def _hgrn_out(o, gp, ng):
    on = o * lax.rsqrt(jnp.mean(o * o, axis=-1, keepdims=True) + EPS)
    return on * ng * jax.nn.silu(gp)


def _cumsum_rows(x):
    n, row, sh = x.shape[0], _rows(x.shape), 1
    while sh < n:
        x = x + jnp.where(row >= sh, pltpu.roll(x, sh, 0), 0.0)
        sh *= 2
    return x


def _rev_cumsum_rows(x):
    n, row, sh = x.shape[0], _rows(x.shape), 1
    while sh < n:
        x = x + jnp.where(row < n - sh, pltpu.roll(x, n - sh, 0), 0.0)
        sh *= 2
    return x


def _hgrn_fwd(proj, qcol, fcol, vcol, gcol, lb, ng, name):
    T = proj.shape[0]
    W = lb.shape[1]
    hd = W // N_HEADS
    tt = _tile(T, 256, SUB)
    ns = tt // SUB
    q0, f0, v0, g0 = qcol // hd, fcol // hd, vcol // hd, gcol // hd

    def body(q_ref, f_ref, v_ref, g_ref, lb_ref, ng_ref, y_ref, o_ref, st_ref, s_ref):
        t = pl.program_id(1)

        @pl.when(t == 0)
        def _():
            s_ref[...] = jnp.zeros_like(s_ref)

        lbv = lb_ref[...]
        row = _rows((SUB, hd))

        def sub(j, carry):
            rs = pl.ds(pl.multiple_of(j * SUB, SUB), SUB)
            logf, k, qf = _hgrn_pt(f_ref[rs, :], q_ref[rs, :], lbv)
            v = v_ref[rs, :]
            b = _cumsum_rows(logf)
            b_last = b[SUB - 1:SUB, :]
            S = s_ref[...]
            st_ref[j] = S
            intra = jnp.zeros((SUB, hd), f32)
            for r in range(SUB):
                e = jnp.exp(jnp.minimum(b[r:r + 1, :] - b, 0.0))
                m = jnp.where(row <= r, qf[r:r + 1, :] * k * e, 0.0)
                p = jnp.sum(m, axis=1, keepdims=True)
                intra = jnp.where(row == r, jnp.sum(p * v, axis=0, keepdims=True), intra)
            o_ref[rs, :] = _mxdot(qf * jnp.exp(b), S, _NT) + intra
            s_ref[...] = S * jnp.exp(b_last) + _mxdot(v, k * jnp.exp(b_last - b), _TN)
            return carry

        lax.fori_loop(0, ns, sub, 0)
        y_ref[...] = _hgrn_out(o_ref[...], g_ref[...], ng_ref[...]).astype(y_ref.dtype)

    def slab(c0):
        return pl.BlockSpec((tt, hd), functools.partial(lambda h, t, c0: (t, c0 + h), c0=c0))

    hspec = pl.BlockSpec((tt, hd), lambda h, t: (t, h))
    cspec = pl.BlockSpec((1, hd), lambda h, t: (0, h))
    return pl.pallas_call(
        body, name=name, grid=(N_HEADS, T // tt),
        in_specs=[slab(q0), slab(f0), slab(v0), slab(g0), cspec, cspec],
        out_specs=[hspec, hspec, pl.BlockSpec((None, ns, hd, hd), lambda h, t: (h, t, 0, 0))],
        out_shape=[jax.ShapeDtypeStruct((T, W), _MXU), jax.ShapeDtypeStruct((T, W), f32),
                   jax.ShapeDtypeStruct((N_HEADS, T // SUB, hd, hd), f32)],
        scratch_shapes=[pltpu.VMEM((hd, hd), f32)],
        compiler_params=_params(("parallel", "arbitrary")),
    )(proj, proj, proj, proj, lb, ng)


def _hgrn_bwd(dy, dcol, proj, qcol, fcol, vcol, gcol, lb, ng, o, states, name):
    T = proj.shape[0]
    W = lb.shape[1]
    hd = W // N_HEADS
    tt = _tile(T, 256, SUB)
    ns, nt = tt // SUB, T // tt
    q0, f0, v0, g0, d0 = qcol // hd, fcol // hd, vcol // hd, gcol // hd, dcol // hd

    def body(dy_ref, q_ref, f_ref, v_ref, g_ref, lb_ref, ng_ref, o_ref, st_ref,
             dq_ref, df_ref, dv_ref, dg_ref, dlb_ref, dng_ref, ds_ref, do_ref):
        t = pl.program_id(1)

        @pl.when(t == 0)
        def _():
            ds_ref[...] = jnp.zeros_like(ds_ref)
            dlb_ref[...] = jnp.zeros_like(dlb_ref)
            dng_ref[...] = jnp.zeros_like(dng_ref)

        _, vjp_out = jax.vjp(_hgrn_out, o_ref[...], g_ref[...], ng_ref[...])
        do, dgp, dng = vjp_out(dy_ref[...])
        do_ref[...] = do
        dg_ref[...] = dgp
        dng_ref[...] += dng
        lbv = lb_ref[...]
        row = _rows((SUB, hd))

        def sub(jj, carry):
            j = ns - 1 - jj
            rs = pl.ds(pl.multiple_of(j * SUB, SUB), SUB)
            z, qp = f_ref[rs, :], q_ref[rs, :]
            (logf, k, qf), vjp_pt = jax.vjp(_hgrn_pt, z, qp, lbv)
            v = v_ref[rs, :]
            dO = do_ref[rs, :]
            b = _cumsum_rows(logf)
            b_last = b[SUB - 1:SUB, :]
            S = st_ref[j]
            dS = ds_ref[...]
            eb = jnp.exp(b)
            kd = jnp.exp(b_last - b)
            d = jnp.exp(b_last)
            qe, ke = qf * eb, k * kd
            dqe = _mxdot(dO, S, _NN)
            dke = _mxdot(v, dS, _NN)
            dv = _mxdot(ke, dS, _NT)
            dd = jnp.sum(dS * S, axis=0, keepdims=True)
            ds_ref[...] = dS * d + _mxdot(dO, qe, _TN)
            dq_i = jnp.zeros((SUB, hd), f32)
            dk_i = jnp.zeros((SUB, hd), f32)
            for r in range(SUB):
                em = jnp.where(row <= r, jnp.exp(jnp.minimum(b[r:r + 1, :] - b, 0.0)), 0.0)
                ke_r = k * em
                qr, dor = qf[r:r + 1, :], dO[r:r + 1, :]
                p = jnp.sum(qr * ke_r, axis=1, keepdims=True)
                dv = dv + p * dor
                dp = jnp.sum(dor * v, axis=1, keepdims=True)
                dq_i = jnp.where(row == r, jnp.sum(dp * ke_r, axis=0, keepdims=True), dq_i)
                dk_i = dk_i + dp * (qr * em)
            dqf = dqe * eb + dq_i
            dk = dke * kd + dk_i
            dke_ke = dke * ke
            db = dqe * qe - dke_ke + qf * dq_i - k * dk_i
            db_last = jnp.sum(dke_ke, axis=0, keepdims=True) + dd * d
            db = db + jnp.where(row == SUB - 1, db_last, 0.0)
            dz, dqp, dlb = vjp_pt((_rev_cumsum_rows(db), dk, dqf))
            dq_ref[rs, :] = dqp
            df_ref[rs, :] = dz
            dv_ref[rs, :] = dv
            dlb_ref[...] += dlb
            return carry

        lax.fori_loop(0, ns, sub, 0)

    def slab(c0):
        return pl.BlockSpec((tt, hd), functools.partial(lambda h, t, c0: (nt - 1 - t, c0 + h), c0=c0))

    hspec = pl.BlockSpec((tt, hd), lambda h, t: (nt - 1 - t, h))
    cspec = pl.BlockSpec((1, hd), lambda h, t: (0, h))
    return pl.pallas_call(
        body, name=name, grid=(N_HEADS, nt),
        in_specs=[slab(d0), slab(q0), slab(f0), slab(v0), slab(g0), cspec, cspec, hspec,
                  pl.BlockSpec((None, ns, hd, hd), lambda h, t: (h, nt - 1 - t, 0, 0))],
        out_specs=[hspec, hspec, hspec, hspec, cspec, cspec],
        out_shape=[jax.ShapeDtypeStruct((T, W), f32)] * 4 + [jax.ShapeDtypeStruct((1, W), f32)] * 2,
        scratch_shapes=[pltpu.VMEM((hd, hd), f32), pltpu.VMEM((tt, hd), f32)],
        compiler_params=_params(("parallel", "arbitrary")),
    )(dy, proj, proj, proj, proj, lb, ng, o, states)


def _lower_bounds(logits, name):
    def fn(lg):
        sm = jax.nn.softmax(lg, axis=0)
        run, rows_ = None, []
        for j in range(lg.shape[0]):
            run = sm[j:j + 1, :] if run is None else run + sm[j:j + 1, :]
            rows_.append(run - sm[0:1, :])
        return jnp.concatenate(rows_, axis=0)
    return fn


def _lb_fwd(logits, name):
    fn = _lower_bounds(logits, name)

    def body(l_ref, o_ref):
        o_ref[...] = fn(l_ref[...])

    return pl.pallas_call(body, name=name, out_shape=jax.ShapeDtypeStruct(logits.shape, f32))(logits)


def _lb_bwd(logits, dlb, name):
    fn = _lower_bounds(logits, name)

    def body(l_ref, d_ref, o_ref):
        _, vjp = jax.vjp(fn, l_ref[...])
        o_ref[...] = vjp(d_ref[...])[0]

    return pl.pallas_call(body, name=name, out_shape=jax.ShapeDtypeStruct(logits.shape, f32))(logits, dlb)


def _ln_silu(d, g, b):
    mu = jnp.mean(d, axis=-1, keepdims=True)
    xc = d - mu
    y = xc * lax.rsqrt(jnp.mean(xc * xc, axis=-1, keepdims=True) + EPS)
    return jax.nn.silu(y * g + b)


def _ln_fwd(d, g, b, name):
    T, W = d.shape
    tt = _tile(T, 256, 8)

    def body(d_ref, g_ref, b_ref, o_ref):
        o_ref[...] = _ln_silu(d_ref[...], g_ref[...], b_ref[...]).astype(o_ref.dtype)

    return pl.pallas_call(
        body, name=name, grid=(T // tt,),
        in_specs=[pl.BlockSpec((tt, W), lambda t: (t, 0))] + [pl.BlockSpec((1, W), lambda t: (0, 0))] * 2,
        out_specs=pl.BlockSpec((tt, W), lambda t: (t, 0)), out_shape=jax.ShapeDtypeStruct((T, W), _MXU),
        compiler_params=_params(("parallel",)),
    )(d, g, b)


def _ln_bwd(dy, dcol0, d, g, b, name):
    T, W = d.shape
    tt = _tile(T, 256, 8)
    c0 = dcol0 // W
    assert dcol0 % W == 0

    def body(dy_ref, d_ref, g_ref, b_ref, dd_ref, dg_ref, db_ref):
        @pl.when(pl.program_id(0) == 0)
        def _():
            dg_ref[...] = jnp.zeros_like(dg_ref)
            db_ref[...] = jnp.zeros_like(db_ref)

        _, vjp = jax.vjp(_ln_silu, d_ref[...], g_ref[...], b_ref[...])
        dd, dg, db = vjp(dy_ref[...])
        dd_ref[...] = dd
        dg_ref[...] += dg
        db_ref[...] += db

    cspec = pl.BlockSpec((1, W), lambda t: (0, 0))
    return pl.pallas_call(
        body, name=name, grid=(T // tt,),
        in_specs=[pl.BlockSpec((tt, W), lambda t: (t, c0)), pl.BlockSpec((tt, W), lambda t: (t, 0)), cspec, cspec],
        out_specs=[pl.BlockSpec((tt, W), lambda t: (t, 0)), cspec, cspec],
        out_shape=[jax.ShapeDtypeStruct((T, W), f32), jax.ShapeDtypeStruct((1, W), f32), jax.ShapeDtypeStruct((1, W), f32)],
        compiler_params=_params(("arbitrary",)),
    )(dy, d, g, b)


def _adamw(w, m, v, parts, name):
    R, C = w.shape
    tr, tc = _tile(R, 512, 8), _tile(C, 512, LANE)
    npart = len(parts)

    def body(*refs):
        w_ref, m_ref, v_ref = refs[:3]
        g_ref, d_ref, mo_ref, vo_ref = refs[3 + npart:]
        g = refs[3][...].astype(f32)
        for p_ref in refs[4:3 + npart]:
            g = g + p_ref[...].astype(f32)
        mm = ADAM_B1 * m_ref[...] + (1.0 - ADAM_B1) * g
        vv = ADAM_B2 * v_ref[...] + (1.0 - ADAM_B2) * jnp.square(g)
        m_hat = mm / (1.0 - ADAM_B1 ** ADAM_STEP)
        v_hat = vv / (1.0 - ADAM_B2 ** ADAM_STEP)
        g_ref[...] = g
        d_ref[...] = -ADAM_LR * (m_hat / (jnp.sqrt(v_hat) + ADAM_EPS) + ADAM_WD * w_ref[...])
        mo_ref[...] = mm
        vo_ref[...] = vv

    spec = pl.BlockSpec((tr, tc), lambda i, j: (i, j))
    return pl.pallas_call(
        body, name=name, grid=(R // tr, C // tc), in_specs=[spec] * (3 + npart), out_specs=[spec] * 4,
        out_shape=[jax.ShapeDtypeStruct((R, C), f32)] * 4, compiler_params=_params(("parallel", "parallel")),
    )(w, m, v, *parts)


def _adamw_layer(w3, m3, v3, layer, parts, prev, name):
    L, R, C = w3.shape
    tr, tc = _tile(R, 128, 8), _tile(C, 2048, LANE)
    npart = len(parts)

    def body(*refs):
        w_ref, m_ref, v_ref = refs[:3]
        g_ref, d_ref, mo_ref, vo_ref = refs[3 + npart + 4:]
        g = refs[3][...].astype(f32)
        for p_ref in refs[4:3 + npart]:
            g = g + p_ref[...].astype(f32)
        mm = ADAM_B1 * m_ref[...] + (1.0 - ADAM_B1) * g
        vv = ADAM_B2 * v_ref[...] + (1.0 - ADAM_B2) * jnp.square(g)
        m_hat = mm / (1.0 - ADAM_B1 ** ADAM_STEP)
        v_hat = vv / (1.0 - ADAM_B2 ** ADAM_STEP)
        g_ref[...] = g
        d_ref[...] = -ADAM_LR * (m_hat / (jnp.sqrt(v_hat) + ADAM_EPS) + ADAM_WD * w_ref[...])
        mo_ref[...] = mm
        vo_ref[...] = vv

    spec3 = pl.BlockSpec((None, tr, tc), lambda i, j: (layer, i, j))
    spec2 = pl.BlockSpec((tr, tc), lambda i, j: (i, j))
    return pl.pallas_call(
        body, name=name, grid=(R // tr, C // tc), in_specs=[spec3] * 3 + [spec2] * npart + [_ANY] * 4,
        out_specs=[spec3] * 4, out_shape=[jax.ShapeDtypeStruct((L, R, C), f32)] * 4,
        input_output_aliases={3 + npart + k: k for k in range(4)},
        compiler_params=_params(("parallel", "parallel")),
    )(w3, m3, v3, *parts, *prev)


def _sum_own_recv(g, chip, col, recv, name):
    S, R, C = recv.shape
    tr, tc = _tile(R, 256, 8), _tile(C, 2048, LANE)
    nbr, nbc = R // tr, C // tc

    def body(chip_ref, o_ref, r_ref, out_ref):
        acc = o_ref[...].astype(f32)
        for s in range(S):
            acc = acc + r_ref[s].astype(f32)
        out_ref[...] = acc

    if col:
        own_map = lambda i, j, c: (i, c[0] * nbc + j)
    else:
        own_map = lambda i, j, c: (c[0] * nbr + i, j)
    grid_spec = pltpu.PrefetchScalarGridSpec(
        num_scalar_prefetch=1, grid=(nbr, nbc),
        in_specs=[pl.BlockSpec((tr, tc), own_map), pl.BlockSpec((S, tr, tc), lambda i, j, c: (0, i, j))],
        out_specs=pl.BlockSpec((tr, tc), lambda i, j, c: (i, j)))
    return pl.pallas_call(
        body, name=name, grid_spec=grid_spec, out_shape=jax.ShapeDtypeStruct((R, C), f32),
        compiler_params=_params(("parallel", "parallel")),
    )(chip, g, recv)


def _sum_slots(recv, name):
    S, R, C = recv.shape
    tr, tc = _tile(R, 512, 8), _tile(C, 512, LANE)

    def body(r_ref, o_ref):
        acc = r_ref[0].astype(f32)
        for s in range(1, S):
            acc = acc + r_ref[s].astype(f32)
        o_ref[...] = acc

    return pl.pallas_call(
        body, name=name, grid=(R // tr, C // tc), in_specs=[pl.BlockSpec((S, tr, tc), lambda i, j: (0, i, j))],
        out_specs=pl.BlockSpec((tr, tc), lambda i, j: (i, j)), out_shape=jax.ShapeDtypeStruct((R, C), f32),
        compiler_params=_params(("parallel", "parallel")),
    )(recv)


_CHIP_FLIPS = ((1, 0), (0, 1), (1, 1))
_ANY = pl.BlockSpec(memory_space=pl.ANY)


def _me():
    return lax.axis_index("x"), lax.axis_index("y"), lax.axis_index("c")


def _allgather_chips(local, col, name):
    L, R, C = local.shape
    out_shape = (L, R, 4 * C) if col else (L, 4 * R, C)

    def body(loc_ref, out_ref, send_sems, recv_sems, local_sem):
        x, y, c = _me()

        def block(px, py):
            s = 2 * px + py
            if col:
                return out_ref.at[:, :, pl.ds(s * C, C)]
            return out_ref.at[:, pl.ds(s * R, R), :]

        mine = pltpu.make_async_copy(loc_ref, block(x, y), local_sem)
        mine.start()
        sends = []
        for j, (fx, fy) in enumerate(_CHIP_FLIPS):
            cp = pltpu.make_async_remote_copy(src_ref=loc_ref, dst_ref=block(x, y), send_sem=send_sems.at[j],
                                              recv_sem=recv_sems.at[j], device_id=(x ^ fx, y ^ fy, c), device_id_type=MESH)
            cp.start()
            sends.append(cp)
        for j, (fx, fy) in enumerate(_CHIP_FLIPS):
            pltpu.make_async_remote_copy(src_ref=loc_ref, dst_ref=block(x ^ fx, y ^ fy), send_sem=send_sems.at[j],
                                         recv_sem=recv_sems.at[j], device_id=(x ^ fx, y ^ fy, c),
                                         device_id_type=MESH).wait_recv()
        for cp in sends:
            cp.wait_send()
        mine.wait()

    return pl.pallas_call(
        body, name=name, in_specs=[_ANY], out_specs=_ANY, out_shape=jax.ShapeDtypeStruct(out_shape, local.dtype),
        scratch_shapes=[pltpu.SemaphoreType.DMA((3,)), pltpu.SemaphoreType.DMA((3,)), pltpu.SemaphoreType.DMA],
    )(local)


def _gather_all(v, name):
    def body(v_ref, out_ref, send_sems, recv_sems, local_sem):
        x, y, c = _me()
        me_slot = out_ref.at[4 * x + 2 * y + c]
        mine = pltpu.make_async_copy(v_ref, me_slot, local_sem)
        mine.start()
        sends = []
        for m in range(1, 8):
            peer = (x ^ (m >> 2), y ^ ((m >> 1) & 1), c ^ (m & 1))
            cp = pltpu.make_async_remote_copy(src_ref=v_ref, dst_ref=me_slot, send_sem=send_sems.at[m - 1],
                                              recv_sem=recv_sems.at[m - 1], device_id=peer, device_id_type=MESH)
            cp.start()
            sends.append(cp)
        for m in range(1, 8):
            peer = (x ^ (m >> 2), y ^ ((m >> 1) & 1), c ^ (m & 1))
            pltpu.make_async_remote_copy(src_ref=v_ref, dst_ref=out_ref.at[4 * peer[0] + 2 * peer[1] + peer[2]],
                                         send_sem=send_sems.at[m - 1], recv_sem=recv_sems.at[m - 1], device_id=peer,
                                         device_id_type=MESH).wait_recv()
        for cp in sends:
            cp.wait_send()
        mine.wait()

    return pl.pallas_call(
        body, name=name, in_specs=[_ANY], out_specs=_ANY, out_shape=jax.ShapeDtypeStruct((8,) + v.shape, v.dtype),
        scratch_shapes=[pltpu.SemaphoreType.DMA((7,)), pltpu.SemaphoreType.DMA((7,)), pltpu.SemaphoreType.DMA],
    )(v)


_HBM = pl.BlockSpec(memory_space=pltpu.HBM)
_SEM = pl.BlockSpec(memory_space=pltpu.SEMAPHORE)
_EFFECT = pltpu.SideEffectType.DATAFLOW_SIDE_EFFECTING


def _chip_block(ref, s, col, n):
    return ref.at[:, pl.ds(s * n, n)] if col else ref.at[pl.ds(s * n, n), :]


_NSEM = {'gather': 7, 'scatter': 6, 'swap': 2}


def _xchg_copies(kind, src_ref, land_ref, col, sems):
    x, y, c = _me()
    if kind == 'swap':
        return [pltpu.make_async_remote_copy(src_ref=src_ref, dst_ref=land_ref, send_sem=sems[0], recv_sem=sems[1],
                                             device_id=(x, y, 1 - c), device_id_type=MESH)]
    cps = []
    if kind == 'gather':
        n = src_ref.shape[1] if col else src_ref.shape[0]
        cps.append(pltpu.make_async_copy(src_ref, _chip_block(land_ref, 2 * x + y, col, n), sems[6]))
    for j, (fx, fy) in enumerate(_CHIP_FLIPS):
        px, py = x ^ fx, y ^ fy
        if kind == 'gather':
            n = src_ref.shape[1] if col else src_ref.shape[0]
            src, dst = src_ref, _chip_block(land_ref, 2 * x + y, col, n)
        else:
            n = land_ref.shape[2] if col else land_ref.shape[1]
            src, dst = _chip_block(src_ref, 2 * px + py, col, n), land_ref.at[j]
        cps.append(pltpu.make_async_remote_copy(src_ref=src, dst_ref=dst, send_sem=sems[2 * j], recv_sem=sems[2 * j + 1],
                                                device_id=(px, py, c), device_id_type=MESH))
    return cps


def _xchg_start(kind, srcs, lands, cols, after, name):
    n = len(srcs)
    per = _NSEM[kind]
    nsem = per * n

    def body(*refs):
        src_refs, land_refs = refs[:n], refs[n:2 * n]
        sems = refs[2 * n + 1:2 * n + 1 + nsem]
        token = refs[-1]
        for i in range(n):
            for cp in _xchg_copies(kind, src_refs[i], land_refs[i], cols[i], sems[per * i:per * i + per]):
                cp.start()
        token[...] = jnp.zeros_like(token)

    hbm = lambda a: pltpu.HBM(a.shape, a.dtype)
    outs = pl.pallas_call(
        body, name=name,
        out_shape=tuple([pltpu.SemaphoreType.DMA(())] * nsem + [hbm(a) for a in srcs] + [hbm(a) for a in lands]
                        + [jax.ShapeDtypeStruct((8, LANE), f32)]),
        in_specs=tuple([_HBM] * (2 * n) + [_ANY]),
        out_specs=tuple([_SEM] * nsem + [_HBM] * (2 * n) + [pl.BlockSpec(memory_space=pltpu.VMEM)]),
        input_output_aliases={i: nsem + i for i in range(2 * n)},
        compiler_params=pltpu.CompilerParams(has_side_effects=_EFFECT),
    )(*[pltpu.with_memory_space_constraint(a, pltpu.HBM) for a in list(srcs) + list(lands)], after)
    return outs[:nsem], outs[nsem:nsem + n], outs[nsem + n:nsem + 2 * n], outs[-1]


def _xchg_wait(kind, started, cols, after, name):
    sems, srcs, lands, _ = started
    n = len(srcs)
    per = _NSEM[kind]
    nsem = per * n

    def body(*refs):
        src_refs, land_refs = refs[:n], refs[n:2 * n]
        sem_refs = refs[2 * n:2 * n + nsem]
        for i in range(n):
            for cp in _xchg_copies(kind, src_refs[i], land_refs[i], cols[i], sem_refs[per * i:per * i + per]):
                if cp.is_remote:
                    cp.wait_send()
                    cp.wait_recv()
                else:
                    cp.wait()

    hbm = lambda a: pltpu.HBM(a.shape, a.dtype)
    outs = pl.pallas_call(
        body, name=name, out_shape=tuple([hbm(a) for a in srcs] + [hbm(a) for a in lands]),
        in_specs=tuple([_HBM] * (2 * n) + [_SEM] * nsem + [_ANY]), out_specs=tuple([_HBM] * (2 * n)),
        input_output_aliases={i: i for i in range(2 * n)},
        compiler_params=pltpu.CompilerParams(has_side_effects=_EFFECT),
    )(*srcs, *lands, *sems, after)
    return outs[:n], outs[n:]


_PACK_ROWS = 512


def _pack(arrs):
    flat = jnp.concatenate([a.reshape(-1).astype(f32) for a in arrs])
    n = flat.shape[0]
    rows = -(-n // (_PACK_ROWS * LANE)) * _PACK_ROWS
    return jnp.pad(flat, (0, rows * LANE - n)).reshape(rows, LANE)


def _unpack(buf, shapes):
    flat, outs, off = buf.reshape(-1), [], 0
    for s in shapes:
        n = 1
        for d_ in s:
            n *= d_
        outs.append(flat[off:off + n].reshape(s))
        off += n
    return outs


def _step(p):
    x0 = p['x'][0]
    tgt = p['loss_target'][0]
    T, D = x0.shape
    W = D // 2
    depth = p['ln_mix_g'].shape[0]
    chip = 2 * lax.axis_index("x") + lax.axis_index("y")

    def group_names(layer, grp):
        if grp == 'ffn':
            return [('ffn_w_gate', layer), ('ffn_w_up', layer), ('ffn_w_down', layer)]
        pre = 'ev' if layer % 2 == 0 else 'od'
        return [(pre + '_w_in', layer // 2), (pre + '_w_out', layer // 2)]

    def gather_start(layer, grp, after):
        srcs, lands, cols = [], [], []
        for n, l in group_names(layer, grp):
            loc = p[n][l].astype(_MXU)
            R, C = loc.shape
            col = _BIG_COL[n]
            land = lax.empty((R, 4 * C) if col else (4 * R, C), _MXU)
            srcs.append(loc)
            lands.append(land)
            cols.append(col)
        return _xchg_start('gather', srcs, lands, cols, after, name=f"ag_start_{grp}{layer}"), cols

    def gather_wait(started, layer, grp, after):
        st, cols = started
        _, lands = _xchg_wait('gather', st, cols, after, name=f"ag_wait_{grp}{layer}")
        return dict(zip([n for n, _ in group_names(layer, grp)], lands))

    groups = [(layer, grp) for layer in range(depth) for grp in ('mix', 'ffn')]
    pending = {groups[0]: gather_start(*groups[0], x0)}
    pending[groups[1]] = gather_start(*groups[1], pending[groups[0]][0][3])
    loose = [pending[groups[1]][0][3]]

    def take_weights(gi, after):
        wts = gather_wait(pending.pop(groups[gi]), *groups[gi], after)
        if gi + 2 < len(groups):
            pending[groups[gi + 2]] = gather_start(*groups[gi + 2], next(iter(wts.values())))
            loose.append(pending[groups[gi + 2]][0][3])
        tok = None
        while loose:
            t = loose.pop()
            tok = t if tok is None else tok + t
        return wts, tok

    small_rows = []
    cl = W // 4
    for n in _SMALL_SHARDED:
        small_rows.append(p[n].reshape(-1, cl))
    srows = [a.shape[0] for a in small_rows]
    spack = jnp.concatenate(small_rows, axis=0)
    spad = -(-spack.shape[0] // 8) * 8
    spack = jnp.pad(spack, ((0, spad - spack.shape[0]), (0, 0))) + loose.pop()[0:1, 0:1]
    sfull = _allgather_chips(spack[None], False, name="ag_small")[0].reshape(4, spad, cl)
    small = {}
    off = 0
    for n, r in zip(_SMALL_SHARDED, srows):
        blk = sfull[:, off:off + r, :]
        lead = p[n].shape[:-1]
        q = p[n].shape[-1] // cl
        blk = blk.reshape((4,) + lead + (q, cl))
        blk = jnp.moveaxis(blk, 0, len(lead))
        small[n] = blk.reshape(lead + (4 * q * cl,))
        off += r

    lbs = _lb_fwd(p['hgrn_lb_logits'], name="lb_fwd")

    def row(a, tok=None):
        a = a.reshape(1, -1)
        return a if tok is None else a + tok[0:1, 0:1]

    saved = []
    full = {}
    x = x0
    for layer in range(depth):
        j = layer // 2
        s = {'x_in': x}
        wts, tok = take_weights(2 * layer, sfull if layer == 0 else x)
        full[layer] = wts
        h = _rmsnorm_fwd(x, row(p['ln_mix_g'][layer], tok), name="rms_fwd")
        s['h'] = h
        if layer % 2 == 0:
            proj = _matmul(h, wts['ev_w_in'], 'nn', f32, "mm_ev_in", bias=row(p['ev_b_in'][j]))
            xc = _conv_fwd(proj, 0, small['lru_conv_w'][j], row(p['lru_conv_b'][j]), W, name="lru_conv_fwd")
            ba, bx = p['lru_ba'][j][:, None, :], p['lru_bx'][j][:, None, :]
            a, u = _lru_gates_fwd(xc, p['lru_wa'][j], ba, p['lru_wx'][j], bx, row(p['lru_lambda'][j]), name="lru_gates_fwd")
            hl, y_a = _lru_scan_fwd(a, u, proj, W, name="lru_scan_fwd")
            y_b, o, states = _hgrn_fwd(proj, 2 * W, 3 * W, 4 * W, 5 * W, row(lbs[j]), row(p['hgrn_norm_g'][j]),
                                       name="hgrn_fwd")
            s.update(proj=proj, xc=xc, a=a, hl=hl, o=o, states=states)
            ycat = jnp.concatenate([y_a, y_b], axis=1)
            w_out = wts['ev_w_out']
        else:
            proj = _matmul(h, wts['od_w_in'], 'nn', f32, "mm_od_in", bias=row(small['od_b_in'][j]))
            pp = _pw(lambda a_, b_: a_ * b_, [(proj, W), (proj, 2 * W)], [f32], W, name="sc_mul")
            cp = _conv_fwd(pp, 0, small['sc_conv_w'][j], None, W, name="sc_conv_fwd")
            y_c = _pw(lambda a_, b_: a_ * b_, [(proj, 0), (cp, 0)], [_MXU], W, name="sc_out")
            glu = _pw(lambda a_, b_: a_ * jax.nn.sigmoid(b_), [(proj, 3 * W), (proj, 4 * W)], [f32], W, name="cf_glu")
            dcv = _conv_fwd(glu, 0, small['cf_conv_w'][j], row(small['cf_conv_b'][j]), W, name="cf_conv_fwd")
            y_d = _ln_fwd(dcv, row(small['cf_ln_g'][j]), row(small['cf_ln_b'][j]), name="cf_ln_fwd")
            s.update(proj=proj, pp=pp, cp=cp, glu=glu, dcv=dcv)
            ycat = jnp.concatenate([y_c, y_d], axis=1)
            w_out = wts['od_w_out']
        s['ycat'] = ycat
        x = _matmul(ycat, w_out, 'nn', f32, "mm_mix_out", add=x)
        s['x_mid'] = x
        wts, tok = take_weights(2 * layer + 1, x)
        full[layer].update(wts)
        h2 = _rmsnorm_fwd(x, row(p['ln_ffn_g'][layer], tok), name="rms_fwd")
        gate, up, act = _ffn_in(h2, wts['ffn_w_gate'], wts['ffn_w_up'], name="ffn_in")
        x =_matmul(act, wts['ffn_w_down'], 'nn', f32, "mm_ffn_out", add=x, tk=1408)
        s.update(h2=h2, gate=gate, up=up, act=act)
        saved.append(s)

    loss_b, dx, dg_final = _final_loss(x, row(p['ln_final_g']), tgt, name="final_loss")

    scat = {}
    tok = None

    def scatter_start(layer, grp, grads):
        srcs, lands, cols = [], [], []
        for (n, l), g in zip(group_names(layer, grp), grads, strict=True):
            R, C = p[n].shape[1:]
            srcs.append(g)
            lands.append(lax.empty((3, R, C), _WIRE))
            cols.append(_BIG_COL[n])
        st = _xchg_start('scatter', srcs, lands, cols, grads[0], name=f"rs_start_{grp}{layer}")
        scat[(layer, grp)] = (st, cols)
        return st[3]

    gs = {n: [None] * p[n].shape[0] for n in _IN_NAMES[1:] if n not in _BIG and n != 'ln_final_g'}
    for layer in reversed(range(depth)):
        j = layer // 2
        s = saved[layer]
        F = s['gate'].shape[1]
        wts = full[layer]
        dgate, dup = _ffn_dact(dx, wts['ffn_w_down'], s['gate'], s['up'], name="ffn_dact")
        g_down = _matmul(s['act'], dx, 'tn', _WIRE, "mm_dw_down", tm=1408, tn=1024, tk=512)
        g_gate = _matmul(s['h2'], dgate, 'tn', _WIRE, "mm_dw_in", tm=1024, tn=1408, tk=512)
        g_up = _matmul(s['h2'], dup, 'tn', _WIRE, "mm_dw_in", tm=1024, tn=1408, tk=512)
        tok = scatter_start(layer, 'ffn', [g_gate, g_up, g_down])
        dh2 = _matmul(dgate, wts['ffn_w_gate'], 'nt', f32, "mm_ffn_dh", tk=1408)
        dh2 = _matmul(dup, wts['ffn_w_up'], 'nt', f32, "mm_ffn_dh_acc", add=dh2, tk=1408)
        dx, gs['ln_ffn_g'][layer] = _rmsnorm_bwd(dh2, s['x_mid'], row(p['ln_ffn_g'][layer], tok), dx, name="rms_bwd")
        if layer % 2 == 0:
            w_out, w_in, n_out, n_in = wts['ev_w_out'], wts['ev_w_in'], 'ev_w_out', 'ev_w_in'
        else:
            w_out, w_in, n_out, n_in = wts['od_w_out'], wts['od_w_in'], 'od_w_out', 'od_w_in'
        dycat = _matmul(dx, w_out, 'nt', f32, "mm_mix_dy")
        g_out = _matmul(s['ycat'], dx, 'tn', _WIRE, "mm_dw_out", tm=1024, tn=1024, tk=512)
        proj = s['proj']
        if layer % 2 == 0:
            ba, bx = p['lru_ba'][j][:, None, :], p['lru_bx'][j][:, None, :]
            lam_g, dgate_a = _lru_scan_bwd(dycat, 0, proj, W, s['hl'], s['a'], name="lru_scan_bwd")
            dxc, dwa, dba, dwx, dbx, dlam = _lru_gates_bwd(lam_g, s['hl'], s['xc'], p['lru_wa'][j], ba, p['lru_wx'][j], bx,
                                                           row(p['lru_lambda'][j]), name="lru_gates_bwd")
            dxa = _conv_bwd_dx(dxc, small['lru_conv_w'][j], name="lru_conv_dx")
            dcw, dcb = _conv_bwd_dw(dxc, proj, 0, small['lru_conv_w'][j].shape[0], name="lru_conv_dw")
            dq, df, dv, dgp, dlb, dng = _hgrn_bwd(dycat, W, proj, 2 * W, 3 * W, 4 * W, 5 * W, row(lbs[j]),
                                                  row(p['hgrn_norm_g'][j]), s['o'], s['states'], name="hgrn_bwd")
            gs['lru_wa'][j], gs['lru_ba'][j], gs['lru_wx'][j], gs['lru_bx'][j] = dwa, dba[:, 0, :], dwx, dbx[:, 0, :]
            gs['lru_lambda'][j], gs['lru_conv_w'][j], gs['lru_conv_b'][j] = dlam[0], dcw, dcb[0]
            gs['hgrn_lb_logits'][j], gs['hgrn_norm_g'][j] = dlb[0], dng[0]
            dproj = jnp.concatenate([dxa, dgate_a, dq, df, dv, dgp], axis=1)
        else:
            def sc_bwd1(dy_, cp_, sb_):
                return dy_ * cp_, dy_ * sb_

            dsb, dcp = _pw(sc_bwd1, [(dycat, 0), (s['cp'], 0), (proj, 0)], [f32, f32], W, name="sc_bwd1")
            dpp = _conv_bwd_dx(dcp, small['sc_conv_w'][j], name="sc_conv_dx")
            dscw, _ = _conv_bwd_dw(dcp, s['pp'], 0, small['sc_conv_w'][j].shape[0], name="sc_conv_dw")

            def sc_bwd2(dp_, sc_, sv_):
                return dp_ * sv_, dp_ * sc_

            dsc, dsv = _pw(sc_bwd2, [(dpp, 0), (proj, W), (proj, 2 * W)], [f32, f32], W, name="sc_bwd2")
            dd, dlg, dlbeta = _ln_bwd(dycat, W, s['dcv'], row(small['cf_ln_g'][j]), row(small['cf_ln_b'][j]),
                                      name="cf_ln_bwd")
            dglu = _conv_bwd_dx(dd, small['cf_conv_w'][j], name="cf_conv_dx")
            dcfw, dcfb = _conv_bwd_dw(dd, s['glu'], 0, small['cf_conv_w'][j].shape[0], name="cf_conv_dw")

            def glu_bwd(dg_, cu_, cg_):
                _, vjp = jax.vjp(lambda a_, b_: a_ * jax.nn.sigmoid(b_), cu_, cg_)
                return vjp(dg_)

            dcu, dcg = _pw(glu_bwd, [(dglu, 0), (proj, 3 * W), (proj, 4 * W)], [f32, f32], W, name="cf_glu_bwd")
            gs['sc_conv_w'][j], gs['cf_conv_w'][j], gs['cf_conv_b'][j] = dscw, dcfw, dcfb[0]
            gs['cf_ln_g'][j], gs['cf_ln_b'][j] = dlg[0], dlbeta[0]
            dproj = jnp.concatenate([dsb, dsc, dsv, dcu, dcg], axis=1)
        bname = 'ev_b_in' if layer % 2 == 0 else 'od_b_in'
        gs[bname][j] = _colsum(dproj, name="colsum_" + n_in)[0]
        g_in = _matmul(s['h'], dproj, 'tn', _WIRE, "mm_dw_" + n_in, tm=1024, tn=1536, tk=512)
        tok = scatter_start(layer, 'mix', [g_in, g_out])
        dh = _matmul(dproj, w_in, 'nt', f32, "mm_dh_" + n_in)
        dx, gs['ln_mix_g'][layer] = _rmsnorm_bwd(dh, s['x_in'], row(p['ln_mix_g'][layer], tok), dx, name="rms_bwd")

    g_small = {n: jnp.stack(v_) for n, v_ in gs.items()}
    g_small['ln_mix_g'] = g_small['ln_mix_g'][:, 0, :]
    g_small['ln_ffn_g'] = g_small['ln_ffn_g'][:, 0, :]
    g_small['ln_final_g'] = dg_final[0]
    g_small['hgrn_lb_logits'] = _lb_bwd(p['hgrn_lb_logits'], g_small['hgrn_lb_logits'], name="lb_bwd")
    small_names = _SMALL_REPL + _SMALL_SHARDED
    pack = _pack([g_small[n] for n in small_names])
    tot = _sum_slots(_gather_all(pack, name="gather_small_grads"), name="sum_small_grads")
    g_tot = dict(zip(small_names, _unpack(tot, [g_small[n].shape for n in small_names])))
    for n in _SMALL_SHARDED:
        lead = p[n].shape[:-1]
        q = p[n].shape[-1] // cl
        blk = g_tot[n].reshape(lead + (4, q * cl))
        g_tot[n] = lax.dynamic_index_in_dim(blk, chip, axis=len(lead), keepdims=False)

    outs = {}
    shapes = [p[n].shape for n in small_names]
    res = _adamw(_pack([p[n] for n in small_names]), _pack([p['m_' + n] for n in small_names]),
                 _pack([p['v_' + n] for n in small_names]), [_pack([g_tot[n] for n in small_names])], name="adamw_small")
    for kind, buf in zip(('grad', 'delta', 'new_m', 'new_v'), res):
        for n, a in zip(small_names, _unpack(buf, shapes)):
            outs[kind + '_' + n] = a

    acc = {n: tuple(lax.empty(p[n].shape, f32) for _ in range(4)) for n in _BIG}
    chip1 = jnp.reshape(chip, (1,)).astype(jnp.int32)
    def finish(swap, after):
        st, names, tag = swap
        parts, others = _xchg_wait('swap', st, [None] * len(names), after, name=f"swap_wait_{tag}")
        for (n, l), part, other in zip(names, parts, others, strict=True):
            acc[n] = tuple(_adamw_layer(p[n], p['m_' + n], p['v_' + n], l, [part, other], acc[n], name=f"adamw_{n}"))

    prev_swap = None
    for layer, grp in reversed(groups):
        st, cols = scat.pop((layer, grp))
        srcs, recvs = _xchg_wait('scatter', st, cols, dx, name=f"rs_wait_{grp}{layer}")
        names = group_names(layer, grp)
        parts = [_sum_own_recv(g, chip1, col, recv, name=f"sum_{n}")
                 for (n, l), g, recv, col in zip(names, srcs, recvs, cols, strict=True)]
        st = _xchg_start('swap', parts, [lax.empty(a.shape, f32) for a in parts], [None] * len(parts), parts[0],
                         name=f"swap_start_{grp}{layer}")
        if prev_swap is not None:
            finish(prev_swap, st[3])
        prev_swap = (st, names, f"{grp}{layer}")
    finish(prev_swap, prev_swap[0][3])
    for n in _BIG:
        for kind, buf in zip(('grad', 'delta', 'new_m', 'new_v'), acc[n]):
            outs[kind + '_' + n] = buf

    loss = lax.psum(loss_b[0, 0], ("x", "y", "c"))
    weights = _IN_NAMES[1:]
    return (loss, dx[None], *[outs['grad_' + n] for n in weights], *[outs['delta_' + n] for n in weights],
            *[outs['new_m_' + n] for n in weights], *[outs['new_v_' + n] for n in weights])


def kernel(x, ln_mix_g, ln_ffn_g, ln_final_g, ev_w_in, ev_b_in, lru_conv_w, lru_conv_b, lru_wa, lru_ba, lru_wx, lru_bx, lru_lambda, hgrn_lb_logits, hgrn_norm_g, ev_w_out, od_w_in, od_b_in, sc_conv_w, cf_conv_w, cf_conv_b, cf_ln_g, cf_ln_b, od_w_out, ffn_w_gate, ffn_w_up, ffn_w_down, loss_target, m_ln_mix_g, m_ln_ffn_g, m_ln_final_g, m_ev_w_in, m_ev_b_in, m_lru_conv_w, m_lru_conv_b, m_lru_wa, m_lru_ba, m_lru_wx, m_lru_bx, m_lru_lambda, m_hgrn_lb_logits, m_hgrn_norm_g, m_ev_w_out, m_od_w_in, m_od_b_in, m_sc_conv_w, m_cf_conv_w, m_cf_conv_b, m_cf_ln_g, m_cf_ln_b, m_od_w_out, m_ffn_w_gate, m_ffn_w_up, m_ffn_w_down, v_ln_mix_g, v_ln_ffn_g, v_ln_final_g, v_ev_w_in, v_ev_b_in, v_lru_conv_w, v_lru_conv_b, v_lru_wa, v_lru_ba, v_lru_wx, v_lru_bx, v_lru_lambda, v_hgrn_lb_logits, v_hgrn_norm_g, v_ev_w_out, v_od_w_in, v_od_b_in, v_sc_conv_w, v_cf_conv_w, v_cf_conv_b, v_cf_ln_g, v_cf_ln_b, v_od_w_out, v_ffn_w_gate, v_ffn_w_up, v_ffn_w_down):
    vals = locals()
    p = {n: vals[n] for n in _IN_NAMES + ['loss_target']}
    for n in _IN_NAMES[1:]:
        p['m_' + n] = vals['m_' + n]
        p['v_' + n] = vals['v_' + n]
    return _step(p)
```

```python
import functools

import jax
import jax.numpy as jnp
from jax import lax
from jax.experimental import pallas as pl
from jax.experimental.pallas import tpu as pltpu

f32 = jnp.float32
_MXU = jnp.bfloat16
_WIRE = jnp.bfloat16

N_HEADS = 8
LRU_C = 8.0
EPS = 1e-6
F_FLOOR = 1e-30
SUB = 16
HGRN_HPB = 2
ADAM_LR, ADAM_B1, ADAM_B2, ADAM_EPS, ADAM_WD, ADAM_STEP = 0.001, 0.9, 0.999, 1e-08, 0.01, 10
V7X_VMEM_LIMIT = 48 * 1024 * 1024
LANE = 128
MESH = pl.DeviceIdType.MESH

_IN_NAMES = ['x', 'ln_mix_g', 'ln_ffn_g', 'ln_final_g', 'ev_w_in', 'ev_b_in', 'lru_conv_w', 'lru_conv_b', 'lru_wa', 'lru_ba',
             'lru_wx', 'lru_bx', 'lru_lambda', 'hgrn_lb_logits', 'hgrn_norm_g', 'ev_w_out', 'od_w_in', 'od_b_in', 'sc_conv_w',
             'cf_conv_w', 'cf_conv_b', 'cf_ln_g', 'cf_ln_b', 'od_w_out', 'ffn_w_gate', 'ffn_w_up', 'ffn_w_down']
_BIG = ['ev_w_in', 'ev_w_out', 'od_w_in', 'od_w_out', 'ffn_w_gate', 'ffn_w_up', 'ffn_w_down']
_BIG_COL = {'ev_w_in': True, 'ev_w_out': False, 'od_w_in': True, 'od_w_out': False, 'ffn_w_gate': True, 'ffn_w_up': True,
            'ffn_w_down': False}
_SMALL_SHARDED = ['lru_conv_w', 'od_b_in', 'sc_conv_w', 'cf_conv_w', 'cf_conv_b', 'cf_ln_g', 'cf_ln_b']
_SMALL_REPL = ['ln_mix_g', 'ln_ffn_g', 'ln_final_g', 'ev_b_in', 'lru_conv_b', 'lru_wa', 'lru_ba', 'lru_wx', 'lru_bx',
               'lru_lambda', 'hgrn_lb_logits', 'hgrn_norm_g']


def _tile(n, pref, align):
    if n <= pref:
        return n
    t = (pref // align) * align
    while t >= align:
        if n % t == 0:
            return t
        t -= align
    return n


def _params(sem):
    return pltpu.CompilerParams(dimension_semantics=sem, vmem_limit_bytes=V7X_VMEM_LIMIT)


def _rows(shape):
    return lax.broadcasted_iota(jnp.int32, shape, 0)


def _mxdot(a, b, dims=(((1,), (0,)), ((), ()))):
    return lax.dot_general(a.astype(_MXU), b.astype(_MXU), dims, preferred_element_type=f32)


_NN = (((1,), (0,)), ((), ()))
_NT = (((1,), (1,)), ((), ()))
_TN = (((0,), (0,)), ((), ()))


def _matmul(a, b, mode, out_dtype, name, bias=None, add=None, tm=1024, tn=512, tk=2048, second=None):
    if mode == 'nn':
        (M, K), (K2, N) = a.shape, b.shape
    elif mode == 'nt':
        (M, K), (N, K2) = a.shape, b.shape
    else:
        (K, M), (K2, N) = a.shape, b.shape
    assert K == K2, (name, a.shape, b.shape)
    tm, tn, tk = _tile(M, tm, LANE), _tile(N, tn, LANE), _tile(K, tk, LANE)
    nk = K // tk
    dims = {'nn': _NN, 'nt': _NT, 'tn': _TN}[mode]
    has_bias, has_add, has_second = bias is not None, add is not None, second is not None

    def body(*refs):
        a_ref, b_ref = refs[0], refs[1]
        pos = 2
        if has_second:
            a2_ref, b2_ref = refs[2], refs[3]
            pos = 4
        bias_ref = add_ref = None
        if has_bias:
            bias_ref = refs[pos]
            pos += 1
        if has_add:
            add_ref = refs[pos]
            pos += 1
        o_ref, acc_ref = refs[pos], refs[pos + 1]
        k = pl.program_id(2)

        @pl.when(k == 0)
        def _():
            acc_ref[...] = jnp.zeros_like(acc_ref)

        prod = _mxdot(a_ref[...], b_ref[...], dims)
        if has_second:
            prod = prod + _mxdot(a2_ref[...], b2_ref[...], dims)
        acc_ref[...] += prod

        @pl.when(k == nk - 1)
        def _():
            r = acc_ref[...]
            if has_bias:
                r = r + bias_ref[...]
            if has_add:
                r = r + add_ref[...]
            o_ref[...] = r.astype(o_ref.dtype)

    if mode == 'tn':
        a_spec = pl.BlockSpec((tk, tm), lambda i, j, k: (k, i))
    else:
        a_spec = pl.BlockSpec((tm, tk), lambda i, j, k: (i, k))
    if mode == 'nt':
        b_spec = pl.BlockSpec((tn, tk), lambda i, j, k: (j, k))
    else:
        b_spec = pl.BlockSpec((tk, tn), lambda i, j, k: (k, j))
    in_specs, args = [a_spec, b_spec], [a, b]
    if has_second:
        assert second[0].shape == a.shape and second[1].shape == b.shape
        in_specs += [a_spec, b_spec]
        args += list(second)
    if has_bias:
        in_specs.append(pl.BlockSpec((1, tn), lambda i, j, k: (0, j)))
        args.append(bias)
    if has_add:
        in_specs.append(pl.BlockSpec((tm, tn), lambda i, j, k: (i, j)))
        args.append(add)
    return pl.pallas_call(
        body, name=name, grid=(M // tm, N // tn, nk), in_specs=in_specs,
        out_specs=pl.BlockSpec((tm, tn), lambda i, j, k: (i, j)),
        out_shape=jax.ShapeDtypeStruct((M, N), out_dtype),
        scratch_shapes=[pltpu.VMEM((tm, tn), f32)],
        compiler_params=_params(("parallel", "parallel", "arbitrary")),
    )(*args)


def _ffn_in(h2, wg, wu, name):
    (M, K), N = h2.shape, wg.shape[1]
    tm, tn = _tile(M, 1024, LANE), _tile(N, 512, LANE)

    def body(a_ref, g_ref, u_ref, go_ref, uo_ref, act_ref):
        a = a_ref[...]
        g, u = _mxdot(a, g_ref[...]), _mxdot(a, u_ref[...])
        go_ref[...] = g.astype(go_ref.dtype)
        uo_ref[...] = u.astype(uo_ref.dtype)
        act_ref[...] = (jax.nn.silu(g) * u).astype(act_ref.dtype)

    wspec = pl.BlockSpec((K, tn), lambda i, j: (0, j))
    ospec = pl.BlockSpec((tm, tn), lambda i, j: (i, j))
    return pl.pallas_call(
        body, name=name, grid=(M // tm, N // tn), in_specs=[pl.BlockSpec((tm, K), lambda i, j: (i, 0)), wspec, wspec],
        out_specs=[ospec] * 3, out_shape=[jax.ShapeDtypeStruct((M, N), _MXU)] * 3,
        compiler_params=_params(("parallel", "parallel")),
    )(h2, wg, wu)


def _ffn_dact(dx, wd, gate, up, name):
    (M, K), N = dx.shape, wd.shape[0]
    tm, tn = _tile(M, 512, LANE), _tile(N, 512, LANE)

    def body(a_ref, w_ref, g_ref, u_ref, dg_ref, du_ref):
        da = _mxdot(a_ref[...], w_ref[...], _NT)
        _, vjp = jax.vjp(lambda gg, uu: jax.nn.silu(gg) * uu, g_ref[...].astype(f32), u_ref[...].astype(f32))
        dg, du = vjp(da)
        dg_ref[...] = dg.astype(dg_ref.dtype)
        du_ref[...] = du.astype(du_ref.dtype)

    ospec = pl.BlockSpec((tm, tn), lambda i, j: (i, j))
    return pl.pallas_call(
        body, name=name, grid=(M // tm, N // tn),
        in_specs=[pl.BlockSpec((tm, K), lambda i, j: (i, 0)), pl.BlockSpec((tn, K), lambda i, j: (j, 0)), ospec, ospec],
        out_specs=[ospec] * 2, out_shape=[jax.ShapeDtypeStruct((M, N), _MXU)] * 2,
        compiler_params=_params(("parallel", "parallel")),
    )(dx, wd, gate, up)


def _pw(fn, slabs, out_dtypes, width, name, consts=()):
    T = slabs[0][0].shape[0]
    tt, cb = _tile(T, 512, 8), _tile(width, 512, LANE)
    nin, ncst = len(slabs), len(consts)

    def body(*refs):
        res = fn(*[r[...] for r in refs[:nin + ncst]])
        if not isinstance(res, (tuple, list)):
            res = (res,)
        for r, o in zip(res, refs[nin + ncst:], strict=True):
            o[...] = r.astype(o.dtype)

    in_specs, args = [], []
    for arr, col0 in slabs:
        assert col0 % cb == 0
        in_specs.append(pl.BlockSpec((tt, cb), functools.partial(lambda t, c, c0: (t, c0 + c), c0=col0 // cb)))
        args.append(arr)
    for cst in consts:
        in_specs.append(pl.BlockSpec((1, cb), lambda t, c: (0, c)))
        args.append(cst)
    outs = pl.pallas_call(
        body, name=name, grid=(T // tt, width // cb), in_specs=in_specs,
        out_specs=[pl.BlockSpec((tt, cb), lambda t, c: (t, c)) for _ in out_dtypes],
        out_shape=[jax.ShapeDtypeStruct((T, width), d) for d in out_dtypes],
        compiler_params=_params(("parallel", "parallel")),
    )(*args)
    return outs[0] if len(out_dtypes) == 1 else outs


def _rmsnorm_fwd(x, g, name):
    T, D = x.shape
    tt = _tile(T, 256, 8)

    def body(x_ref, g_ref, o_ref):
        xv = x_ref[...]
        r = lax.rsqrt(jnp.mean(xv * xv, axis=-1, keepdims=True) + EPS)
        o_ref[...] = (xv * r * g_ref[...]).astype(o_ref.dtype)

    return pl.pallas_call(
        body, name=name, grid=(T // tt,),
        in_specs=[pl.BlockSpec((tt, D), lambda t: (t, 0)), pl.BlockSpec((1, D), lambda t: (0, 0))],
        out_specs=pl.BlockSpec((tt, D), lambda t: (t, 0)),
        out_shape=jax.ShapeDtypeStruct((T, D), _MXU), compiler_params=_params(("parallel",)),
    )(x, g)


def _rmsnorm_bwd(dh, x, g, dx_in, name):
    T, D = x.shape
    tt = _tile(T, 256, 8)

    def body(dh_ref, x_ref, g_ref, dxi_ref, dx_ref, dg_ref):
        t = pl.program_id(0)
        xv, d = x_ref[...], dh_ref[...]
        r = lax.rsqrt(jnp.mean(xv * xv, axis=-1, keepdims=True) + EPS)
        n = xv * r
        dn = d * g_ref[...]
        dx_ref[...] = dxi_ref[...] + r * (dn - n * jnp.mean(dn * n, axis=-1, keepdims=True))

        @pl.when(t == 0)
        def _():
            dg_ref[...] = jnp.zeros_like(dg_ref)

        dg_ref[...] += jnp.sum(d * n, axis=0, keepdims=True)

    return pl.pallas_call(
        body, name=name, grid=(T // tt,),
        in_specs=[pl.BlockSpec((tt, D), lambda t: (t, 0)), pl.BlockSpec((tt, D), lambda t: (t, 0)),
                  pl.BlockSpec((1, D), lambda t: (0, 0)), pl.BlockSpec((tt, D), lambda t: (t, 0))],
        out_specs=[pl.BlockSpec((tt, D), lambda t: (t, 0)), pl.BlockSpec((1, D), lambda t: (0, 0))],
        out_shape=[jax.ShapeDtypeStruct((T, D), f32), jax.ShapeDtypeStruct((1, D), f32)],
        compiler_params=_params(("arbitrary",)),
    )(dh, x, g, dx_in)


def _final_loss(x, g, tgt, name):
    T, D = x.shape
    tt = _tile(T, 256, 8)

    def body(x_ref, g_ref, t_ref, l_ref, dx_ref, dg_ref):
        t = pl.program_id(0)
        xv = x_ref[...]
        r = lax.rsqrt(jnp.mean(xv * xv, axis=-1, keepdims=True) + EPS)
        n = xv * r
        e = n * g_ref[...] - t_ref[...]
        part = 0.5 * jnp.sum(jnp.mean(e * e, axis=-1, keepdims=True), axis=0, keepdims=True)
        dy = e * (1.0 / D)
        dn = dy * g_ref[...]
        dx_ref[...] = r * (dn - n * jnp.mean(dn * n, axis=-1, keepdims=True))

        @pl.when(t == 0)
        def _():
            dg_ref[...] = jnp.zeros_like(dg_ref)
            l_ref[...] = jnp.zeros_like(l_ref)

        dg_ref[...] += jnp.sum(dy * n, axis=0, keepdims=True)
        l_ref[...] += jnp.broadcast_to(part, l_ref.shape)

    return pl.pallas_call(
        body, name=name, grid=(T // tt,),
        in_specs=[pl.BlockSpec((tt, D), lambda t: (t, 0)), pl.BlockSpec((1, D), lambda t: (0, 0)),
                  pl.BlockSpec((tt, D), lambda t: (t, 0))],
        out_specs=[pl.BlockSpec((1, LANE), lambda t: (0, 0)), pl.BlockSpec((tt, D), lambda t: (t, 0)),
                   pl.BlockSpec((1, D), lambda t: (0, 0))],
        out_shape=[jax.ShapeDtypeStruct((1, LANE), f32), jax.ShapeDtypeStruct((T, D), f32),
                   jax.ShapeDtypeStruct((1, D), f32)],
        compiler_params=_params(("arbitrary",)),
    )(x, g, tgt)


def _colsum(x, name):
    T, N = x.shape
    tt, cb = _tile(T, 512, 8), _tile(N, 512, LANE)

    def body(x_ref, o_ref):
        @pl.when(pl.program_id(1) == 0)
        def _():
            o_ref[...] = jnp.zeros_like(o_ref)

        o_ref[...] += jnp.sum(x_ref[...].astype(f32), axis=0, keepdims=True)

    return pl.pallas_call(
        body, name=name, grid=(N // cb, T // tt), in_specs=[pl.BlockSpec((tt, cb), lambda c, t: (t, c))],
        out_specs=pl.BlockSpec((1, cb), lambda c, t: (0, c)), out_shape=jax.ShapeDtypeStruct((1, N), f32),
        compiler_params=_params(("parallel", "arbitrary")),
    )(x)


def _shift_down(cur, prev, j):
    if j == 0:
        return cur
    n = cur.shape[0]
    return jnp.where(_rows(cur.shape) < j, pltpu.roll(prev, j, 0), pltpu.roll(cur, j, 0))


def _shift_up(cur, nxt, j):
    if j == 0:
        return cur
    n = cur.shape[0]
    return jnp.where(_rows(cur.shape) >= n - j, pltpu.roll(nxt, n - j, 0), pltpu.roll(cur, n - j, 0))


def _conv_tiles(T, C, K):
    tt, cb = _tile(T, 256, 8), _tile(C, 256, LANE)
    assert tt >= K, (tt, K)
    return tt, cb


def _conv_fwd(x, col0, w, b, C, name):
    T, K = x.shape[0], w.shape[0]
    tt, cb = _conv_tiles(T, C, K)
    c0 = col0 // cb
    assert col0 % cb == 0
    has_b = b is not None

    def body(*refs):
        cur_ref, prev_ref, w_ref = refs[:3]
        o_ref = refs[-1]
        t = pl.program_id(1)
        cur = cur_ref[...]
        prev = jnp.where(t > 0, prev_ref[...], 0.0)
        wv = w_ref[...]
        acc = jnp.zeros_like(cur)
        for k in range(K):
            acc = acc + wv[k:k + 1, :] * _shift_down(cur, prev, K - 1 - k)
        if has_b:
            acc = acc + refs[3][...]
        o_ref[...] = acc

    in_specs = [pl.BlockSpec((tt, cb), lambda c, t: (t, c0 + c)),
                pl.BlockSpec((tt, cb), lambda c, t: (jnp.maximum(t - 1, 0), c0 + c)),
                pl.BlockSpec((K, cb), lambda c, t: (0, c))]
    args = [x, x, w]
    if has_b:
        in_specs.append(pl.BlockSpec((1, cb), lambda c, t: (0, c)))
        args.append(b)
    return pl.pallas_call(
        body, name=name, grid=(C // cb, T // tt), in_specs=in_specs,
        out_specs=pl.BlockSpec((tt, cb), lambda c, t: (t, c)), out_shape=jax.ShapeDtypeStruct((T, C), f32),
        compiler_params=_params(("parallel", "parallel")),
    )(*args)


def _conv_bwd_dx(dy, w, name, out_dtype=f32):
    T, C = dy.shape
    K = w.shape[0]
    tt, cb = _conv_tiles(T, C, K)
    nt = T // tt

    def body(cur_ref, nxt_ref, w_ref, o_ref):
        t = pl.program_id(1)
        cur = cur_ref[...]
        nxt = jnp.where(t < nt - 1, nxt_ref[...], 0.0)
        wv = w_ref[...]
        acc = jnp.zeros_like(cur)
        for k in range(K):
            acc = acc + wv[k:k + 1, :] * _shift_up(cur, nxt, K - 1 - k)
        o_ref[...] = acc.astype(o_ref.dtype)

    return pl.pallas_call(
        body, name=name, grid=(C // cb, nt),
        in_specs=[pl.BlockSpec((tt, cb), lambda c, t: (t, c)),
                  pl.BlockSpec((tt, cb), lambda c, t: (jnp.minimum(t + 1, nt - 1), c)),
                  pl.BlockSpec((K, cb), lambda c, t: (0, c))],
        out_specs=pl.BlockSpec((tt, cb), lambda c, t: (t, c)), out_shape=jax.ShapeDtypeStruct((T, C), out_dtype),
        compiler_params=_params(("parallel", "parallel")),
    )(dy, dy, w)


def _conv_bwd_dw(dy, x, col0, K, name):
    T, C = dy.shape
    tt, cb = _conv_tiles(T, C, K)
    c0 = col0 // cb
    assert col0 % cb == 0

    def body(dy_ref, cur_ref, prev_ref, dw_ref, db_ref):
        t = pl.program_id(1)

        @pl.when(t == 0)
        def _():
            dw_ref[...] = jnp.zeros_like(dw_ref)
            db_ref[...] = jnp.zeros_like(db_ref)

        d = dy_ref[...]
        cur = cur_ref[...]
        prev = jnp.where(t > 0, prev_ref[...], 0.0)
        for k in range(K):
            row = jnp.sum(d * _shift_down(cur, prev, K - 1 - k), axis=0, keepdims=True)
            dw_ref[pl.ds(k, 1), :] = dw_ref[pl.ds(k, 1), :] + row
        db_ref[...] += jnp.sum(d, axis=0, keepdims=True)

    return pl.pallas_call(
        body, name=name, grid=(C // cb, T // tt),
        in_specs=[pl.BlockSpec((tt, cb), lambda c, t: (t, c)),
                  pl.BlockSpec((tt, cb), lambda c, t: (t, c0 + c)),
                  pl.BlockSpec((tt, cb), lambda c, t: (jnp.maximum(t - 1, 0), c0 + c))],
        out_specs=[pl.BlockSpec((K, cb), lambda c, t: (0, c)), pl.BlockSpec((1, cb), lambda c, t: (0, c))],
        out_shape=[jax.ShapeDtypeStruct((K, C), f32), jax.ShapeDtypeStruct((1, C), f32)],
        compiler_params=_params(("parallel", "arbitrary")),
    )(dy, x, x)


def _expm1(z):
    poly = z * (1.0 + z * (0.5 + z * (1.0 / 6.0 + z * (1.0 / 24.0 + z * (1.0 / 120.0)))))
    return jnp.where(jnp.abs(z) < 0.1, poly, jnp.exp(z) - 1.0)


def _lru_pt(xc, rp, ip, lam, first):
    r = jax.nn.sigmoid(rp)
    i = jax.nn.sigmoid(ip)
    log_a = -LRU_C * r * jax.nn.softplus(-lam)
    a = jnp.exp(log_a)
    mult = jnp.sqrt(jnp.maximum(-_expm1(2.0 * log_a), 0.0))
    mult = jnp.where(first, 1.0, mult)
    return a, mult * i * xc


def _first_mask(shape, t):
    return jnp.logical_and(_rows(shape) == 0, t == 0)


def _lru_gates_fwd(xc, wa, ba, wx, bx, lam, name):
    T, W = xc.shape
    hd = W // N_HEADS
    tt = _tile(T, 512, 8)

    def body(xc_ref, wa_ref, ba_ref, wx_ref, bx_ref, lam_ref, a_ref, u_ref):
        t = pl.program_id(1)
        x = xc_ref[...]
        rp = _mxdot(x, wa_ref[...]) + ba_ref[...]
        ip = _mxdot(x, wx_ref[...]) + bx_ref[...]
        a, u = _lru_pt(x, rp, ip, lam_ref[...], _first_mask(x.shape, t))
        a_ref[...] = a
        u_ref[...] = u

    wspec = pl.BlockSpec((None, hd, hd), lambda h, t: (h, 0, 0))
    bspec = pl.BlockSpec((None, 1, hd), lambda h, t: (h, 0, 0))
    tspec = pl.BlockSpec((tt, hd), lambda h, t: (t, h))
    return pl.pallas_call(
        body, name=name, grid=(N_HEADS, T // tt),
        in_specs=[tspec, wspec, bspec, wspec, bspec, pl.BlockSpec((1, hd), lambda h, t: (0, h))],
        out_specs=[tspec, tspec], out_shape=[jax.ShapeDtypeStruct((T, W), f32)] * 2,
        compiler_params=_params(("parallel", "parallel")),
    )(xc, wa, ba, wx, bx, lam)


def _lru_scan_fwd(a, u, gate, gcol0, name):
    T, W = a.shape
    tt, cb = _tile(T, 256, 8), _tile(W, 512, LANE)
    g0 = gcol0 // cb
    assert gcol0 % cb == 0

    def body(a_ref, u_ref, g_ref, h_ref, y_ref, carry_ref):
        t = pl.program_id(1)

        @pl.when(t == 0)
        def _():
            carry_ref[...] = jnp.zeros_like(carry_ref)

        def step(i, h):
            base = pl.multiple_of(i * 8, 8)
            a8, u8 = a_ref[pl.ds(base, 8), :], u_ref[pl.ds(base, 8), :]
            rows = []
            for j in range(8):
                h = a8[j:j + 1, :] * h + u8[j:j + 1, :]
                rows.append(h)
            h_ref[pl.ds(base, 8), :] = jnp.concatenate(rows, axis=0)
            return h

        h_last = lax.fori_loop(0, tt // 8, step, carry_ref[0:1, :])
        carry_ref[...] = jnp.broadcast_to(h_last, carry_ref.shape)
        y_ref[...] = (h_ref[...] * jax.nn.gelu(g_ref[...])).astype(y_ref.dtype)

    tspec = pl.BlockSpec((tt, cb), lambda c, t: (t, c))
    return pl.pallas_call(
        body, name=name, grid=(W // cb, T // tt),
        in_specs=[tspec, tspec, pl.BlockSpec((tt, cb), lambda c, t: (t, g0 + c))],
        out_specs=[tspec, tspec],
        out_shape=[jax.ShapeDtypeStruct((T, W), f32), jax.ShapeDtypeStruct((T, W), _MXU)],
        scratch_shapes=[pltpu.VMEM((8, cb), f32)],
        compiler_params=_params(("parallel", "arbitrary")),
    )(a, u, gate)


def _lru_scan_bwd(dy, dcol0, gate, gcol0, h, a, name):
    T, W = a.shape
    tt, cb = _tile(T, 256, 8), _tile(W, 512, LANE)
    nt = T // tt
    d0, g0 = dcol0 // cb, gcol0 // cb
    assert dcol0 % cb == 0 and gcol0 % cb == 0

    def body(dy_ref, g_ref, h_ref, a_ref, lam_ref, dg_ref, carry_ref, dh_ref):
        t = pl.program_id(1)

        @pl.when(t == 0)
        def _():
            carry_ref[...] = jnp.zeros_like(carry_ref)

        _, vjp = jax.vjp(lambda hh, gg: hh * jax.nn.gelu(gg), h_ref[...], g_ref[...])
        dh, dg = vjp(dy_ref[...])
        dg_ref[...] = dg.astype(dg_ref.dtype)
        dh_ref[...] = dh

        def step(i, c):
            base = pl.multiple_of((tt // 8 - 1 - i) * 8, 8)
            a8, d8 = a_ref[pl.ds(base, 8), :], dh_ref[pl.ds(base, 8), :]
            rows = [None] * 8
            for j in range(7, -1, -1):
                lam = d8[j:j + 1, :] + c
                c = a8[j:j + 1, :] * lam
                rows[j] = lam
            lam_ref[pl.ds(base, 8), :] = jnp.concatenate(rows, axis=0)
            return c

        c_last = lax.fori_loop(0, tt // 8, step, carry_ref[0:1, :])
        carry_ref[...] = jnp.broadcast_to(c_last, carry_ref.shape)

    rev = lambda c, t: (nt - 1 - t, c)
    tspec = pl.BlockSpec((tt, cb), rev)
    return pl.pallas_call(
        body, name=name, grid=(W // cb, nt),
        in_specs=[pl.BlockSpec((tt, cb), lambda c, t: (nt - 1 - t, d0 + c)),
                  pl.BlockSpec((tt, cb), lambda c, t: (nt - 1 - t, g0 + c)), tspec, tspec],
        out_specs=[tspec, tspec], out_shape=[jax.ShapeDtypeStruct((T, W), f32), jax.ShapeDtypeStruct((T, W), _MXU)],
        scratch_shapes=[pltpu.VMEM((8, cb), f32), pltpu.VMEM((tt, cb), f32)],
        compiler_params=_params(("parallel", "arbitrary")),
    )(dy, gate, h, a)


def _lru_gates_bwd(lam_g, h, xc, wa, ba, wx, bx, lam, name):
    T, W = xc.shape
    hd = W // N_HEADS
    tt = _tile(T, 512, 8)

    def body(lg_ref, h_ref, hp_ref, xc_ref, wa_ref, ba_ref, wx_ref, bx_ref, lam_ref,
             dxc_ref, dwa_ref, dba_ref, dwx_ref, dbx_ref, dlam_ref):
        t = pl.program_id(1)

        @pl.when(t == 0)
        def _():
            for r in (dwa_ref, dba_ref, dwx_ref, dbx_ref, dlam_ref):
                r[...] = jnp.zeros_like(r)

        x = xc_ref[...]
        lg = lg_ref[...]
        h_prev = _shift_down(h_ref[...], jnp.where(t > 0, hp_ref[...], 0.0), 1)
        rp = _mxdot(x, wa_ref[...]) + ba_ref[...]
        ip = _mxdot(x, wx_ref[...]) + bx_ref[...]
        first = _first_mask(x.shape, t)
        _, vjp = jax.vjp(lambda xx, r_, i_, l_: _lru_pt(xx, r_, i_, l_, first), x, rp, ip, lam_ref[...])
        dx, drp, dip, dl = vjp((lg * h_prev, lg))
        dxc_ref[...] = dx + _mxdot(drp, wa_ref[...], _NT) + _mxdot(dip, wx_ref[...], _NT)
        dwa_ref[...] += _mxdot(x, drp, _TN)
        dwx_ref[...] += _mxdot(x, dip, _TN)
        dba_ref[...] += jnp.sum(drp, axis=0, keepdims=True)
        dbx_ref[...] += jnp.sum(dip, axis=0, keepdims=True)
        dlam_ref[...] += dl

    wspec = pl.BlockSpec((None, hd, hd), lambda h_, t: (h_, 0, 0))
    bspec = pl.BlockSpec((None, 1, hd), lambda h_, t: (h_, 0, 0))
    tspec = pl.BlockSpec((tt, hd), lambda h_, t: (t, h_))
    pspec = pl.BlockSpec((tt, hd), lambda h_, t: (jnp.maximum(t - 1, 0), h_))
    lspec = pl.BlockSpec((1, hd), lambda h_, t: (0, h_))
    return pl.pallas_call(
        body, name=name, grid=(N_HEADS, T // tt),
        in_specs=[tspec, tspec, pspec, tspec, wspec, bspec, wspec, bspec, lspec],
        out_specs=[tspec, wspec, bspec, wspec, bspec, lspec],
        out_shape=[jax.ShapeDtypeStruct((T, W), f32), jax.ShapeDtypeStruct((N_HEADS, hd, hd), f32),
                   jax.ShapeDtypeStruct((N_HEADS, 1, hd), f32), jax.ShapeDtypeStruct((N_HEADS, hd, hd), f32),
                   jax.ShapeDtypeStruct((N_HEADS, 1, hd), f32), jax.ShapeDtypeStruct((1, W), f32)],
        compiler_params=_params(("parallel", "arbitrary")),
    )(lam_g, h, h, xc, wa, ba, wx, bx, lam)


def _hgrn_pt(z, qp, lb):
    sig = jax.nn.sigmoid(z)
    fg = lb + (1.0 - lb) * sig
    logf = jnp.log(jnp.maximum(fg, F_FLOOR))
    k = (1.0 - lb) * (1.0 - sig)
    return logf, k, jax.nn.silu(qp)


def _hgrn_out(o, gp, ng):
    on = o * lax.rsqrt(jnp.mean(o * o, axis=-1, keepdims=True) + EPS)
    return on * ng * jax.nn.silu(gp)


def _cumsum_rows(x):
    n, row, sh = x.shape[0], _rows(x.shape), 1
    while sh < n:
        x = x + jnp.where(row >= sh, pltpu.roll(x, sh, 0), 0.0)
        sh *= 2
    return x


def _rev_cumsum_rows(x):
    n, row, sh = x.shape[0], _rows(x.shape), 1
    while sh < n:
        x = x + jnp.where(row < n - sh, pltpu.roll(x, n - sh, 0), 0.0)
        sh *= 2
    return x


def _hgrn_fwd(proj, qcol, fcol, vcol, gcol, lb, ng, name):
    T = proj.shape[0]
    W = lb.shape[1]
    hd = W // N_HEADS
    bw = HGRN_HPB * hd
    tt = _tile(T, 256, SUB)
    ns = tt // SUB
    q0, f0, v0, g0 = qcol // bw, fcol // bw, vcol // bw, gcol // bw

    def body(q_ref, f_ref, v_ref, g_ref, lb_ref, ng_ref, y_ref, o_ref, st_ref, s_ref):
        t = pl.program_id(1)

        @pl.when(t == 0)
        def _():
            s_ref[...] = jnp.zeros_like(s_ref)

        row = _rows((SUB, hd))

        def sub(j, carry):
            rs = pl.ds(pl.multiple_of(j * SUB, SUB), SUB)
            for hh in range(HGRN_HPB):
                cs = slice(hh * hd, (hh + 1) * hd)
                logf, k, qf = _hgrn_pt(f_ref[rs, cs], q_ref[rs, cs], lb_ref[:, cs])
                v = v_ref[rs, cs]
                b = _cumsum_rows(logf)
                b_last = b[SUB - 1:SUB, :]
                S = s_ref[hh]
                st_ref[hh, j] = S
                intra = jnp.zeros((SUB, hd), f32)
                for r in range(SUB):
                    e = jnp.exp(jnp.minimum(b[r:r + 1, :] - b, 0.0))
                    m = jnp.where(row <= r, qf[r:r + 1, :] * k * e, 0.0)
                    p = jnp.sum(m, axis=1, keepdims=True)
                    intra = jnp.where(row == r, jnp.sum(p * v, axis=0, keepdims=True), intra)
                o_ref[rs, cs] = _mxdot(qf * jnp.exp(b), S, _NT) + intra
                s_ref[hh] = S * jnp.exp(b_last) + _mxdot(v, k * jnp.exp(b_last - b), _TN)
            return carry

        lax.fori_loop(0, ns, sub, 0)
        for hh in range(HGRN_HPB):
            cs = slice(hh * hd, (hh + 1) * hd)
            y_ref[:, cs] = _hgrn_out(o_ref[:, cs], g_ref[:, cs], ng_ref[:, cs]).astype(y_ref.dtype)

    def slab(c0):
        return pl.BlockSpec((tt, bw), functools.partial(lambda h, t, c0: (t, c0 + h), c0=c0))

    hspec = pl.BlockSpec((tt, bw), lambda h, t: (t, h))
    cspec = pl.BlockSpec((1, bw), lambda h, t: (0, h))
    return pl.pallas_call(
        body, name=name, grid=(N_HEADS // HGRN_HPB, T // tt),
        in_specs=[slab(q0), slab(f0), slab(v0), slab(g0), cspec, cspec],
        out_specs=[hspec, hspec, pl.BlockSpec((HGRN_HPB, ns, hd, hd), lambda h, t: (h, t, 0, 0))],
        out_shape=[jax.ShapeDtypeStruct((T, W), _MXU), jax.ShapeDtypeStruct((T, W), f32),
                   jax.ShapeDtypeStruct((N_HEADS, T // SUB, hd, hd), f32)],
        scratch_shapes=[pltpu.VMEM((HGRN_HPB, hd, hd), f32)],
        compiler_params=_params(("parallel", "arbitrary")),
    )(proj, proj, proj, proj, lb, ng)


def _hgrn_bwd(dy, dcol, proj, qcol, fcol, vcol, gcol, lb, ng, o, states, name):
    T = proj.shape[0]
    W = lb.shape[1]
    hd = W // N_HEADS
    bw = HGRN_HPB * hd
    tt = _tile(T, 256, SUB)
    ns, nt = tt // SUB, T // tt
    q0, f0, v0, g0, d0 = qcol // bw, fcol // bw, vcol // bw, gcol // bw, dcol // bw

    def body(dy_ref, q_ref, f_ref, v_ref, g_ref, lb_ref, ng_ref, o_ref, st_ref,
             dq_ref, df_ref, dv_ref, dg_ref, dlb_ref, dng_ref, ds_ref, do_ref):
        t = pl.program_id(1)

        @pl.when(t == 0)
        def _():
            ds_ref[...] = jnp.zeros_like(ds_ref)
            dlb_ref[...] = jnp.zeros_like(dlb_ref)
            dng_ref[...] = jnp.zeros_like(dng_ref)

        for hh in range(HGRN_HPB):
            cs = slice(hh * hd, (hh + 1) * hd)
            _, vjp_out = jax.vjp(_hgrn_out, o_ref[:, cs], g_ref[:, cs], ng_ref[:, cs])
            do, dgp, dng = vjp_out(dy_ref[:, cs])
            do_ref[:, cs] = do
            dg_ref[:, cs] = dgp.astype(dg_ref.dtype)
            dng_ref[:, cs] += dng
        row = _rows((SUB, hd))

        def sub(jj, carry):
            j = ns - 1 - jj
            rs = pl.ds(pl.multiple_of(j * SUB, SUB), SUB)
            for hh in range(HGRN_HPB):
                cs = slice(hh * hd, (hh + 1) * hd)
                (logf, k, qf), vjp_pt = jax.vjp(_hgrn_pt, f_ref[rs, cs], q_ref[rs, cs], lb_ref[:, cs])
                v = v_ref[rs, cs]
                dO = do_ref[rs, cs]
                b = _cumsum_rows(logf)
                b_last = b[SUB - 1:SUB, :]
                S = st_ref[hh, j]
                dS = ds_ref[hh]
                eb = jnp.exp(b)
                kd = jnp.exp(b_last - b)
                d = jnp.exp(b_last)
                qe, ke = qf * eb, k * kd
                dqe = _mxdot(dO, S, _NN)
                dke = _mxdot(v, dS, _NN)
                dv = _mxdot(ke, dS, _NT)
                dd = jnp.sum(dS * S, axis=0, keepdims=True)
                ds_ref[hh] = dS * d + _mxdot(dO, qe, _TN)
                dq_i = jnp.zeros((SUB, hd), f32)
                dk_i = jnp.zeros((SUB, hd), f32)
                for r in range(SUB):
                    em = jnp.where(row <= r, jnp.exp(jnp.minimum(b[r:r + 1, :] - b, 0.0)), 0.0)
                    ke_r = k * em
                    qr, dor = qf[r:r + 1, :], dO[r:r + 1, :]
                    p = jnp.sum(qr * ke_r, axis=1, keepdims=True)
                    dv = dv + p * dor
                    dp = jnp.sum(dor * v, axis=1, keepdims=True)
                    dq_i = jnp.where(row == r, jnp.sum(dp * ke_r, axis=0, keepdims=True), dq_i)
                    dk_i = dk_i + dp * (qr * em)
                dqf = dqe * eb + dq_i
                dk = dke * kd + dk_i
                dke_ke = dke * ke
                db = dqe * qe - dke_ke + qf * dq_i - k * dk_i
                db_last = jnp.sum(dke_ke, axis=0, keepdims=True) + dd * d
                db = db + jnp.where(row == SUB - 1, db_last, 0.0)
                dz, dqp, dlb = vjp_pt((_rev_cumsum_rows(db), dk, dqf))
                dq_ref[rs, cs] = dqp.astype(dq_ref.dtype)
                df_ref[rs, cs] = dz.astype(df_ref.dtype)
                dv_ref[rs, cs] = dv.astype(dv_ref.dtype)
                dlb_ref[:, cs] += dlb
            return carry

        lax.fori_loop(0, ns, sub, 0)

    def slab(c0):
        return pl.BlockSpec((tt, bw), functools.partial(lambda h, t, c0: (nt - 1 - t, c0 + h), c0=c0))

    hspec = pl.BlockSpec((tt, bw), lambda h, t: (nt - 1 - t, h))
    cspec = pl.BlockSpec((1, bw), lambda h, t: (0, h))
    return pl.pallas_call(
        body, name=name, grid=(N_HEADS // HGRN_HPB, nt),
        in_specs=[slab(d0), slab(q0), slab(f0), slab(v0), slab(g0), cspec, cspec, hspec,
                  pl.BlockSpec((HGRN_HPB, ns, hd, hd), lambda h, t: (h, nt - 1 - t, 0, 0))],
        out_specs=[hspec, hspec, hspec, hspec, cspec, cspec],
        out_shape=[jax.ShapeDtypeStruct((T, W), _MXU)] * 4 + [jax.ShapeDtypeStruct((1, W), f32)] * 2,
        scratch_shapes=[pltpu.VMEM((HGRN_HPB, hd, hd), f32), pltpu.VMEM((tt, bw), f32)],
        compiler_params=_params(("parallel", "arbitrary")),
    )(dy, proj, proj, proj, proj, lb, ng, o, states)


def _lower_bounds(logits, name):
    def fn(lg):
        sm = jax.nn.softmax(lg, axis=0)
        run, rows_ = None, []
        for j in range(lg.shape[0]):
            run = sm[j:j + 1, :] if run is None else run + sm[j:j + 1, :]
            rows_.append(run - sm[0:1, :])
        return jnp.concatenate(rows_, axis=0)
    return fn


def _lb_fwd(logits, name):
    fn = _lower_bounds(logits, name)

    def body(l_ref, o_ref):
        o_ref[...] = fn(l_ref[...])

    return pl.pallas_call(body, name=name, out_shape=jax.ShapeDtypeStruct(logits.shape, f32))(logits)


def _lb_bwd(logits, dlb, name):
    fn = _lower_bounds(logits, name)

    def body(l_ref, d_ref, o_ref):
        _, vjp = jax.vjp(fn, l_ref[...])
        o_ref[...] = vjp(d_ref[...])[0]

    return pl.pallas_call(body, name=name, out_shape=jax.ShapeDtypeStruct(logits.shape, f32))(logits, dlb)


def _ln_silu(d, g, b):
    mu = jnp.mean(d, axis=-1, keepdims=True)
    xc = d - mu
    y = xc * lax.rsqrt(jnp.mean(xc * xc, axis=-1, keepdims=True) + EPS)
    return jax.nn.silu(y * g + b)


def _ln_fwd(d, g, b, name):
    T, W = d.shape
    tt = _tile(T, 256, 8)

    def body(d_ref, g_ref, b_ref, o_ref):
        o_ref[...] = _ln_silu(d_ref[...], g_ref[...], b_ref[...]).astype(o_ref.dtype)

    return pl.pallas_call(
        body, name=name, grid=(T // tt,),
        in_specs=[pl.BlockSpec((tt, W), lambda t: (t, 0))] + [pl.BlockSpec((1, W), lambda t: (0, 0))] * 2,
        out_specs=pl.BlockSpec((tt, W), lambda t: (t, 0)), out_shape=jax.ShapeDtypeStruct((T, W), _MXU),
        compiler_params=_params(("parallel",)),
    )(d, g, b)


def _ln_bwd(dy, dcol0, d, g, b, name):
    T, W = d.shape
    tt = _tile(T, 256, 8)
    c0 = dcol0 // W
    assert dcol0 % W == 0

    def body(dy_ref, d_ref, g_ref, b_ref, dd_ref, dg_ref, db_ref):
        @pl.when(pl.program_id(0) == 0)
        def _():
            dg_ref[...] = jnp.zeros_like(dg_ref)
            db_ref[...] = jnp.zeros_like(db_ref)

        _, vjp = jax.vjp(_ln_silu, d_ref[...], g_ref[...], b_ref[...])
        dd, dg, db = vjp(dy_ref[...])
        dd_ref[...] = dd
        dg_ref[...] += dg
        db_ref[...] += db

    cspec = pl.BlockSpec((1, W), lambda t: (0, 0))
    return pl.pallas_call(
        body, name=name, grid=(T // tt,),
        in_specs=[pl.BlockSpec((tt, W), lambda t: (t, c0)), pl.BlockSpec((tt, W), lambda t: (t, 0)), cspec, cspec],
        out_specs=[pl.BlockSpec((tt, W), lambda t: (t, 0)), cspec, cspec],
        out_shape=[jax.ShapeDtypeStruct((T, W), f32), jax.ShapeDtypeStruct((1, W), f32), jax.ShapeDtypeStruct((1, W), f32)],
        compiler_params=_params(("arbitrary",)),
    )(dy, d, g, b)


def _adamw(w, m, v, parts, name):
    R, C = w.shape
    tr, tc = _tile(R, 512, 8), _tile(C, 512, LANE)
    npart = len(parts)

    def body(*refs):
        w_ref, m_ref, v_ref = refs[:3]
        g_ref, d_ref, mo_ref, vo_ref = refs[3 + npart:]
        g = refs[3][...].astype(f32)
        for p_ref in refs[4:3 + npart]:
            g = g + p_ref[...].astype(f32)
        mm = ADAM_B1 * m_ref[...] + (1.0 - ADAM_B1) * g
        vv = ADAM_B2 * v_ref[...] + (1.0 - ADAM_B2) * jnp.square(g)
        m_hat = mm / (1.0 - ADAM_B1 ** ADAM_STEP)
        v_hat = vv / (1.0 - ADAM_B2 ** ADAM_STEP)
        g_ref[...] = g
        d_ref[...] = -ADAM_LR * (m_hat / (jnp.sqrt(v_hat) + ADAM_EPS) + ADAM_WD * w_ref[...])
        mo_ref[...] = mm
        vo_ref[...] = vv

    spec = pl.BlockSpec((tr, tc), lambda i, j: (i, j))
    return pl.pallas_call(
        body, name=name, grid=(R // tr, C // tc), in_specs=[spec] * (3 + npart), out_specs=[spec] * 4,
        out_shape=[jax.ShapeDtypeStruct((R, C), f32)] * 4, compiler_params=_params(("parallel", "parallel")),
    )(w, m, v, *parts)


def _adamw_layer(w3, m3, v3, layer, parts, prev, name):
    L, R, C = w3.shape
    tr, tc = _tile(R, 128, 8), _tile(C, 2048, LANE)
    npart = len(parts)

    def body(*refs):
        w_ref, m_ref, v_ref = refs[:3]
        g_ref, d_ref, mo_ref, vo_ref = refs[3 + npart + 4:]
        g = refs[3][...].astype(f32)
        for p_ref in refs[4:3 + npart]:
            g = g + p_ref[...].astype(f32)
        mm = ADAM_B1 * m_ref[...] + (1.0 - ADAM_B1) * g
        vv = ADAM_B2 * v_ref[...] + (1.0 - ADAM_B2) * jnp.square(g)
        m_hat = mm / (1.0 - ADAM_B1 ** ADAM_STEP)
        v_hat = vv / (1.0 - ADAM_B2 ** ADAM_STEP)
        g_ref[...] = g
        d_ref[...] = -ADAM_LR * (m_hat / (jnp.sqrt(v_hat) + ADAM_EPS) + ADAM_WD * w_ref[...])
        mo_ref[...] = mm
        vo_ref[...] = vv

    spec3 = pl.BlockSpec((None, tr, tc), lambda i, j: (layer, i, j))
    spec2 = pl.BlockSpec((tr, tc), lambda i, j: (i, j))
    return pl.pallas_call(
        body, name=name, grid=(R // tr, C // tc), in_specs=[spec3] * 3 + [spec2] * npart + [_ANY] * 4,
        out_specs=[spec3] * 4, out_shape=[jax.ShapeDtypeStruct((L, R, C), f32)] * 4,
        input_output_aliases={3 + npart + k: k for k in range(4)},
        compiler_params=_params(("parallel", "parallel")),
    )(w3, m3, v3, *parts, *prev)


def _sum_own_recv(g, chip, col, recv, name):
    S, R, C = recv.shape
    tr, tc = _tile(R, 256, 8), _tile(C, 2048, LANE)
    nbr, nbc = R // tr, C // tc

    def body(chip_ref, o_ref, r_ref, out_ref):
        acc = o_ref[...].astype(f32)
        for s in range(S):
            acc = acc + r_ref[s].astype(f32)
        out_ref[...] = acc

    if col:
        own_map = lambda i, j, c: (i, c[0] * nbc + j)
    else:
        own_map = lambda i, j, c: (c[0] * nbr + i, j)
    grid_spec = pltpu.PrefetchScalarGridSpec(
        num_scalar_prefetch=1, grid=(nbr, nbc),
        in_specs=[pl.BlockSpec((tr, tc), own_map), pl.BlockSpec((S, tr, tc), lambda i, j, c: (0, i, j))],
        out_specs=pl.BlockSpec((tr, tc), lambda i, j, c: (i, j)))
    return pl.pallas_call(
        body, name=name, grid_spec=grid_spec, out_shape=jax.ShapeDtypeStruct((R, C), f32),
        compiler_params=_params(("parallel", "parallel")),
    )(chip, g, recv)


def _sum_slots(recv, name):
    S, R, C = recv.shape
    tr, tc = _tile(R, 512, 8), _tile(C, 512, LANE)

    def body(r_ref, o_ref):
        acc = r_ref[0].astype(f32)
        for s in range(1, S):
            acc = acc + r_ref[s].astype(f32)
        o_ref[...] = acc

    return pl.pallas_call(
        body, name=name, grid=(R // tr, C // tc), in_specs=[pl.BlockSpec((S, tr, tc), lambda i, j: (0, i, j))],
        out_specs=pl.BlockSpec((tr, tc), lambda i, j: (i, j)), out_shape=jax.ShapeDtypeStruct((R, C), f32),
        compiler_params=_params(("parallel", "parallel")),
    )(recv)


_CHIP_FLIPS = ((1, 0), (0, 1), (1, 1))
_ANY = pl.BlockSpec(memory_space=pl.ANY)


def _me():
    return lax.axis_index("x"), lax.axis_index("y"), lax.axis_index("c")


def _allgather_chips(local, col, name):
    L, R, C = local.shape
    out_shape = (L, R, 4 * C) if col else (L, 4 * R, C)

    def body(loc_ref, out_ref, send_sems, recv_sems, local_sem):
        x, y, c = _me()

        def block(px, py):
            s = 2 * px + py
            if col:
                return out_ref.at[:, :, pl.ds(s * C, C)]
            return out_ref.at[:, pl.ds(s * R, R), :]

        mine = pltpu.make_async_copy(loc_ref, block(x, y), local_sem)
        mine.start()
        sends = []
        for j, (fx, fy) in enumerate(_CHIP_FLIPS):
            cp = pltpu.make_async_remote_copy(src_ref=loc_ref, dst_ref=block(x, y), send_sem=send_sems.at[j],
                                              recv_sem=recv_sems.at[j], device_id=(x ^ fx, y ^ fy, c), device_id_type=MESH)
            cp.start()
            sends.append(cp)
        for j, (fx, fy) in enumerate(_CHIP_FLIPS):
            pltpu.make_async_remote_copy(src_ref=loc_ref, dst_ref=block(x ^ fx, y ^ fy), send_sem=send_sems.at[j],
                                         recv_sem=recv_sems.at[j], device_id=(x ^ fx, y ^ fy, c),
                                         device_id_type=MESH).wait_recv()
        for cp in sends:
            cp.wait_send()
        mine.wait()

    return pl.pallas_call(
        body, name=name, in_specs=[_ANY], out_specs=_ANY, out_shape=jax.ShapeDtypeStruct(out_shape, local.dtype),
        scratch_shapes=[pltpu.SemaphoreType.DMA((3,)), pltpu.SemaphoreType.DMA((3,)), pltpu.SemaphoreType.DMA],
    )(local)


def _gather_all(v, name):
    def body(v_ref, out_ref, send_sems, recv_sems, local_sem):
        x, y, c = _me()
        me_slot = out_ref.at[4 * x + 2 * y + c]
        mine = pltpu.make_async_copy(v_ref, me_slot, local_sem)
        mine.start()
        sends = []
        for m in range(1, 8):
            peer = (x ^ (m >> 2), y ^ ((m >> 1) & 1), c ^ (m & 1))
            cp = pltpu.make_async_remote_copy(src_ref=v_ref, dst_ref=me_slot, send_sem=send_sems.at[m - 1],
                                              recv_sem=recv_sems.at[m - 1], device_id=peer, device_id_type=MESH)
            cp.start()
            sends.append(cp)
        for m in range(1, 8):
            peer = (x ^ (m >> 2), y ^ ((m >> 1) & 1), c ^ (m & 1))
            pltpu.make_async_remote_copy(src_ref=v_ref, dst_ref=out_ref.at[4 * peer[0] + 2 * peer[1] + peer[2]],
                                         send_sem=send_sems.at[m - 1], recv_sem=recv_sems.at[m - 1], device_id=peer,
                                         device_id_type=MESH).wait_recv()
        for cp in sends:
            cp.wait_send()
        mine.wait()

    return pl.pallas_call(
        body, name=name, in_specs=[_ANY], out_specs=_ANY, out_shape=jax.ShapeDtypeStruct((8,) + v.shape, v.dtype),
        scratch_shapes=[pltpu.SemaphoreType.DMA((7,)), pltpu.SemaphoreType.DMA((7,)), pltpu.SemaphoreType.DMA],
    )(v)


_HBM = pl.BlockSpec(memory_space=pltpu.HBM)
_SEM = pl.BlockSpec(memory_space=pltpu.SEMAPHORE)
_EFFECT = pltpu.SideEffectType.DATAFLOW_SIDE_EFFECTING


def _chip_block(ref, s, col, n):
    return ref.at[:, pl.ds(s * n, n)] if col else ref.at[pl.ds(s * n, n), :]


_NSEM = {'gather': 7, 'scatter': 6, 'swap': 2}


def _xchg_copies(kind, src_ref, land_ref, col, sems):
    x, y, c = _me()
    if kind == 'swap':
        return [pltpu.make_async_remote_copy(src_ref=src_ref, dst_ref=land_ref, send_sem=sems[0], recv_sem=sems[1],
                                             device_id=(x, y, 1 - c), device_id_type=MESH)]
    cps = []
    if kind == 'gather':
        n = src_ref.shape[1] if col else src_ref.shape[0]
        cps.append(pltpu.make_async_copy(src_ref, _chip_block(land_ref, 2 * x + y, col, n), sems[6]))
    for j, (fx, fy) in enumerate(_CHIP_FLIPS):
        px, py = x ^ fx, y ^ fy
        if kind == 'gather':
            n = src_ref.shape[1] if col else src_ref.shape[0]
            src, dst = src_ref, _chip_block(land_ref, 2 * x + y, col, n)
        else:
            n = land_ref.shape[2] if col else land_ref.shape[1]
            src, dst = _chip_block(src_ref, 2 * px + py, col, n), land_ref.at[j]
        cps.append(pltpu.make_async_remote_copy(src_ref=src, dst_ref=dst, send_sem=sems[2 * j], recv_sem=sems[2 * j + 1],
                                                device_id=(px, py, c), device_id_type=MESH))
    return cps


def _xchg_start(kind, srcs, lands, cols, after, name):
    n = len(srcs)
    per = _NSEM[kind]
    nsem = per * n

    def body(*refs):
        src_refs, land_refs = refs[:n], refs[n:2 * n]
        sems = refs[2 * n + 1:2 * n + 1 + nsem]
        token = refs[-1]
        for i in range(n):
            for cp in _xchg_copies(kind, src_refs[i], land_refs[i], cols[i], sems[per * i:per * i + per]):
                cp.start()
        token[...] = jnp.zeros_like(token)

    hbm = lambda a: pltpu.HBM(a.shape, a.dtype)
    outs = pl.pallas_call(
        body, name=name,
        out_shape=tuple([pltpu.SemaphoreType.DMA(())] * nsem + [hbm(a) for a in srcs] + [hbm(a) for a in lands]
                        + [jax.ShapeDtypeStruct((8, LANE), f32)]),
        in_specs=tuple([_HBM] * (2 * n) + [_ANY]),
        out_specs=tuple([_SEM] * nsem + [_HBM] * (2 * n) + [pl.BlockSpec(memory_space=pltpu.VMEM)]),
        input_output_aliases={i: nsem + i for i in range(2 * n)},
        compiler_params=pltpu.CompilerParams(has_side_effects=_EFFECT),
    )(*[pltpu.with_memory_space_constraint(a, pltpu.HBM) for a in list(srcs) + list(lands)], after)
    return outs[:nsem], outs[nsem:nsem + n], outs[nsem + n:nsem + 2 * n], outs[-1]


def _xchg_wait(kind, started, cols, after, name):
    sems, srcs, lands, _ = started
    n = len(srcs)
    per = _NSEM[kind]
    nsem = per * n

    def body(*refs):
        src_refs, land_refs = refs[:n], refs[n:2 * n]
        sem_refs = refs[2 * n:2 * n + nsem]
        for i in range(n):
            for cp in _xchg_copies(kind, src_refs[i], land_refs[i], cols[i], sem_refs[per * i:per * i + per]):
                if cp.is_remote:
                    cp.wait_send()
                    cp.wait_recv()
                else:
                    cp.wait()

    hbm = lambda a: pltpu.HBM(a.shape, a.dtype)
    outs = pl.pallas_call(
        body, name=name, out_shape=tuple([hbm(a) for a in srcs] + [hbm(a) for a in lands]),
        in_specs=tuple([_HBM] * (2 * n) + [_SEM] * nsem + [_ANY]), out_specs=tuple([_HBM] * (2 * n)),
        input_output_aliases={i: i for i in range(2 * n)},
        compiler_params=pltpu.CompilerParams(has_side_effects=_EFFECT),
    )(*srcs, *lands, *sems, after)
    return outs[:n], outs[n:]


_PACK_ROWS = 512


def _pack(arrs):
    flat = jnp.concatenate([a.reshape(-1).astype(f32) for a in arrs])
    n = flat.shape[0]
    rows = -(-n // (_PACK_ROWS * LANE)) * _PACK_ROWS
    return jnp.pad(flat, (0, rows * LANE - n)).reshape(rows, LANE)


def _unpack(buf, shapes):
    flat, outs, off = buf.reshape(-1), [], 0
    for s in shapes:
        n = 1
        for d_ in s:
            n *= d_
        outs.append(flat[off:off + n].reshape(s))
        off += n
    return outs


def _step(p):
    x0 = p['x'][0]
    tgt = p['loss_target'][0]
    T, D = x0.shape
    W = D // 2
    depth = p['ln_mix_g'].shape[0]
    chip = 2 * lax.axis_index("x") + lax.axis_index("y")
    chip1 = jnp.reshape(chip, (1,)).astype(jnp.int32)

    def group_names(layer, grp):
        if grp == 'ffn':
            return [('ffn_w_gate', layer), ('ffn_w_up', layer), ('ffn_w_down', layer)]
        pre = 'ev' if layer % 2 == 0 else 'od'
        return [(pre + '_w_in', layer // 2), (pre + '_w_out', layer // 2)]

    def gather_start(layer, grp, after):
        srcs, lands, cols = [], [], []
        for n, l in group_names(layer, grp):
            loc = p[n][l].astype(_MXU)
            R, C = loc.shape
            col = _BIG_COL[n]
            land = lax.empty((R, 4 * C) if col else (4 * R, C), _MXU)
            srcs.append(loc)
            lands.append(land)
            cols.append(col)
        return _xchg_start('gather', srcs, lands, cols, after, name=f"ag_start_{grp}{layer}"), cols

    def gather_wait(started, layer, grp, after):
        st, cols = started
        _, lands = _xchg_wait('gather', st, cols, after, name=f"ag_wait_{grp}{layer}")
        return dict(zip([n for n, _ in group_names(layer, grp)], lands))

    groups = [(layer, grp) for layer in range(depth) for grp in ('mix', 'ffn')]
    pending = {}
    loose = []

    def take_weights(gi, after):
        wts = gather_wait(pending.pop(groups[gi]), *groups[gi], after)
        if gi + 2 < len(groups):
            pending[groups[gi + 2]] = gather_start(*groups[gi + 2], next(iter(wts.values())))
            loose.append(pending[groups[gi + 2]][0][3])
        tok = None
        while loose:
            t = loose.pop()
            tok = t if tok is None else tok + t
        return wts, tok

    small_rows = []
    cl = W // 4
    for n in _SMALL_SHARDED:
        small_rows.append(p[n].reshape(-1, cl))
    srows = [a.shape[0] for a in small_rows]
    spack = jnp.concatenate(small_rows, axis=0)
    spad = -(-spack.shape[0] // 8) * 8
    spack = jnp.pad(spack, ((0, spad - spack.shape[0]), (0, 0)))
    sfull = _allgather_chips(spack[None], False, name="ag_small")[0].reshape(4, spad, cl)
    pending[groups[0]] = gather_start(*groups[0], sfull)
    pending[groups[1]] = gather_start(*groups[1], pending[groups[0]][0][3])
    loose.append(pending[groups[1]][0][3])
    small = {}
    off = 0
    for n, r in zip(_SMALL_SHARDED, srows):
        blk = sfull[:, off:off + r, :]
        lead = p[n].shape[:-1]
        q = p[n].shape[-1] // cl
        blk = blk.reshape((4,) + lead + (q, cl))
        blk = jnp.moveaxis(blk, 0, len(lead))
        small[n] = blk.reshape(lead + (4 * q * cl,))
        off += r

    lbs = _lb_fwd(p['hgrn_lb_logits'], name="lb_fwd")

    def row(a, tok=None):
        a = a.reshape(1, -1)
        return a if tok is None else a + tok[0:1, 0:1]

    saved = []
    full = {}
    x = x0
    for layer in range(depth):
        j = layer // 2
        s = {'x_in': x}
        wts, tok = take_weights(2 * layer, x)
        full[layer] = wts
        h = _rmsnorm_fwd(x, row(p['ln_mix_g'][layer], tok), name="rms_fwd")
        s['h'] = h
        if layer % 2 == 0:
            proj = _matmul(h, wts['ev_w_in'], 'nn', f32, "mm_ev_in", bias=row(p['ev_b_in'][j]))
            xc = _conv_fwd(proj, 0, small['lru_conv_w'][j], row(p['lru_conv_b'][j]), W, name="lru_conv_fwd")
            ba, bx = p['lru_ba'][j][:, None, :], p['lru_bx'][j][:, None, :]
            a, u = _lru_gates_fwd(xc, p['lru_wa'][j], ba, p['lru_wx'][j], bx, row(p['lru_lambda'][j]), name="lru_gates_fwd")
            hl, y_a = _lru_scan_fwd(a, u, proj, W, name="lru_scan_fwd")
            y_b, o, states = _hgrn_fwd(proj, 2 * W, 3 * W, 4 * W, 5 * W, row(lbs[j]), row(p['hgrn_norm_g'][j]),
                                       name="hgrn_fwd")
            s.update(proj=proj, xc=xc, a=a, hl=hl, o=o, states=states)
            ycat = jnp.concatenate([y_a, y_b], axis=1)
            w_out = wts['ev_w_out']
        else:
            proj = _matmul(h, wts['od_w_in'], 'nn', f32, "mm_od_in", bias=row(small['od_b_in'][j]))
            pp = _pw(lambda a_, b_: a_ * b_, [(proj, W), (proj, 2 * W)], [f32], W, name="sc_mul")
            cp = _conv_fwd(pp, 0, small['sc_conv_w'][j], None, W, name="sc_conv_fwd")
            y_c = _pw(lambda a_, b_: a_ * b_, [(proj, 0), (cp, 0)], [_MXU], W, name="sc_out")
            glu = _pw(lambda a_, b_: a_ * jax.nn.sigmoid(b_), [(proj, 3 * W), (proj, 4 * W)], [f32], W, name="cf_glu")
            dcv = _conv_fwd(glu, 0, small['cf_conv_w'][j], row(small['cf_conv_b'][j]), W, name="cf_conv_fwd")
            y_d = _ln_fwd(dcv, row(small['cf_ln_g'][j]), row(small['cf_ln_b'][j]), name="cf_ln_fwd")
            s.update(proj=proj, pp=pp, cp=cp, glu=glu, dcv=dcv)
            ycat = jnp.concatenate([y_c, y_d], axis=1)
            w_out = wts['od_w_out']
        s['ycat'] = ycat
        x = _matmul(ycat, w_out, 'nn', f32, "mm_mix_out", add=x)
        s['x_mid'] = x
        wts, tok = take_weights(2 * layer + 1, x)
        full[layer].update(wts)
        h2 = _rmsnorm_fwd(x, row(p['ln_ffn_g'][layer], tok), name="rms_fwd")
        gate, up, act = _ffn_in(h2, wts['ffn_w_gate'], wts['ffn_w_up'], name="ffn_in")
        x =_matmul(act, wts['ffn_w_down'], 'nn', f32, "mm_ffn_out", add=x, tk=1408)
        s.update(h2=h2, gate=gate, up=up, act=act)
        saved.append(s)

    loss_b, dx, dg_final = _final_loss(x, row(p['ln_final_g']), tgt, name="final_loss")

    scat = {}
    tok = None

    def scatter_start(layer, grp, grads):
        srcs, lands, cols = [], [], []
        for (n, l), g in zip(group_names(layer, grp), grads, strict=True):
            R, C = p[n].shape[1:]
            srcs.append(g)
            lands.append(lax.empty((3, R, C), _WIRE))
            cols.append(_BIG_COL[n])
        st = _xchg_start('scatter', srcs, lands, cols, chip1, name=f"rs_start_{grp}{layer}")
        scat[(layer, grp)] = (st, cols)
        return st[3]

    gs = {n: [None] * p[n].shape[0] for n in _IN_NAMES[1:] if n not in _BIG and n != 'ln_final_g'}
    for layer in reversed(range(depth)):
        j = layer // 2
        s = saved[layer]
        F = s['gate'].shape[1]
        wts = full[layer]
        dgate, dup = _ffn_dact(dx, wts['ffn_w_down'], s['gate'], s['up'], name="ffn_dact")
        g_down = _matmul(s['act'], dx, 'tn', _WIRE, "mm_dw_down", tm=1408, tn=1024, tk=512)
        g_gate = _matmul(s['h2'], dgate, 'tn', _WIRE, "mm_dw_in", tm=1024, tn=1408, tk=512)
        g_up = _matmul(s['h2'], dup, 'tn', _WIRE, "mm_dw_in", tm=1024, tn=1408, tk=512)
        tok = scatter_start(layer, 'ffn', [g_gate, g_up, g_down])
        dh2 = _matmul(dgate, wts['ffn_w_gate'], 'nt', f32, "mm_ffn_dh", tk=1408, second=(dup, wts['ffn_w_up']))
        dx, gs['ln_ffn_g'][layer] = _rmsnorm_bwd(dh2, s['x_mid'], row(p['ln_ffn_g'][layer], tok), dx, name="rms_bwd")
        if layer % 2 == 0:
            w_out, w_in, n_out, n_in = wts['ev_w_out'], wts['ev_w_in'], 'ev_w_out', 'ev_w_in'
        else:
            w_out, w_in, n_out, n_in = wts['od_w_out'], wts['od_w_in'], 'od_w_out', 'od_w_in'
        dycat = _matmul(dx, w_out, 'nt', f32, "mm_mix_dy")
        g_out = _matmul(s['ycat'], dx, 'tn', _WIRE, "mm_dw_out", tm=1024, tn=1024, tk=512)
        proj = s['proj']
        if layer % 2 == 0:
            ba, bx = p['lru_ba'][j][:, None, :], p['lru_bx'][j][:, None, :]
            lam_g, dgate_a = _lru_scan_bwd(dycat, 0, proj, W, s['hl'], s['a'], name="lru_scan_bwd")
            dxc, dwa, dba, dwx, dbx, dlam = _lru_gates_bwd(lam_g, s['hl'], s['xc'], p['lru_wa'][j], ba, p['lru_wx'][j], bx,
                                                           row(p['lru_lambda'][j]), name="lru_gates_bwd")
            dxa = _conv_bwd_dx(dxc, small['lru_conv_w'][j], name="lru_conv_dx", out_dtype=_MXU)
            dcw, dcb = _conv_bwd_dw(dxc, proj, 0, small['lru_conv_w'][j].shape[0], name="lru_conv_dw")
            dq, df, dv, dgp, dlb, dng = _hgrn_bwd(dycat, W, proj, 2 * W, 3 * W, 4 * W, 5 * W, row(lbs[j]),
                                                  row(p['hgrn_norm_g'][j]), s['o'], s['states'], name="hgrn_bwd")
            gs['lru_wa'][j], gs['lru_ba'][j], gs['lru_wx'][j], gs['lru_bx'][j] = dwa, dba[:, 0, :], dwx, dbx[:, 0, :]
            gs['lru_lambda'][j], gs['lru_conv_w'][j], gs['lru_conv_b'][j] = dlam[0], dcw, dcb[0]
            gs['hgrn_lb_logits'][j], gs['hgrn_norm_g'][j] = dlb[0], dng[0]
            dproj = jnp.concatenate([dxa, dgate_a, dq, df, dv, dgp], axis=1)
        else:
            def sc_bwd1(dy_, cp_, sb_):
                return dy_ * cp_, dy_ * sb_

            dsb, dcp = _pw(sc_bwd1, [(dycat, 0), (s['cp'], 0), (proj, 0)], [_MXU, f32], W, name="sc_bwd1")
            dpp = _conv_bwd_dx(dcp, small['sc_conv_w'][j], name="sc_conv_dx")
            dscw, _ = _conv_bwd_dw(dcp, s['pp'], 0, small['sc_conv_w'][j].shape[0], name="sc_conv_dw")

            def sc_bwd2(dp_, sc_, sv_):
                return dp_ * sv_, dp_ * sc_

            dsc, dsv = _pw(sc_bwd2, [(dpp, 0), (proj, W), (proj, 2 * W)], [_MXU, _MXU], W, name="sc_bwd2")
            dd, dlg, dlbeta = _ln_bwd(dycat, W, s['dcv'], row(small['cf_ln_g'][j]), row(small['cf_ln_b'][j]),
                                      name="cf_ln_bwd")
            dglu = _conv_bwd_dx(dd, small['cf_conv_w'][j], name="cf_conv_dx")
            dcfw, dcfb = _conv_bwd_dw(dd, s['glu'], 0, small['cf_conv_w'][j].shape[0], name="cf_conv_dw")

            def glu_bwd(dg_, cu_, cg_):
                _, vjp = jax.vjp(lambda a_, b_: a_ * jax.nn.sigmoid(b_), cu_, cg_)
                return vjp(dg_)

            dcu, dcg = _pw(glu_bwd, [(dglu, 0), (proj, 3 * W), (proj, 4 * W)], [_MXU, _MXU], W, name="cf_glu_bwd")
            gs['sc_conv_w'][j], gs['cf_conv_w'][j], gs['cf_conv_b'][j] = dscw, dcfw, dcfb[0]
            gs['cf_ln_g'][j], gs['cf_ln_b'][j] = dlg[0], dlbeta[0]
            dproj = jnp.concatenate([dsb, dsc, dsv, dcu, dcg], axis=1)
        bname = 'ev_b_in' if layer % 2 == 0 else 'od_b_in'
        gs[bname][j] = _colsum(dproj, name="colsum_" + n_in)[0]
        g_in = _matmul(s['h'], dproj, 'tn', _WIRE, "mm_dw_" + n_in, tm=1024, tn=1536, tk=512)
        tok = scatter_start(layer, 'mix', [g_in, g_out])
        dh = _matmul(dproj, w_in, 'nt', f32, "mm_dh_" + n_in)
        dx, gs['ln_mix_g'][layer] = _rmsnorm_bwd(dh, s['x_in'], row(p['ln_mix_g'][layer], tok), dx, name="rms_bwd")

    g_small = {n: jnp.stack(v_) for n, v_ in gs.items()}
    g_small['ln_mix_g'] = g_small['ln_mix_g'][:, 0, :]
    g_small['ln_ffn_g'] = g_small['ln_ffn_g'][:, 0, :]
    g_small['ln_final_g'] = dg_final[0]
    g_small['hgrn_lb_logits'] = _lb_bwd(p['hgrn_lb_logits'], g_small['hgrn_lb_logits'], name="lb_bwd")
    small_names = _SMALL_REPL + _SMALL_SHARDED
    pack = _pack([g_small[n] for n in small_names])
    tot = _sum_slots(_gather_all(pack, name="gather_small_grads"), name="sum_small_grads")
    g_tot = dict(zip(small_names, _unpack(tot, [g_small[n].shape for n in small_names])))
    for n in _SMALL_SHARDED:
        lead = p[n].shape[:-1]
        q = p[n].shape[-1] // cl
        blk = g_tot[n].reshape(lead + (4, q * cl))
        g_tot[n] = lax.dynamic_index_in_dim(blk, chip, axis=len(lead), keepdims=False)

    outs = {}
    shapes = [p[n].shape for n in small_names]
    res = _adamw(_pack([p[n] for n in small_names]), _pack([p['m_' + n] for n in small_names]),
                 _pack([p['v_' + n] for n in small_names]), [_pack([g_tot[n] for n in small_names])], name="adamw_small")
    for kind, buf in zip(('grad', 'delta', 'new_m', 'new_v'), res):
        for n, a in zip(small_names, _unpack(buf, shapes)):
            outs[kind + '_' + n] = a

    acc = {n: tuple(lax.empty(p[n].shape, f32) for _ in range(4)) for n in _BIG}

    def finish(swap, after):
        st, names, tag = swap
        parts, others = _xchg_wait('swap', st, [None] * len(names), after, name=f"swap_wait_{tag}")
        for (n, l), part, other in zip(names, parts, others, strict=True):
            acc[n] = tuple(_adamw_layer(p[n], p['m_' + n], p['v_' + n], l, [part, other], acc[n], name=f"adamw_{n}"))

    prev_swap = None
    for layer, grp in reversed(groups):
        st, cols = scat.pop((layer, grp))
        srcs, recvs = _xchg_wait('scatter', st, cols, dx, name=f"rs_wait_{grp}{layer}")
        names = group_names(layer, grp)
        parts = [_sum_own_recv(g, chip1, col, recv, name=f"sum_{n}")
                 for (n, l), g, recv, col in zip(names, srcs, recvs, cols, strict=True)]
        st = _xchg_start('swap', parts, [lax.empty(a.shape, f32) for a in parts], [None] * len(parts), recvs[0],
                         name=f"swap_start_{grp}{layer}")
        if prev_swap is not None:
            finish(prev_swap, st[3])
        prev_swap = (st, names, f"{grp}{layer}")
    finish(prev_swap, prev_swap[0][3])
    for n in _BIG:
        for kind, buf in zip(('grad', 'delta', 'new_m', 'new_v'), acc[n]):
            outs[kind + '_' + n] = buf

    loss = lax.psum(loss_b[0, 0], ("x", "y", "c"))
    weights = _IN_NAMES[1:]
    return (loss, dx[None], *[outs['grad_' + n] for n in weights], *[outs['delta_' + n] for n in weights],
            *[outs['new_m_' + n] for n in weights], *[outs['new_v_' + n] for n in weights])


def kernel(x, ln_mix_g, ln_ffn_g, ln_final_g, ev_w_in, ev_b_in, lru_conv_w, lru_conv_b, lru_wa, lru_ba, lru_wx, lru_bx, lru_lambda, hgrn_lb_logits, hgrn_norm_g, ev_w_out, od_w_in, od_b_in, sc_conv_w, cf_conv_w, cf_conv_b, cf_ln_g, cf_ln_b, od_w_out, ffn_w_gate, ffn_w_up, ffn_w_down, loss_target, m_ln_mix_g, m_ln_ffn_g, m_ln_final_g, m_ev_w_in, m_ev_b_in, m_lru_conv_w, m_lru_conv_b, m_lru_wa, m_lru_ba, m_lru_wx, m_lru_bx, m_lru_lambda, m_hgrn_lb_logits, m_hgrn_norm_g, m_ev_w_out, m_od_w_in, m_od_b_in, m_sc_conv_w, m_cf_conv_w, m_cf_conv_b, m_cf_ln_g, m_cf_ln_b, m_od_w_out, m_ffn_w_gate, m_ffn_w_up, m_ffn_w_down, v_ln_mix_g, v_ln_ffn_g, v_ln_final_g, v_ev_w_in, v_ev_b_in, v_lru_conv_w, v_lru_conv_b, v_lru_wa, v_lru_ba, v_lru_wx, v_lru_bx, v_lru_lambda, v_hgrn_lb_logits, v_hgrn_norm_g, v_ev_w_out, v_od_w_in, v_od_b_in, v_sc_conv_w, v_cf_conv_w, v_cf_conv_b, v_cf_ln_g, v_cf_ln_b, v_od_w_out, v_ffn_w_gate, v_ffn_w_up, v_ffn_w_down):
    vals = locals()
    p = {n: vals[n] for n in _IN_NAMES + ['loss_target']}
    for n in _IN_NAMES[1:]:
        p['m_' + n] = vals['m_' + n]
        p['v_' + n] = vals['v_' + n]
    return _step(p)
```

```python
import functools

import jax
import jax.numpy as jnp
from jax import lax
from jax.experimental import pallas as pl
from jax.experimental.pallas import tpu as pltpu

f32 = jnp.float32
_MXU = jnp.bfloat16
_WIRE = jnp.bfloat16

N_HEADS = 8
LRU_C = 8.0
EPS = 1e-6
F_FLOOR = 1e-30
SUB = 16
HGRN_HPB = 2
ADAM_LR, ADAM_B1, ADAM_B2, ADAM_EPS, ADAM_WD, ADAM_STEP = 0.001, 0.9, 0.999, 1e-08, 0.01, 10
V7X_VMEM_LIMIT = 48 * 1024 * 1024
LANE = 128
MESH = pl.DeviceIdType.MESH

_IN_NAMES = ['x', 'ln_mix_g', 'ln_ffn_g', 'ln_final_g', 'ev_w_in', 'ev_b_in', 'lru_conv_w', 'lru_conv_b', 'lru_wa', 'lru_ba',
             'lru_wx', 'lru_bx', 'lru_lambda', 'hgrn_lb_logits', 'hgrn_norm_g', 'ev_w_out', 'od_w_in', 'od_b_in', 'sc_conv_w',
             'cf_conv_w', 'cf_conv_b', 'cf_ln_g', 'cf_ln_b', 'od_w_out', 'ffn_w_gate', 'ffn_w_up', 'ffn_w_down']
_BIG = ['ev_w_in', 'ev_w_out', 'od_w_in', 'od_w_out', 'ffn_w_gate', 'ffn_w_up', 'ffn_w_down']
_BIG_COL = {'ev_w_in': True, 'ev_w_out': False, 'od_w_in': True, 'od_w_out': False, 'ffn_w_gate': True, 'ffn_w_up': True,
            'ffn_w_down': False}
_SMALL_SHARDED = ['lru_conv_w', 'od_b_in', 'sc_conv_w', 'cf_conv_w', 'cf_conv_b', 'cf_ln_g', 'cf_ln_b']
_SMALL_REPL = ['ln_mix_g', 'ln_ffn_g', 'ln_final_g', 'ev_b_in', 'lru_conv_b', 'lru_wa', 'lru_ba', 'lru_wx', 'lru_bx',
               'lru_lambda', 'hgrn_lb_logits', 'hgrn_norm_g']


def _tile(n, pref, align):
    if n <= pref:
        return n
    t = (pref // align) * align
    while t >= align:
        if n % t == 0:
            return t
        t -= align
    return n


def _params(sem):
    return pltpu.CompilerParams(dimension_semantics=sem, vmem_limit_bytes=V7X_VMEM_LIMIT)


def _rows(shape):
    return lax.broadcasted_iota(jnp.int32, shape, 0)


def _mxdot(a, b, dims=(((1,), (0,)), ((), ()))):
    return lax.dot_general(a.astype(_MXU), b.astype(_MXU), dims, preferred_element_type=f32)


_NN = (((1,), (0,)), ((), ()))
_NT = (((1,), (1,)), ((), ()))
_TN = (((0,), (0,)), ((), ()))


def _matmul(a, b, mode, out_dtype, name, bias=None, add=None, tm=1024, tn=512, tk=2048, second=None):
    if mode == 'nn':
        (M, K), (K2, N) = a.shape, b.shape
    elif mode == 'nt':
        (M, K), (N, K2) = a.shape, b.shape
    else:
        (K, M), (K2, N) = a.shape, b.shape
    assert K == K2, (name, a.shape, b.shape)
    tm, tn, tk = _tile(M, tm, LANE), _tile(N, tn, LANE), _tile(K, tk, LANE)
    nk = K // tk
    dims = {'nn': _NN, 'nt': _NT, 'tn': _TN}[mode]
    has_bias, has_add, has_second = bias is not None, add is not None, second is not None

    def body(*refs):
        a_ref, b_ref = refs[0], refs[1]
        pos = 2
        if has_second:
            a2_ref, b2_ref = refs[2], refs[3]
            pos = 4
        bias_ref = add_ref = None
        if has_bias:
            bias_ref = refs[pos]
            pos += 1
        if has_add:
            add_ref = refs[pos]
            pos += 1
        o_ref, acc_ref = refs[pos], refs[pos + 1]
        k = pl.program_id(2)

        @pl.when(k == 0)
        def _():
            acc_ref[...] = jnp.zeros_like(acc_ref)

        prod = _mxdot(a_ref[...], b_ref[...], dims)
        if has_second:
            prod = prod + _mxdot(a2_ref[...], b2_ref[...], dims)
        acc_ref[...] += prod

        @pl.when(k == nk - 1)
        def _():
            r = acc_ref[...]
            if has_bias:
                r = r + bias_ref[...]
            if has_add:
                r = r + add_ref[...]
            o_ref[...] = r.astype(o_ref.dtype)

    if mode == 'tn':
        a_spec = pl.BlockSpec((tk, tm), lambda i, j, k: (k, i))
    else:
        a_spec = pl.BlockSpec((tm, tk), lambda i, j, k: (i, k))
    if mode == 'nt':
        b_spec = pl.BlockSpec((tn, tk), lambda i, j, k: (j, k))
    else:
        b_spec = pl.BlockSpec((tk, tn), lambda i, j, k: (k, j))
    in_specs, args = [a_spec, b_spec], [a, b]
    if has_second:
        assert second[0].shape == a.shape and second[1].shape == b.shape
        in_specs += [a_spec, b_spec]
        args += list(second)
    if has_bias:
        in_specs.append(pl.BlockSpec((1, tn), lambda i, j, k: (0, j)))
        args.append(bias)
    if has_add:
        in_specs.append(pl.BlockSpec((tm, tn), lambda i, j, k: (i, j)))
        args.append(add)
    return pl.pallas_call(
        body, name=name, grid=(M // tm, N // tn, nk), in_specs=in_specs,
        out_specs=pl.BlockSpec((tm, tn), lambda i, j, k: (i, j)),
        out_shape=jax.ShapeDtypeStruct((M, N), out_dtype),
        scratch_shapes=[pltpu.VMEM((tm, tn), f32)],
        compiler_params=_params(("parallel", "parallel", "arbitrary")),
    )(*args)


def _ffn_in(h2, wg, wu, name):
    (M, K), N = h2.shape, wg.shape[1]
    tm, tn = _tile(M, 1024, LANE), _tile(N, 512, LANE)

    def body(a_ref, g_ref, u_ref, go_ref, uo_ref, act_ref):
        a = a_ref[...]
        g, u = _mxdot(a, g_ref[...]), _mxdot(a, u_ref[...])
        go_ref[...] = g.astype(go_ref.dtype)
        uo_ref[...] = u.astype(uo_ref.dtype)
        act_ref[...] = (jax.nn.silu(g) * u).astype(act_ref.dtype)

    wspec = pl.BlockSpec((K, tn), lambda i, j: (0, j))
    ospec = pl.BlockSpec((tm, tn), lambda i, j: (i, j))
    return pl.pallas_call(
        body, name=name, grid=(M // tm, N // tn), in_specs=[pl.BlockSpec((tm, K), lambda i, j: (i, 0)), wspec, wspec],
        out_specs=[ospec] * 3, out_shape=[jax.ShapeDtypeStruct((M, N), _MXU)] * 3,
        compiler_params=_params(("parallel", "parallel")),
    )(h2, wg, wu)


def _ffn_dact(dx, wd, gate, up, name):
    (M, K), N = dx.shape, wd.shape[0]
    tm, tn = _tile(M, 512, LANE), _tile(N, 512, LANE)

    def body(a_ref, w_ref, g_ref, u_ref, dg_ref, du_ref):
        da = _mxdot(a_ref[...], w_ref[...], _NT)
        _, vjp = jax.vjp(lambda gg, uu: jax.nn.silu(gg) * uu, g_ref[...].astype(f32), u_ref[...].astype(f32))
        dg, du = vjp(da)
        dg_ref[...] = dg.astype(dg_ref.dtype)
        du_ref[...] = du.astype(du_ref.dtype)

    ospec = pl.BlockSpec((tm, tn), lambda i, j: (i, j))
    return pl.pallas_call(
        body, name=name, grid=(M // tm, N // tn),
        in_specs=[pl.BlockSpec((tm, K), lambda i, j: (i, 0)), pl.BlockSpec((tn, K), lambda i, j: (j, 0)), ospec, ospec],
        out_specs=[ospec] * 2, out_shape=[jax.ShapeDtypeStruct((M, N), _MXU)] * 2,
        compiler_params=_params(("parallel", "parallel")),
    )(dx, wd, gate, up)


def _pw(fn, slabs, out_dtypes, width, name, consts=()):
    T = slabs[0][0].shape[0]
    tt, cb = _tile(T, 512, 8), _tile(width, 512, LANE)
    nin, ncst = len(slabs), len(consts)

    def body(*refs):
        res = fn(*[r[...] for r in refs[:nin + ncst]])
        if not isinstance(res, (tuple, list)):
            res = (res,)
        for r, o in zip(res, refs[nin + ncst:], strict=True):
            o[...] = r.astype(o.dtype)

    in_specs, args = [], []
    for arr, col0 in slabs:
        assert col0 % cb == 0
        in_specs.append(pl.BlockSpec((tt, cb), functools.partial(lambda t, c, c0: (t, c0 + c), c0=col0 // cb)))
        args.append(arr)
    for cst in consts:
        in_specs.append(pl.BlockSpec((1, cb), lambda t, c: (0, c)))
        args.append(cst)
    outs = pl.pallas_call(
        body, name=name, grid=(T // tt, width // cb), in_specs=in_specs,
        out_specs=[pl.BlockSpec((tt, cb), lambda t, c: (t, c)) for _ in out_dtypes],
        out_shape=[jax.ShapeDtypeStruct((T, width), d) for d in out_dtypes],
        compiler_params=_params(("parallel", "parallel")),
    )(*args)
    return outs[0] if len(out_dtypes) == 1 else outs


def _rmsnorm_fwd(x, g, name):
    T, D = x.shape
    tt = _tile(T, 256, 8)

    def body(x_ref, g_ref, o_ref):
        xv = x_ref[...]
        r = lax.rsqrt(jnp.mean(xv * xv, axis=-1, keepdims=True) + EPS)
        o_ref[...] = (xv * r * g_ref[...]).astype(o_ref.dtype)

    return pl.pallas_call(
        body, name=name, grid=(T // tt,),
        in_specs=[pl.BlockSpec((tt, D), lambda t: (t, 0)), pl.BlockSpec((1, D), lambda t: (0, 0))],
        out_specs=pl.BlockSpec((tt, D), lambda t: (t, 0)),
        out_shape=jax.ShapeDtypeStruct((T, D), _MXU), compiler_params=_params(("parallel",)),
    )(x, g)


def _rmsnorm_bwd(dh, x, g, dx_in, name):
    T, D = x.shape
    tt = _tile(T, 256, 8)

    def body(dh_ref, x_ref, g_ref, dxi_ref, dx_ref, dg_ref):
        t = pl.program_id(0)
        xv, d = x_ref[...], dh_ref[...]
        r = lax.rsqrt(jnp.mean(xv * xv, axis=-1, keepdims=True) + EPS)
        n = xv * r
        dn = d * g_ref[...]
        dx_ref[...] = dxi_ref[...] + r * (dn - n * jnp.mean(dn * n, axis=-1, keepdims=True))

        @pl.when(t == 0)
        def _():
            dg_ref[...] = jnp.zeros_like(dg_ref)

        dg_ref[...] += jnp.sum(d * n, axis=0, keepdims=True)

    return pl.pallas_call(
        body, name=name, grid=(T // tt,),
        in_specs=[pl.BlockSpec((tt, D), lambda t: (t, 0)), pl.BlockSpec((tt, D), lambda t: (t, 0)),
                  pl.BlockSpec((1, D), lambda t: (0, 0)), pl.BlockSpec((tt, D), lambda t: (t, 0))],
        out_specs=[pl.BlockSpec((tt, D), lambda t: (t, 0)), pl.BlockSpec((1, D), lambda t: (0, 0))],
        out_shape=[jax.ShapeDtypeStruct((T, D), f32), jax.ShapeDtypeStruct((1, D), f32)],
        compiler_params=_params(("arbitrary",)),
    )(dh, x, g, dx_in)


def _final_loss(x, g, tgt, name):
    T, D = x.shape
    tt = _tile(T, 256, 8)

    def body(x_ref, g_ref, t_ref, l_ref, dx_ref, dg_ref):
        t = pl.program_id(0)
        xv = x_ref[...]
        r = lax.rsqrt(jnp.mean(xv * xv, axis=-1, keepdims=True) + EPS)
        n = xv * r
        e = n * g_ref[...] - t_ref[...]
        part = 0.5 * jnp.sum(jnp.mean(e * e, axis=-1, keepdims=True), axis=0, keepdims=True)
        dy = e * (1.0 / D)
        dn = dy * g_ref[...]
        dx_ref[...] = r * (dn - n * jnp.mean(dn * n, axis=-1, keepdims=True))

        @pl.when(t == 0)
        def _():
            dg_ref[...] = jnp.zeros_like(dg_ref)
            l_ref[...] = jnp.zeros_like(l_ref)

        dg_ref[...] += jnp.sum(dy * n, axis=0, keepdims=True)
        l_ref[...] += jnp.broadcast_to(part, l_ref.shape)

    return pl.pallas_call(
        body, name=name, grid=(T // tt,),
        in_specs=[pl.BlockSpec((tt, D), lambda t: (t, 0)), pl.BlockSpec((1, D), lambda t: (0, 0)),
                  pl.BlockSpec((tt, D), lambda t: (t, 0))],
        out_specs=[pl.BlockSpec((1, LANE), lambda t: (0, 0)), pl.BlockSpec((tt, D), lambda t: (t, 0)),
                   pl.BlockSpec((1, D), lambda t: (0, 0))],
        out_shape=[jax.ShapeDtypeStruct((1, LANE), f32), jax.ShapeDtypeStruct((T, D), f32),
                   jax.ShapeDtypeStruct((1, D), f32)],
        compiler_params=_params(("arbitrary",)),
    )(x, g, tgt)


def _colsum(x, name):
    T, N = x.shape
    tt, cb = _tile(T, 512, 8), _tile(N, 512, LANE)

    def body(x_ref, o_ref):
        @pl.when(pl.program_id(1) == 0)
        def _():
            o_ref[...] = jnp.zeros_like(o_ref)

        o_ref[...] += jnp.sum(x_ref[...].astype(f32), axis=0, keepdims=True)

    return pl.pallas_call(
        body, name=name, grid=(N // cb, T // tt), in_specs=[pl.BlockSpec((tt, cb), lambda c, t: (t, c))],
        out_specs=pl.BlockSpec((1, cb), lambda c, t: (0, c)), out_shape=jax.ShapeDtypeStruct((1, N), f32),
        compiler_params=_params(("parallel", "arbitrary")),
    )(x)


def _shift_down(cur, prev, j):
    if j == 0:
        return cur
    n = cur.shape[0]
    return jnp.where(_rows(cur.shape) < j, pltpu.roll(prev, j, 0), pltpu.roll(cur, j, 0))


def _shift_up(cur, nxt, j):
    if j == 0:
        return cur
    n = cur.shape[0]
    return jnp.where(_rows(cur.shape) >= n - j, pltpu.roll(nxt, n - j, 0), pltpu.roll(cur, n - j, 0))


def _conv_tiles(T, C, K):
    tt, cb = _tile(T, 256, 8), _tile(C, 256, LANE)
    assert tt >= K, (tt, K)
    return tt, cb


def _conv_fwd(x, col0, w, b, C, name):
    T, K = x.shape[0], w.shape[0]
    tt, cb = _conv_tiles(T, C, K)
    c0 = col0 // cb
    assert col0 % cb == 0
    has_b = b is not None

    def body(*refs):
        cur_ref, prev_ref, w_ref = refs[:3]
        o_ref = refs[-1]
        t = pl.program_id(1)
        cur = cur_ref[...]
        prev = jnp.where(t > 0, prev_ref[...], 0.0)
        wv = w_ref[...]
        acc = jnp.zeros_like(cur)
        for k in range(K):
            acc = acc + wv[k:k + 1, :] * _shift_down(cur, prev, K - 1 - k)
        if has_b:
            acc = acc + refs[3][...]
        o_ref[...] = acc

    in_specs = [pl.BlockSpec((tt, cb), lambda c, t: (t, c0 + c)),
                pl.BlockSpec((tt, cb), lambda c, t: (jnp.maximum(t - 1, 0), c0 + c)),
                pl.BlockSpec((K, cb), lambda c, t: (0, c))]
    args = [x, x, w]
    if has_b:
        in_specs.append(pl.BlockSpec((1, cb), lambda c, t: (0, c)))
        args.append(b)
    return pl.pallas_call(
        body, name=name, grid=(C // cb, T // tt), in_specs=in_specs,
        out_specs=pl.BlockSpec((tt, cb), lambda c, t: (t, c)), out_shape=jax.ShapeDtypeStruct((T, C), f32),
        compiler_params=_params(("parallel", "parallel")),
    )(*args)


def _conv_bwd_dx(dy, w, name, out_dtype=f32):
    T, C = dy.shape
    K = w.shape[0]
    tt, cb = _conv_tiles(T, C, K)
    nt = T // tt

    def body(cur_ref, nxt_ref, w_ref, o_ref):
        t = pl.program_id(1)
        cur = cur_ref[...]
        nxt = jnp.where(t < nt - 1, nxt_ref[...], 0.0)
        wv = w_ref[...]
        acc = jnp.zeros_like(cur)
        for k in range(K):
            acc = acc + wv[k:k + 1, :] * _shift_up(cur, nxt, K - 1 - k)
        o_ref[...] = acc.astype(o_ref.dtype)

    return pl.pallas_call(
        body, name=name, grid=(C // cb, nt),
        in_specs=[pl.BlockSpec((tt, cb), lambda c, t: (t, c)),
                  pl.BlockSpec((tt, cb), lambda c, t: (jnp.minimum(t + 1, nt - 1), c)),
                  pl.BlockSpec((K, cb), lambda c, t: (0, c))],
        out_specs=pl.BlockSpec((tt, cb), lambda c, t: (t, c)), out_shape=jax.ShapeDtypeStruct((T, C), out_dtype),
        compiler_params=_params(("parallel", "parallel")),
    )(dy, dy, w)


def _conv_bwd_dw(dy, x, col0, K, name):
    T, C = dy.shape
    tt, cb = _conv_tiles(T, C, K)
    c0 = col0 // cb
    assert col0 % cb == 0

    def body(dy_ref, cur_ref, prev_ref, dw_ref, db_ref):
        t = pl.program_id(1)

        @pl.when(t == 0)
        def _():
            dw_ref[...] = jnp.zeros_like(dw_ref)
            db_ref[...] = jnp.zeros_like(db_ref)

        d = dy_ref[...]
        cur = cur_ref[...]
        prev = jnp.where(t > 0, prev_ref[...], 0.0)
        for k in range(K):
            row = jnp.sum(d * _shift_down(cur, prev, K - 1 - k), axis=0, keepdims=True)
            dw_ref[pl.ds(k, 1), :] = dw_ref[pl.ds(k, 1), :] + row
        db_ref[...] += jnp.sum(d, axis=0, keepdims=True)

    return pl.pallas_call(
        body, name=name, grid=(C // cb, T // tt),
        in_specs=[pl.BlockSpec((tt, cb), lambda c, t: (t, c)),
                  pl.BlockSpec((tt, cb), lambda c, t: (t, c0 + c)),
                  pl.BlockSpec((tt, cb), lambda c, t: (jnp.maximum(t - 1, 0), c0 + c))],
        out_specs=[pl.BlockSpec((K, cb), lambda c, t: (0, c)), pl.BlockSpec((1, cb), lambda c, t: (0, c))],
        out_shape=[jax.ShapeDtypeStruct((K, C), f32), jax.ShapeDtypeStruct((1, C), f32)],
        compiler_params=_params(("parallel", "arbitrary")),
    )(dy, x, x)


def _expm1(z):
    poly = z * (1.0 + z * (0.5 + z * (1.0 / 6.0 + z * (1.0 / 24.0 + z * (1.0 / 120.0)))))
    return jnp.where(jnp.abs(z) < 0.1, poly, jnp.exp(z) - 1.0)


def _lru_pt(xc, rp, ip, lam, first):
    r = jax.nn.sigmoid(rp)
    i = jax.nn.sigmoid(ip)
    log_a = -LRU_C * r * jax.nn.softplus(-lam)
    a = jnp.exp(log_a)
    mult = jnp.sqrt(jnp.maximum(-_expm1(2.0 * log_a), 0.0))
    mult = jnp.where(first, 1.0, mult)
    return a, mult * i * xc


def _first_mask(shape, t):
    return jnp.logical_and(_rows(shape) == 0, t == 0)


def _lru_gates_fwd(xc, wa, ba, wx, bx, lam, name):
    T, W = xc.shape
    hd = W // N_HEADS
    tt = _tile(T, 512, 8)

    def body(xc_ref, wa_ref, ba_ref, wx_ref, bx_ref, lam_ref, a_ref, u_ref):
        t = pl.program_id(1)
        x = xc_ref[...]
        rp = _mxdot(x, wa_ref[...]) + ba_ref[...]
        ip = _mxdot(x, wx_ref[...]) + bx_ref[...]
        a, u = _lru_pt(x, rp, ip, lam_ref[...], _first_mask(x.shape, t))
        a_ref[...] = a
        u_ref[...] = u

    wspec = pl.BlockSpec((None, hd, hd), lambda h, t: (h, 0, 0))
    bspec = pl.BlockSpec((None, 1, hd), lambda h, t: (h, 0, 0))
    tspec = pl.BlockSpec((tt, hd), lambda h, t: (t, h))
    return pl.pallas_call(
        body, name=name, grid=(N_HEADS, T // tt),
        in_specs=[tspec, wspec, bspec, wspec, bspec, pl.BlockSpec((1, hd), lambda h, t: (0, h))],
        out_specs=[tspec, tspec], out_shape=[jax.ShapeDtypeStruct((T, W), f32)] * 2,
        compiler_params=_params(("parallel", "parallel")),
    )(xc, wa, ba, wx, bx, lam)


def _lru_scan_fwd(a, u, gate, gcol0, name):
    T, W = a.shape
    tt, cb = _tile(T, 256, 8), _tile(W, 512, LANE)
    g0 = gcol0 // cb
    assert gcol0 % cb == 0

    def body(a_ref, u_ref, g_ref, h_ref, y_ref, carry_ref):
        t = pl.program_id(1)

        @pl.when(t == 0)
        def _():
            carry_ref[...] = jnp.zeros_like(carry_ref)

        def step(i, h):
            base = pl.multiple_of(i * 8, 8)
            a8, u8 = a_ref[pl.ds(base, 8), :], u_ref[pl.ds(base, 8), :]
            rows = []
            for j in range(8):
                h = a8[j:j + 1, :] * h + u8[j:j + 1, :]
                rows.append(h)
            h_ref[pl.ds(base, 8), :] = jnp.concatenate(rows, axis=0)
            return h

        h_last = lax.fori_loop(0, tt // 8, step, carry_ref[0:1, :])
        carry_ref[...] = jnp.broadcast_to(h_last, carry_ref.shape)
        y_ref[...] = (h_ref[...] * jax.nn.gelu(g_ref[...])).astype(y_ref.dtype)

    tspec = pl.BlockSpec((tt, cb), lambda c, t: (t, c))
    return pl.pallas_call(
        body, name=name, grid=(W // cb, T // tt),
        in_specs=[tspec, tspec, pl.BlockSpec((tt, cb), lambda c, t: (t, g0 + c))],
        out_specs=[tspec, tspec],
        out_shape=[jax.ShapeDtypeStruct((T, W), f32), jax.ShapeDtypeStruct((T, W), _MXU)],
        scratch_shapes=[pltpu.VMEM((8, cb), f32)],
        compiler_params=_params(("parallel", "arbitrary")),
    )(a, u, gate)


def _lru_scan_bwd(dy, dcol0, gate, gcol0, h, a, name):
    T, W = a.shape
    tt, cb = _tile(T, 256, 8), _tile(W, 512, LANE)
    nt = T // tt
    d0, g0 = dcol0 // cb, gcol0 // cb
    assert dcol0 % cb == 0 and gcol0 % cb == 0

    def body(dy_ref, g_ref, h_ref, a_ref, lam_ref, dg_ref, carry_ref, dh_ref):
        t = pl.program_id(1)

        @pl.when(t == 0)
        def _():
            carry_ref[...] = jnp.zeros_like(carry_ref)

        _, vjp = jax.vjp(lambda hh, gg: hh * jax.nn.gelu(gg), h_ref[...], g_ref[...])
        dh, dg = vjp(dy_ref[...])
        dg_ref[...] = dg.astype(dg_ref.dtype)
        dh_ref[...] = dh

        def step(i, c):
            base = pl.multiple_of((tt // 8 - 1 - i) * 8, 8)
            a8, d8 = a_ref[pl.ds(base, 8), :], dh_ref[pl.ds(base, 8), :]
            rows = [None] * 8
            for j in range(7, -1, -1):
                lam = d8[j:j + 1, :] + c
                c = a8[j:j + 1, :] * lam
                rows[j] = lam
            lam_ref[pl.ds(base, 8), :] = jnp.concatenate(rows, axis=0)
            return c

        c_last = lax.fori_loop(0, tt // 8, step, carry_ref[0:1, :])
        carry_ref[...] = jnp.broadcast_to(c_last, carry_ref.shape)

    rev = lambda c, t: (nt - 1 - t, c)
    tspec = pl.BlockSpec((tt, cb), rev)
    return pl.pallas_call(
        body, name=name, grid=(W // cb, nt),
        in_specs=[pl.BlockSpec((tt, cb), lambda c, t: (nt - 1 - t, d0 + c)),
                  pl.BlockSpec((tt, cb), lambda c, t: (nt - 1 - t, g0 + c)), tspec, tspec],
        out_specs=[tspec, tspec], out_shape=[jax.ShapeDtypeStruct((T, W), f32), jax.ShapeDtypeStruct((T, W), _MXU)],
        scratch_shapes=[pltpu.VMEM((8, cb), f32), pltpu.VMEM((tt, cb), f32)],
        compiler_params=_params(("parallel", "arbitrary")),
    )(dy, gate, h, a)


def _lru_gates_bwd(lam_g, h, xc, wa, ba, wx, bx, lam, name):
    T, W = xc.shape
    hd = W // N_HEADS
    tt = _tile(T, 512, 8)

    def body(lg_ref, h_ref, hp_ref, xc_ref, wa_ref, ba_ref, wx_ref, bx_ref, lam_ref,
             dxc_ref, dwa_ref, dba_ref, dwx_ref, dbx_ref, dlam_ref):
        t = pl.program_id(1)

        @pl.when(t == 0)
        def _():
            for r in (dwa_ref, dba_ref, dwx_ref, dbx_ref, dlam_ref):
                r[...] = jnp.zeros_like(r)

        x = xc_ref[...]
        lg = lg_ref[...]
        h_prev = _shift_down(h_ref[...], jnp.where(t > 0, hp_ref[...], 0.0), 1)
        rp = _mxdot(x, wa_ref[...]) + ba_ref[...]
        ip = _mxdot(x, wx_ref[...]) + bx_ref[...]
        first = _first_mask(x.shape, t)
        _, vjp = jax.vjp(lambda xx, r_, i_, l_: _lru_pt(xx, r_, i_, l_, first), x, rp, ip, lam_ref[...])
        dx, drp, dip, dl = vjp((lg * h_prev, lg))
        dxc_ref[...] = dx + _mxdot(drp, wa_ref[...], _NT) + _mxdot(dip, wx_ref[...], _NT)
        dwa_ref[...] += _mxdot(x, drp, _TN)
        dwx_ref[...] += _mxdot(x, dip, _TN)
        dba_ref[...] += jnp.sum(drp, axis=0, keepdims=True)
        dbx_ref[...] += jnp.sum(dip, axis=0, keepdims=True)
        dlam_ref[...] += dl

    wspec = pl.BlockSpec((None, hd, hd), lambda h_, t: (h_, 0, 0))
    bspec = pl.BlockSpec((None, 1, hd), lambda h_, t: (h_, 0, 0))
    tspec = pl.BlockSpec((tt, hd), lambda h_, t: (t, h_))
    pspec = pl.BlockSpec((tt, hd), lambda h_, t: (jnp.maximum(t - 1, 0), h_))
    lspec = pl.BlockSpec((1, hd), lambda h_, t: (0, h_))
    return pl.pallas_call(
        body, name=name, grid=(N_HEADS, T // tt),
        in_specs=[tspec, tspec, pspec, tspec, wspec, bspec, wspec, bspec, lspec],
        out_specs=[tspec, wspec, bspec, wspec, bspec, lspec],
        out_shape=[jax.ShapeDtypeStruct((T, W), f32), jax.ShapeDtypeStruct((N_HEADS, hd, hd), f32),
                   jax.ShapeDtypeStruct((N_HEADS, 1, hd), f32), jax.ShapeDtypeStruct((N_HEADS, hd, hd), f32),
                   jax.ShapeDtypeStruct((N_HEADS, 1, hd), f32), jax.ShapeDtypeStruct((1, W), f32)],
        compiler_params=_params(("parallel", "arbitrary")),
    )(lam_g, h, h, xc, wa, ba, wx, bx, lam)


def _hgrn_pt(z, qp, lb):
    sig = jax.nn.sigmoid(z)
    fg = lb + (1.0 - lb) * sig
    logf = jnp.log(jnp.maximum(fg, F_FLOOR))
    k = (1.0 - lb) * (1.0 - sig)
    return logf, k, jax.nn.silu(qp)


def _hgrn_out(o, gp, ng):
    on = o * lax.rsqrt(jnp.mean(o * o, axis=-1, keepdims=True) + EPS)
    return on * ng * jax.nn.silu(gp)


def _cumsum_rows(x):
    n, row, sh = x.shape[0], _rows(x.shape), 1
    while sh < n:
        x = x + jnp.where(row >= sh, pltpu.roll(x, sh, 0), 0.0)
        sh *= 2
    return x


def _rev_cumsum_rows(x):
    n, row, sh = x.shape[0], _rows(x.shape), 1
    while sh < n:
        x = x + jnp.where(row < n - sh, pltpu.roll(x, n - sh, 0), 0.0)
        sh *= 2
    return x


def _hgrn_fwd(proj, qcol, fcol, vcol, gcol, lb, ng, name):
    T = proj.shape[0]
    W = lb.shape[1]
    hd = W // N_HEADS
    bw = HGRN_HPB * hd
    tt = _tile(T, 256, SUB)
    ns = tt // SUB
    q0, f0, v0, g0 = qcol // bw, fcol // bw, vcol // bw, gcol // bw

    def body(q_ref, f_ref, v_ref, g_ref, lb_ref, ng_ref, y_ref, o_ref, st_ref, s_ref):
        t = pl.program_id(1)

        @pl.when(t == 0)
        def _():
            s_ref[...] = jnp.zeros_like(s_ref)

        row = _rows((SUB, hd))

        def sub(j, carry):
            rs = pl.ds(pl.multiple_of(j * SUB, SUB), SUB)
            for hh in range(HGRN_HPB):
                cs = slice(hh * hd, (hh + 1) * hd)
                logf, k, qf = _hgrn_pt(f_ref[rs, cs], q_ref[rs, cs], lb_ref[:, cs])
                v = v_ref[rs, cs]
                b = _cumsum_rows(logf)
                b_last = b[SUB - 1:SUB, :]
                S = s_ref[hh]
                st_ref[hh, j] = S
                intra = jnp.zeros((SUB, hd), f32)
                for r in range(SUB):
                    e = jnp.exp(jnp.minimum(b[r:r + 1, :] - b, 0.0))
                    m = jnp.where(row <= r, qf[r:r + 1, :] * k * e, 0.0)
                    p = jnp.sum(m, axis=1, keepdims=True)
                    intra = jnp.where(row == r, jnp.sum(p * v, axis=0, keepdims=True), intra)
                o_ref[rs, cs] = _mxdot(qf * jnp.exp(b), S, _NT) + intra
                s_ref[hh] = S * jnp.exp(b_last) + _mxdot(v, k * jnp.exp(b_last - b), _TN)
            return carry

        lax.fori_loop(0, ns, sub, 0)
        for hh in range(HGRN_HPB):
            cs = slice(hh * hd, (hh + 1) * hd)
            y_ref[:, cs] = _hgrn_out(o_ref[:, cs], g_ref[:, cs], ng_ref[:, cs]).astype(y_ref.dtype)

    def slab(c0):
        return pl.BlockSpec((tt, bw), functools.partial(lambda h, t, c0: (t, c0 + h), c0=c0))

    hspec = pl.BlockSpec((tt, bw), lambda h, t: (t, h))
    cspec = pl.BlockSpec((1, bw), lambda h, t: (0, h))
    return pl.pallas_call(
        body, name=name, grid=(N_HEADS // HGRN_HPB, T // tt),
        in_specs=[slab(q0), slab(f0), slab(v0), slab(g0), cspec, cspec],
        out_specs=[hspec, hspec, pl.BlockSpec((HGRN_HPB, ns, hd, hd), lambda h, t: (h, t, 0, 0))],
        out_shape=[jax.ShapeDtypeStruct((T, W), _MXU), jax.ShapeDtypeStruct((T, W), f32),
                   jax.ShapeDtypeStruct((N_HEADS, T // SUB, hd, hd), f32)],
        scratch_shapes=[pltpu.VMEM((HGRN_HPB, hd, hd), f32)],
        compiler_params=_params(("parallel", "arbitrary")),
    )(proj, proj, proj, proj, lb, ng)


def _hgrn_bwd(dy, dcol, proj, qcol, fcol, vcol, gcol, lb, ng, o, states, name):
    T = proj.shape[0]
    W = lb.shape[1]
    hd = W // N_HEADS
    bw = HGRN_HPB * hd
    tt = _tile(T, 256, SUB)
    ns, nt = tt // SUB, T // tt
    q0, f0, v0, g0, d0 = qcol // bw, fcol // bw, vcol // bw, gcol // bw, dcol // bw

    def body(dy_ref, q_ref, f_ref, v_ref, g_ref, lb_ref, ng_ref, o_ref, st_ref,
             dq_ref, df_ref, dv_ref, dg_ref, dlb_ref, dng_ref, ds_ref, do_ref):
        t = pl.program_id(1)

        @pl.when(t == 0)
        def _():
            ds_ref[...] = jnp.zeros_like(ds_ref)
            dlb_ref[...] = jnp.zeros_like(dlb_ref)
            dng_ref[...] = jnp.zeros_like(dng_ref)

        for hh in range(HGRN_HPB):
            cs = slice(hh * hd, (hh + 1) * hd)
            _, vjp_out = jax.vjp(_hgrn_out, o_ref[:, cs], g_ref[:, cs], ng_ref[:, cs])
            do, dgp, dng = vjp_out(dy_ref[:, cs])
            do_ref[:, cs] = do
            dg_ref[:, cs] = dgp.astype(dg_ref.dtype)
            dng_ref[:, cs] += dng
        row = _rows((SUB, hd))

        def sub(jj, carry):
            j = ns - 1 - jj
            rs = pl.ds(pl.multiple_of(j * SUB, SUB), SUB)
            for hh in range(HGRN_HPB):
                cs = slice(hh * hd, (hh + 1) * hd)
                (logf, k, qf), vjp_pt = jax.vjp(_hgrn_pt, f_ref[rs, cs], q_ref[rs, cs], lb_ref[:, cs])
                v = v_ref[rs, cs]
                dO = do_ref[rs, cs]
                b = _cumsum_rows(logf)
                b_last = b[SUB - 1:SUB, :]
                S = st_ref[hh, j]
                dS = ds_ref[hh]
                eb = jnp.exp(b)
                kd = jnp.exp(b_last - b)
                d = jnp.exp(b_last)
                qe, ke = qf * eb, k * kd
                dqe = _mxdot(dO, S, _NN)
                dke = _mxdot(v, dS, _NN)
                dv = _mxdot(ke, dS, _NT)
                dd = jnp.sum(dS * S, axis=0, keepdims=True)
                ds_ref[hh] = dS * d + _mxdot(dO, qe, _TN)
                dq_i = jnp.zeros((SUB, hd), f32)
                dk_i = jnp.zeros((SUB, hd), f32)
                for r in range(SUB):
                    em = jnp.where(row <= r, jnp.exp(jnp.minimum(b[r:r + 1, :] - b, 0.0)), 0.0)
                    ke_r = k * em
                    qr, dor = qf[r:r + 1, :], dO[r:r + 1, :]
                    p = jnp.sum(qr * ke_r, axis=1, keepdims=True)
                    dv = dv + p * dor
                    dp = jnp.sum(dor * v, axis=1, keepdims=True)
                    dq_i = jnp.where(row == r, jnp.sum(dp * ke_r, axis=0, keepdims=True), dq_i)
                    dk_i = dk_i + dp * (qr * em)
                dqf = dqe * eb + dq_i
                dk = dke * kd + dk_i
                dke_ke = dke * ke
                db = dqe * qe - dke_ke + qf * dq_i - k * dk_i
                db_last = jnp.sum(dke_ke, axis=0, keepdims=True) + dd * d
                db = db + jnp.where(row == SUB - 1, db_last, 0.0)
                dz, dqp, dlb = vjp_pt((_rev_cumsum_rows(db), dk, dqf))
                dq_ref[rs, cs] = dqp.astype(dq_ref.dtype)
                df_ref[rs, cs] = dz.astype(df_ref.dtype)
                dv_ref[rs, cs] = dv.astype(dv_ref.dtype)
                dlb_ref[:, cs] += dlb
            return carry

        lax.fori_loop(0, ns, sub, 0)

    def slab(c0):
        return pl.BlockSpec((tt, bw), functools.partial(lambda h, t, c0: (nt - 1 - t, c0 + h), c0=c0))

    hspec = pl.BlockSpec((tt, bw), lambda h, t: (nt - 1 - t, h))
    cspec = pl.BlockSpec((1, bw), lambda h, t: (0, h))
    return pl.pallas_call(
        body, name=name, grid=(N_HEADS // HGRN_HPB, nt),
        in_specs=[slab(d0), slab(q0), slab(f0), slab(v0), slab(g0), cspec, cspec, hspec,
                  pl.BlockSpec((HGRN_HPB, ns, hd, hd), lambda h, t: (h, nt - 1 - t, 0, 0))],
        out_specs=[hspec, hspec, hspec, hspec, cspec, cspec],
        out_shape=[jax.ShapeDtypeStruct((T, W), _MXU)] * 4 + [jax.ShapeDtypeStruct((1, W), f32)] * 2,
        scratch_shapes=[pltpu.VMEM((HGRN_HPB, hd, hd), f32), pltpu.VMEM((tt, bw), f32)],
        compiler_params=_params(("parallel", "arbitrary")),
    )(dy, proj, proj, proj, proj, lb, ng, o, states)


def _lower_bounds(logits, name):
    def fn(lg):
        sm = jax.nn.softmax(lg, axis=0)
        run, rows_ = None, []
        for j in range(lg.shape[0]):
            run = sm[j:j + 1, :] if run is None else run + sm[j:j + 1, :]
            rows_.append(run - sm[0:1, :])
        return jnp.concatenate(rows_, axis=0)
    return fn


def _lb_fwd(logits, name):
    fn = _lower_bounds(logits, name)

    def body(l_ref, o_ref):
        o_ref[...] = fn(l_ref[...])

    return pl.pallas_call(body, name=name, out_shape=jax.ShapeDtypeStruct(logits.shape, f32))(logits)


def _lb_bwd(logits, dlb, name):
    fn = _lower_bounds(logits, name)

    def body(l_ref, d_ref, o_ref):
        _, vjp = jax.vjp(fn, l_ref[...])
        o_ref[...] = vjp(d_ref[...])[0]

    return pl.pallas_call(body, name=name, out_shape=jax.ShapeDtypeStruct(logits.shape, f32))(logits, dlb)


def _ln_silu(d, g, b):
    mu = jnp.mean(d, axis=-1, keepdims=True)
    xc = d - mu
    y = xc * lax.rsqrt(jnp.mean(xc * xc, axis=-1, keepdims=True) + EPS)
    return jax.nn.silu(y * g + b)


def _ln_fwd(d, g, b, name):
    T, W = d.shape
    tt = _tile(T, 256, 8)

    def body(d_ref, g_ref, b_ref, o_ref):
        o_ref[...] = _ln_silu(d_ref[...], g_ref[...], b_ref[...]).astype(o_ref.dtype)

    return pl.pallas_call(
        body, name=name, grid=(T // tt,),
        in_specs=[pl.BlockSpec((tt, W), lambda t: (t, 0))] + [pl.BlockSpec((1, W), lambda t: (0, 0))] * 2,
        out_specs=pl.BlockSpec((tt, W), lambda t: (t, 0)), out_shape=jax.ShapeDtypeStruct((T, W), _MXU),
        compiler_params=_params(("parallel",)),
    )(d, g, b)


def _ln_bwd(dy, dcol0, d, g, b, name):
    T, W = d.shape
    tt = _tile(T, 256, 8)
    c0 = dcol0 // W
    assert dcol0 % W == 0

    def body(dy_ref, d_ref, g_ref, b_ref, dd_ref, dg_ref, db_ref):
        @pl.when(pl.program_id(0) == 0)
        def _():
            dg_ref[...] = jnp.zeros_like(dg_ref)
            db_ref[...] = jnp.zeros_like(db_ref)

        _, vjp = jax.vjp(_ln_silu, d_ref[...], g_ref[...], b_ref[...])
        dd, dg, db = vjp(dy_ref[...])
        dd_ref[...] = dd
        dg_ref[...] += dg
        db_ref[...] += db

    cspec = pl.BlockSpec((1, W), lambda t: (0, 0))
    return pl.pallas_call(
        body, name=name, grid=(T // tt,),
        in_specs=[pl.BlockSpec((tt, W), lambda t: (t, c0)), pl.BlockSpec((tt, W), lambda t: (t, 0)), cspec, cspec],
        out_specs=[pl.BlockSpec((tt, W), lambda t: (t, 0)), cspec, cspec],
        out_shape=[jax.ShapeDtypeStruct((T, W), f32), jax.ShapeDtypeStruct((1, W), f32), jax.ShapeDtypeStruct((1, W), f32)],
        compiler_params=_params(("arbitrary",)),
    )(dy, d, g, b)


def _adamw(w, m, v, parts, name):
    R, C = w.shape
    tr, tc = _tile(R, 512, 8), _tile(C, 512, LANE)
    npart = len(parts)

    def body(*refs):
        w_ref, m_ref, v_ref = refs[:3]
        g_ref, d_ref, mo_ref, vo_ref = refs[3 + npart:]
        g = refs[3][...].astype(f32)
        for p_ref in refs[4:3 + npart]:
            g = g + p_ref[...].astype(f32)
        mm = ADAM_B1 * m_ref[...] + (1.0 - ADAM_B1) * g
        vv = ADAM_B2 * v_ref[...] + (1.0 - ADAM_B2) * jnp.square(g)
        m_hat = mm / (1.0 - ADAM_B1 ** ADAM_STEP)
        v_hat = vv / (1.0 - ADAM_B2 ** ADAM_STEP)
        g_ref[...] = g
        d_ref[...] = -ADAM_LR * (m_hat / (jnp.sqrt(v_hat) + ADAM_EPS) + ADAM_WD * w_ref[...])
        mo_ref[...] = mm
        vo_ref[...] = vv

    spec = pl.BlockSpec((tr, tc), lambda i, j: (i, j))
    return pl.pallas_call(
        body, name=name, grid=(R // tr, C // tc), in_specs=[spec] * (3 + npart), out_specs=[spec] * 4,
        out_shape=[jax.ShapeDtypeStruct((R, C), f32)] * 4, compiler_params=_params(("parallel", "parallel")),
    )(w, m, v, *parts)


def _adamw_layer(w3, m3, v3, layer, parts, prev, name):
    L, R, C = w3.shape
    tr, tc = _tile(R, 128, 8), _tile(C, 2048, LANE)
    npart = len(parts)

    def body(*refs):
        w_ref, m_ref, v_ref = refs[:3]
        g_ref, d_ref, mo_ref, vo_ref = refs[3 + npart + 4:]
        g = refs[3][...].astype(f32)
        for p_ref in refs[4:3 + npart]:
            g = g + p_ref[...].astype(f32)
        mm = ADAM_B1 * m_ref[...] + (1.0 - ADAM_B1) * g
        vv = ADAM_B2 * v_ref[...] + (1.0 - ADAM_B2) * jnp.square(g)
        m_hat = mm / (1.0 - ADAM_B1 ** ADAM_STEP)
        v_hat = vv / (1.0 - ADAM_B2 ** ADAM_STEP)
        g_ref[...] = g
        d_ref[...] = -ADAM_LR * (m_hat / (jnp.sqrt(v_hat) + ADAM_EPS) + ADAM_WD * w_ref[...])
        mo_ref[...] = mm
        vo_ref[...] = vv

    spec3 = pl.BlockSpec((None, tr, tc), lambda i, j: (layer, i, j))
    spec2 = pl.BlockSpec((tr, tc), lambda i, j: (i, j))
    return pl.pallas_call(
        body, name=name, grid=(R // tr, C // tc), in_specs=[spec3] * 3 + [spec2] * npart + [_ANY] * 4,
        out_specs=[spec3] * 4, out_shape=[jax.ShapeDtypeStruct((L, R, C), f32)] * 4,
        input_output_aliases={3 + npart + k: k for k in range(4)},
        compiler_params=_params(("parallel", "parallel")),
    )(w3, m3, v3, *parts, *prev)


def _sum_own_recv(g, chip, col, recv, name):
    S, R, C = recv.shape
    tr, tc = _tile(R, 256, 8), _tile(C, 2048, LANE)
    nbr, nbc = R // tr, C // tc

    def body(chip_ref, o_ref, r_ref, out_ref):
        acc = o_ref[...].astype(f32)
        for s in range(S):
            acc = acc + r_ref[s].astype(f32)
        out_ref[...] = acc

    if col:
        own_map = lambda i, j, c: (i, c[0] * nbc + j)
    else:
        own_map = lambda i, j, c: (c[0] * nbr + i, j)
    grid_spec = pltpu.PrefetchScalarGridSpec(
        num_scalar_prefetch=1, grid=(nbr, nbc),
        in_specs=[pl.BlockSpec((tr, tc), own_map), pl.BlockSpec((S, tr, tc), lambda i, j, c: (0, i, j))],
        out_specs=pl.BlockSpec((tr, tc), lambda i, j, c: (i, j)))
    return pl.pallas_call(
        body, name=name, grid_spec=grid_spec, out_shape=jax.ShapeDtypeStruct((R, C), f32),
        compiler_params=_params(("parallel", "parallel")),
    )(chip, g, recv)


def _sum_slots(recv, name):
    S, R, C = recv.shape
    tr, tc = _tile(R, 512, 8), _tile(C, 512, LANE)

    def body(r_ref, o_ref):
        acc = r_ref[0].astype(f32)
        for s in range(1, S):
            acc = acc + r_ref[s].astype(f32)
        o_ref[...] = acc

    return pl.pallas_call(
        body, name=name, grid=(R // tr, C // tc), in_specs=[pl.BlockSpec((S, tr, tc), lambda i, j: (0, i, j))],
        out_specs=pl.BlockSpec((tr, tc), lambda i, j: (i, j)), out_shape=jax.ShapeDtypeStruct((R, C), f32),
        compiler_params=_params(("parallel", "parallel")),
    )(recv)


_CHIP_FLIPS = ((1, 0), (0, 1), (1, 1))
_ANY = pl.BlockSpec(memory_space=pl.ANY)


def _me():
    return lax.axis_index("x"), lax.axis_index("y"), lax.axis_index("c")


def _allgather_chips(local, col, name):
    L, R, C = local.shape
    out_shape = (L, R, 4 * C) if col else (L, 4 * R, C)

    def body(loc_ref, out_ref, send_sems, recv_sems, local_sem):
        x, y, c = _me()

        def block(px, py):
            s = 2 * px + py
            if col:
                return out_ref.at[:, :, pl.ds(s * C, C)]
            return out_ref.at[:, pl.ds(s * R, R), :]

        mine = pltpu.make_async_copy(loc_ref, block(x, y), local_sem)
        mine.start()
        sends = []
        for j, (fx, fy) in enumerate(_CHIP_FLIPS):
            cp = pltpu.make_async_remote_copy(src_ref=loc_ref, dst_ref=block(x, y), send_sem=send_sems.at[j],
                                              recv_sem=recv_sems.at[j], device_id=(x ^ fx, y ^ fy, c), device_id_type=MESH)
            cp.start()
            sends.append(cp)
        for j, (fx, fy) in enumerate(_CHIP_FLIPS):
            pltpu.make_async_remote_copy(src_ref=loc_ref, dst_ref=block(x ^ fx, y ^ fy), send_sem=send_sems.at[j],
                                         recv_sem=recv_sems.at[j], device_id=(x ^ fx, y ^ fy, c),
                                         device_id_type=MESH).wait_recv()
        for cp in sends:
            cp.wait_send()
        mine.wait()

    return pl.pallas_call(
        body, name=name, in_specs=[_ANY], out_specs=_ANY, out_shape=jax.ShapeDtypeStruct(out_shape, local.dtype),
        scratch_shapes=[pltpu.SemaphoreType.DMA((3,)), pltpu.SemaphoreType.DMA((3,)), pltpu.SemaphoreType.DMA],
    )(local)


_HBM = pl.BlockSpec(memory_space=pltpu.HBM)
_SEM = pl.BlockSpec(memory_space=pltpu.SEMAPHORE)
_EFFECT = pltpu.SideEffectType.DATAFLOW_SIDE_EFFECTING


def _chip_block(ref, s, col, n):
    return ref.at[:, pl.ds(s * n, n)] if col else ref.at[pl.ds(s * n, n), :]


_NSEM = {'gather': 7, 'relay': 6, 'scatter': 6, 'swap': 2, 'all': 15}


def _half_block(ref, s, col, n, half):
    rows = (ref.shape[0] if col else n) // 2
    if col:
        return ref.at[pl.ds(half * rows, rows), pl.ds(s * n, n)]
    return ref.at[pl.ds(s * n + half * rows, rows), :]


def _xchg_copies(kind, src_ref, land_ref, col, sems):
    x, y, c = _me()
    if kind == 'swap':
        return [pltpu.make_async_remote_copy(src_ref=src_ref, dst_ref=land_ref, send_sem=sems[0], recv_sem=sems[1],
                                             device_id=(x, y, 1 - c), device_id_type=MESH)]
    if kind == 'all':
        me_slot = land_ref.at[4 * x + 2 * y + c]
        cps = [pltpu.make_async_copy(src_ref, me_slot, sems[14])]
        for m in range(1, 8):
            cps.append(pltpu.make_async_remote_copy(
                src_ref=src_ref, dst_ref=me_slot, send_sem=sems[2 * m - 2], recv_sem=sems[2 * m - 1],
                device_id=(x ^ (m >> 2), y ^ ((m >> 1) & 1), c ^ (m & 1)), device_id_type=MESH))
        return cps
    cps = []
    if kind == 'gather':
        n = src_ref.shape[1] if col else src_ref.shape[0]
        cps.append(pltpu.make_async_copy(src_ref, _chip_block(land_ref, 2 * x + y, col, n), sems[6]))
    for j, (fx, fy) in enumerate(_CHIP_FLIPS):
        px, py = x ^ fx, y ^ fy
        peer = (px, py, c)
        if kind == 'gather':
            rows = src_ref.shape[0] // 2
            src, dst = src_ref.at[pl.ds(c * rows, rows), :], _half_block(land_ref, 2 * x + y, col, n, c)
        elif kind == 'relay':
            n = land_ref.shape[1] // 4 if col else land_ref.shape[0] // 4
            src = dst = _half_block(land_ref, 2 * px + py, col, n, c)
            peer = (x, y, 1 - c)
        else:
            n = land_ref.shape[2] if col else land_ref.shape[1]
            src, dst = _chip_block(src_ref, 2 * px + py, col, n), land_ref.at[j]
        cps.append(pltpu.make_async_remote_copy(src_ref=src, dst_ref=dst, send_sem=sems[2 * j], recv_sem=sems[2 * j + 1],
                                                device_id=peer, device_id_type=MESH))
    return cps


def _xchg_start(kind, srcs, lands, cols, after, name):
    n = len(lands)
    arrs = (list(srcs) if srcs is not None else []) + list(lands)
    na = len(arrs)
    per = _NSEM[kind]
    nsem = per * n

    def body(*refs):
        land_refs = refs[na - n:na]
        src_refs = refs[:n] if srcs is not None else land_refs
        sems = refs[na + 1:na + 1 + nsem]
        token = refs[-1]
        for i in range(n):
            for cp in _xchg_copies(kind, src_refs[i], land_refs[i], cols[i], sems[per * i:per * i + per]):
                cp.start()
        token[...] = jnp.zeros_like(token)

    hbm = lambda a: pltpu.HBM(a.shape, a.dtype)
    outs = pl.pallas_call(
        body, name=name,
        out_shape=tuple([pltpu.SemaphoreType.DMA(())] * nsem + [hbm(a) for a in arrs] + [jax.ShapeDtypeStruct((8, LANE), f32)]),
        in_specs=tuple([_HBM] * na + [_ANY]),
        out_specs=tuple([_SEM] * nsem + [_HBM] * na + [pl.BlockSpec(memory_space=pltpu.VMEM)]),
        input_output_aliases={i: nsem + i for i in range(na)},
        compiler_params=pltpu.CompilerParams(has_side_effects=_EFFECT),
    )(*[pltpu.with_memory_space_constraint(a, pltpu.HBM) for a in arrs], after)
    thru = outs[nsem:nsem + na]
    return outs[:nsem], (thru[:n] if srcs is not None else None), thru[na - n:], outs[-1]


def _xchg_wait(kind, started, cols, after, name):
    sems, srcs, lands, _ = started
    n = len(lands)
    arrs = (list(srcs) if srcs is not None else []) + list(lands)
    na = len(arrs)
    per = _NSEM[kind]
    nsem = per * n

    def body(*refs):
        land_refs = refs[na - n:na]
        src_refs = refs[:n] if srcs is not None else land_refs
        sem_refs = refs[na:na + nsem]
        for i in range(n):
            for cp in _xchg_copies(kind, src_refs[i], land_refs[i], cols[i], sem_refs[per * i:per * i + per]):
                if cp.is_remote:
                    cp.wait_send()
                    cp.wait_recv()
                else:
                    cp.wait()

    hbm = lambda a: pltpu.HBM(a.shape, a.dtype)
    outs = pl.pallas_call(
        body, name=name, out_shape=tuple(hbm(a) for a in arrs),
        in_specs=tuple([_HBM] * na + [_SEM] * nsem + [_ANY]), out_specs=tuple([_HBM] * na),
        input_output_aliases={i: i for i in range(na)},
        compiler_params=pltpu.CompilerParams(has_side_effects=_EFFECT),
    )(*arrs, *sems, after)
    return (outs[:n] if srcs is not None else None), outs[na - n:]


_PACK_ROWS = 512


def _pack(arrs):
    flat = jnp.concatenate([a.reshape(-1).astype(f32) for a in arrs])
    n = flat.shape[0]
    rows = -(-n // (_PACK_ROWS * LANE)) * _PACK_ROWS
    return jnp.pad(flat, (0, rows * LANE - n)).reshape(rows, LANE)


def _unpack(buf, shapes):
    flat, outs, off = buf.reshape(-1), [], 0
    for s in shapes:
        n = 1
        for d_ in s:
            n *= d_
        outs.append(flat[off:off + n].reshape(s))
        off += n
    return outs


def _step(p):
    x0 = p['x'][0]
    tgt = p['loss_target'][0]
    T, D = x0.shape
    W = D // 2
    depth = p['ln_mix_g'].shape[0]
    chip = 2 * lax.axis_index("x") + lax.axis_index("y")
    chip1 = jnp.reshape(chip, (1,)).astype(jnp.int32)

    def group_names(layer, grp):
        if grp == 'ffn':
            return [('ffn_w_gate', layer), ('ffn_w_up', layer), ('ffn_w_down', layer)]
        pre = 'ev' if layer % 2 == 0 else 'od'
        return [(pre + '_w_in', layer // 2), (pre + '_w_out', layer // 2)]

    def gather_start(layer, grp, after):
        srcs, lands, cols = [], [], []
        for n, l in group_names(layer, grp):
            loc = p[n][l].astype(_MXU)
            R, C = loc.shape
            col = _BIG_COL[n]
            land = lax.empty((R, 4 * C) if col else (4 * R, C), _MXU)
            srcs.append(loc)
            lands.append(land)
            cols.append(col)
        return _xchg_start('gather', srcs, lands, cols, after, name=f"ag_start_{grp}{layer}"), cols

    def gather_relay(started, layer, grp, after):
        st, cols = started
        _, lands = _xchg_wait('gather', st, cols, after, name=f"ag_wait_{grp}{layer}")
        return _xchg_start('relay', None, lands, cols, chip1, name=f"ag_relay_{grp}{layer}"), cols

    def gather_finish(relayed, layer, grp, after):
        st, cols = relayed
        _, lands = _xchg_wait('relay', st, cols, after, name=f"ag_done_{grp}{layer}")
        return dict(zip([n for n, _ in group_names(layer, grp)], lands))

    groups = [(layer, grp) for layer in range(depth) for grp in ('mix', 'ffn')]
    pending, relayed = {}, {}
    loose = []

    def take_weights(gi, after):
        wts = gather_finish(relayed.pop(groups[gi]), *groups[gi], after)
        if gi + 2 < len(groups):
            pending[groups[gi + 2]] = gather_start(*groups[gi + 2], next(iter(wts.values())))
            loose.append(pending[groups[gi + 2]][0][3])
        return wts

    def advance(gi, after):
        if gi + 1 < len(groups):
            relayed[groups[gi + 1]] = gather_relay(pending.pop(groups[gi + 1]), *groups[gi + 1], after)
            loose.append(relayed[groups[gi + 1]][0][3])

    def tied(a):
        a = a.reshape(1, -1)
        while loose:
            a = a + loose.pop()[0:1, 0:1]
        return a

    small_rows = []
    cl = W // 4
    for n in _SMALL_SHARDED:
        small_rows.append(p[n].reshape(-1, cl))
    srows = [a.shape[0] for a in small_rows]
    spack = jnp.concatenate(small_rows, axis=0)
    spad = -(-spack.shape[0] // 8) * 8
    spack = jnp.pad(spack, ((0, spad - spack.shape[0]), (0, 0)))
    sfull = _allgather_chips(spack[None], False, name="ag_small")[0].reshape(4, spad, cl)
    pending[groups[0]] = gather_start(*groups[0], sfull)
    pending[groups[1]] = gather_start(*groups[1], pending[groups[0]][0][3])
    relayed[groups[0]] = gather_relay(pending.pop(groups[0]), *groups[0], pending[groups[1]][0][3])
    loose.append(relayed[groups[0]][0][3])
    small = {}
    off = 0
    for n, r in zip(_SMALL_SHARDED, srows):
        blk = sfull[:, off:off + r, :]
        lead = p[n].shape[:-1]
        q = p[n].shape[-1] // cl
        blk = blk.reshape((4,) + lead + (q, cl))
        blk = jnp.moveaxis(blk, 0, len(lead))
        small[n] = blk.reshape(lead + (4 * q * cl,))
        off += r

    lbs = _lb_fwd(p['hgrn_lb_logits'], name="lb_fwd")

    def row(a, tok=None):
        a = a.reshape(1, -1)
        return a if tok is None else a + tok[0:1, 0:1]

    saved = []
    full = {}
    x = x0
    for layer in range(depth):
        j = layer // 2
        s = {'x_in': x}
        wts = take_weights(2 * layer, x)
        full[layer] = wts
        h = _rmsnorm_fwd(x, tied(p['ln_mix_g'][layer]), name="rms_fwd")
        s['h'] = h
        if layer % 2 == 0:
            proj = _matmul(h, wts['ev_w_in'], 'nn', f32, "mm_ev_in", bias=row(p['ev_b_in'][j]))
            advance(2 * layer, proj)
            xc = _conv_fwd(proj, 0, small['lru_conv_w'][j], tied(p['lru_conv_b'][j]), W, name="lru_conv_fwd")
            ba, bx = p['lru_ba'][j][:, None, :], p['lru_bx'][j][:, None, :]
            a, u = _lru_gates_fwd(xc, p['lru_wa'][j], ba, p['lru_wx'][j], bx, row(p['lru_lambda'][j]), name="lru_gates_fwd")
            hl, y_a = _lru_scan_fwd(a, u, proj, W, name="lru_scan_fwd")
            y_b, o, states = _hgrn_fwd(proj, 2 * W, 3 * W, 4 * W, 5 * W, row(lbs[j]), row(p['hgrn_norm_g'][j]),
                                       name="hgrn_fwd")
            s.update(proj=proj, xc=xc, a=a, hl=hl, o=o, states=states)
            ycat = jnp.concatenate([y_a, y_b], axis=1)
            w_out = wts['ev_w_out']
        else:
            proj = _matmul(h, wts['od_w_in'], 'nn', f32, "mm_od_in", bias=row(small['od_b_in'][j]))
            advance(2 * layer, proj)
            pp = _pw(lambda a_, b_: a_ * b_, [(proj, W), (proj, 2 * W)], [f32], W, name="sc_mul")
            cp = _conv_fwd(pp, 0, small['sc_conv_w'][j], None, W, name="sc_conv_fwd")
            y_c = _pw(lambda a_, b_: a_ * b_, [(proj, 0), (cp, 0)], [_MXU], W, name="sc_out")
            glu = _pw(lambda a_, b_: a_ * jax.nn.sigmoid(b_), [(proj, 3 * W), (proj, 4 * W)], [f32], W, name="cf_glu")
            dcv = _conv_fwd(glu, 0, small['cf_conv_w'][j], tied(small['cf_conv_b'][j]), W, name="cf_conv_fwd")
            y_d = _ln_fwd(dcv, row(small['cf_ln_g'][j]), row(small['cf_ln_b'][j]), name="cf_ln_fwd")
            s.update(proj=proj, pp=pp, cp=cp, glu=glu, dcv=dcv)
            ycat = jnp.concatenate([y_c, y_d], axis=1)
            w_out = wts['od_w_out']
        s['ycat'] = ycat
        x = _matmul(ycat, w_out, 'nn', f32, "mm_mix_out", add=x)
        s['x_mid'] = x
        wts = take_weights(2 * layer + 1, x)
        full[layer].update(wts)
        h2 = _rmsnorm_fwd(x, tied(p['ln_ffn_g'][layer]), name="rms_fwd")
        gate, up, act = _ffn_in(h2, wts['ffn_w_gate'], wts['ffn_w_up'], name="ffn_in")
        advance(2 * layer + 1, act)
        x = _matmul(act, wts['ffn_w_down'], 'nn', f32, "mm_ffn_out", add=x, tk=1408)
        s.update(h2=h2, gate=gate, up=up, act=act)
        saved.append(s)

    loss_b, dx, dg_final = _final_loss(x, tied(p['ln_final_g']), tgt, name="final_loss")

    scat = {}
    tok = None

    def scatter_start(layer, grp, grads):
        srcs, lands, cols = [], [], []
        for (n, l), g in zip(group_names(layer, grp), grads, strict=True):
            R, C = p[n].shape[1:]
            srcs.append(g)
            lands.append(lax.empty((3, R, C), _WIRE))
            cols.append(_BIG_COL[n])
        st = _xchg_start('scatter', srcs, lands, cols, chip1, name=f"rs_start_{grp}{layer}")
        scat[(layer, grp)] = (st, cols)
        return st[3]

    gs = {n: [None] * p[n].shape[0] for n in _IN_NAMES[1:] if n not in _BIG and n != 'ln_final_g'}
    for layer in reversed(range(depth)):
        j = layer // 2
        s = saved[layer]
        F = s['gate'].shape[1]
        wts = full[layer]
        dgate, dup = _ffn_dact(dx, wts['ffn_w_down'], s['gate'], s['up'], name="ffn_dact")
        g_down = _matmul(s['act'], dx, 'tn', _WIRE, "mm_dw_down", tm=1408, tn=1024, tk=512)
        g_gate = _matmul(s['h2'], dgate, 'tn', _WIRE, "mm_dw_in", tm=1024, tn=1408, tk=512)
        g_up = _matmul(s['h2'], dup, 'tn', _WIRE, "mm_dw_in", tm=1024, tn=1408, tk=512)
        tok = scatter_start(layer, 'ffn', [g_gate, g_up, g_down])
        dh2 = _matmul(dgate, wts['ffn_w_gate'], 'nt', f32, "mm_ffn_dh", tk=1408, second=(dup, wts['ffn_w_up']))
        dx, gs['ln_ffn_g'][layer] = _rmsnorm_bwd(dh2, s['x_mid'], row(p['ln_ffn_g'][layer], tok), dx, name="rms_bwd")
        if layer % 2 == 0:
            w_out, w_in, n_out, n_in = wts['ev_w_out'], wts['ev_w_in'], 'ev_w_out', 'ev_w_in'
        else:
            w_out, w_in, n_out, n_in = wts['od_w_out'], wts['od_w_in'], 'od_w_out', 'od_w_in'
        dycat = _matmul(dx, w_out, 'nt', f32, "mm_mix_dy")
        g_out = _matmul(s['ycat'], dx, 'tn', _WIRE, "mm_dw_out", tm=1024, tn=1024, tk=512)
        proj = s['proj']
        if layer % 2 == 0:
            ba, bx = p['lru_ba'][j][:, None, :], p['lru_bx'][j][:, None, :]
            lam_g, dgate_a = _lru_scan_bwd(dycat, 0, proj, W, s['hl'], s['a'], name="lru_scan_bwd")
            dxc, dwa, dba, dwx, dbx, dlam = _lru_gates_bwd(lam_g, s['hl'], s['xc'], p['lru_wa'][j], ba, p['lru_wx'][j], bx,
                                                           row(p['lru_lambda'][j]), name="lru_gates_bwd")
            dxa = _conv_bwd_dx(dxc, small['lru_conv_w'][j], name="lru_conv_dx", out_dtype=_MXU)
            dcw, dcb = _conv_bwd_dw(dxc, proj, 0, small['lru_conv_w'][j].shape[0], name="lru_conv_dw")
            dq, df, dv, dgp, dlb, dng = _hgrn_bwd(dycat, W, proj, 2 * W, 3 * W, 4 * W, 5 * W, row(lbs[j]),
                                                  row(p['hgrn_norm_g'][j]), s['o'], s['states'], name="hgrn_bwd")
            gs['lru_wa'][j], gs['lru_ba'][j], gs['lru_wx'][j], gs['lru_bx'][j] = dwa, dba[:, 0, :], dwx, dbx[:, 0, :]
            gs['lru_lambda'][j], gs['lru_conv_w'][j], gs['lru_conv_b'][j] = dlam[0], dcw, dcb[0]
            gs['hgrn_lb_logits'][j], gs['hgrn_norm_g'][j] = dlb[0], dng[0]
            dproj = jnp.concatenate([dxa, dgate_a, dq, df, dv, dgp], axis=1)
        else:
            def sc_bwd1(dy_, cp_, sb_):
                return dy_ * cp_, dy_ * sb_

            dsb, dcp = _pw(sc_bwd1, [(dycat, 0), (s['cp'], 0), (proj, 0)], [_MXU, f32], W, name="sc_bwd1")
            dpp = _conv_bwd_dx(dcp, small['sc_conv_w'][j], name="sc_conv_dx")
            dscw, _ = _conv_bwd_dw(dcp, s['pp'], 0, small['sc_conv_w'][j].shape[0], name="sc_conv_dw")

            def sc_bwd2(dp_, sc_, sv_):
                return dp_ * sv_, dp_ * sc_

            dsc, dsv = _pw(sc_bwd2, [(dpp, 0), (proj, W), (proj, 2 * W)], [_MXU, _MXU], W, name="sc_bwd2")
            dd, dlg, dlbeta = _ln_bwd(dycat, W, s['dcv'], row(small['cf_ln_g'][j]), row(small['cf_ln_b'][j]),
                                      name="cf_ln_bwd")
            dglu = _conv_bwd_dx(dd, small['cf_conv_w'][j], name="cf_conv_dx")
            dcfw, dcfb = _conv_bwd_dw(dd, s['glu'], 0, small['cf_conv_w'][j].shape[0], name="cf_conv_dw")

            def glu_bwd(dg_, cu_, cg_):
                _, vjp = jax.vjp(lambda a_, b_: a_ * jax.nn.sigmoid(b_), cu_, cg_)
                return vjp(dg_)

            dcu, dcg = _pw(glu_bwd, [(dglu, 0), (proj, 3 * W), (proj, 4 * W)], [_MXU, _MXU], W, name="cf_glu_bwd")
            gs['sc_conv_w'][j], gs['cf_conv_w'][j], gs['cf_conv_b'][j] = dscw, dcfw, dcfb[0]
            gs['cf_ln_g'][j], gs['cf_ln_b'][j] = dlg[0], dlbeta[0]
            dproj = jnp.concatenate([dsb, dsc, dsv, dcu, dcg], axis=1)
        bname = 'ev_b_in' if layer % 2 == 0 else 'od_b_in'
        gs[bname][j] = _colsum(dproj, name="colsum_" + n_in)[0]
        g_in = _matmul(s['h'], dproj, 'tn', _WIRE, "mm_dw_" + n_in, tm=1024, tn=1536, tk=512)
        tok = scatter_start(layer, 'mix', [g_in, g_out])
        dh = _matmul(dproj, w_in, 'nt', f32, "mm_dh_" + n_in)
        dx, gs['ln_mix_g'][layer] = _rmsnorm_bwd(dh, s['x_in'], row(p['ln_mix_g'][layer], tok), dx, name="rms_bwd")

    g_small = {n: jnp.stack(v_) for n, v_ in gs.items()}
    g_small['ln_mix_g'] = g_small['ln_mix_g'][:, 0, :]
    g_small['ln_ffn_g'] = g_small['ln_ffn_g'][:, 0, :]
    g_small['ln_final_g'] = dg_final[0]
    g_small['hgrn_lb_logits'] = _lb_bwd(p['hgrn_lb_logits'], g_small['hgrn_lb_logits'], name="lb_bwd")
    small_names = _SMALL_REPL + _SMALL_SHARDED
    pack = _pack([g_small[n] for n in small_names])
    small_xchg = _xchg_start('all', [pack], [lax.empty((8,) + pack.shape, f32)], [None], chip1, name="small_grads_start")
    outs = {}

    acc = {n: tuple(lax.empty(p[n].shape, f32) for _ in range(4)) for n in _BIG}
    after_tail = small_xchg[3]

    def finish(swap, after):
        st, names, tag = swap
        parts, others = _xchg_wait('swap', st, [None] * len(names), after, name=f"swap_wait_{tag}")
        for (n, l), part, other in zip(names, parts, others, strict=True):
            acc[n] = tuple(_adamw_layer(p[n], p['m_' + n], p['v_' + n], l, [part, other], acc[n], name=f"adamw_{n}"))

    prev_swap = None
    for layer, grp in reversed(groups):
        st, cols = scat.pop((layer, grp))
        srcs, recvs = _xchg_wait('scatter', st, cols, after_tail, name=f"rs_wait_{grp}{layer}")
        names = group_names(layer, grp)
        parts = [_sum_own_recv(g, chip1, col, recv, name=f"sum_{n}")
                 for (n, l), g, recv, col in zip(names, srcs, recvs, cols, strict=True)]
        st = _xchg_start('swap', parts, [lax.empty(a.shape, f32) for a in parts], [None] * len(parts), recvs[0],
                         name=f"swap_start_{grp}{layer}")
        if prev_swap is not None:
            finish(prev_swap, st[3])
        prev_swap = (st, names, f"{grp}{layer}")
    finish(prev_swap, prev_swap[0][3])
    for n in _BIG:
        for kind, buf in zip(('grad', 'delta', 'new_m', 'new_v'), acc[n]):
            outs[kind + '_' + n] = buf

    _, (recv8,) = _xchg_wait('all', small_xchg, [None], acc[_BIG[0]][0], name="small_grads_wait")
    tot = _sum_slots(recv8, name="sum_small_grads")
    g_tot = dict(zip(small_names, _unpack(tot, [g_small[n].shape for n in small_names])))
    for n in _SMALL_SHARDED:
        lead = p[n].shape[:-1]
        q = p[n].shape[-1] // cl
        blk = g_tot[n].reshape(lead + (4, q * cl))
        g_tot[n] = lax.dynamic_index_in_dim(blk, chip, axis=len(lead), keepdims=False)
    shapes = [p[n].shape for n in small_names]
    res = _adamw(_pack([p[n] for n in small_names]), _pack([p['m_' + n] for n in small_names]),
                 _pack([p['v_' + n] for n in small_names]), [_pack([g_tot[n] for n in small_names])], name="adamw_small")
    for kind, buf in zip(('grad', 'delta', 'new_m', 'new_v'), res):
        for n, a in zip(small_names, _unpack(buf, shapes)):
            outs[kind + '_' + n] = a

    loss = lax.psum(loss_b[0, 0], ("x", "y", "c"))
    weights = _IN_NAMES[1:]
    return (loss, dx[None], *[outs['grad_' + n] for n in weights], *[outs['delta_' + n] for n in weights],
            *[outs['new_m_' + n] for n in weights], *[outs['new_v_' + n] for n in weights])


def kernel(x, ln_mix_g, ln_ffn_g, ln_final_g, ev_w_in, ev_b_in, lru_conv_w, lru_conv_b, lru_wa, lru_ba, lru_wx, lru_bx, lru_lambda, hgrn_lb_logits, hgrn_norm_g, ev_w_out, od_w_in, od_b_in, sc_conv_w, cf_conv_w, cf_conv_b, cf_ln_g, cf_ln_b, od_w_out, ffn_w_gate, ffn_w_up, ffn_w_down, loss_target, m_ln_mix_g, m_ln_ffn_g, m_ln_final_g, m_ev_w_in, m_ev_b_in, m_lru_conv_w, m_lru_conv_b, m_lru_wa, m_lru_ba, m_lru_wx, m_lru_bx, m_lru_lambda, m_hgrn_lb_logits, m_hgrn_norm_g, m_ev_w_out, m_od_w_in, m_od_b_in, m_sc_conv_w, m_cf_conv_w, m_cf_conv_b, m_cf_ln_g, m_cf_ln_b, m_od_w_out, m_ffn_w_gate, m_ffn_w_up, m_ffn_w_down, v_ln_mix_g, v_ln_ffn_g, v_ln_final_g, v_ev_w_in, v_ev_b_in, v_lru_conv_w, v_lru_conv_b, v_lru_wa, v_lru_ba, v_lru_wx, v_lru_bx, v_lru_lambda, v_hgrn_lb_logits, v_hgrn_norm_g, v_ev_w_out, v_od_w_in, v_od_b_in, v_sc_conv_w, v_cf_conv_w, v_cf_conv_b, v_cf_ln_g, v_cf_ln_b, v_od_w_out, v_ffn_w_gate, v_ffn_w_up, v_ffn_w_down):
    vals = locals()
    p = {n: vals[n] for n in _IN_NAMES + ['loss_target']}
    for n in _IN_NAMES[1:]:
        p['m_' + n] = vals['m_' + n]
        p['v_' + n] = vals['v_' + n]
    return _step(p)
```

```python
import functools

import jax
import jax.numpy as jnp
from jax import lax
from jax.experimental import pallas as pl
from jax.experimental.pallas import tpu as pltpu

f32 = jnp.float32
_MXU = jnp.bfloat16
_WIRE = jnp.bfloat16

N_HEADS = 8
LRU_C = 8.0
EPS = 1e-6
F_FLOOR = 1e-30
SUB = 16
HGRN_HPB = 2
ADAM_LR, ADAM_B1, ADAM_B2, ADAM_EPS, ADAM_WD, ADAM_STEP = 0.001, 0.9, 0.999, 1e-08, 0.01, 10
V7X_VMEM_LIMIT = 48 * 1024 * 1024
LANE = 128
MESH = pl.DeviceIdType.MESH

_IN_NAMES = ['x', 'ln_mix_g', 'ln_ffn_g', 'ln_final_g', 'ev_w_in', 'ev_b_in', 'lru_conv_w', 'lru_conv_b', 'lru_wa', 'lru_ba',
             'lru_wx', 'lru_bx', 'lru_lambda', 'hgrn_lb_logits', 'hgrn_norm_g', 'ev_w_out', 'od_w_in', 'od_b_in', 'sc_conv_w',
             'cf_conv_w', 'cf_conv_b', 'cf_ln_g', 'cf_ln_b', 'od_w_out', 'ffn_w_gate', 'ffn_w_up', 'ffn_w_down']
_BIG = ['ev_w_in', 'ev_w_out', 'od_w_in', 'od_w_out', 'ffn_w_gate', 'ffn_w_up', 'ffn_w_down']
_BIG_COL = {'ev_w_in': True, 'ev_w_out': False, 'od_w_in': True, 'od_w_out': False, 'ffn_w_gate': True, 'ffn_w_up': True,
            'ffn_w_down': False}
_SMALL_SHARDED = ['lru_conv_w', 'od_b_in', 'sc_conv_w', 'cf_conv_w', 'cf_conv_b', 'cf_ln_g', 'cf_ln_b']
_SMALL_REPL = ['ln_mix_g', 'ln_ffn_g', 'ln_final_g', 'ev_b_in', 'lru_conv_b', 'lru_wa', 'lru_ba', 'lru_wx', 'lru_bx',
               'lru_lambda', 'hgrn_lb_logits', 'hgrn_norm_g']


def _tile(n, pref, align):
    if n <= pref:
        return n
    t = (pref // align) * align
    while t >= align:
        if n % t == 0:
            return t
        t -= align
    return n


def _params(sem):
    return pltpu.CompilerParams(dimension_semantics=sem, vmem_limit_bytes=V7X_VMEM_LIMIT)


def _rows(shape):
    return lax.broadcasted_iota(jnp.int32, shape, 0)


def _mxdot(a, b, dims=(((1,), (0,)), ((), ()))):
    return lax.dot_general(a.astype(_MXU), b.astype(_MXU), dims, preferred_element_type=f32)


_NN = (((1,), (0,)), ((), ()))
_NT = (((1,), (1,)), ((), ()))
_TN = (((0,), (0,)), ((), ()))


def _matmul(a, b, mode, out_dtype, name, bias=None, add=None, tm=1024, tn=512, tk=2048, second=None):
    if mode == 'nn':
        (M, K), (K2, N) = a.shape, b.shape
    elif mode == 'nt':
        (M, K), (N, K2) = a.shape, b.shape
    else:
        (K, M), (K2, N) = a.shape, b.shape
    assert K == K2, (name, a.shape, b.shape)
    tm, tn, tk = _tile(M, tm, LANE), _tile(N, tn, LANE), _tile(K, tk, LANE)
    nk = K // tk
    dims = {'nn': _NN, 'nt': _NT, 'tn': _TN}[mode]
    has_bias, has_add, has_second = bias is not None, add is not None, second is not None

    def body(*refs):
        a_ref, b_ref = refs[0], refs[1]
        pos = 2
        if has_second:
            a2_ref, b2_ref = refs[2], refs[3]
            pos = 4
        bias_ref = add_ref = None
        if has_bias:
            bias_ref = refs[pos]
            pos += 1
        if has_add:
            add_ref = refs[pos]
            pos += 1
        o_ref, acc_ref = refs[pos], refs[pos + 1]
        k = pl.program_id(2)

        @pl.when(k == 0)
        def _():
            acc_ref[...] = jnp.zeros_like(acc_ref)

        prod = _mxdot(a_ref[...], b_ref[...], dims)
        if has_second:
            prod = prod + _mxdot(a2_ref[...], b2_ref[...], dims)
        acc_ref[...] += prod

        @pl.when(k == nk - 1)
        def _():
            r = acc_ref[...]
            if has_bias:
                r = r + bias_ref[...]
            if has_add:
                r = r + add_ref[...]
            o_ref[...] = r.astype(o_ref.dtype)

    if mode == 'tn':
        a_spec = pl.BlockSpec((tk, tm), lambda i, j, k: (k, i))
    else:
        a_spec = pl.BlockSpec((tm, tk), lambda i, j, k: (i, k))
    if mode == 'nt':
        b_spec = pl.BlockSpec((tn, tk), lambda i, j, k: (j, k))
    else:
        b_spec = pl.BlockSpec((tk, tn), lambda i, j, k: (k, j))
    in_specs, args = [a_spec, b_spec], [a, b]
    if has_second:
        assert second[0].shape == a.shape and second[1].shape == b.shape
        in_specs += [a_spec, b_spec]
        args += list(second)
    if has_bias:
        in_specs.append(pl.BlockSpec((1, tn), lambda i, j, k: (0, j)))
        args.append(bias)
    if has_add:
        in_specs.append(pl.BlockSpec((tm, tn), lambda i, j, k: (i, j)))
        args.append(add)
    return pl.pallas_call(
        body, name=name, grid=(M // tm, N // tn, nk), in_specs=in_specs,
        out_specs=pl.BlockSpec((tm, tn), lambda i, j, k: (i, j)),
        out_shape=jax.ShapeDtypeStruct((M, N), out_dtype),
        scratch_shapes=[pltpu.VMEM((tm, tn), f32)],
        compiler_params=_params(("parallel", "parallel", "arbitrary")),
    )(*args)


def _ffn_in(h2, wg, wu, name):
    (M, K), N = h2.shape, wg.shape[1]
    tm, tn = _tile(M, 1024, LANE), _tile(N, 512, LANE)

    def body(a_ref, g_ref, u_ref, go_ref, uo_ref, act_ref):
        a = a_ref[...]
        g, u = _mxdot(a, g_ref[...]), _mxdot(a, u_ref[...])
        go_ref[...] = g.astype(go_ref.dtype)
        uo_ref[...] = u.astype(uo_ref.dtype)
        act_ref[...] = (jax.nn.silu(g) * u).astype(act_ref.dtype)

    wspec = pl.BlockSpec((K, tn), lambda i, j: (0, j))
    ospec = pl.BlockSpec((tm, tn), lambda i, j: (i, j))
    return pl.pallas_call(
        body, name=name, grid=(M // tm, N // tn), in_specs=[pl.BlockSpec((tm, K), lambda i, j: (i, 0)), wspec, wspec],
        out_specs=[ospec] * 3, out_shape=[jax.ShapeDtypeStruct((M, N), _MXU)] * 3,
        compiler_params=_params(("parallel", "parallel")),
    )(h2, wg, wu)


def _ffn_dact(dx, wd, gate, up, name):
    (M, K), N = dx.shape, wd.shape[0]
    tm, tn = _tile(M, 512, LANE), _tile(N, 1408, LANE)

    def body(a_ref, w_ref, g_ref, u_ref, dg_ref, du_ref):
        da = _mxdot(a_ref[...], w_ref[...], _NT)
        _, vjp = jax.vjp(lambda gg, uu: jax.nn.silu(gg) * uu, g_ref[...].astype(f32), u_ref[...].astype(f32))
        dg, du = vjp(da)
        dg_ref[...] = dg.astype(dg_ref.dtype)
        du_ref[...] = du.astype(du_ref.dtype)

    ospec = pl.BlockSpec((tm, tn), lambda i, j: (i, j))
    return pl.pallas_call(
        body, name=name, grid=(M // tm, N // tn),
        in_specs=[pl.BlockSpec((tm, K), lambda i, j: (i, 0)), pl.BlockSpec((tn, K), lambda i, j: (j, 0)), ospec, ospec],
        out_specs=[ospec] * 2, out_shape=[jax.ShapeDtypeStruct((M, N), _MXU)] * 2,
        compiler_params=_params(("parallel", "parallel")),
    )(dx, wd, gate, up)


def _pw(fn, slabs, out_dtypes, width, name, consts=()):
    T = slabs[0][0].shape[0]
    tt, cb = _tile(T, 512, 8), _tile(width, 512, LANE)
    nin, ncst = len(slabs), len(consts)

    def body(*refs):
        res = fn(*[r[...] for r in refs[:nin + ncst]])
        if not isinstance(res, (tuple, list)):
            res = (res,)
        for r, o in zip(res, refs[nin + ncst:], strict=True):
            o[...] = r.astype(o.dtype)

    in_specs, args = [], []
    for arr, col0 in slabs:
        assert col0 % cb == 0
        in_specs.append(pl.BlockSpec((tt, cb), functools.partial(lambda t, c, c0: (t, c0 + c), c0=col0 // cb)))
        args.append(arr)
    for cst in consts:
        in_specs.append(pl.BlockSpec((1, cb), lambda t, c: (0, c)))
        args.append(cst)
    outs = pl.pallas_call(
        body, name=name, grid=(T // tt, width // cb), in_specs=in_specs,
        out_specs=[pl.BlockSpec((tt, cb), lambda t, c: (t, c)) for _ in out_dtypes],
        out_shape=[jax.ShapeDtypeStruct((T, width), d) for d in out_dtypes],
        compiler_params=_params(("parallel", "parallel")),
    )(*args)
    return outs[0] if len(out_dtypes) == 1 else outs


def _rmsnorm_fwd(x, g, name):
    T, D = x.shape
    tt = _tile(T, 256, 8)

    def body(x_ref, g_ref, o_ref):
        xv = x_ref[...]
        r = lax.rsqrt(jnp.mean(xv * xv, axis=-1, keepdims=True) + EPS)
        o_ref[...] = (xv * r * g_ref[...]).astype(o_ref.dtype)

    return pl.pallas_call(
        body, name=name, grid=(T // tt,),
        in_specs=[pl.BlockSpec((tt, D), lambda t: (t, 0)), pl.BlockSpec((1, D), lambda t: (0, 0))],
        out_specs=pl.BlockSpec((tt, D), lambda t: (t, 0)),
        out_shape=jax.ShapeDtypeStruct((T, D), _MXU), compiler_params=_params(("parallel",)),
    )(x, g)


def _rmsnorm_bwd(dh, x, g, dx_in, name):
    T, D = x.shape
    tt = _tile(T, 256, 8)

    def body(dh_ref, x_ref, g_ref, dxi_ref, dx_ref, dxb_ref, dg_ref):
        t = pl.program_id(0)
        xv, d = x_ref[...], dh_ref[...]
        r = lax.rsqrt(jnp.mean(xv * xv, axis=-1, keepdims=True) + EPS)
        n = xv * r
        dn = d * g_ref[...]
        dx = dxi_ref[...] + r * (dn - n * jnp.mean(dn * n, axis=-1, keepdims=True))
        dx_ref[...] = dx
        dxb_ref[...] = dx.astype(dxb_ref.dtype)

        @pl.when(t == 0)
        def _():
            dg_ref[...] = jnp.zeros_like(dg_ref)

        dg_ref[...] += jnp.sum(d * n, axis=0, keepdims=True)

    return pl.pallas_call(
        body, name=name, grid=(T // tt,),
        in_specs=[pl.BlockSpec((tt, D), lambda t: (t, 0)), pl.BlockSpec((tt, D), lambda t: (t, 0)),
                  pl.BlockSpec((1, D), lambda t: (0, 0)), pl.BlockSpec((tt, D), lambda t: (t, 0))],
        out_specs=[pl.BlockSpec((tt, D), lambda t: (t, 0)), pl.BlockSpec((tt, D), lambda t: (t, 0)),
                   pl.BlockSpec((1, D), lambda t: (0, 0))],
        out_shape=[jax.ShapeDtypeStruct((T, D), f32), jax.ShapeDtypeStruct((T, D), _MXU), jax.ShapeDtypeStruct((1, D), f32)],
        compiler_params=_params(("arbitrary",)),
    )(dh, x, g, dx_in)


def _final_loss(x, g, tgt, name):
    T, D = x.shape
    tt = _tile(T, 256, 8)

    def body(x_ref, g_ref, t_ref, l_ref, dx_ref, dxb_ref, dg_ref):
        t = pl.program_id(0)
        xv = x_ref[...]
        r = lax.rsqrt(jnp.mean(xv * xv, axis=-1, keepdims=True) + EPS)
        n = xv * r
        e = n * g_ref[...] - t_ref[...]
        part = 0.5 * jnp.sum(jnp.mean(e * e, axis=-1, keepdims=True), axis=0, keepdims=True)
        dy = e * (1.0 / D)
        dn = dy * g_ref[...]
        dx = r * (dn - n * jnp.mean(dn * n, axis=-1, keepdims=True))
        dx_ref[...] = dx
        dxb_ref[...] = dx.astype(dxb_ref.dtype)

        @pl.when(t == 0)
        def _():
            dg_ref[...] = jnp.zeros_like(dg_ref)
            l_ref[...] = jnp.zeros_like(l_ref)

        dg_ref[...] += jnp.sum(dy * n, axis=0, keepdims=True)
        l_ref[...] += jnp.broadcast_to(part, l_ref.shape)

    return pl.pallas_call(
        body, name=name, grid=(T // tt,),
        in_specs=[pl.BlockSpec((tt, D), lambda t: (t, 0)), pl.BlockSpec((1, D), lambda t: (0, 0)),
                  pl.BlockSpec((tt, D), lambda t: (t, 0))],
        out_specs=[pl.BlockSpec((1, LANE), lambda t: (0, 0)), pl.BlockSpec((tt, D), lambda t: (t, 0)),
                   pl.BlockSpec((tt, D), lambda t: (t, 0)), pl.BlockSpec((1, D), lambda t: (0, 0))],
        out_shape=[jax.ShapeDtypeStruct((1, LANE), f32), jax.ShapeDtypeStruct((T, D), f32),
                   jax.ShapeDtypeStruct((T, D), _MXU), jax.ShapeDtypeStruct((1, D), f32)],
        compiler_params=_params(("arbitrary",)),
    )(x, g, tgt)


def _colsum(x, name):
    T, N = x.shape
    tt, cb = _tile(T, 512, 8), _tile(N, 512, LANE)

    def body(x_ref, o_ref):
        @pl.when(pl.program_id(1) == 0)
        def _():
            o_ref[...] = jnp.zeros_like(o_ref)

        o_ref[...] += jnp.sum(x_ref[...].astype(f32), axis=0, keepdims=True)

    return pl.pallas_call(
        body, name=name, grid=(N // cb, T // tt), in_specs=[pl.BlockSpec((tt, cb), lambda c, t: (t, c))],
        out_specs=pl.BlockSpec((1, cb), lambda c, t: (0, c)), out_shape=jax.ShapeDtypeStruct((1, N), f32),
        compiler_params=_params(("parallel", "arbitrary")),
    )(x)


def _shift_down(cur, prev, j):
    if j == 0:
        return cur
    n = cur.shape[0]
    return jnp.where(_rows(cur.shape) < j, pltpu.roll(prev, j, 0), pltpu.roll(cur, j, 0))


def _shift_up(cur, nxt, j):
    if j == 0:
        return cur
    n = cur.shape[0]
    return jnp.where(_rows(cur.shape) >= n - j, pltpu.roll(nxt, n - j, 0), pltpu.roll(cur, n - j, 0))


def _conv_tiles(T, C, K):
    tt, cb = _tile(T, 256, 8), _tile(C, 256, LANE)
    assert tt >= K, (tt, K)
    return tt, cb


def _conv_fwd(x, col0, w, b, C, name):
    T, K = x.shape[0], w.shape[0]
    tt, cb = _conv_tiles(T, C, K)
    c0 = col0 // cb
    assert col0 % cb == 0
    has_b = b is not None

    def body(*refs):
        cur_ref, prev_ref, w_ref = refs[:3]
        o_ref = refs[-1]
        t = pl.program_id(1)
        cur = cur_ref[...]
        prev = jnp.where(t > 0, prev_ref[...], 0.0)
        wv = w_ref[...]
        acc = jnp.zeros_like(cur)
        for k in range(K):
            acc = acc + wv[k:k + 1, :] * _shift_down(cur, prev, K - 1 - k)
        if has_b:
            acc = acc + refs[3][...]
        o_ref[...] = acc

    in_specs = [pl.BlockSpec((tt, cb), lambda c, t: (t, c0 + c)),
                pl.BlockSpec((tt, cb), lambda c, t: (jnp.maximum(t - 1, 0), c0 + c)),
                pl.BlockSpec((K, cb), lambda c, t: (0, c))]
    args = [x, x, w]
    if has_b:
        in_specs.append(pl.BlockSpec((1, cb), lambda c, t: (0, c)))
        args.append(b)
    return pl.pallas_call(
        body, name=name, grid=(C // cb, T // tt), in_specs=in_specs,
        out_specs=pl.BlockSpec((tt, cb), lambda c, t: (t, c)), out_shape=jax.ShapeDtypeStruct((T, C), f32),
        compiler_params=_params(("parallel", "parallel")),
    )(*args)


def _conv_bwd_dx(dy, w, name, out_dtype=f32):
    T, C = dy.shape
    K = w.shape[0]
    tt, cb = _conv_tiles(T, C, K)
    nt = T // tt

    def body(cur_ref, nxt_ref, w_ref, o_ref):
        t = pl.program_id(1)
        cur = cur_ref[...]
        nxt = jnp.where(t < nt - 1, nxt_ref[...], 0.0)
        wv = w_ref[...]
        acc = jnp.zeros_like(cur)
        for k in range(K):
            acc = acc + wv[k:k + 1, :] * _shift_up(cur, nxt, K - 1 - k)
        o_ref[...] = acc.astype(o_ref.dtype)

    return pl.pallas_call(
        body, name=name, grid=(C // cb, nt),
        in_specs=[pl.BlockSpec((tt, cb), lambda c, t: (t, c)),
                  pl.BlockSpec((tt, cb), lambda c, t: (jnp.minimum(t + 1, nt - 1), c)),
                  pl.BlockSpec((K, cb), lambda c, t: (0, c))],
        out_specs=pl.BlockSpec((tt, cb), lambda c, t: (t, c)), out_shape=jax.ShapeDtypeStruct((T, C), out_dtype),
        compiler_params=_params(("parallel", "parallel")),
    )(dy, dy, w)


def _conv_bwd_dw(dy, x, col0, K, name):
    T, C = dy.shape
    tt, cb = _conv_tiles(T, C, K)
    c0 = col0 // cb
    assert col0 % cb == 0

    def body(dy_ref, cur_ref, prev_ref, dw_ref, db_ref):
        t = pl.program_id(1)

        @pl.when(t == 0)
        def _():
            dw_ref[...] = jnp.zeros_like(dw_ref)
            db_ref[...] = jnp.zeros_like(db_ref)

        d = dy_ref[...]
        cur = cur_ref[...]
        prev = jnp.where(t > 0, prev_ref[...], 0.0)
        for k in range(K):
            row = jnp.sum(d * _shift_down(cur, prev, K - 1 - k), axis=0, keepdims=True)
            dw_ref[pl.ds(k, 1), :] = dw_ref[pl.ds(k, 1), :] + row
        db_ref[...] += jnp.sum(d, axis=0, keepdims=True)

    return pl.pallas_call(
        body, name=name, grid=(C // cb, T // tt),
        in_specs=[pl.BlockSpec((tt, cb), lambda c, t: (t, c)),
                  pl.BlockSpec((tt, cb), lambda c, t: (t, c0 + c)),
                  pl.BlockSpec((tt, cb), lambda c, t: (jnp.maximum(t - 1, 0), c0 + c))],
        out_specs=[pl.BlockSpec((K, cb), lambda c, t: (0, c)), pl.BlockSpec((1, cb), lambda c, t: (0, c))],
        out_shape=[jax.ShapeDtypeStruct((K, C), f32), jax.ShapeDtypeStruct((1, C), f32)],
        compiler_params=_params(("parallel", "arbitrary")),
    )(dy, x, x)


def _expm1(z):
    poly = z * (1.0 + z * (0.5 + z * (1.0 / 6.0 + z * (1.0 / 24.0 + z * (1.0 / 120.0)))))
    return jnp.where(jnp.abs(z) < 0.1, poly, jnp.exp(z) - 1.0)


def _lru_pt(xc, rp, ip, lam, first):
    r = jax.nn.sigmoid(rp)
    i = jax.nn.sigmoid(ip)
    log_a = -LRU_C * r * jax.nn.softplus(-lam)
    a = jnp.exp(log_a)
    mult = jnp.sqrt(jnp.maximum(-_expm1(2.0 * log_a), 0.0))
    mult = jnp.where(first, 1.0, mult)
    return a, mult * i * xc


def _first_mask(shape, t):
    return jnp.logical_and(_rows(shape) == 0, t == 0)


def _lru_gates_fwd(xc, wa, ba, wx, bx, lam, name):
    T, W = xc.shape
    hd = W // N_HEADS
    tt = _tile(T, 512, 8)

    def body(xc_ref, wa_ref, ba_ref, wx_ref, bx_ref, lam_ref, a_ref, u_ref):
        t = pl.program_id(1)
        x = xc_ref[...]
        rp = _mxdot(x, wa_ref[...]) + ba_ref[...]
        ip = _mxdot(x, wx_ref[...]) + bx_ref[...]
        a, u = _lru_pt(x, rp, ip, lam_ref[...], _first_mask(x.shape, t))
        a_ref[...] = a
        u_ref[...] = u

    wspec = pl.BlockSpec((None, hd, hd), lambda h, t: (h, 0, 0))
    bspec = pl.BlockSpec((None, 1, hd), lambda h, t: (h, 0, 0))
    tspec = pl.BlockSpec((tt, hd), lambda h, t: (t, h))
    return pl.pallas_call(
        body, name=name, grid=(N_HEADS, T // tt),
        in_specs=[tspec, wspec, bspec, wspec, bspec, pl.BlockSpec((1, hd), lambda h, t: (0, h))],
        out_specs=[tspec, tspec], out_shape=[jax.ShapeDtypeStruct((T, W), f32)] * 2,
        compiler_params=_params(("parallel", "parallel")),
    )(xc, wa, ba, wx, bx, lam)


def _lru_scan_fwd(a, u, gate, gcol0, name):
    T, W = a.shape
    tt, cb = _tile(T, 256, 8), _tile(W, 512, LANE)
    g0 = gcol0 // cb
    assert gcol0 % cb == 0

    def body(a_ref, u_ref, g_ref, h_ref, y_ref, carry_ref):
        t = pl.program_id(1)

        @pl.when(t == 0)
        def _():
            carry_ref[...] = jnp.zeros_like(carry_ref)

        def step(i, h):
            base = pl.multiple_of(i * 8, 8)
            a8, u8 = a_ref[pl.ds(base, 8), :], u_ref[pl.ds(base, 8), :]
            rows = []
            for j in range(8):
                h = a8[j:j + 1, :] * h + u8[j:j + 1, :]
                rows.append(h)
            h_ref[pl.ds(base, 8), :] = jnp.concatenate(rows, axis=0)
            return h

        h_last = lax.fori_loop(0, tt // 8, step, carry_ref[0:1, :])
        carry_ref[...] = jnp.broadcast_to(h_last, carry_ref.shape)
        y_ref[...] = (h_ref[...] * jax.nn.gelu(g_ref[...])).astype(y_ref.dtype)

    tspec = pl.BlockSpec((tt, cb), lambda c, t: (t, c))
    return pl.pallas_call(
        body, name=name, grid=(W // cb, T // tt),
        in_specs=[tspec, tspec, pl.BlockSpec((tt, cb), lambda c, t: (t, g0 + c))],
        out_specs=[tspec, tspec],
        out_shape=[jax.ShapeDtypeStruct((T, W), f32), jax.ShapeDtypeStruct((T, W), _MXU)],
        scratch_shapes=[pltpu.VMEM((8, cb), f32)],
        compiler_params=_params(("parallel", "arbitrary")),
    )(a, u, gate)


def _lru_scan_bwd(dy, dcol0, gate, gcol0, h, a, name):
    T, W = a.shape
    tt, cb = _tile(T, 256, 8), _tile(W, 512, LANE)
    nt = T // tt
    d0, g0 = dcol0 // cb, gcol0 // cb
    assert dcol0 % cb == 0 and gcol0 % cb == 0

    def body(dy_ref, g_ref, h_ref, a_ref, lam_ref, dg_ref, carry_ref, dh_ref):
        t = pl.program_id(1)

        @pl.when(t == 0)
        def _():
            carry_ref[...] = jnp.zeros_like(carry_ref)

        _, vjp = jax.vjp(lambda hh, gg: hh * jax.nn.gelu(gg), h_ref[...], g_ref[...])
        dh, dg = vjp(dy_ref[...])
        dg_ref[...] = dg.astype(dg_ref.dtype)
        dh_ref[...] = dh

        def step(i, c):
            base = pl.multiple_of((tt // 8 - 1 - i) * 8, 8)
            a8, d8 = a_ref[pl.ds(base, 8), :], dh_ref[pl.ds(base, 8), :]
            rows = [None] * 8
            for j in range(7, -1, -1):
                lam = d8[j:j + 1, :] + c
                c = a8[j:j + 1, :] * lam
                rows[j] = lam
            lam_ref[pl.ds(base, 8), :] = jnp.concatenate(rows, axis=0)
            return c

        c_last = lax.fori_loop(0, tt // 8, step, carry_ref[0:1, :])
        carry_ref[...] = jnp.broadcast_to(c_last, carry_ref.shape)

    rev = lambda c, t: (nt - 1 - t, c)
    tspec = pl.BlockSpec((tt, cb), rev)
    return pl.pallas_call(
        body, name=name, grid=(W // cb, nt),
        in_specs=[pl.BlockSpec((tt, cb), lambda c, t: (nt - 1 - t, d0 + c)),
                  pl.BlockSpec((tt, cb), lambda c, t: (nt - 1 - t, g0 + c)), tspec, tspec],
        out_specs=[tspec, tspec], out_shape=[jax.ShapeDtypeStruct((T, W), f32), jax.ShapeDtypeStruct((T, W), _MXU)],
        scratch_shapes=[pltpu.VMEM((8, cb), f32), pltpu.VMEM((tt, cb), f32)],
        compiler_params=_params(("parallel", "arbitrary")),
    )(dy, gate, h, a)


def _lru_gates_bwd(lam_g, h, xc, wa, ba, wx, bx, lam, name):
    T, W = xc.shape
    hd = W // N_HEADS
    tt = _tile(T, 512, 8)

    def body(lg_ref, h_ref, hp_ref, xc_ref, wa_ref, ba_ref, wx_ref, bx_ref, lam_ref,
             dxc_ref, dwa_ref, dba_ref, dwx_ref, dbx_ref, dlam_ref):
        t = pl.program_id(1)

        @pl.when(t == 0)
        def _():
            for r in (dwa_ref, dba_ref, dwx_ref, dbx_ref, dlam_ref):
                r[...] = jnp.zeros_like(r)

        x = xc_ref[...]
        lg = lg_ref[...]
        h_prev = _shift_down(h_ref[...], jnp.where(t > 0, hp_ref[...], 0.0), 1)
        rp = _mxdot(x, wa_ref[...]) + ba_ref[...]
        ip = _mxdot(x, wx_ref[...]) + bx_ref[...]
        first = _first_mask(x.shape, t)
        _, vjp = jax.vjp(lambda xx, r_, i_, l_: _lru_pt(xx, r_, i_, l_, first), x, rp, ip, lam_ref[...])
        dx, drp, dip, dl = vjp((lg * h_prev, lg))
        dxc_ref[...] = dx + _mxdot(drp, wa_ref[...], _NT) + _mxdot(dip, wx_ref[...], _NT)
        dwa_ref[...] += _mxdot(x, drp, _TN)
        dwx_ref[...] += _mxdot(x, dip, _TN)
        dba_ref[...] += jnp.sum(drp, axis=0, keepdims=True)
        dbx_ref[...] += jnp.sum(dip, axis=0, keepdims=True)
        dlam_ref[...] += dl

    wspec = pl.BlockSpec((None, hd, hd), lambda h_, t: (h_, 0, 0))
    bspec = pl.BlockSpec((None, 1, hd), lambda h_, t: (h_, 0, 0))
    tspec = pl.BlockSpec((tt, hd), lambda h_, t: (t, h_))
    pspec = pl.BlockSpec((tt, hd), lambda h_, t: (jnp.maximum(t - 1, 0), h_))
    lspec = pl.BlockSpec((1, hd), lambda h_, t: (0, h_))
    return pl.pallas_call(
        body, name=name, grid=(N_HEADS, T // tt),
        in_specs=[tspec, tspec, pspec, tspec, wspec, bspec, wspec, bspec, lspec],
        out_specs=[tspec, wspec, bspec, wspec, bspec, lspec],
        out_shape=[jax.ShapeDtypeStruct((T, W), f32), jax.ShapeDtypeStruct((N_HEADS, hd, hd), f32),
                   jax.ShapeDtypeStruct((N_HEADS, 1, hd), f32), jax.ShapeDtypeStruct((N_HEADS, hd, hd), f32),
                   jax.ShapeDtypeStruct((N_HEADS, 1, hd), f32), jax.ShapeDtypeStruct((1, W), f32)],
        compiler_params=_params(("parallel", "arbitrary")),
    )(lam_g, h, h, xc, wa, ba, wx, bx, lam)


def _hgrn_pt(z, qp, lb):
    sig = jax.nn.sigmoid(z)
    fg = lb + (1.0 - lb) * sig
    logf = jnp.log(jnp.maximum(fg, F_FLOOR))
    k = (1.0 - lb) * (1.0 - sig)
    return logf, k, jax.nn.silu(qp)


def _hgrn_out(o, gp, ng):
    on = o * lax.rsqrt(jnp.mean(o * o, axis=-1, keepdims=True) + EPS)
    return on * ng * jax.nn.silu(gp)


def _cumsum_rows(x):
    n, row, sh = x.shape[0], _rows(x.shape), 1
    while sh < n:
        x = x + jnp.where(row >= sh, pltpu.roll(x, sh, 0), 0.0)
        sh *= 2
    return x


def _rev_cumsum_rows(x):
    n, row, sh = x.shape[0], _rows(x.shape), 1
    while sh < n:
        x = x + jnp.where(row < n - sh, pltpu.roll(x, n - sh, 0), 0.0)
        sh *= 2
    return x


def _hgrn_fwd(proj, qcol, fcol, vcol, gcol, lb, ng, name):
    T = proj.shape[0]
    W = lb.shape[1]
    hd = W // N_HEADS
    bw = HGRN_HPB * hd
    tt = _tile(T, 256, SUB)
    ns = tt // SUB
    q0, f0, v0, g0 = qcol // bw, fcol // bw, vcol // bw, gcol // bw

    def body(q_ref, f_ref, v_ref, g_ref, lb_ref, ng_ref, y_ref, o_ref, st_ref, s_ref):
        t = pl.program_id(1)

        @pl.when(t == 0)
        def _():
            s_ref[...] = jnp.zeros_like(s_ref)

        row = _rows((SUB, hd))

        def sub(j, carry):
            rs = pl.ds(pl.multiple_of(j * SUB, SUB), SUB)
            for hh in range(HGRN_HPB):
                cs = slice(hh * hd, (hh + 1) * hd)
                logf, k, qf = _hgrn_pt(f_ref[rs, cs], q_ref[rs, cs], lb_ref[:, cs])
                v = v_ref[rs, cs]
                b = _cumsum_rows(logf)
                b_last = b[SUB - 1:SUB, :]
                S = s_ref[hh]
                st_ref[hh, j] = S
                intra = jnp.zeros((SUB, hd), f32)
                for r in range(SUB):
                    e = jnp.exp(jnp.minimum(b[r:r + 1, :] - b, 0.0))
                    m = jnp.where(row <= r, qf[r:r + 1, :] * k * e, 0.0)
                    p = jnp.sum(m, axis=1, keepdims=True)
                    intra = jnp.where(row == r, jnp.sum(p * v, axis=0, keepdims=True), intra)
                o_ref[rs, cs] = _mxdot(qf * jnp.exp(b), S, _NT) + intra
                s_ref[hh] = S * jnp.exp(b_last) + _mxdot(v, k * jnp.exp(b_last - b), _TN)
            return carry

        lax.fori_loop(0, ns, sub, 0)
        for hh in range(HGRN_HPB):
            cs = slice(hh * hd, (hh + 1) * hd)
            y_ref[:, cs] = _hgrn_out(o_ref[:, cs], g_ref[:, cs], ng_ref[:, cs]).astype(y_ref.dtype)

    def slab(c0):
        return pl.BlockSpec((tt, bw), functools.partial(lambda h, t, c0: (t, c0 + h), c0=c0))

    hspec = pl.BlockSpec((tt, bw), lambda h, t: (t, h))
    cspec = pl.BlockSpec((1, bw), lambda h, t: (0, h))
    return pl.pallas_call(
        body, name=name, grid=(N_HEADS // HGRN_HPB, T // tt),
        in_specs=[slab(q0), slab(f0), slab(v0), slab(g0), cspec, cspec],
        out_specs=[hspec, hspec, pl.BlockSpec((HGRN_HPB, ns, hd, hd), lambda h, t: (h, t, 0, 0))],
        out_shape=[jax.ShapeDtypeStruct((T, W), _MXU), jax.ShapeDtypeStruct((T, W), f32),
                   jax.ShapeDtypeStruct((N_HEADS, T // SUB, hd, hd), f32)],
        scratch_shapes=[pltpu.VMEM((HGRN_HPB, hd, hd), f32)],
        compiler_params=_params(("parallel", "arbitrary")),
    )(proj, proj, proj, proj, lb, ng)


def _hgrn_bwd(dy, dcol, proj, qcol, fcol, vcol, gcol, lb, ng, o, states, name):
    T = proj.shape[0]
    W = lb.shape[1]
    hd = W // N_HEADS
    bw = HGRN_HPB * hd
    tt = _tile(T, 256, SUB)
    ns, nt = tt // SUB, T // tt
    q0, f0, v0, g0, d0 = qcol // bw, fcol // bw, vcol // bw, gcol // bw, dcol // bw

    def body(dy_ref, q_ref, f_ref, v_ref, g_ref, lb_ref, ng_ref, o_ref, st_ref,
             dq_ref, df_ref, dv_ref, dg_ref, dlb_ref, dng_ref, ds_ref, do_ref):
        t = pl.program_id(1)

        @pl.when(t == 0)
        def _():
            ds_ref[...] = jnp.zeros_like(ds_ref)
            dlb_ref[...] = jnp.zeros_like(dlb_ref)
            dng_ref[...] = jnp.zeros_like(dng_ref)

        for hh in range(HGRN_HPB):
            cs = slice(hh * hd, (hh + 1) * hd)
            _, vjp_out = jax.vjp(_hgrn_out, o_ref[:, cs], g_ref[:, cs], ng_ref[:, cs])
            do, dgp, dng = vjp_out(dy_ref[:, cs])
            do_ref[:, cs] = do
            dg_ref[:, cs] = dgp.astype(dg_ref.dtype)
            dng_ref[:, cs] += dng
        row = _rows((SUB, hd))

        def sub(jj, carry):
            j = ns - 1 - jj
            rs = pl.ds(pl.multiple_of(j * SUB, SUB), SUB)
            for hh in range(HGRN_HPB):
                cs = slice(hh * hd, (hh + 1) * hd)
                (logf, k, qf), vjp_pt = jax.vjp(_hgrn_pt, f_ref[rs, cs], q_ref[rs, cs], lb_ref[:, cs])
                v = v_ref[rs, cs]
                dO = do_ref[rs, cs]
                b = _cumsum_rows(logf)
                b_last = b[SUB - 1:SUB, :]
                S = st_ref[hh, j]
                dS = ds_ref[hh]
                eb = jnp.exp(b)
                kd = jnp.exp(b_last - b)
                d = jnp.exp(b_last)
                qe, ke = qf * eb, k * kd
                dqe = _mxdot(dO, S, _NN)
                dke = _mxdot(v, dS, _NN)
                dv = _mxdot(ke, dS, _NT)
                dd = jnp.sum(dS * S, axis=0, keepdims=True)
                ds_ref[hh] = dS * d + _mxdot(dO, qe, _TN)
                dq_i = jnp.zeros((SUB, hd), f32)
                dk_i = jnp.zeros((SUB, hd), f32)
                for r in range(SUB):
                    em = jnp.where(row <= r, jnp.exp(jnp.minimum(b[r:r + 1, :] - b, 0.0)), 0.0)
                    ke_r = k * em
                    qr, dor = qf[r:r + 1, :], dO[r:r + 1, :]
                    p = jnp.sum(qr * ke_r, axis=1, keepdims=True)
                    dv = dv + p * dor
                    dp = jnp.sum(dor * v, axis=1, keepdims=True)
                    dq_i = jnp.where(row == r, jnp.sum(dp * ke_r, axis=0, keepdims=True), dq_i)
                    dk_i = dk_i + dp * (qr * em)
                dqf = dqe * eb + dq_i
                dk = dke * kd + dk_i
                dke_ke = dke * ke
                db = dqe * qe - dke_ke + qf * dq_i - k * dk_i
                db_last = jnp.sum(dke_ke, axis=0, keepdims=True) + dd * d
                db = db + jnp.where(row == SUB - 1, db_last, 0.0)
                dz, dqp, dlb = vjp_pt((_rev_cumsum_rows(db), dk, dqf))
                dq_ref[rs, cs] = dqp.astype(dq_ref.dtype)
                df_ref[rs, cs] = dz.astype(df_ref.dtype)
                dv_ref[rs, cs] = dv.astype(dv_ref.dtype)
                dlb_ref[:, cs] += dlb
            return carry

        lax.fori_loop(0, ns, sub, 0)

    def slab(c0):
        return pl.BlockSpec((tt, bw), functools.partial(lambda h, t, c0: (nt - 1 - t, c0 + h), c0=c0))

    hspec = pl.BlockSpec((tt, bw), lambda h, t: (nt - 1 - t, h))
    cspec = pl.BlockSpec((1, bw), lambda h, t: (0, h))
    return pl.pallas_call(
        body, name=name, grid=(N_HEADS // HGRN_HPB, nt),
        in_specs=[slab(d0), slab(q0), slab(f0), slab(v0), slab(g0), cspec, cspec, hspec,
                  pl.BlockSpec((HGRN_HPB, ns, hd, hd), lambda h, t: (h, nt - 1 - t, 0, 0))],
        out_specs=[hspec, hspec, hspec, hspec, cspec, cspec],
        out_shape=[jax.ShapeDtypeStruct((T, W), _MXU)] * 4 + [jax.ShapeDtypeStruct((1, W), f32)] * 2,
        scratch_shapes=[pltpu.VMEM((HGRN_HPB, hd, hd), f32), pltpu.VMEM((tt, bw), f32)],
        compiler_params=_params(("parallel", "arbitrary")),
    )(dy, proj, proj, proj, proj, lb, ng, o, states)


def _lower_bounds(logits, name):
    def fn(lg):
        sm = jax.nn.softmax(lg, axis=0)
        run, rows_ = None, []
        for j in range(lg.shape[0]):
            run = sm[j:j + 1, :] if run is None else run + sm[j:j + 1, :]
            rows_.append(run - sm[0:1, :])
        return jnp.concatenate(rows_, axis=0)
    return fn


def _lb_fwd(logits, name):
    fn = _lower_bounds(logits, name)

    def body(l_ref, o_ref):
        o_ref[...] = fn(l_ref[...])

    return pl.pallas_call(body, name=name, out_shape=jax.ShapeDtypeStruct(logits.shape, f32))(logits)


def _lb_bwd(logits, dlb, name):
    fn = _lower_bounds(logits, name)

    def body(l_ref, d_ref, o_ref):
        _, vjp = jax.vjp(fn, l_ref[...])
        o_ref[...] = vjp(d_ref[...])[0]

    return pl.pallas_call(body, name=name, out_shape=jax.ShapeDtypeStruct(logits.shape, f32))(logits, dlb)


def _ln_silu(d, g, b):
    mu = jnp.mean(d, axis=-1, keepdims=True)
    xc = d - mu
    y = xc * lax.rsqrt(jnp.mean(xc * xc, axis=-1, keepdims=True) + EPS)
    return jax.nn.silu(y * g + b)


def _ln_fwd(d, g, b, name):
    T, W = d.shape
    tt = _tile(T, 256, 8)

    def body(d_ref, g_ref, b_ref, o_ref):
        o_ref[...] = _ln_silu(d_ref[...], g_ref[...], b_ref[...]).astype(o_ref.dtype)

    return pl.pallas_call(
        body, name=name, grid=(T // tt,),
        in_specs=[pl.BlockSpec((tt, W), lambda t: (t, 0))] + [pl.BlockSpec((1, W), lambda t: (0, 0))] * 2,
        out_specs=pl.BlockSpec((tt, W), lambda t: (t, 0)), out_shape=jax.ShapeDtypeStruct((T, W), _MXU),
        compiler_params=_params(("parallel",)),
    )(d, g, b)


def _ln_bwd(dy, dcol0, d, g, b, name):
    T, W = d.shape
    tt = _tile(T, 256, 8)
    c0 = dcol0 // W
    assert dcol0 % W == 0

    def body(dy_ref, d_ref, g_ref, b_ref, dd_ref, dg_ref, db_ref):
        @pl.when(pl.program_id(0) == 0)
        def _():
            dg_ref[...] = jnp.zeros_like(dg_ref)
            db_ref[...] = jnp.zeros_like(db_ref)

        _, vjp = jax.vjp(_ln_silu, d_ref[...], g_ref[...], b_ref[...])
        dd, dg, db = vjp(dy_ref[...])
        dd_ref[...] = dd
        dg_ref[...] += dg
        db_ref[...] += db

    cspec = pl.BlockSpec((1, W), lambda t: (0, 0))
    return pl.pallas_call(
        body, name=name, grid=(T // tt,),
        in_specs=[pl.BlockSpec((tt, W), lambda t: (t, c0)), pl.BlockSpec((tt, W), lambda t: (t, 0)), cspec, cspec],
        out_specs=[pl.BlockSpec((tt, W), lambda t: (t, 0)), cspec, cspec],
        out_shape=[jax.ShapeDtypeStruct((T, W), f32), jax.ShapeDtypeStruct((1, W), f32), jax.ShapeDtypeStruct((1, W), f32)],
        compiler_params=_params(("arbitrary",)),
    )(dy, d, g, b)


def _adamw(w, m, v, parts, name):
    R, C = w.shape
    tr, tc = _tile(R, 512, 8), _tile(C, 512, LANE)
    npart = len(parts)

    def body(*refs):
        w_ref, m_ref, v_ref = refs[:3]
        g_ref, d_ref, mo_ref, vo_ref = refs[3 + npart:]
        g = refs[3][...].astype(f32)
        for p_ref in refs[4:3 + npart]:
            g = g + p_ref[...].astype(f32)
        mm = ADAM_B1 * m_ref[...] + (1.0 - ADAM_B1) * g
        vv = ADAM_B2 * v_ref[...] + (1.0 - ADAM_B2) * jnp.square(g)
        m_hat = mm / (1.0 - ADAM_B1 ** ADAM_STEP)
        v_hat = vv / (1.0 - ADAM_B2 ** ADAM_STEP)
        g_ref[...] = g
        d_ref[...] = -ADAM_LR * (m_hat / (jnp.sqrt(v_hat) + ADAM_EPS) + ADAM_WD * w_ref[...])
        mo_ref[...] = mm
        vo_ref[...] = vv

    spec = pl.BlockSpec((tr, tc), lambda i, j: (i, j))
    return pl.pallas_call(
        body, name=name, grid=(R // tr, C // tc), in_specs=[spec] * (3 + npart), out_specs=[spec] * 4,
        out_shape=[jax.ShapeDtypeStruct((R, C), f32)] * 4, compiler_params=_params(("parallel", "parallel")),
    )(w, m, v, *parts)


def _adamw_layer(w3, m3, v3, layer, parts, prev, name):
    L, R, C = w3.shape
    tr, tc = _tile(R, 128, 8), _tile(C, 2048, LANE)
    npart = len(parts)

    def body(*refs):
        w_ref, m_ref, v_ref = refs[:3]
        g_ref, d_ref, mo_ref, vo_ref = refs[3 + npart + 4:]
        g = refs[3][...].astype(f32)
        for p_ref in refs[4:3 + npart]:
            g = g + p_ref[...].astype(f32)
        mm = ADAM_B1 * m_ref[...] + (1.0 - ADAM_B1) * g
        vv = ADAM_B2 * v_ref[...] + (1.0 - ADAM_B2) * jnp.square(g)
        m_hat = mm / (1.0 - ADAM_B1 ** ADAM_STEP)
        v_hat = vv / (1.0 - ADAM_B2 ** ADAM_STEP)
        g_ref[...] = g
        d_ref[...] = -ADAM_LR * (m_hat / (jnp.sqrt(v_hat) + ADAM_EPS) + ADAM_WD * w_ref[...])
        mo_ref[...] = mm
        vo_ref[...] = vv

    spec3 = pl.BlockSpec((None, tr, tc), lambda i, j: (layer, i, j))
    spec2 = pl.BlockSpec((tr, tc), lambda i, j: (i, j))
    return pl.pallas_call(
        body, name=name, grid=(R // tr, C // tc), in_specs=[spec3] * 3 + [spec2] * npart + [_ANY] * 4,
        out_specs=[spec3] * 4, out_shape=[jax.ShapeDtypeStruct((L, R, C), f32)] * 4,
        input_output_aliases={3 + npart + k: k for k in range(4)},
        compiler_params=_params(("parallel", "parallel")),
    )(w3, m3, v3, *parts, *prev)


def _sum_own_recv(g, chip, col, recv, name):
    S, R, C = recv.shape
    tr, tc = _tile(R, 256, 8), _tile(C, 2048, LANE)
    nbr, nbc = R // tr, C // tc

    def body(chip_ref, o_ref, r_ref, out_ref):
        acc = o_ref[...].astype(f32)
        for s in range(S):
            acc = acc + r_ref[s].astype(f32)
        out_ref[...] = acc

    if col:
        own_map = lambda i, j, c: (i, c[0] * nbc + j)
    else:
        own_map = lambda i, j, c: (c[0] * nbr + i, j)
    grid_spec = pltpu.PrefetchScalarGridSpec(
        num_scalar_prefetch=1, grid=(nbr, nbc),
        in_specs=[pl.BlockSpec((tr, tc), own_map), pl.BlockSpec((S, tr, tc), lambda i, j, c: (0, i, j))],
        out_specs=pl.BlockSpec((tr, tc), lambda i, j, c: (i, j)))
    return pl.pallas_call(
        body, name=name, grid_spec=grid_spec, out_shape=jax.ShapeDtypeStruct((R, C), f32),
        compiler_params=_params(("parallel", "parallel")),
    )(chip, g, recv)


def _sum_slots(recv, name):
    S, R, C = recv.shape
    tr, tc = _tile(R, 512, 8), _tile(C, 512, LANE)

    def body(r_ref, o_ref):
        acc = r_ref[0].astype(f32)
        for s in range(1, S):
            acc = acc + r_ref[s].astype(f32)
        o_ref[...] = acc

    return pl.pallas_call(
        body, name=name, grid=(R // tr, C // tc), in_specs=[pl.BlockSpec((S, tr, tc), lambda i, j: (0, i, j))],
        out_specs=pl.BlockSpec((tr, tc), lambda i, j: (i, j)), out_shape=jax.ShapeDtypeStruct((R, C), f32),
        compiler_params=_params(("parallel", "parallel")),
    )(recv)


_CHIP_FLIPS = ((1, 0), (0, 1), (1, 1))
_ANY = pl.BlockSpec(memory_space=pl.ANY)


def _me():
    return lax.axis_index("x"), lax.axis_index("y"), lax.axis_index("c")


def _allgather_chips(local, col, name):
    L, R, C = local.shape
    out_shape = (L, R, 4 * C) if col else (L, 4 * R, C)

    def body(loc_ref, out_ref, send_sems, recv_sems, local_sem):
        x, y, c = _me()

        def block(px, py):
            s = 2 * px + py
            if col:
                return out_ref.at[:, :, pl.ds(s * C, C)]
            return out_ref.at[:, pl.ds(s * R, R), :]

        mine = pltpu.make_async_copy(loc_ref, block(x, y), local_sem)
        mine.start()
        sends = []
        for j, (fx, fy) in enumerate(_CHIP_FLIPS):
            cp = pltpu.make_async_remote_copy(src_ref=loc_ref, dst_ref=block(x, y), send_sem=send_sems.at[j],
                                              recv_sem=recv_sems.at[j], device_id=(x ^ fx, y ^ fy, c), device_id_type=MESH)
            cp.start()
            sends.append(cp)
        for j, (fx, fy) in enumerate(_CHIP_FLIPS):
            pltpu.make_async_remote_copy(src_ref=loc_ref, dst_ref=block(x ^ fx, y ^ fy), send_sem=send_sems.at[j],
                                         recv_sem=recv_sems.at[j], device_id=(x ^ fx, y ^ fy, c),
                                         device_id_type=MESH).wait_recv()
        for cp in sends:
            cp.wait_send()
        mine.wait()

    return pl.pallas_call(
        body, name=name, in_specs=[_ANY], out_specs=_ANY, out_shape=jax.ShapeDtypeStruct(out_shape, local.dtype),
        scratch_shapes=[pltpu.SemaphoreType.DMA((3,)), pltpu.SemaphoreType.DMA((3,)), pltpu.SemaphoreType.DMA],
    )(local)


_HBM = pl.BlockSpec(memory_space=pltpu.HBM)
_SEM = pl.BlockSpec(memory_space=pltpu.SEMAPHORE)
_EFFECT = pltpu.SideEffectType.DATAFLOW_SIDE_EFFECTING


def _chip_block(ref, s, col, n):
    return ref.at[:, pl.ds(s * n, n)] if col else ref.at[pl.ds(s * n, n), :]


_NSEM = {'gather': 7, 'relay': 6, 'scatter': 6, 'swap': 2, 'all': 15}


def _half_block(ref, s, col, n, half):
    rows = (ref.shape[0] if col else n) // 2
    if col:
        return ref.at[pl.ds(half * rows, rows), pl.ds(s * n, n)]
    return ref.at[pl.ds(s * n + half * rows, rows), :]


def _xchg_copies(kind, src_ref, land_ref, col, sems):
    x, y, c = _me()
    if kind == 'swap':
        return [pltpu.make_async_remote_copy(src_ref=src_ref, dst_ref=land_ref, send_sem=sems[0], recv_sem=sems[1],
                                             device_id=(x, y, 1 - c), device_id_type=MESH)]
    if kind == 'all':
        me_slot = land_ref.at[4 * x + 2 * y + c]
        cps = [pltpu.make_async_copy(src_ref, me_slot, sems[14])]
        for m in range(1, 8):
            cps.append(pltpu.make_async_remote_copy(
                src_ref=src_ref, dst_ref=me_slot, send_sem=sems[2 * m - 2], recv_sem=sems[2 * m - 1],
                device_id=(x ^ (m >> 2), y ^ ((m >> 1) & 1), c ^ (m & 1)), device_id_type=MESH))
        return cps
    cps = []
    if kind == 'gather':
        n = src_ref.shape[1] if col else src_ref.shape[0]
        cps.append(pltpu.make_async_copy(src_ref, _chip_block(land_ref, 2 * x + y, col, n), sems[6]))
    for j, (fx, fy) in enumerate(_CHIP_FLIPS):
        px, py = x ^ fx, y ^ fy
        peer = (px, py, c)
        if kind == 'gather':
            rows = src_ref.shape[0] // 2
            src, dst = src_ref.at[pl.ds(c * rows, rows), :], _half_block(land_ref, 2 * x + y, col, n, c)
        elif kind == 'relay':
            n = land_ref.shape[1] // 4 if col else land_ref.shape[0] // 4
            src = dst = _half_block(land_ref, 2 * px + py, col, n, c)
            peer = (x, y, 1 - c)
        else:
            n = land_ref.shape[2] if col else land_ref.shape[1]
            src, dst = _chip_block(src_ref, 2 * px + py, col, n), land_ref.at[j]
        cps.append(pltpu.make_async_remote_copy(src_ref=src, dst_ref=dst, send_sem=sems[2 * j], recv_sem=sems[2 * j + 1],
                                                device_id=peer, device_id_type=MESH))
    return cps


def _xchg_start(kind, srcs, lands, cols, after, name):
    n = len(lands)
    arrs = (list(srcs) if srcs is not None else []) + list(lands)
    na = len(arrs)
    per = _NSEM[kind]
    nsem = per * n

    def body(*refs):
        land_refs = refs[na - n:na]
        src_refs = refs[:n] if srcs is not None else land_refs
        sems = refs[na + 1:na + 1 + nsem]
        token = refs[-1]
        for i in range(n):
            for cp in _xchg_copies(kind, src_refs[i], land_refs[i], cols[i], sems[per * i:per * i + per]):
                cp.start()
        token[...] = jnp.zeros_like(token)

    hbm = lambda a: pltpu.HBM(a.shape, a.dtype)
    outs = pl.pallas_call(
        body, name=name,
        out_shape=tuple([pltpu.SemaphoreType.DMA(())] * nsem + [hbm(a) for a in arrs] + [jax.ShapeDtypeStruct((8, LANE), f32)]),
        in_specs=tuple([_HBM] * na + [_ANY]),
        out_specs=tuple([_SEM] * nsem + [_HBM] * na + [pl.BlockSpec(memory_space=pltpu.VMEM)]),
        input_output_aliases={i: nsem + i for i in range(na)},
        compiler_params=pltpu.CompilerParams(has_side_effects=_EFFECT),
    )(*[pltpu.with_memory_space_constraint(a, pltpu.HBM) for a in arrs], after)
    thru = outs[nsem:nsem + na]
    return outs[:nsem], (thru[:n] if srcs is not None else None), thru[na - n:], outs[-1]


def _xchg_wait(kind, started, cols, after, name):
    sems, srcs, lands, _ = started
    n = len(lands)
    arrs = (list(srcs) if srcs is not None else []) + list(lands)
    na = len(arrs)
    per = _NSEM[kind]
    nsem = per * n

    def body(*refs):
        land_refs = refs[na - n:na]
        src_refs = refs[:n] if srcs is not None else land_refs
        sem_refs = refs[na:na + nsem]
        for i in range(n):
            for cp in _xchg_copies(kind, src_refs[i], land_refs[i], cols[i], sem_refs[per * i:per * i + per]):
                if cp.is_remote:
                    cp.wait_send()
                    cp.wait_recv()
                else:
                    cp.wait()

    hbm = lambda a: pltpu.HBM(a.shape, a.dtype)
    outs = pl.pallas_call(
        body, name=name, out_shape=tuple(hbm(a) for a in arrs),
        in_specs=tuple([_HBM] * na + [_SEM] * nsem + [_ANY]), out_specs=tuple([_HBM] * na),
        input_output_aliases={i: i for i in range(na)},
        compiler_params=pltpu.CompilerParams(has_side_effects=_EFFECT),
    )(*arrs, *sems, after)
    return (outs[:n] if srcs is not None else None), outs[na - n:]


_PACK_ROWS = 512


def _pack(arrs):
    flat = jnp.concatenate([a.reshape(-1).astype(f32) for a in arrs])
    n = flat.shape[0]
    rows = -(-n // (_PACK_ROWS * LANE)) * _PACK_ROWS
    return jnp.pad(flat, (0, rows * LANE - n)).reshape(rows, LANE)


def _unpack(buf, shapes):
    flat, outs, off = buf.reshape(-1), [], 0
    for s in shapes:
        n = 1
        for d_ in s:
            n *= d_
        outs.append(flat[off:off + n].reshape(s))
        off += n
    return outs


def _step(p):
    x0 = p['x'][0]
    tgt = p['loss_target'][0]
    T, D = x0.shape
    W = D // 2
    depth = p['ln_mix_g'].shape[0]
    chip = 2 * lax.axis_index("x") + lax.axis_index("y")
    chip1 = jnp.reshape(chip, (1,)).astype(jnp.int32)

    def group_names(layer, grp):
        if grp == 'ffn':
            return [('ffn_w_gate', layer), ('ffn_w_up', layer), ('ffn_w_down', layer)]
        pre = 'ev' if layer % 2 == 0 else 'od'
        return [(pre + '_w_in', layer // 2), (pre + '_w_out', layer // 2)]

    def gather_start(layer, grp, after):
        srcs, lands, cols = [], [], []
        for n, l in group_names(layer, grp):
            loc = p[n][l].astype(_MXU)
            R, C = loc.shape
            col = _BIG_COL[n]
            land = lax.empty((R, 4 * C) if col else (4 * R, C), _MXU)
            srcs.append(loc)
            lands.append(land)
            cols.append(col)
        return _xchg_start('gather', srcs, lands, cols, after, name=f"ag_start_{grp}{layer}"), cols

    def gather_relay(started, layer, grp, after):
        st, cols = started
        _, lands = _xchg_wait('gather', st, cols, after, name=f"ag_wait_{grp}{layer}")
        return _xchg_start('relay', None, lands, cols, chip1, name=f"ag_relay_{grp}{layer}"), cols

    def gather_finish(relayed, layer, grp, after):
        st, cols = relayed
        _, lands = _xchg_wait('relay', st, cols, after, name=f"ag_done_{grp}{layer}")
        return dict(zip([n for n, _ in group_names(layer, grp)], lands))

    groups = [(layer, grp) for layer in range(depth) for grp in ('mix', 'ffn')]
    pending, relayed = {}, {}
    loose = []

    def take_weights(gi, after):
        wts = gather_finish(relayed.pop(groups[gi]), *groups[gi], after)
        if gi + 2 < len(groups):
            pending[groups[gi + 2]] = gather_start(*groups[gi + 2], next(iter(wts.values())))
            loose.append(pending[groups[gi + 2]][0][3])
        return wts

    def advance(gi, after):
        if gi + 1 < len(groups):
            relayed[groups[gi + 1]] = gather_relay(pending.pop(groups[gi + 1]), *groups[gi + 1], after)
            loose.append(relayed[groups[gi + 1]][0][3])

    def tied(a):
        a = a.reshape(1, -1)
        while loose:
            a = a + loose.pop()[0:1, 0:1]
        return a

    small_rows = []
    cl = W // 4
    for n in _SMALL_SHARDED:
        small_rows.append(p[n].reshape(-1, cl))
    srows = [a.shape[0] for a in small_rows]
    spack = jnp.concatenate(small_rows, axis=0)
    spad = -(-spack.shape[0] // 8) * 8
    spack = jnp.pad(spack, ((0, spad - spack.shape[0]), (0, 0)))
    sfull = _allgather_chips(spack[None], False, name="ag_small")[0].reshape(4, spad, cl)
    pending[groups[0]] = gather_start(*groups[0], sfull)
    pending[groups[1]] = gather_start(*groups[1], pending[groups[0]][0][3])
    relayed[groups[0]] = gather_relay(pending.pop(groups[0]), *groups[0], pending[groups[1]][0][3])
    loose.append(relayed[groups[0]][0][3])
    small = {}
    off = 0
    for n, r in zip(_SMALL_SHARDED, srows):
        blk = sfull[:, off:off + r, :]
        lead = p[n].shape[:-1]
        q = p[n].shape[-1] // cl
        blk = blk.reshape((4,) + lead + (q, cl))
        blk = jnp.moveaxis(blk, 0, len(lead))
        small[n] = blk.reshape(lead + (4 * q * cl,))
        off += r

    lbs = _lb_fwd(p['hgrn_lb_logits'], name="lb_fwd")

    def row(a, tok=None):
        a = a.reshape(1, -1)
        return a if tok is None else a + tok[0:1, 0:1]

    saved = []
    full = {}
    x = x0
    for layer in range(depth):
        j = layer // 2
        s = {'x_in': x}
        wts = take_weights(2 * layer, x)
        full[layer] = wts
        h = _rmsnorm_fwd(x, tied(p['ln_mix_g'][layer]), name="rms_fwd")
        s['h'] = h
        if layer % 2 == 0:
            proj = _matmul(h, wts['ev_w_in'], 'nn', f32, "mm_ev_in", bias=row(p['ev_b_in'][j]))
            xc = _conv_fwd(proj, 0, small['lru_conv_w'][j], row(p['lru_conv_b'][j]), W, name="lru_conv_fwd")
            ba, bx = p['lru_ba'][j][:, None, :], p['lru_bx'][j][:, None, :]
            a, u = _lru_gates_fwd(xc, p['lru_wa'][j], ba, p['lru_wx'][j], bx, row(p['lru_lambda'][j]), name="lru_gates_fwd")
            hl, y_a = _lru_scan_fwd(a, u, proj, W, name="lru_scan_fwd")
            y_b, o, states = _hgrn_fwd(proj, 2 * W, 3 * W, 4 * W, 5 * W, row(lbs[j]), row(p['hgrn_norm_g'][j]),
                                       name="hgrn_fwd")
            s.update(proj=proj, xc=xc, a=a, hl=hl, o=o, states=states)
            ycat = jnp.concatenate([y_a, y_b], axis=1)
            w_out = wts['ev_w_out']
        else:
            proj = _matmul(h, wts['od_w_in'], 'nn', f32, "mm_od_in", bias=row(small['od_b_in'][j]))
            pp = _pw(lambda a_, b_: a_ * b_, [(proj, W), (proj, 2 * W)], [f32], W, name="sc_mul")
            cp = _conv_fwd(pp, 0, small['sc_conv_w'][j], None, W, name="sc_conv_fwd")
            y_c = _pw(lambda a_, b_: a_ * b_, [(proj, 0), (cp, 0)], [_MXU], W, name="sc_out")
            glu = _pw(lambda a_, b_: a_ * jax.nn.sigmoid(b_), [(proj, 3 * W), (proj, 4 * W)], [f32], W, name="cf_glu")
            dcv = _conv_fwd(glu, 0, small['cf_conv_w'][j], row(small['cf_conv_b'][j]), W, name="cf_conv_fwd")
            y_d = _ln_fwd(dcv, row(small['cf_ln_g'][j]), row(small['cf_ln_b'][j]), name="cf_ln_fwd")
            s.update(proj=proj, pp=pp, cp=cp, glu=glu, dcv=dcv)
            ycat = jnp.concatenate([y_c, y_d], axis=1)
            w_out = wts['od_w_out']
        s['ycat'] = ycat
        advance(2 * layer, ycat)
        x = _matmul(ycat, w_out, 'nn', f32, "mm_mix_out", add=x)
        s['x_mid'] = x
        wts = take_weights(2 * layer + 1, x)
        full[layer].update(wts)
        h2 = _rmsnorm_fwd(x, tied(p['ln_ffn_g'][layer]), name="rms_fwd")
        gate, up, act = _ffn_in(h2, wts['ffn_w_gate'], wts['ffn_w_up'], name="ffn_in")
        advance(2 * layer + 1, act)
        x = _matmul(act, wts['ffn_w_down'], 'nn', f32, "mm_ffn_out", add=x, tn=1024, tk=1408)
        s.update(h2=h2, gate=gate, up=up, act=act)
        saved.append(s)

    loss_b, dx, dxb, dg_final = _final_loss(x, tied(p['ln_final_g']), tgt, name="final_loss")

    scat = {}
    tok = None

    def scatter_start(layer, grp, grads):
        srcs, lands, cols = [], [], []
        for (n, l), g in zip(group_names(layer, grp), grads, strict=True):
            R, C = p[n].shape[1:]
            srcs.append(g)
            lands.append(lax.empty((3, R, C), _WIRE))
            cols.append(_BIG_COL[n])
        st = _xchg_start('scatter', srcs, lands, cols, chip1, name=f"rs_start_{grp}{layer}")
        scat[(layer, grp)] = (st, cols)
        return st[3]

    gs = {n: [None] * p[n].shape[0] for n in _IN_NAMES[1:] if n not in _BIG and n != 'ln_final_g'}
    for layer in reversed(range(depth)):
        j = layer // 2
        s = saved[layer]
        F = s['gate'].shape[1]
        wts = full[layer]
        dgate, dup = _ffn_dact(dxb, wts['ffn_w_down'], s['gate'], s['up'], name="ffn_dact")
        g_down = _matmul(s['act'], dxb, 'tn', _WIRE, "mm_dw_down", tm=1408, tn=1024, tk=1024)
        g_gate = _matmul(s['h2'], dgate, 'tn', _WIRE, "mm_dw_in", tm=1024, tn=1408, tk=1024)
        g_up = _matmul(s['h2'], dup, 'tn', _WIRE, "mm_dw_in", tm=1024, tn=1408, tk=1024)
        tok = scatter_start(layer, 'ffn', [g_gate, g_up, g_down])
        dh2 = _matmul(dgate, wts['ffn_w_gate'], 'nt', f32, "mm_ffn_dh", tn=2048, tk=512, second=(dup, wts['ffn_w_up']))
        dx, dxb, gs['ln_ffn_g'][layer] = _rmsnorm_bwd(dh2, s['x_mid'], row(p['ln_ffn_g'][layer], tok), dx, name="rms_bwd")
        if layer % 2 == 0:
            w_out, w_in, n_out, n_in = wts['ev_w_out'], wts['ev_w_in'], 'ev_w_out', 'ev_w_in'
        else:
            w_out, w_in, n_out, n_in = wts['od_w_out'], wts['od_w_in'], 'od_w_out', 'od_w_in'
        dycat = _matmul(dxb, w_out, 'nt', f32, "mm_mix_dy")
        g_out = _matmul(s['ycat'], dxb, 'tn', _WIRE, "mm_dw_out", tm=1024, tn=1024, tk=1024)
        proj = s['proj']
        if layer % 2 == 0:
            ba, bx = p['lru_ba'][j][:, None, :], p['lru_bx'][j][:, None, :]
            lam_g, dgate_a = _lru_scan_bwd(dycat, 0, proj, W, s['hl'], s['a'], name="lru_scan_bwd")
            dxc, dwa, dba, dwx, dbx, dlam = _lru_gates_bwd(lam_g, s['hl'], s['xc'], p['lru_wa'][j], ba, p['lru_wx'][j], bx,
                                                           row(p['lru_lambda'][j]), name="lru_gates_bwd")
            dxa = _conv_bwd_dx(dxc, small['lru_conv_w'][j], name="lru_conv_dx", out_dtype=_MXU)
            dcw, dcb = _conv_bwd_dw(dxc, proj, 0, small['lru_conv_w'][j].shape[0], name="lru_conv_dw")
            dq, df, dv, dgp, dlb, dng = _hgrn_bwd(dycat, W, proj, 2 * W, 3 * W, 4 * W, 5 * W, row(lbs[j]),
                                                  row(p['hgrn_norm_g'][j]), s['o'], s['states'], name="hgrn_bwd")
            gs['lru_wa'][j], gs['lru_ba'][j], gs['lru_wx'][j], gs['lru_bx'][j] = dwa, dba[:, 0, :], dwx, dbx[:, 0, :]
            gs['lru_lambda'][j], gs['lru_conv_w'][j], gs['lru_conv_b'][j] = dlam[0], dcw, dcb[0]
            gs['hgrn_lb_logits'][j], gs['hgrn_norm_g'][j] = dlb[0], dng[0]
            dproj = jnp.concatenate([dxa, dgate_a, dq, df, dv, dgp], axis=1)
        else:
            def sc_bwd1(dy_, cp_, sb_):
                return dy_ * cp_, dy_ * sb_

            dsb, dcp = _pw(sc_bwd1, [(dycat, 0), (s['cp'], 0), (proj, 0)], [_MXU, f32], W, name="sc_bwd1")
            dpp = _conv_bwd_dx(dcp, small['sc_conv_w'][j], name="sc_conv_dx")
            dscw, _ = _conv_bwd_dw(dcp, s['pp'], 0, small['sc_conv_w'][j].shape[0], name="sc_conv_dw")

            def sc_bwd2(dp_, sc_, sv_):
                return dp_ * sv_, dp_ * sc_

            dsc, dsv = _pw(sc_bwd2, [(dpp, 0), (proj, W), (proj, 2 * W)], [_MXU, _MXU], W, name="sc_bwd2")
            dd, dlg, dlbeta = _ln_bwd(dycat, W, s['dcv'], row(small['cf_ln_g'][j]), row(small['cf_ln_b'][j]),
                                      name="cf_ln_bwd")
            dglu = _conv_bwd_dx(dd, small['cf_conv_w'][j], name="cf_conv_dx")
            dcfw, dcfb = _conv_bwd_dw(dd, s['glu'], 0, small['cf_conv_w'][j].shape[0], name="cf_conv_dw")

            def glu_bwd(dg_, cu_, cg_):
                _, vjp = jax.vjp(lambda a_, b_: a_ * jax.nn.sigmoid(b_), cu_, cg_)
                return vjp(dg_)

            dcu, dcg = _pw(glu_bwd, [(dglu, 0), (proj, 3 * W), (proj, 4 * W)], [_MXU, _MXU], W, name="cf_glu_bwd")
            gs['sc_conv_w'][j], gs['cf_conv_w'][j], gs['cf_conv_b'][j] = dscw, dcfw, dcfb[0]
            gs['cf_ln_g'][j], gs['cf_ln_b'][j] = dlg[0], dlbeta[0]
            dproj = jnp.concatenate([dsb, dsc, dsv, dcu, dcg], axis=1)
        bname = 'ev_b_in' if layer % 2 == 0 else 'od_b_in'
        gs[bname][j] = _colsum(dproj, name="colsum_" + n_in)[0]
        g_in = _matmul(s['h'], dproj, 'tn', _WIRE, "mm_dw_" + n_in, tm=1024, tn=1536, tk=512)
        tok = scatter_start(layer, 'mix', [g_in, g_out])
        dh = _matmul(dproj, w_in, 'nt', f32, "mm_dh_" + n_in)
        dx, dxb, gs['ln_mix_g'][layer] = _rmsnorm_bwd(dh, s['x_in'], row(p['ln_mix_g'][layer], tok), dx, name="rms_bwd")

    g_small = {n: jnp.stack(v_) for n, v_ in gs.items()}
    g_small['ln_mix_g'] = g_small['ln_mix_g'][:, 0, :]
    g_small['ln_ffn_g'] = g_small['ln_ffn_g'][:, 0, :]
    g_small['ln_final_g'] = dg_final[0]
    g_small['hgrn_lb_logits'] = _lb_bwd(p['hgrn_lb_logits'], g_small['hgrn_lb_logits'], name="lb_bwd")
    small_names = _SMALL_REPL + _SMALL_SHARDED
    pack = _pack([g_small[n] for n in small_names])
    small_xchg = _xchg_start('all', [pack], [lax.empty((8,) + pack.shape, f32)], [None], chip1, name="small_grads_start")
    outs = {}

    acc = {n: tuple(lax.empty(p[n].shape, f32) for _ in range(4)) for n in _BIG}
    after_tail = small_xchg[3]

    def finish(swap, after):
        st, names, tag = swap
        parts, others = _xchg_wait('swap', st, [None] * len(names), after, name=f"swap_wait_{tag}")
        for (n, l), part, other in zip(names, parts, others, strict=True):
            acc[n] = tuple(_adamw_layer(p[n], p['m_' + n], p['v_' + n], l, [part, other], acc[n], name=f"adamw_{n}"))

    prev_swap = None
    for layer, grp in reversed(groups):
        st, cols = scat.pop((layer, grp))
        srcs, recvs = _xchg_wait('scatter', st, cols, after_tail, name=f"rs_wait_{grp}{layer}")
        names = group_names(layer, grp)
        parts = [_sum_own_recv(g, chip1, col, recv, name=f"sum_{n}")
                 for (n, l), g, recv, col in zip(names, srcs, recvs, cols, strict=True)]
        st = _xchg_start('swap', parts, [lax.empty(a.shape, f32) for a in parts], [None] * len(parts), recvs[0],
                         name=f"swap_start_{grp}{layer}")
        if prev_swap is not None:
            finish(prev_swap, st[3])
        prev_swap = (st, names, f"{grp}{layer}")
    finish(prev_swap, prev_swap[0][3])
    for n in _BIG:
        for kind, buf in zip(('grad', 'delta', 'new_m', 'new_v'), acc[n]):
            outs[kind + '_' + n] = buf

    _, (recv8,) = _xchg_wait('all', small_xchg, [None], acc[_BIG[0]][0], name="small_grads_wait")
    tot = _sum_slots(recv8, name="sum_small_grads")
    g_tot = dict(zip(small_names, _unpack(tot, [g_small[n].shape for n in small_names])))
    for n in _SMALL_SHARDED:
        lead = p[n].shape[:-1]
        q = p[n].shape[-1] // cl
        blk = g_tot[n].reshape(lead + (4, q * cl))
        g_tot[n] = lax.dynamic_index_in_dim(blk, chip, axis=len(lead), keepdims=False)
    shapes = [p[n].shape for n in small_names]
    res = _adamw(_pack([p[n] for n in small_names]), _pack([p['m_' + n] for n in small_names]),
                 _pack([p['v_' + n] for n in small_names]), [_pack([g_tot[n] for n in small_names])], name="adamw_small")
    for kind, buf in zip(('grad', 'delta', 'new_m', 'new_v'), res):
        for n, a in zip(small_names, _unpack(buf, shapes)):
            outs[kind + '_' + n] = a

    loss = lax.psum(loss_b[0, 0], ("x", "y", "c"))
    weights = _IN_NAMES[1:]
    return (loss, dx[None], *[outs['grad_' + n] for n in weights], *[outs['delta_' + n] for n in weights],
            *[outs['new_m_' + n] for n in weights], *[outs['new_v_' + n] for n in weights])


def kernel(x, ln_mix_g, ln_ffn_g, ln_final_g, ev_w_in, ev_b_in, lru_conv_w, lru_conv_b, lru_wa, lru_ba, lru_wx, lru_bx, lru_lambda, hgrn_lb_logits, hgrn_norm_g, ev_w_out, od_w_in, od_b_in, sc_conv_w, cf_conv_w, cf_conv_b, cf_ln_g, cf_ln_b, od_w_out, ffn_w_gate, ffn_w_up, ffn_w_down, loss_target, m_ln_mix_g, m_ln_ffn_g, m_ln_final_g, m_ev_w_in, m_ev_b_in, m_lru_conv_w, m_lru_conv_b, m_lru_wa, m_lru_ba, m_lru_wx, m_lru_bx, m_lru_lambda, m_hgrn_lb_logits, m_hgrn_norm_g, m_ev_w_out, m_od_w_in, m_od_b_in, m_sc_conv_w, m_cf_conv_w, m_cf_conv_b, m_cf_ln_g, m_cf_ln_b, m_od_w_out, m_ffn_w_gate, m_ffn_w_up, m_ffn_w_down, v_ln_mix_g, v_ln_ffn_g, v_ln_final_g, v_ev_w_in, v_ev_b_in, v_lru_conv_w, v_lru_conv_b, v_lru_wa, v_lru_ba, v_lru_wx, v_lru_bx, v_lru_lambda, v_hgrn_lb_logits, v_hgrn_norm_g, v_ev_w_out, v_od_w_in, v_od_b_in, v_sc_conv_w, v_cf_conv_w, v_cf_conv_b, v_cf_ln_g, v_cf_ln_b, v_od_w_out, v_ffn_w_gate, v_ffn_w_up, v_ffn_w_down):
    vals = locals()
    p = {n: vals[n] for n in _IN_NAMES + ['loss_target']}
    for n in _IN_NAMES[1:]:
        p['m_' + n] = vals['m_' + n]
        p['v_' + n] = vals['v_' + n]
    return _step(p)
```

```python
import functools

import jax
import jax.numpy as jnp
from jax import lax
from jax.experimental import pallas as pl
from jax.experimental.pallas import tpu as pltpu

f32 = jnp.float32
_MXU = jnp.bfloat16
_WIRE = jnp.bfloat16

N_HEADS = 8
LRU_C = 8.0
EPS = 1e-6
F_FLOOR = 1e-30
SUB = 16
HALF = SUB // 2
HGRN_HPB = 2
ADAM_LR, ADAM_B1, ADAM_B2, ADAM_EPS, ADAM_WD, ADAM_STEP = 0.001, 0.9, 0.999, 1e-08, 0.01, 10
V7X_VMEM_LIMIT = 48 * 1024 * 1024
LANE = 128
MESH = pl.DeviceIdType.MESH

_IN_NAMES = ['x', 'ln_mix_g', 'ln_ffn_g', 'ln_final_g', 'ev_w_in', 'ev_b_in', 'lru_conv_w', 'lru_conv_b', 'lru_wa', 'lru_ba',
             'lru_wx', 'lru_bx', 'lru_lambda', 'hgrn_lb_logits', 'hgrn_norm_g', 'ev_w_out', 'od_w_in', 'od_b_in', 'sc_conv_w',
             'cf_conv_w', 'cf_conv_b', 'cf_ln_g', 'cf_ln_b', 'od_w_out', 'ffn_w_gate', 'ffn_w_up', 'ffn_w_down']
_BIG = ['ev_w_in', 'ev_w_out', 'od_w_in', 'od_w_out', 'ffn_w_gate', 'ffn_w_up', 'ffn_w_down']
_BIG_COL = {'ev_w_in': True, 'ev_w_out': False, 'od_w_in': True, 'od_w_out': False, 'ffn_w_gate': True, 'ffn_w_up': True,
            'ffn_w_down': False}
_SMALL_SHARDED = ['lru_conv_w', 'od_b_in', 'sc_conv_w', 'cf_conv_w', 'cf_conv_b', 'cf_ln_g', 'cf_ln_b']
_SMALL_REPL = ['ln_mix_g', 'ln_ffn_g', 'ln_final_g', 'ev_b_in', 'lru_conv_b', 'lru_wa', 'lru_ba', 'lru_wx', 'lru_bx',
               'lru_lambda', 'hgrn_lb_logits', 'hgrn_norm_g']


def _tile(n, pref, align):
    if n <= pref:
        return n
    t = (pref // align) * align
    while t >= align:
        if n % t == 0:
            return t
        t -= align
    return n


def _params(sem):
    return pltpu.CompilerParams(dimension_semantics=sem, vmem_limit_bytes=V7X_VMEM_LIMIT)


def _rows(shape):
    return lax.broadcasted_iota(jnp.int32, shape, 0)


def _mxdot(a, b, dims=(((1,), (0,)), ((), ()))):
    return lax.dot_general(a.astype(_MXU), b.astype(_MXU), dims, preferred_element_type=f32)


_NN = (((1,), (0,)), ((), ()))
_NT = (((1,), (1,)), ((), ()))
_TN = (((0,), (0,)), ((), ()))


def _matmul(a, b, mode, out_dtype, name, bias=None, add=None, tm=1024, tn=512, tk=2048, second=None):
    if mode == 'nn':
        (M, K), (K2, N) = a.shape, b.shape
    elif mode == 'nt':
        (M, K), (N, K2) = a.shape, b.shape
    else:
        (K, M), (K2, N) = a.shape, b.shape
    assert K == K2, (name, a.shape, b.shape)
    tm, tn, tk = _tile(M, tm, LANE), _tile(N, tn, LANE), _tile(K, tk, LANE)
    nk = K // tk
    dims = {'nn': _NN, 'nt': _NT, 'tn': _TN}[mode]
    has_bias, has_add, has_second = bias is not None, add is not None, second is not None

    def body(*refs):
        a_ref, b_ref = refs[0], refs[1]
        pos = 2
        if has_second:
            a2_ref, b2_ref = refs[2], refs[3]
            pos = 4
        bias_ref = add_ref = None
        if has_bias:
            bias_ref = refs[pos]
            pos += 1
        if has_add:
            add_ref = refs[pos]
            pos += 1
        o_ref, acc_ref = refs[pos], refs[pos + 1]
        k = pl.program_id(2)

        @pl.when(k == 0)
        def _():
            acc_ref[...] = jnp.zeros_like(acc_ref)

        prod = _mxdot(a_ref[...], b_ref[...], dims)
        if has_second:
            prod = prod + _mxdot(a2_ref[...], b2_ref[...], dims)
        acc_ref[...] += prod

        @pl.when(k == nk - 1)
        def _():
            r = acc_ref[...]
            if has_bias:
                r = r + bias_ref[...]
            if has_add:
                r = r + add_ref[...]
            o_ref[...] = r.astype(o_ref.dtype)

    if mode == 'tn':
        a_spec = pl.BlockSpec((tk, tm), lambda i, j, k: (k, i))
    else:
        a_spec = pl.BlockSpec((tm, tk), lambda i, j, k: (i, k))
    if mode == 'nt':
        b_spec = pl.BlockSpec((tn, tk), lambda i, j, k: (j, k))
    else:
        b_spec = pl.BlockSpec((tk, tn), lambda i, j, k: (k, j))
    in_specs, args = [a_spec, b_spec], [a, b]
    if has_second:
        assert second[0].shape == a.shape and second[1].shape == b.shape
        in_specs += [a_spec, b_spec]
        args += list(second)
    if has_bias:
        in_specs.append(pl.BlockSpec((1, tn), lambda i, j, k: (0, j)))
        args.append(bias)
    if has_add:
        in_specs.append(pl.BlockSpec((tm, tn), lambda i, j, k: (i, j)))
        args.append(add)
    return pl.pallas_call(
        body, name=name, grid=(M // tm, N // tn, nk), in_specs=in_specs,
        out_specs=pl.BlockSpec((tm, tn), lambda i, j, k: (i, j)),
        out_shape=jax.ShapeDtypeStruct((M, N), out_dtype),
        scratch_shapes=[pltpu.VMEM((tm, tn), f32)],
        compiler_params=_params(("parallel", "parallel", "arbitrary")),
    )(*args)


def _ffn_in(h2, wg, wu, name):
    (M, K), N = h2.shape, wg.shape[1]
    tm, tn = _tile(M, 1024, LANE), _tile(N, 512, LANE)

    def body(a_ref, g_ref, u_ref, go_ref, uo_ref, act_ref):
        a = a_ref[...]
        g, u = _mxdot(a, g_ref[...]), _mxdot(a, u_ref[...])
        go_ref[...] = g.astype(go_ref.dtype)
        uo_ref[...] = u.astype(uo_ref.dtype)
        act_ref[...] = (jax.nn.silu(g) * u).astype(act_ref.dtype)

    wspec = pl.BlockSpec((K, tn), lambda i, j: (0, j))
    ospec = pl.BlockSpec((tm, tn), lambda i, j: (i, j))
    return pl.pallas_call(
        body, name=name, grid=(M // tm, N // tn), in_specs=[pl.BlockSpec((tm, K), lambda i, j: (i, 0)), wspec, wspec],
        out_specs=[ospec] * 3, out_shape=[jax.ShapeDtypeStruct((M, N), _MXU)] * 3,
        compiler_params=_params(("parallel", "parallel")),
    )(h2, wg, wu)


def _ffn_dact(dx, wd, gate, up, name):
    (M, K), N = dx.shape, wd.shape[0]
    tm, tn = _tile(M, 512, LANE), _tile(N, 1408, LANE)

    def body(a_ref, w_ref, g_ref, u_ref, dg_ref, du_ref):
        da = _mxdot(a_ref[...], w_ref[...], _NT)
        _, vjp = jax.vjp(lambda gg, uu: jax.nn.silu(gg) * uu, g_ref[...].astype(f32), u_ref[...].astype(f32))
        dg, du = vjp(da)
        dg_ref[...] = dg.astype(dg_ref.dtype)
        du_ref[...] = du.astype(du_ref.dtype)

    ospec = pl.BlockSpec((tm, tn), lambda i, j: (i, j))
    return pl.pallas_call(
        body, name=name, grid=(M // tm, N // tn),
        in_specs=[pl.BlockSpec((tm, K), lambda i, j: (i, 0)), pl.BlockSpec((tn, K), lambda i, j: (j, 0)), ospec, ospec],
        out_specs=[ospec] * 2, out_shape=[jax.ShapeDtypeStruct((M, N), _MXU)] * 2,
        compiler_params=_params(("parallel", "parallel")),
    )(dx, wd, gate, up)


def _pw(fn, slabs, out_dtypes, width, name, consts=()):
    T = slabs[0][0].shape[0]
    tt, cb = _tile(T, 512, 8), _tile(width, 512, LANE)
    nin, ncst = len(slabs), len(consts)

    def body(*refs):
        res = fn(*[r[...] for r in refs[:nin + ncst]])
        if not isinstance(res, (tuple, list)):
            res = (res,)
        for r, o in zip(res, refs[nin + ncst:], strict=True):
            o[...] = r.astype(o.dtype)

    in_specs, args = [], []
    for arr, col0 in slabs:
        assert col0 % cb == 0
        in_specs.append(pl.BlockSpec((tt, cb), functools.partial(lambda t, c, c0: (t, c0 + c), c0=col0 // cb)))
        args.append(arr)
    for cst in consts:
        in_specs.append(pl.BlockSpec((1, cb), lambda t, c: (0, c)))
        args.append(cst)
    outs = pl.pallas_call(
        body, name=name, grid=(T // tt, width // cb), in_specs=in_specs,
        out_specs=[pl.BlockSpec((tt, cb), lambda t, c: (t, c)) for _ in out_dtypes],
        out_shape=[jax.ShapeDtypeStruct((T, width), d) for d in out_dtypes],
        compiler_params=_params(("parallel", "parallel")),
    )(*args)
    return outs[0] if len(out_dtypes) == 1 else outs


def _rmsnorm_fwd(x, g, name):
    T, D = x.shape
    tt = _tile(T, 256, 8)

    def body(x_ref, g_ref, o_ref):
        xv = x_ref[...]
        r = lax.rsqrt(jnp.mean(xv * xv, axis=-1, keepdims=True) + EPS)
        o_ref[...] = (xv * r * g_ref[...]).astype(o_ref.dtype)

    return pl.pallas_call(
        body, name=name, grid=(T // tt,),
        in_specs=[pl.BlockSpec((tt, D), lambda t: (t, 0)), pl.BlockSpec((1, D), lambda t: (0, 0))],
        out_specs=pl.BlockSpec((tt, D), lambda t: (t, 0)),
        out_shape=jax.ShapeDtypeStruct((T, D), _MXU), compiler_params=_params(("parallel",)),
    )(x, g)


def _rmsnorm_bwd(dh, x, g, dx_in, name):
    T, D = x.shape
    tt = _tile(T, 256, 8)

    def body(dh_ref, x_ref, g_ref, dxi_ref, dx_ref, dxb_ref, dg_ref):
        t = pl.program_id(0)
        xv, d = x_ref[...], dh_ref[...]
        r = lax.rsqrt(jnp.mean(xv * xv, axis=-1, keepdims=True) + EPS)
        n = xv * r
        dn = d * g_ref[...]
        dx = dxi_ref[...] + r * (dn - n * jnp.mean(dn * n, axis=-1, keepdims=True))
        dx_ref[...] = dx
        dxb_ref[...] = dx.astype(dxb_ref.dtype)

        @pl.when(t == 0)
        def _():
            dg_ref[...] = jnp.zeros_like(dg_ref)

        dg_ref[...] += jnp.sum(d * n, axis=0, keepdims=True)

    return pl.pallas_call(
        body, name=name, grid=(T // tt,),
        in_specs=[pl.BlockSpec((tt, D), lambda t: (t, 0)), pl.BlockSpec((tt, D), lambda t: (t, 0)),
                  pl.BlockSpec((1, D), lambda t: (0, 0)), pl.BlockSpec((tt, D), lambda t: (t, 0))],
        out_specs=[pl.BlockSpec((tt, D), lambda t: (t, 0)), pl.BlockSpec((tt, D), lambda t: (t, 0)),
                   pl.BlockSpec((1, D), lambda t: (0, 0))],
        out_shape=[jax.ShapeDtypeStruct((T, D), f32), jax.ShapeDtypeStruct((T, D), _MXU), jax.ShapeDtypeStruct((1, D), f32)],
        compiler_params=_params(("arbitrary",)),
    )(dh, x, g, dx_in)


def _final_loss(x, g, tgt, name):
    T, D = x.shape
    tt = _tile(T, 256, 8)

    def body(x_ref, g_ref, t_ref, l_ref, dx_ref, dxb_ref, dg_ref):
        t = pl.program_id(0)
        xv = x_ref[...]
        r = lax.rsqrt(jnp.mean(xv * xv, axis=-1, keepdims=True) + EPS)
        n = xv * r
        e = n * g_ref[...] - t_ref[...]
        part = 0.5 * jnp.sum(jnp.mean(e * e, axis=-1, keepdims=True), axis=0, keepdims=True)
        dy = e * (1.0 / D)
        dn = dy * g_ref[...]
        dx = r * (dn - n * jnp.mean(dn * n, axis=-1, keepdims=True))
        dx_ref[...] = dx
        dxb_ref[...] = dx.astype(dxb_ref.dtype)

        @pl.when(t == 0)
        def _():
            dg_ref[...] = jnp.zeros_like(dg_ref)
            l_ref[...] = jnp.zeros_like(l_ref)

        dg_ref[...] += jnp.sum(dy * n, axis=0, keepdims=True)
        l_ref[...] += jnp.broadcast_to(part, l_ref.shape)

    return pl.pallas_call(
        body, name=name, grid=(T // tt,),
        in_specs=[pl.BlockSpec((tt, D), lambda t: (t, 0)), pl.BlockSpec((1, D), lambda t: (0, 0)),
                  pl.BlockSpec((tt, D), lambda t: (t, 0))],
        out_specs=[pl.BlockSpec((1, LANE), lambda t: (0, 0)), pl.BlockSpec((tt, D), lambda t: (t, 0)),
                   pl.BlockSpec((tt, D), lambda t: (t, 0)), pl.BlockSpec((1, D), lambda t: (0, 0))],
        out_shape=[jax.ShapeDtypeStruct((1, LANE), f32), jax.ShapeDtypeStruct((T, D), f32),
                   jax.ShapeDtypeStruct((T, D), _MXU), jax.ShapeDtypeStruct((1, D), f32)],
        compiler_params=_params(("arbitrary",)),
    )(x, g, tgt)


def _assemble(pieces, name):
    T, W = pieces[0].shape
    n = len(pieces)
    tt = _tile(T, 512, 16)

    def body(*refs):
        o_ref, cs_ref = refs[n], refs[n + 1]
        t, q = pl.program_id(0), pl.program_id(1)

        @pl.when(jnp.logical_and(t == 0, q == 0))
        def _():
            cs_ref[...] = jnp.zeros_like(cs_ref)

        for i in range(n):
            @pl.when(q == i)
            def _(i=i):
                v = refs[i][...]
                o_ref[...] = v
                cs_ref[:, i * W:(i + 1) * W] += jnp.sum(v.astype(f32), axis=0, keepdims=True)

    return pl.pallas_call(
        body, name=name, grid=(T // tt, n), in_specs=[pl.BlockSpec((tt, W), lambda t, q: (t, 0))] * n,
        out_specs=[pl.BlockSpec((tt, W), lambda t, q: (t, q)), pl.BlockSpec((1, n * W), lambda t, q: (0, 0))],
        out_shape=[jax.ShapeDtypeStruct((T, n * W), pieces[0].dtype), jax.ShapeDtypeStruct((1, n * W), f32)],
        compiler_params=_params(("arbitrary", "arbitrary")),
    )(*pieces)


def _shift_down(cur, prev, j):
    if j == 0:
        return cur
    n = cur.shape[0]
    return jnp.where(_rows(cur.shape) < j, pltpu.roll(prev, j, 0), pltpu.roll(cur, j, 0))


def _shift_up(cur, nxt, j):
    if j == 0:
        return cur
    n = cur.shape[0]
    return jnp.where(_rows(cur.shape) >= n - j, pltpu.roll(nxt, n - j, 0), pltpu.roll(cur, n - j, 0))


def _conv_tiles(T, C, K):
    big = K <= 8
    tt, cb = _tile(T, 512 if big else 256, 8), _tile(C, 512 if big else 256, LANE)
    assert tt >= K, (tt, K)
    return tt, cb


def _conv_fwd(x, col0, w, b, C, name):
    T, K = x.shape[0], w.shape[0]
    tt, cb = _conv_tiles(T, C, K)
    c0 = col0 // cb
    assert col0 % cb == 0
    has_b = b is not None

    def body(*refs):
        cur_ref, prev_ref, w_ref = refs[:3]
        o_ref = refs[-1]
        t = pl.program_id(1)
        cur = cur_ref[...]
        prev = jnp.where(t > 0, prev_ref[...], 0.0)
        wv = w_ref[...]
        acc = jnp.zeros_like(cur)
        for k in range(K):
            acc = acc + wv[k:k + 1, :] * _shift_down(cur, prev, K - 1 - k)
        if has_b:
            acc = acc + refs[3][...]
        o_ref[...] = acc

    in_specs = [pl.BlockSpec((tt, cb), lambda c, t: (t, c0 + c)),
                pl.BlockSpec((tt, cb), lambda c, t: (jnp.maximum(t - 1, 0), c0 + c)),
                pl.BlockSpec((K, cb), lambda c, t: (0, c))]
    args = [x, x, w]
    if has_b:
        in_specs.append(pl.BlockSpec((1, cb), lambda c, t: (0, c)))
        args.append(b)
    return pl.pallas_call(
        body, name=name, grid=(C // cb, T // tt), in_specs=in_specs,
        out_specs=pl.BlockSpec((tt, cb), lambda c, t: (t, c)), out_shape=jax.ShapeDtypeStruct((T, C), f32),
        compiler_params=_params(("parallel", "parallel")),
    )(*args)


def _conv_bwd_dx(dy, w, name, out_dtype=f32):
    T, C = dy.shape
    K = w.shape[0]
    tt, cb = _conv_tiles(T, C, K)
    nt = T // tt

    def body(cur_ref, nxt_ref, w_ref, o_ref):
        t = pl.program_id(1)
        cur = cur_ref[...]
        nxt = jnp.where(t < nt - 1, nxt_ref[...], 0.0)
        wv = w_ref[...]
        acc = jnp.zeros_like(cur)
        for k in range(K):
            acc = acc + wv[k:k + 1, :] * _shift_up(cur, nxt, K - 1 - k)
        o_ref[...] = acc.astype(o_ref.dtype)

    return pl.pallas_call(
        body, name=name, grid=(C // cb, nt),
        in_specs=[pl.BlockSpec((tt, cb), lambda c, t: (t, c)),
                  pl.BlockSpec((tt, cb), lambda c, t: (jnp.minimum(t + 1, nt - 1), c)),
                  pl.BlockSpec((K, cb), lambda c, t: (0, c))],
        out_specs=pl.BlockSpec((tt, cb), lambda c, t: (t, c)), out_shape=jax.ShapeDtypeStruct((T, C), out_dtype),
        compiler_params=_params(("parallel", "parallel")),
    )(dy, dy, w)


def _conv_bwd_dw(dy, x, col0, K, name):
    T, C = dy.shape
    tt, cb = _conv_tiles(T, C, K)
    c0 = col0 // cb
    assert col0 % cb == 0

    def body(dy_ref, cur_ref, prev_ref, dw_ref, db_ref):
        t = pl.program_id(1)

        @pl.when(t == 0)
        def _():
            dw_ref[...] = jnp.zeros_like(dw_ref)
            db_ref[...] = jnp.zeros_like(db_ref)

        d = dy_ref[...]
        cur = cur_ref[...]
        prev = jnp.where(t > 0, prev_ref[...], 0.0)
        for k in range(K):
            row = jnp.sum(d * _shift_down(cur, prev, K - 1 - k), axis=0, keepdims=True)
            dw_ref[pl.ds(k, 1), :] = dw_ref[pl.ds(k, 1), :] + row
        db_ref[...] += jnp.sum(d, axis=0, keepdims=True)

    return pl.pallas_call(
        body, name=name, grid=(C // cb, T // tt),
        in_specs=[pl.BlockSpec((tt, cb), lambda c, t: (t, c)),
                  pl.BlockSpec((tt, cb), lambda c, t: (t, c0 + c)),
                  pl.BlockSpec((tt, cb), lambda c, t: (jnp.maximum(t - 1, 0), c0 + c))],
        out_specs=[pl.BlockSpec((K, cb), lambda c, t: (0, c)), pl.BlockSpec((1, cb), lambda c, t: (0, c))],
        out_shape=[jax.ShapeDtypeStruct((K, C), f32), jax.ShapeDtypeStruct((1, C), f32)],
        compiler_params=_params(("parallel", "arbitrary")),
    )(dy, x, x)


def _expm1(z):
    poly = z * (1.0 + z * (0.5 + z * (1.0 / 6.0 + z * (1.0 / 24.0 + z * (1.0 / 120.0)))))
    return jnp.where(jnp.abs(z) < 0.1, poly, jnp.exp(z) - 1.0)


def _lru_pt(xc, rp, ip, lam, first):
    r = jax.nn.sigmoid(rp)
    i = jax.nn.sigmoid(ip)
    log_a = -LRU_C * r * jax.nn.softplus(-lam)
    a = jnp.exp(log_a)
    mult = jnp.sqrt(jnp.maximum(-_expm1(2.0 * log_a), 0.0))
    mult = jnp.where(first, 1.0, mult)
    return a, mult * i * xc


def _first_mask(shape, t):
    return jnp.logical_and(_rows(shape) == 0, t == 0)


def _lru_gates_fwd(xc, wa, ba, wx, bx, lam, name):
    T, W = xc.shape
    hd = W // N_HEADS
    tt = _tile(T, 512, 8)

    def body(xc_ref, wa_ref, ba_ref, wx_ref, bx_ref, lam_ref, a_ref, u_ref):
        t = pl.program_id(1)
        x = xc_ref[...]
        rp = _mxdot(x, wa_ref[...]) + ba_ref[...]
        ip = _mxdot(x, wx_ref[...]) + bx_ref[...]
        a, u = _lru_pt(x, rp, ip, lam_ref[...], _first_mask(x.shape, t))
        a_ref[...] = a
        u_ref[...] = u

    wspec = pl.BlockSpec((None, hd, hd), lambda h, t: (h, 0, 0))
    bspec = pl.BlockSpec((None, 1, hd), lambda h, t: (h, 0, 0))
    tspec = pl.BlockSpec((tt, hd), lambda h, t: (t, h))
    return pl.pallas_call(
        body, name=name, grid=(N_HEADS, T // tt),
        in_specs=[tspec, wspec, bspec, wspec, bspec, pl.BlockSpec((1, hd), lambda h, t: (0, h))],
        out_specs=[tspec, tspec], out_shape=[jax.ShapeDtypeStruct((T, W), f32)] * 2,
        compiler_params=_params(("parallel", "parallel")),
    )(xc, wa, ba, wx, bx, lam)


def _lru_scan_fwd(a, u, gate, gcol0, name):
    T, W = a.shape
    tt, cb = _tile(T, 256, 8), _tile(W, 512, LANE)
    g0 = gcol0 // cb
    assert gcol0 % cb == 0

    def body(a_ref, u_ref, g_ref, h_ref, y_ref, carry_ref):
        t = pl.program_id(1)

        @pl.when(t == 0)
        def _():
            carry_ref[...] = jnp.zeros_like(carry_ref)

        def step(i, h):
            base = pl.multiple_of(i * 8, 8)
            a8, u8 = a_ref[pl.ds(base, 8), :], u_ref[pl.ds(base, 8), :]
            rows = []
            for j in range(8):
                h = a8[j:j + 1, :] * h + u8[j:j + 1, :]
                rows.append(h)
            h_ref[pl.ds(base, 8), :] = jnp.concatenate(rows, axis=0)
            return h

        h_last = lax.fori_loop(0, tt // 8, step, carry_ref[0:1, :])
        carry_ref[...] = jnp.broadcast_to(h_last, carry_ref.shape)
        y_ref[...] = (h_ref[...] * jax.nn.gelu(g_ref[...])).astype(y_ref.dtype)

    tspec = pl.BlockSpec((tt, cb), lambda c, t: (t, c))
    return pl.pallas_call(
        body, name=name, grid=(W // cb, T // tt),
        in_specs=[tspec, tspec, pl.BlockSpec((tt, cb), lambda c, t: (t, g0 + c))],
        out_specs=[tspec, tspec],
        out_shape=[jax.ShapeDtypeStruct((T, W), f32), jax.ShapeDtypeStruct((T, W), _MXU)],
        scratch_shapes=[pltpu.VMEM((8, cb), f32)],
        compiler_params=_params(("parallel", "arbitrary")),
    )(a, u, gate)


def _lru_scan_bwd(dy, dcol0, gate, gcol0, h, a, name):
    T, W = a.shape
    tt, cb = _tile(T, 256, 8), _tile(W, 512, LANE)
    nt = T // tt
    d0, g0 = dcol0 // cb, gcol0 // cb
    assert dcol0 % cb == 0 and gcol0 % cb == 0

    def body(dy_ref, g_ref, h_ref, a_ref, lam_ref, dg_ref, carry_ref, dh_ref):
        t = pl.program_id(1)

        @pl.when(t == 0)
        def _():
            carry_ref[...] = jnp.zeros_like(carry_ref)

        _, vjp = jax.vjp(lambda hh, gg: hh * jax.nn.gelu(gg), h_ref[...], g_ref[...])
        dh, dg = vjp(dy_ref[...])
        dg_ref[...] = dg.astype(dg_ref.dtype)
        dh_ref[...] = dh

        def step(i, c):
            base = pl.multiple_of((tt // 8 - 1 - i) * 8, 8)
            a8, d8 = a_ref[pl.ds(base, 8), :], dh_ref[pl.ds(base, 8), :]
            rows = [None] * 8
            for j in range(7, -1, -1):
                lam = d8[j:j + 1, :] + c
                c = a8[j:j + 1, :] * lam
                rows[j] = lam
            lam_ref[pl.ds(base, 8), :] = jnp.concatenate(rows, axis=0)
            return c

        c_last = lax.fori_loop(0, tt // 8, step, carry_ref[0:1, :])
        carry_ref[...] = jnp.broadcast_to(c_last, carry_ref.shape)

    rev = lambda c, t: (nt - 1 - t, c)
    tspec = pl.BlockSpec((tt, cb), rev)
    return pl.pallas_call(
        body, name=name, grid=(W // cb, nt),
        in_specs=[pl.BlockSpec((tt, cb), lambda c, t: (nt - 1 - t, d0 + c)),
                  pl.BlockSpec((tt, cb), lambda c, t: (nt - 1 - t, g0 + c)), tspec, tspec],
        out_specs=[tspec, tspec], out_shape=[jax.ShapeDtypeStruct((T, W), f32), jax.ShapeDtypeStruct((T, W), _MXU)],
        scratch_shapes=[pltpu.VMEM((8, cb), f32), pltpu.VMEM((tt, cb), f32)],
        compiler_params=_params(("parallel", "arbitrary")),
    )(dy, gate, h, a)


def _lru_gates_bwd(lam_g, h, xc, wa, ba, wx, bx, lam, name):
    T, W = xc.shape
    hd = W // N_HEADS
    tt = _tile(T, 512, 8)

    def body(lg_ref, h_ref, hp_ref, xc_ref, wa_ref, ba_ref, wx_ref, bx_ref, lam_ref,
             dxc_ref, dwa_ref, dba_ref, dwx_ref, dbx_ref, dlam_ref):
        t = pl.program_id(1)

        @pl.when(t == 0)
        def _():
            for r in (dwa_ref, dba_ref, dwx_ref, dbx_ref, dlam_ref):
                r[...] = jnp.zeros_like(r)

        x = xc_ref[...]
        lg = lg_ref[...]
        h_prev = _shift_down(h_ref[...], jnp.where(t > 0, hp_ref[...], 0.0), 1)
        rp = _mxdot(x, wa_ref[...]) + ba_ref[...]
        ip = _mxdot(x, wx_ref[...]) + bx_ref[...]
        first = _first_mask(x.shape, t)
        _, vjp = jax.vjp(lambda xx, r_, i_, l_: _lru_pt(xx, r_, i_, l_, first), x, rp, ip, lam_ref[...])
        dx, drp, dip, dl = vjp((lg * h_prev, lg))
        dxc_ref[...] = dx + _mxdot(drp, wa_ref[...], _NT) + _mxdot(dip, wx_ref[...], _NT)
        dwa_ref[...] += _mxdot(x, drp, _TN)
        dwx_ref[...] += _mxdot(x, dip, _TN)
        dba_ref[...] += jnp.sum(drp, axis=0, keepdims=True)
        dbx_ref[...] += jnp.sum(dip, axis=0, keepdims=True)
        dlam_ref[...] += dl

    wspec = pl.BlockSpec((None, hd, hd), lambda h_, t: (h_, 0, 0))
    bspec = pl.BlockSpec((None, 1, hd), lambda h_, t: (h_, 0, 0))
    tspec = pl.BlockSpec((tt, hd), lambda h_, t: (t, h_))
    pspec = pl.BlockSpec((tt, hd), lambda h_, t: (jnp.maximum(t - 1, 0), h_))
    lspec = pl.BlockSpec((1, hd), lambda h_, t: (0, h_))
    return pl.pallas_call(
        body, name=name, grid=(N_HEADS, T // tt),
        in_specs=[tspec, tspec, pspec, tspec, wspec, bspec, wspec, bspec, lspec],
        out_specs=[tspec, wspec, bspec, wspec, bspec, lspec],
        out_shape=[jax.ShapeDtypeStruct((T, W), f32), jax.ShapeDtypeStruct((N_HEADS, hd, hd), f32),
                   jax.ShapeDtypeStruct((N_HEADS, 1, hd), f32), jax.ShapeDtypeStruct((N_HEADS, hd, hd), f32),
                   jax.ShapeDtypeStruct((N_HEADS, 1, hd), f32), jax.ShapeDtypeStruct((1, W), f32)],
        compiler_params=_params(("parallel", "arbitrary")),
    )(lam_g, h, h, xc, wa, ba, wx, bx, lam)


def _hgrn_pt(z, qp, lb):
    sig = jax.nn.sigmoid(z)
    fg = lb + (1.0 - lb) * sig
    logf = jnp.log(jnp.maximum(fg, F_FLOOR))
    k = (1.0 - lb) * (1.0 - sig)
    return logf, k, jax.nn.silu(qp)


def _hgrn_out(o, gp, ng):
    on = o * lax.rsqrt(jnp.mean(o * o, axis=-1, keepdims=True) + EPS)
    return on * ng * jax.nn.silu(gp)


def _cumsum_rows(x):
    n, row, sh = x.shape[0], _rows(x.shape), 1
    while sh < n:
        x = x + jnp.where(row >= sh, pltpu.roll(x, sh, 0), 0.0)
        sh *= 2
    return x


def _rev_cumsum_rows(x):
    n, row, sh = x.shape[0], _rows(x.shape), 1
    while sh < n:
        x = x + jnp.where(row < n - sh, pltpu.roll(x, n - sh, 0), 0.0)
        sh *= 2
    return x


def _hgrn_fwd(proj, qcol, fcol, vcol, gcol, lb, ng, name):
    T = proj.shape[0]
    W = lb.shape[1]
    hd = W // N_HEADS
    bw = HGRN_HPB * hd
    tt = _tile(T, 256, SUB)
    ns = tt // SUB
    q0, f0, v0, g0 = qcol // bw, fcol // bw, vcol // bw, gcol // bw

    def body(q_ref, f_ref, v_ref, g_ref, lb_ref, ng_ref, y_ref, o_ref, st_ref, s_ref):
        t = pl.program_id(1)

        @pl.when(t == 0)
        def _():
            s_ref[...] = jnp.zeros_like(s_ref)

        row = _rows((SUB, hd))

        def sub(j, carry):
            rs = pl.ds(pl.multiple_of(j * SUB, SUB), SUB)
            for hh in range(HGRN_HPB):
                cs = slice(hh * hd, (hh + 1) * hd)
                logf, k, qf = _hgrn_pt(f_ref[rs, cs], q_ref[rs, cs], lb_ref[:, cs])
                v = v_ref[rs, cs]
                b = _cumsum_rows(logf)
                b_last = b[SUB - 1:SUB, :]
                S = s_ref[hh]
                st_ref[hh, j] = S
                halves = [jnp.zeros((HALF, hd), f32), jnp.zeros((HALF, hd), f32)]
                for r in range(SUB):
                    n = HALF if r < HALF else SUB
                    e = jnp.exp(jnp.minimum(b[r:r + 1, :] - b[:n], 0.0))
                    m = jnp.where(row[:n] <= r, qf[r:r + 1, :] * k[:n] * e, 0.0)
                    p = jnp.sum(m, axis=1, keepdims=True)
                    o_r = jnp.sum(p * v[:n], axis=0, keepdims=True)
                    halves[r // HALF] = jnp.where(row[:HALF] == r % HALF, o_r, halves[r // HALF])
                o_ref[rs, cs] = _mxdot(qf * jnp.exp(b), S, _NT) + jnp.concatenate(halves, axis=0)
                s_ref[hh] = S * jnp.exp(b_last) + _mxdot(v, k * jnp.exp(b_last - b), _TN)
            return carry

        lax.fori_loop(0, ns, sub, 0)
        for hh in range(HGRN_HPB):
            cs = slice(hh * hd, (hh + 1) * hd)
            y_ref[:, cs] = _hgrn_out(o_ref[:, cs], g_ref[:, cs], ng_ref[:, cs]).astype(y_ref.dtype)

    def slab(c0):
        return pl.BlockSpec((tt, bw), functools.partial(lambda h, t, c0: (t, c0 + h), c0=c0))

    hspec = pl.BlockSpec((tt, bw), lambda h, t: (t, h))
    cspec = pl.BlockSpec((1, bw), lambda h, t: (0, h))
    return pl.pallas_call(
        body, name=name, grid=(N_HEADS // HGRN_HPB, T // tt),
        in_specs=[slab(q0), slab(f0), slab(v0), slab(g0), cspec, cspec],
        out_specs=[hspec, hspec, pl.BlockSpec((HGRN_HPB, ns, hd, hd), lambda h, t: (h, t, 0, 0))],
        out_shape=[jax.ShapeDtypeStruct((T, W), _MXU), jax.ShapeDtypeStruct((T, W), f32),
                   jax.ShapeDtypeStruct((N_HEADS, T // SUB, hd, hd), f32)],
        scratch_shapes=[pltpu.VMEM((HGRN_HPB, hd, hd), f32)],
        compiler_params=_params(("parallel", "arbitrary")),
    )(proj, proj, proj, proj, lb, ng)


def _hgrn_bwd(dy, dcol, proj, qcol, fcol, vcol, gcol, lb, ng, o, states, name):
    T = proj.shape[0]
    W = lb.shape[1]
    hd = W // N_HEADS
    bw = HGRN_HPB * hd
    tt = _tile(T, 256, SUB)
    ns, nt = tt // SUB, T // tt
    q0, f0, v0, g0, d0 = qcol // bw, fcol // bw, vcol // bw, gcol // bw, dcol // bw

    def body(dy_ref, q_ref, f_ref, v_ref, g_ref, lb_ref, ng_ref, o_ref, st_ref,
             dq_ref, df_ref, dv_ref, dg_ref, dlb_ref, dng_ref, ds_ref, do_ref):
        t = pl.program_id(1)

        @pl.when(t == 0)
        def _():
            ds_ref[...] = jnp.zeros_like(ds_ref)
            dlb_ref[...] = jnp.zeros_like(dlb_ref)
            dng_ref[...] = jnp.zeros_like(dng_ref)

        for hh in range(HGRN_HPB):
            cs = slice(hh * hd, (hh + 1) * hd)
            _, vjp_out = jax.vjp(_hgrn_out, o_ref[:, cs], g_ref[:, cs], ng_ref[:, cs])
            do, dgp, dng = vjp_out(dy_ref[:, cs])
            do_ref[:, cs] = do
            dg_ref[:, cs] = dgp.astype(dg_ref.dtype)
            dng_ref[:, cs] += dng
        row = _rows((SUB, hd))

        def sub(jj, carry):
            j = ns - 1 - jj
            rs = pl.ds(pl.multiple_of(j * SUB, SUB), SUB)
            for hh in range(HGRN_HPB):
                cs = slice(hh * hd, (hh + 1) * hd)
                (logf, k, qf), vjp_pt = jax.vjp(_hgrn_pt, f_ref[rs, cs], q_ref[rs, cs], lb_ref[:, cs])
                v = v_ref[rs, cs]
                dO = do_ref[rs, cs]
                b = _cumsum_rows(logf)
                b_last = b[SUB - 1:SUB, :]
                S = st_ref[hh, j]
                dS = ds_ref[hh]
                eb = jnp.exp(b)
                kd = jnp.exp(b_last - b)
                d = jnp.exp(b_last)
                qe, ke = qf * eb, k * kd
                dqe = _mxdot(dO, S, _NN)
                dke = _mxdot(v, dS, _NN)
                dv = _mxdot(ke, dS, _NT)
                dd = jnp.sum(dS * S, axis=0, keepdims=True)
                ds_ref[hh] = dS * d + _mxdot(dO, qe, _TN)
                dq_h = [jnp.zeros((HALF, hd), f32), jnp.zeros((HALF, hd), f32)]
                dk_acc = {HALF: jnp.zeros((HALF, hd), f32), SUB: jnp.zeros((SUB, hd), f32)}
                dv_acc = {HALF: jnp.zeros((HALF, hd), f32), SUB: jnp.zeros((SUB, hd), f32)}
                for r in range(SUB):
                    n = HALF if r < HALF else SUB
                    em = jnp.where(row[:n] <= r, jnp.exp(jnp.minimum(b[r:r + 1, :] - b[:n], 0.0)), 0.0)
                    ke_r = k[:n] * em
                    qr, dor = qf[r:r + 1, :], dO[r:r + 1, :]
                    p = jnp.sum(qr * ke_r, axis=1, keepdims=True)
                    dv_acc[n] = dv_acc[n] + p * dor
                    dp = jnp.sum(dor * v[:n], axis=1, keepdims=True)
                    dq_r = jnp.sum(dp * ke_r, axis=0, keepdims=True)
                    dq_h[r // HALF] = jnp.where(row[:HALF] == r % HALF, dq_r, dq_h[r // HALF])
                    dk_acc[n] = dk_acc[n] + dp * (qr * em)
                pad = jnp.zeros((SUB - HALF, hd), f32)
                dq_i = jnp.concatenate(dq_h, axis=0)
                dk_i = dk_acc[SUB] + jnp.concatenate([dk_acc[HALF], pad], axis=0)
                dv = dv + dv_acc[SUB] + jnp.concatenate([dv_acc[HALF], pad], axis=0)
                dqf = dqe * eb + dq_i
                dk = dke * kd + dk_i
                dke_ke = dke * ke
                db = dqe * qe - dke_ke + qf * dq_i - k * dk_i
                db_last = jnp.sum(dke_ke, axis=0, keepdims=True) + dd * d
                db = db + jnp.where(row == SUB - 1, db_last, 0.0)
                dz, dqp, dlb = vjp_pt((_rev_cumsum_rows(db), dk, dqf))
                dq_ref[rs, cs] = dqp.astype(dq_ref.dtype)
                df_ref[rs, cs] = dz.astype(df_ref.dtype)
                dv_ref[rs, cs] = dv.astype(dv_ref.dtype)
                dlb_ref[:, cs] += dlb
            return carry

        lax.fori_loop(0, ns, sub, 0)

    def slab(c0):
        return pl.BlockSpec((tt, bw), functools.partial(lambda h, t, c0: (nt - 1 - t, c0 + h), c0=c0))

    hspec = pl.BlockSpec((tt, bw), lambda h, t: (nt - 1 - t, h))
    cspec = pl.BlockSpec((1, bw), lambda h, t: (0, h))
    return pl.pallas_call(
        body, name=name, grid=(N_HEADS // HGRN_HPB, nt),
        in_specs=[slab(d0), slab(q0), slab(f0), slab(v0), slab(g0), cspec, cspec, hspec,
                  pl.BlockSpec((HGRN_HPB, ns, hd, hd), lambda h, t: (h, nt - 1 - t, 0, 0))],
        out_specs=[hspec, hspec, hspec, hspec, cspec, cspec],
        out_shape=[jax.ShapeDtypeStruct((T, W), _MXU)] * 4 + [jax.ShapeDtypeStruct((1, W), f32)] * 2,
        scratch_shapes=[pltpu.VMEM((HGRN_HPB, hd, hd), f32), pltpu.VMEM((tt, bw), f32)],
        compiler_params=_params(("parallel", "arbitrary")),
    )(dy, proj, proj, proj, proj, lb, ng, o, states)


def _lower_bounds(logits, name):
    def fn(lg):
        sm = jax.nn.softmax(lg, axis=0)
        run, rows_ = None, []
        for j in range(lg.shape[0]):
            run = sm[j:j + 1, :] if run is None else run + sm[j:j + 1, :]
            rows_.append(run - sm[0:1, :])
        return jnp.concatenate(rows_, axis=0)
    return fn


def _lb_fwd(logits, name):
    fn = _lower_bounds(logits, name)

    def body(l_ref, o_ref):
        o_ref[...] = fn(l_ref[...])

    return pl.pallas_call(body, name=name, out_shape=jax.ShapeDtypeStruct(logits.shape, f32))(logits)


def _lb_bwd(logits, dlb, name):
    fn = _lower_bounds(logits, name)

    def body(l_ref, d_ref, o_ref):
        _, vjp = jax.vjp(fn, l_ref[...])
        o_ref[...] = vjp(d_ref[...])[0]

    return pl.pallas_call(body, name=name, out_shape=jax.ShapeDtypeStruct(logits.shape, f32))(logits, dlb)


def _ln_silu(d, g, b):
    mu = jnp.mean(d, axis=-1, keepdims=True)
    xc = d - mu
    y = xc * lax.rsqrt(jnp.mean(xc * xc, axis=-1, keepdims=True) + EPS)
    return jax.nn.silu(y * g + b)


def _ln_fwd(d, g, b, name):
    T, W = d.shape
    tt = _tile(T, 256, 8)

    def body(d_ref, g_ref, b_ref, o_ref):
        o_ref[...] = _ln_silu(d_ref[...], g_ref[...], b_ref[...]).astype(o_ref.dtype)

    return pl.pallas_call(
        body, name=name, grid=(T // tt,),
        in_specs=[pl.BlockSpec((tt, W), lambda t: (t, 0))] + [pl.BlockSpec((1, W), lambda t: (0, 0))] * 2,
        out_specs=pl.BlockSpec((tt, W), lambda t: (t, 0)), out_shape=jax.ShapeDtypeStruct((T, W), _MXU),
        compiler_params=_params(("parallel",)),
    )(d, g, b)


def _ln_bwd(dy, dcol0, d, g, b, name):
    T, W = d.shape
    tt = _tile(T, 256, 8)
    c0 = dcol0 // W
    assert dcol0 % W == 0

    def body(dy_ref, d_ref, g_ref, b_ref, dd_ref, dg_ref, db_ref):
        @pl.when(pl.program_id(0) == 0)
        def _():
            dg_ref[...] = jnp.zeros_like(dg_ref)
            db_ref[...] = jnp.zeros_like(db_ref)

        _, vjp = jax.vjp(_ln_silu, d_ref[...], g_ref[...], b_ref[...])
        dd, dg, db = vjp(dy_ref[...])
        dd_ref[...] = dd
        dg_ref[...] += dg
        db_ref[...] += db

    cspec = pl.BlockSpec((1, W), lambda t: (0, 0))
    return pl.pallas_call(
        body, name=name, grid=(T // tt,),
        in_specs=[pl.BlockSpec((tt, W), lambda t: (t, c0)), pl.BlockSpec((tt, W), lambda t: (t, 0)), cspec, cspec],
        out_specs=[pl.BlockSpec((tt, W), lambda t: (t, 0)), cspec, cspec],
        out_shape=[jax.ShapeDtypeStruct((T, W), f32), jax.ShapeDtypeStruct((1, W), f32), jax.ShapeDtypeStruct((1, W), f32)],
        compiler_params=_params(("arbitrary",)),
    )(dy, d, g, b)


def _adamw(w, m, v, parts, name):
    R, C = w.shape
    tr, tc = _tile(R, 512, 8), _tile(C, 512, LANE)
    npart = len(parts)

    def body(*refs):
        w_ref, m_ref, v_ref = refs[:3]
        g_ref, d_ref, mo_ref, vo_ref = refs[3 + npart:]
        g = refs[3][...].astype(f32)
        for p_ref in refs[4:3 + npart]:
            g = g + p_ref[...].astype(f32)
        mm = ADAM_B1 * m_ref[...] + (1.0 - ADAM_B1) * g
        vv = ADAM_B2 * v_ref[...] + (1.0 - ADAM_B2) * jnp.square(g)
        m_hat = mm / (1.0 - ADAM_B1 ** ADAM_STEP)
        v_hat = vv / (1.0 - ADAM_B2 ** ADAM_STEP)
        g_ref[...] = g
        d_ref[...] = -ADAM_LR * (m_hat / (jnp.sqrt(v_hat) + ADAM_EPS) + ADAM_WD * w_ref[...])
        mo_ref[...] = mm
        vo_ref[...] = vv

    spec = pl.BlockSpec((tr, tc), lambda i, j: (i, j))
    return pl.pallas_call(
        body, name=name, grid=(R // tr, C // tc), in_specs=[spec] * (3 + npart), out_specs=[spec] * 4,
        out_shape=[jax.ShapeDtypeStruct((R, C), f32)] * 4, compiler_params=_params(("parallel", "parallel")),
    )(w, m, v, *parts)


def _adamw_layer(w3, m3, v3, layer, parts, prev, name):
    L, R, C = w3.shape
    tr, tc = _tile(R, 128, 8), _tile(C, 2048, LANE)
    npart = len(parts)

    def body(*refs):
        w_ref, m_ref, v_ref = refs[:3]
        g_ref, d_ref, mo_ref, vo_ref = refs[3 + npart + 4:]
        g = refs[3][...].astype(f32)
        for p_ref in refs[4:3 + npart]:
            g = g + p_ref[...].astype(f32)
        mm = ADAM_B1 * m_ref[...] + (1.0 - ADAM_B1) * g
        vv = ADAM_B2 * v_ref[...] + (1.0 - ADAM_B2) * jnp.square(g)
        m_hat = mm / (1.0 - ADAM_B1 ** ADAM_STEP)
        v_hat = vv / (1.0 - ADAM_B2 ** ADAM_STEP)
        g_ref[...] = g
        d_ref[...] = -ADAM_LR * (m_hat / (jnp.sqrt(v_hat) + ADAM_EPS) + ADAM_WD * w_ref[...])
        mo_ref[...] = mm
        vo_ref[...] = vv

    spec3 = pl.BlockSpec((None, tr, tc), lambda i, j: (layer, i, j))
    spec2 = pl.BlockSpec((tr, tc), lambda i, j: (i, j))
    return pl.pallas_call(
        body, name=name, grid=(R // tr, C // tc), in_specs=[spec3] * 3 + [spec2] * npart + [_ANY] * 4,
        out_specs=[spec3] * 4, out_shape=[jax.ShapeDtypeStruct((L, R, C), f32)] * 4,
        input_output_aliases={3 + npart + k: k for k in range(4)},
        compiler_params=_params(("parallel", "parallel")),
    )(w3, m3, v3, *parts, *prev)


def _sum_own_recv(g, chip, col, recv, name):
    S, R, C = recv.shape
    tr, tc = _tile(R, 256, 8), _tile(C, 2048, LANE)
    nbr, nbc = R // tr, C // tc

    def body(chip_ref, o_ref, r_ref, out_ref):
        acc = o_ref[...].astype(f32)
        for s in range(S):
            acc = acc + r_ref[s].astype(f32)
        out_ref[...] = acc

    if col:
        own_map = lambda i, j, c: (i, c[0] * nbc + j)
    else:
        own_map = lambda i, j, c: (c[0] * nbr + i, j)
    grid_spec = pltpu.PrefetchScalarGridSpec(
        num_scalar_prefetch=1, grid=(nbr, nbc),
        in_specs=[pl.BlockSpec((tr, tc), own_map), pl.BlockSpec((S, tr, tc), lambda i, j, c: (0, i, j))],
        out_specs=pl.BlockSpec((tr, tc), lambda i, j, c: (i, j)))
    return pl.pallas_call(
        body, name=name, grid_spec=grid_spec, out_shape=jax.ShapeDtypeStruct((R, C), f32),
        compiler_params=_params(("parallel", "parallel")),
    )(chip, g, recv)


def _sum_slots(recv, name):
    S, R, C = recv.shape
    tr, tc = _tile(R, 512, 8), _tile(C, 512, LANE)

    def body(r_ref, o_ref):
        acc = r_ref[0].astype(f32)
        for s in range(1, S):
            acc = acc + r_ref[s].astype(f32)
        o_ref[...] = acc

    return pl.pallas_call(
        body, name=name, grid=(R // tr, C // tc), in_specs=[pl.BlockSpec((S, tr, tc), lambda i, j: (0, i, j))],
        out_specs=pl.BlockSpec((tr, tc), lambda i, j: (i, j)), out_shape=jax.ShapeDtypeStruct((R, C), f32),
        compiler_params=_params(("parallel", "parallel")),
    )(recv)


_CHIP_FLIPS = ((1, 0), (0, 1), (1, 1))
_ANY = pl.BlockSpec(memory_space=pl.ANY)


def _me():
    return lax.axis_index("x"), lax.axis_index("y"), lax.axis_index("c")


def _allgather_chips(local, col, name):
    L, R, C = local.shape
    out_shape = (L, R, 4 * C) if col else (L, 4 * R, C)

    def body(loc_ref, out_ref, send_sems, recv_sems, local_sem):
        x, y, c = _me()

        def block(px, py):
            s = 2 * px + py
            if col:
                return out_ref.at[:, :, pl.ds(s * C, C)]
            return out_ref.at[:, pl.ds(s * R, R), :]

        mine = pltpu.make_async_copy(loc_ref, block(x, y), local_sem)
        mine.start()
        sends = []
        for j, (fx, fy) in enumerate(_CHIP_FLIPS):
            cp = pltpu.make_async_remote_copy(src_ref=loc_ref, dst_ref=block(x, y), send_sem=send_sems.at[j],
                                              recv_sem=recv_sems.at[j], device_id=(x ^ fx, y ^ fy, c), device_id_type=MESH)
            cp.start()
            sends.append(cp)
        for j, (fx, fy) in enumerate(_CHIP_FLIPS):
            pltpu.make_async_remote_copy(src_ref=loc_ref, dst_ref=block(x ^ fx, y ^ fy), send_sem=send_sems.at[j],
                                         recv_sem=recv_sems.at[j], device_id=(x ^ fx, y ^ fy, c),
                                         device_id_type=MESH).wait_recv()
        for cp in sends:
            cp.wait_send()
        mine.wait()

    return pl.pallas_call(
        body, name=name, in_specs=[_ANY], out_specs=_ANY, out_shape=jax.ShapeDtypeStruct(out_shape, local.dtype),
        scratch_shapes=[pltpu.SemaphoreType.DMA((3,)), pltpu.SemaphoreType.DMA((3,)), pltpu.SemaphoreType.DMA],
    )(local)


_HBM = pl.BlockSpec(memory_space=pltpu.HBM)
_SEM = pl.BlockSpec(memory_space=pltpu.SEMAPHORE)
_EFFECT = pltpu.SideEffectType.DATAFLOW_SIDE_EFFECTING


def _chip_block(ref, s, col, n):
    return ref.at[:, pl.ds(s * n, n)] if col else ref.at[pl.ds(s * n, n), :]


_NSEM = {'gather': 7, 'relay': 6, 'scatter': 6, 'swap': 2, 'all': 15}


def _half_block(ref, s, col, n, half):
    rows = (ref.shape[0] if col else n) // 2
    if col:
        return ref.at[pl.ds(half * rows, rows), pl.ds(s * n, n)]
    return ref.at[pl.ds(s * n + half * rows, rows), :]


def _xchg_copies(kind, src_ref, land_ref, col, sems):
    x, y, c = _me()
    if kind == 'swap':
        return [pltpu.make_async_remote_copy(src_ref=src_ref, dst_ref=land_ref, send_sem=sems[0], recv_sem=sems[1],
                                             device_id=(x, y, 1 - c), device_id_type=MESH)]
    if kind == 'all':
        me_slot = land_ref.at[4 * x + 2 * y + c]
        cps = [pltpu.make_async_copy(src_ref, me_slot, sems[14])]
        for m in range(1, 8):
            cps.append(pltpu.make_async_remote_copy(
                src_ref=src_ref, dst_ref=me_slot, send_sem=sems[2 * m - 2], recv_sem=sems[2 * m - 1],
                device_id=(x ^ (m >> 2), y ^ ((m >> 1) & 1), c ^ (m & 1)), device_id_type=MESH))
        return cps
    cps = []
    if kind == 'gather':
        n = src_ref.shape[1] if col else src_ref.shape[0]
        cps.append(pltpu.make_async_copy(src_ref, _chip_block(land_ref, 2 * x + y, col, n), sems[6]))
    for j, (fx, fy) in enumerate(_CHIP_FLIPS):
        px, py = x ^ fx, y ^ fy
        peer = (px, py, c)
        if kind == 'gather':
            rows = src_ref.shape[0] // 2
            src, dst = src_ref.at[pl.ds(c * rows, rows), :], _half_block(land_ref, 2 * x + y, col, n, c)
        elif kind == 'relay':
            n = land_ref.shape[1] // 4 if col else land_ref.shape[0] // 4
            src = dst = _half_block(land_ref, 2 * px + py, col, n, c)
            peer = (x, y, 1 - c)
        else:
            n = land_ref.shape[2] if col else land_ref.shape[1]
            src, dst = _chip_block(src_ref, 2 * px + py, col, n), land_ref.at[j]
        cps.append(pltpu.make_async_remote_copy(src_ref=src, dst_ref=dst, send_sem=sems[2 * j], recv_sem=sems[2 * j + 1],
                                                device_id=peer, device_id_type=MESH))
    return cps


def _xchg_start(kind, srcs, lands, cols, after, name):
    n = len(lands)
    arrs = (list(srcs) if srcs is not None else []) + list(lands)
    na = len(arrs)
    per = _NSEM[kind]
    nsem = per * n

    def body(*refs):
        land_refs = refs[na - n:na]
        src_refs = refs[:n] if srcs is not None else land_refs
        sems = refs[na + 1:na + 1 + nsem]
        token = refs[-1]
        for i in range(n):
            for cp in _xchg_copies(kind, src_refs[i], land_refs[i], cols[i], sems[per * i:per * i + per]):
                cp.start()
        token[...] = jnp.zeros_like(token)

    hbm = lambda a: pltpu.HBM(a.shape, a.dtype)
    outs = pl.pallas_call(
        body, name=name,
        out_shape=tuple([pltpu.SemaphoreType.DMA(())] * nsem + [hbm(a) for a in arrs] + [jax.ShapeDtypeStruct((8, LANE), f32)]),
        in_specs=tuple([_HBM] * na + [_ANY]),
        out_specs=tuple([_SEM] * nsem + [_HBM] * na + [pl.BlockSpec(memory_space=pltpu.VMEM)]),
        input_output_aliases={i: nsem + i for i in range(na)},
        compiler_params=pltpu.CompilerParams(has_side_effects=_EFFECT),
    )(*[pltpu.with_memory_space_constraint(a, pltpu.HBM) for a in arrs], after)
    thru = outs[nsem:nsem + na]
    return outs[:nsem], (thru[:n] if srcs is not None else None), thru[na - n:], outs[-1]


def _xchg_wait(kind, started, cols, after, name):
    sems, srcs, lands, _ = started
    n = len(lands)
    arrs = (list(srcs) if srcs is not None else []) + list(lands)
    na = len(arrs)
    per = _NSEM[kind]
    nsem = per * n

    def body(*refs):
        land_refs = refs[na - n:na]
        src_refs = refs[:n] if srcs is not None else land_refs
        sem_refs = refs[na:na + nsem]
        for i in range(n):
            for cp in _xchg_copies(kind, src_refs[i], land_refs[i], cols[i], sem_refs[per * i:per * i + per]):
                if cp.is_remote:
                    cp.wait_send()
                    cp.wait_recv()
                else:
                    cp.wait()

    hbm = lambda a: pltpu.HBM(a.shape, a.dtype)
    outs = pl.pallas_call(
        body, name=name, out_shape=tuple(hbm(a) for a in arrs),
        in_specs=tuple([_HBM] * na + [_SEM] * nsem + [_ANY]), out_specs=tuple([_HBM] * na),
        input_output_aliases={i: i for i in range(na)},
        compiler_params=pltpu.CompilerParams(has_side_effects=_EFFECT),
    )(*arrs, *sems, after)
    return (outs[:n] if srcs is not None else None), outs[na - n:]


_PACK_ROWS = 512


def _pack(arrs):
    flat = jnp.concatenate([a.reshape(-1).astype(f32) for a in arrs])
    n = flat.shape[0]
    rows = -(-n // (_PACK_ROWS * LANE)) * _PACK_ROWS
    return jnp.pad(flat, (0, rows * LANE - n)).reshape(rows, LANE)


def _unpack(buf, shapes):
    flat, outs, off = buf.reshape(-1), [], 0
    for s in shapes:
        n = 1
        for d_ in s:
            n *= d_
        outs.append(flat[off:off + n].reshape(s))
        off += n
    return outs


def _step(p):
    x0 = p['x'][0]
    tgt = p['loss_target'][0]
    T, D = x0.shape
    W = D // 2
    depth = p['ln_mix_g'].shape[0]
    chip = 2 * lax.axis_index("x") + lax.axis_index("y")
    chip1 = jnp.reshape(chip, (1,)).astype(jnp.int32)

    def group_names(layer, grp):
        if grp == 'ffn':
            return [('ffn_w_gate', layer), ('ffn_w_up', layer), ('ffn_w_down', layer)]
        pre = 'ev' if layer % 2 == 0 else 'od'
        return [(pre + '_w_in', layer // 2), (pre + '_w_out', layer // 2)]

    def gather_start(layer, grp, after):
        srcs, lands, cols = [], [], []
        for n, l in group_names(layer, grp):
            loc = p[n][l].astype(_MXU)
            R, C = loc.shape
            col = _BIG_COL[n]
            land = lax.empty((R, 4 * C) if col else (4 * R, C), _MXU)
            srcs.append(loc)
            lands.append(land)
            cols.append(col)
        return _xchg_start('gather', srcs, lands, cols, after, name=f"ag_start_{grp}{layer}"), cols

    def gather_relay(started, layer, grp, after):
        st, cols = started
        _, lands = _xchg_wait('gather', st, cols, after, name=f"ag_wait_{grp}{layer}")
        return _xchg_start('relay', None, lands, cols, chip1, name=f"ag_relay_{grp}{layer}"), cols

    def gather_finish(relayed, layer, grp, after):
        st, cols = relayed
        _, lands = _xchg_wait('relay', st, cols, after, name=f"ag_done_{grp}{layer}")
        return dict(zip([n for n, _ in group_names(layer, grp)], lands))

    groups = [(layer, grp) for layer in range(depth) for grp in ('mix', 'ffn')]
    pending, relayed = {}, {}
    loose = []

    def take_weights(gi, after):
        wts = gather_finish(relayed.pop(groups[gi]), *groups[gi], after)
        if gi + 2 < len(groups):
            pending[groups[gi + 2]] = gather_start(*groups[gi + 2], next(iter(wts.values())))
            loose.append(pending[groups[gi + 2]][0][3])
        return wts

    def advance(gi, after):
        if gi + 1 < len(groups):
            relayed[groups[gi + 1]] = gather_relay(pending.pop(groups[gi + 1]), *groups[gi + 1], after)
            loose.append(relayed[groups[gi + 1]][0][3])

    def tied(a):
        a = a.reshape(1, -1)
        while loose:
            a = a + loose.pop()[0:1, 0:1]
        return a

    small_rows = []
    cl = W // 4
    for n in _SMALL_SHARDED:
        small_rows.append(p[n].reshape(-1, cl))
    srows = [a.shape[0] for a in small_rows]
    spack = jnp.concatenate(small_rows, axis=0)
    spad = -(-spack.shape[0] // 8) * 8
    spack = jnp.pad(spack, ((0, spad - spack.shape[0]), (0, 0)))
    sfull = _allgather_chips(spack[None], False, name="ag_small")[0].reshape(4, spad, cl)
    pending[groups[0]] = gather_start(*groups[0], sfull)
    pending[groups[1]] = gather_start(*groups[1], pending[groups[0]][0][3])
    relayed[groups[0]] = gather_relay(pending.pop(groups[0]), *groups[0], pending[groups[1]][0][3])
    loose.append(relayed[groups[0]][0][3])
    small = {}
    off = 0
    for n, r in zip(_SMALL_SHARDED, srows):
        blk = sfull[:, off:off + r, :]
        lead = p[n].shape[:-1]
        q = p[n].shape[-1] // cl
        blk = blk.reshape((4,) + lead + (q, cl))
        blk = jnp.moveaxis(blk, 0, len(lead))
        small[n] = blk.reshape(lead + (4 * q * cl,))
        off += r

    lbs = _lb_fwd(p['hgrn_lb_logits'], name="lb_fwd")

    def row(a, tok=None):
        a = a.reshape(1, -1)
        return a if tok is None else a + tok[0:1, 0:1]

    saved = []
    full = {}
    x = x0
    for layer in range(depth):
        j = layer // 2
        s = {'x_in': x}
        wts = take_weights(2 * layer, x)
        full[layer] = wts
        h = _rmsnorm_fwd(x, tied(p['ln_mix_g'][layer]), name="rms_fwd")
        s['h'] = h
        if layer % 2 == 0:
            proj = _matmul(h, wts['ev_w_in'], 'nn', f32, "mm_ev_in", bias=row(p['ev_b_in'][j]))
            xc = _conv_fwd(proj, 0, small['lru_conv_w'][j], row(p['lru_conv_b'][j]), W, name="lru_conv_fwd")
            ba, bx = p['lru_ba'][j][:, None, :], p['lru_bx'][j][:, None, :]
            a, u = _lru_gates_fwd(xc, p['lru_wa'][j], ba, p['lru_wx'][j], bx, row(p['lru_lambda'][j]), name="lru_gates_fwd")
            hl, y_a = _lru_scan_fwd(a, u, proj, W, name="lru_scan_fwd")
            y_b, o, states = _hgrn_fwd(proj, 2 * W, 3 * W, 4 * W, 5 * W, row(lbs[j]), row(p['hgrn_norm_g'][j]),
                                       name="hgrn_fwd")
            s.update(proj=proj, xc=xc, a=a, hl=hl, o=o, states=states)
            ycat = jnp.concatenate([y_a, y_b], axis=1)
            w_out = wts['ev_w_out']
        else:
            proj = _matmul(h, wts['od_w_in'], 'nn', f32, "mm_od_in", bias=row(small['od_b_in'][j]))
            pp = _pw(lambda a_, b_: a_ * b_, [(proj, W), (proj, 2 * W)], [f32], W, name="sc_mul")
            cp = _conv_fwd(pp, 0, small['sc_conv_w'][j], None, W, name="sc_conv_fwd")
            y_c = _pw(lambda a_, b_: a_ * b_, [(proj, 0), (cp, 0)], [_MXU], W, name="sc_out")
            glu = _pw(lambda a_, b_: a_ * jax.nn.sigmoid(b_), [(proj, 3 * W), (proj, 4 * W)], [f32], W, name="cf_glu")
            dcv = _conv_fwd(glu, 0, small['cf_conv_w'][j], row(small['cf_conv_b'][j]), W, name="cf_conv_fwd")
            y_d = _ln_fwd(dcv, row(small['cf_ln_g'][j]), row(small['cf_ln_b'][j]), name="cf_ln_fwd")
            s.update(proj=proj, pp=pp, cp=cp, glu=glu, dcv=dcv)
            ycat = jnp.concatenate([y_c, y_d], axis=1)
            w_out = wts['od_w_out']
        s['ycat'] = ycat
        advance(2 * layer, ycat)
        x = _matmul(ycat, w_out, 'nn', f32, "mm_mix_out", add=x)
        s['x_mid'] = x
        wts = take_weights(2 * layer + 1, x)
        full[layer].update(wts)
        h2 = _rmsnorm_fwd(x, tied(p['ln_ffn_g'][layer]), name="rms_fwd")
        gate, up, act = _ffn_in(h2, wts['ffn_w_gate'], wts['ffn_w_up'], name="ffn_in")
        advance(2 * layer + 1, act)
        x = _matmul(act, wts['ffn_w_down'], 'nn', f32, "mm_ffn_out", add=x, tn=1024, tk=1408)
        s.update(h2=h2, gate=gate, up=up, act=act)
        saved.append(s)

    loss_b, dx, dxb, dg_final = _final_loss(x, tied(p['ln_final_g']), tgt, name="final_loss")

    scat = {}
    tok = None

    def scatter_start(layer, grp, grads):
        srcs, lands, cols = [], [], []
        for (n, l), g in zip(group_names(layer, grp), grads, strict=True):
            R, C = p[n].shape[1:]
            srcs.append(g)
            lands.append(lax.empty((3, R, C), _WIRE))
            cols.append(_BIG_COL[n])
        st = _xchg_start('scatter', srcs, lands, cols, chip1, name=f"rs_start_{grp}{layer}")
        scat[(layer, grp)] = (st, cols)
        return st[3]

    gs = {n: [None] * p[n].shape[0] for n in _IN_NAMES[1:] if n not in _BIG and n != 'ln_final_g'}
    for layer in reversed(range(depth)):
        j = layer // 2
        s = saved[layer]
        F = s['gate'].shape[1]
        wts = full[layer]
        dgate, dup = _ffn_dact(dxb, wts['ffn_w_down'], s['gate'], s['up'], name="ffn_dact")
        g_down = _matmul(s['act'], dxb, 'tn', _WIRE, "mm_dw_down", tm=1408, tn=1024, tk=1024)
        g_gate = _matmul(s['h2'], dgate, 'tn', _WIRE, "mm_dw_in", tm=1024, tn=1408, tk=1024)
        g_up = _matmul(s['h2'], dup, 'tn', _WIRE, "mm_dw_in", tm=1024, tn=1408, tk=1024)
        tok = scatter_start(layer, 'ffn', [g_gate, g_up, g_down])
        dh2 = _matmul(dgate, wts['ffn_w_gate'], 'nt', f32, "mm_ffn_dh", tn=2048, tk=512, second=(dup, wts['ffn_w_up']))
        dx, dxb, gs['ln_ffn_g'][layer] = _rmsnorm_bwd(dh2, s['x_mid'], row(p['ln_ffn_g'][layer], tok), dx, name="rms_bwd")
        if layer % 2 == 0:
            w_out, w_in, n_out, n_in = wts['ev_w_out'], wts['ev_w_in'], 'ev_w_out', 'ev_w_in'
        else:
            w_out, w_in, n_out, n_in = wts['od_w_out'], wts['od_w_in'], 'od_w_out', 'od_w_in'
        dycat = _matmul(dxb, w_out, 'nt', f32, "mm_mix_dy")
        g_out = _matmul(s['ycat'], dxb, 'tn', _WIRE, "mm_dw_out", tm=1024, tn=1024, tk=1024)
        proj = s['proj']
        if layer % 2 == 0:
            ba, bx = p['lru_ba'][j][:, None, :], p['lru_bx'][j][:, None, :]
            lam_g, dgate_a = _lru_scan_bwd(dycat, 0, proj, W, s['hl'], s['a'], name="lru_scan_bwd")
            dxc, dwa, dba, dwx, dbx, dlam = _lru_gates_bwd(lam_g, s['hl'], s['xc'], p['lru_wa'][j], ba, p['lru_wx'][j], bx,
                                                           row(p['lru_lambda'][j]), name="lru_gates_bwd")
            dxa = _conv_bwd_dx(dxc, small['lru_conv_w'][j], name="lru_conv_dx", out_dtype=_MXU)
            dcw, dcb = _conv_bwd_dw(dxc, proj, 0, small['lru_conv_w'][j].shape[0], name="lru_conv_dw")
            dq, df, dv, dgp, dlb, dng = _hgrn_bwd(dycat, W, proj, 2 * W, 3 * W, 4 * W, 5 * W, row(lbs[j]),
                                                  row(p['hgrn_norm_g'][j]), s['o'], s['states'], name="hgrn_bwd")
            gs['lru_wa'][j], gs['lru_ba'][j], gs['lru_wx'][j], gs['lru_bx'][j] = dwa, dba[:, 0, :], dwx, dbx[:, 0, :]
            gs['lru_lambda'][j], gs['lru_conv_w'][j], gs['lru_conv_b'][j] = dlam[0], dcw, dcb[0]
            gs['hgrn_lb_logits'][j], gs['hgrn_norm_g'][j] = dlb[0], dng[0]
            dproj, dbias = _assemble([dxa, dgate_a, dq, df, dv, dgp], name="dproj_ev")
        else:
            def sc_bwd1(dy_, cp_, sb_):
                return dy_ * cp_, dy_ * sb_

            dsb, dcp = _pw(sc_bwd1, [(dycat, 0), (s['cp'], 0), (proj, 0)], [_MXU, f32], W, name="sc_bwd1")
            dpp = _conv_bwd_dx(dcp, small['sc_conv_w'][j], name="sc_conv_dx")
            dscw, _ = _conv_bwd_dw(dcp, s['pp'], 0, small['sc_conv_w'][j].shape[0], name="sc_conv_dw")

            def sc_bwd2(dp_, sc_, sv_):
                return dp_ * sv_, dp_ * sc_

            dsc, dsv = _pw(sc_bwd2, [(dpp, 0), (proj, W), (proj, 2 * W)], [_MXU, _MXU], W, name="sc_bwd2")
            dd, dlg, dlbeta = _ln_bwd(dycat, W, s['dcv'], row(small['cf_ln_g'][j]), row(small['cf_ln_b'][j]),
                                      name="cf_ln_bwd")
            dglu = _conv_bwd_dx(dd, small['cf_conv_w'][j], name="cf_conv_dx")
            dcfw, dcfb = _conv_bwd_dw(dd, s['glu'], 0, small['cf_conv_w'][j].shape[0], name="cf_conv_dw")

            def glu_bwd(dg_, cu_, cg_):
                _, vjp = jax.vjp(lambda a_, b_: a_ * jax.nn.sigmoid(b_), cu_, cg_)
                return vjp(dg_)

            dcu, dcg = _pw(glu_bwd, [(dglu, 0), (proj, 3 * W), (proj, 4 * W)], [_MXU, _MXU], W, name="cf_glu_bwd")
            gs['sc_conv_w'][j], gs['cf_conv_w'][j], gs['cf_conv_b'][j] = dscw, dcfw, dcfb[0]
            gs['cf_ln_g'][j], gs['cf_ln_b'][j] = dlg[0], dlbeta[0]
            dproj, dbias = _assemble([dsb, dsc, dsv, dcu, dcg], name="dproj_od")
        bname = 'ev_b_in' if layer % 2 == 0 else 'od_b_in'
        gs[bname][j] = dbias[0]
        g_in = _matmul(s['h'], dproj, 'tn', _WIRE, "mm_dw_" + n_in, tm=1024, tn=1536, tk=512)
        tok = scatter_start(layer, 'mix', [g_in, g_out])
        dh = _matmul(dproj, w_in, 'nt', f32, "mm_dh_" + n_in)
        dx, dxb, gs['ln_mix_g'][layer] = _rmsnorm_bwd(dh, s['x_in'], row(p['ln_mix_g'][layer], tok), dx, name="rms_bwd")

    g_small = {n: jnp.stack(v_) for n, v_ in gs.items()}
    g_small['ln_mix_g'] = g_small['ln_mix_g'][:, 0, :]
    g_small['ln_ffn_g'] = g_small['ln_ffn_g'][:, 0, :]
    g_small['ln_final_g'] = dg_final[0]
    g_small['hgrn_lb_logits'] = _lb_bwd(p['hgrn_lb_logits'], g_small['hgrn_lb_logits'], name="lb_bwd")
    small_names = _SMALL_REPL + _SMALL_SHARDED
    pack = _pack([g_small[n] for n in small_names])
    small_xchg = _xchg_start('all', [pack], [lax.empty((8,) + pack.shape, f32)], [None], chip1, name="small_grads_start")
    outs = {}

    acc = {n: tuple(lax.empty(p[n].shape, f32) for _ in range(4)) for n in _BIG}
    after_tail = small_xchg[3]

    def finish(swap, after):
        st, names, tag = swap
        parts, others = _xchg_wait('swap', st, [None] * len(names), after, name=f"swap_wait_{tag}")
        for (n, l), part, other in zip(names, parts, others, strict=True):
            acc[n] = tuple(_adamw_layer(p[n], p['m_' + n], p['v_' + n], l, [part, other], acc[n], name=f"adamw_{n}"))

    prev_swap = None
    for layer, grp in reversed(groups):
        st, cols = scat.pop((layer, grp))
        srcs, recvs = _xchg_wait('scatter', st, cols, after_tail, name=f"rs_wait_{grp}{layer}")
        names = group_names(layer, grp)
        parts = [_sum_own_recv(g, chip1, col, recv, name=f"sum_{n}")
                 for (n, l), g, recv, col in zip(names, srcs, recvs, cols, strict=True)]
        st = _xchg_start('swap', parts, [lax.empty(a.shape, f32) for a in parts], [None] * len(parts), recvs[0],
                         name=f"swap_start_{grp}{layer}")
        if prev_swap is not None:
            finish(prev_swap, st[3])
        prev_swap = (st, names, f"{grp}{layer}")
    finish(prev_swap, prev_swap[0][3])
    for n in _BIG:
        for kind, buf in zip(('grad', 'delta', 'new_m', 'new_v'), acc[n]):
            outs[kind + '_' + n] = buf

    _, (recv8,) = _xchg_wait('all', small_xchg, [None], acc[_BIG[0]][0], name="small_grads_wait")
    tot = _sum_slots(recv8, name="sum_small_grads")
    g_tot = dict(zip(small_names, _unpack(tot, [g_small[n].shape for n in small_names])))
    for n in _SMALL_SHARDED:
        lead = p[n].shape[:-1]
        q = p[n].shape[-1] // cl
        blk = g_tot[n].reshape(lead + (4, q * cl))
        g_tot[n] = lax.dynamic_index_in_dim(blk, chip, axis=len(lead), keepdims=False)
    shapes = [p[n].shape for n in small_names]
    res = _adamw(_pack([p[n] for n in small_names]), _pack([p['m_' + n] for n in small_names]),
                 _pack([p['v_' + n] for n in small_names]), [_pack([g_tot[n] for n in small_names])], name="adamw_small")
    for kind, buf in zip(('grad', 'delta', 'new_m', 'new_v'), res):
        for n, a in zip(small_names, _unpack(buf, shapes)):
            outs[kind + '_' + n] = a

    loss = lax.psum(loss_b[0, 0], ("x", "y", "c"))
    weights = _IN_NAMES[1:]
    return (loss, dx[None], *[outs['grad_' + n] for n in weights], *[outs['delta_' + n] for n in weights],
            *[outs['new_m_' + n] for n in weights], *[outs['new_v_' + n] for n in weights])


def kernel(x, ln_mix_g, ln_ffn_g, ln_final_g, ev_w_in, ev_b_in, lru_conv_w, lru_conv_b, lru_wa, lru_ba, lru_wx, lru_bx, lru_lambda, hgrn_lb_logits, hgrn_norm_g, ev_w_out, od_w_in, od_b_in, sc_conv_w, cf_conv_w, cf_conv_b, cf_ln_g, cf_ln_b, od_w_out, ffn_w_gate, ffn_w_up, ffn_w_down, loss_target, m_ln_mix_g, m_ln_ffn_g, m_ln_final_g, m_ev_w_in, m_ev_b_in, m_lru_conv_w, m_lru_conv_b, m_lru_wa, m_lru_ba, m_lru_wx, m_lru_bx, m_lru_lambda, m_hgrn_lb_logits, m_hgrn_norm_g, m_ev_w_out, m_od_w_in, m_od_b_in, m_sc_conv_w, m_cf_conv_w, m_cf_conv_b, m_cf_ln_g, m_cf_ln_b, m_od_w_out, m_ffn_w_gate, m_ffn_w_up, m_ffn_w_down, v_ln_mix_g, v_ln_ffn_g, v_ln_final_g, v_ev_w_in, v_ev_b_in, v_lru_conv_w, v_lru_conv_b, v_lru_wa, v_lru_ba, v_lru_wx, v_lru_bx, v_lru_lambda, v_hgrn_lb_logits, v_hgrn_norm_g, v_ev_w_out, v_od_w_in, v_od_b_in, v_sc_conv_w, v_cf_conv_w, v_cf_conv_b, v_cf_ln_g, v_cf_ln_b, v_od_w_out, v_ffn_w_gate, v_ffn_w_up, v_ffn_w_down):
    vals = locals()
    p = {n: vals[n] for n in _IN_NAMES + ['loss_target']}
    for n in _IN_NAMES[1:]:
        p['m_' + n] = vals['m_' + n]
        p['v_' + n] = vals['v_' + n]
    return _step(p)
```

```python
import functools

import jax
import jax.numpy as jnp
from jax import lax
from jax.experimental import pallas as pl
from jax.experimental.pallas import tpu as pltpu

f32 = jnp.float32
_MXU = jnp.bfloat16
_WIRE = jnp.bfloat16

N_HEADS = 8
LRU_C = 8.0
EPS = 1e-6
F_FLOOR = 1e-30
SUB = 16
HALF = SUB // 2
HGRN_HPB = 2
ADAM_LR, ADAM_B1, ADAM_B2, ADAM_EPS, ADAM_WD, ADAM_STEP = 0.001, 0.9, 0.999, 1e-08, 0.01, 10
V7X_VMEM_LIMIT = 48 * 1024 * 1024
LANE = 128
MESH = pl.DeviceIdType.MESH

_IN_NAMES = ['x', 'ln_mix_g', 'ln_ffn_g', 'ln_final_g', 'ev_w_in', 'ev_b_in', 'lru_conv_w', 'lru_conv_b', 'lru_wa', 'lru_ba',
             'lru_wx', 'lru_bx', 'lru_lambda', 'hgrn_lb_logits', 'hgrn_norm_g', 'ev_w_out', 'od_w_in', 'od_b_in', 'sc_conv_w',
             'cf_conv_w', 'cf_conv_b', 'cf_ln_g', 'cf_ln_b', 'od_w_out', 'ffn_w_gate', 'ffn_w_up', 'ffn_w_down']
_BIG = ['ev_w_in', 'ev_w_out', 'od_w_in', 'od_w_out', 'ffn_w_gate', 'ffn_w_up', 'ffn_w_down']
_BIG_COL = {'ev_w_in': True, 'ev_w_out': False, 'od_w_in': True, 'od_w_out': False, 'ffn_w_gate': True, 'ffn_w_up': True,
            'ffn_w_down': False}
_SMALL_SHARDED = ['lru_conv_w', 'od_b_in', 'sc_conv_w', 'cf_conv_w', 'cf_conv_b', 'cf_ln_g', 'cf_ln_b']
_SMALL_REPL = ['ln_mix_g', 'ln_ffn_g', 'ln_final_g', 'ev_b_in', 'lru_conv_b', 'lru_wa', 'lru_ba', 'lru_wx', 'lru_bx',
               'lru_lambda', 'hgrn_lb_logits', 'hgrn_norm_g']


def _tile(n, pref, align):
    if n <= pref:
        return n
    t = (pref // align) * align
    while t >= align:
        if n % t == 0:
            return t
        t -= align
    return n


def _params(sem):
    return pltpu.CompilerParams(dimension_semantics=sem, vmem_limit_bytes=V7X_VMEM_LIMIT)


def _rows(shape):
    return lax.broadcasted_iota(jnp.int32, shape, 0)


def _mxdot(a, b, dims=(((1,), (0,)), ((), ()))):
    return lax.dot_general(a.astype(_MXU), b.astype(_MXU), dims, preferred_element_type=f32)


_NN = (((1,), (0,)), ((), ()))
_NT = (((1,), (1,)), ((), ()))
_TN = (((0,), (0,)), ((), ()))


def _matmul(a, b, mode, out_dtype, name, bias=None, add=None, tm=1024, tn=512, tk=2048, second=None):
    if mode == 'nn':
        (M, K), (K2, N) = a.shape, b.shape
    elif mode == 'nt':
        (M, K), (N, K2) = a.shape, b.shape
    else:
        (K, M), (K2, N) = a.shape, b.shape
    assert K == K2, (name, a.shape, b.shape)
    tm, tn, tk = _tile(M, tm, LANE), _tile(N, tn, LANE), _tile(K, tk, LANE)
    nk = K // tk
    dims = {'nn': _NN, 'nt': _NT, 'tn': _TN}[mode]
    has_bias, has_add, has_second = bias is not None, add is not None, second is not None

    def body(*refs):
        a_ref, b_ref = refs[0], refs[1]
        pos = 2
        if has_second:
            a2_ref, b2_ref = refs[2], refs[3]
            pos = 4
        bias_ref = add_ref = None
        if has_bias:
            bias_ref = refs[pos]
            pos += 1
        if has_add:
            add_ref = refs[pos]
            pos += 1
        o_ref, acc_ref = refs[pos], refs[pos + 1]
        k = pl.program_id(2)

        @pl.when(k == 0)
        def _():
            acc_ref[...] = jnp.zeros_like(acc_ref)

        prod = _mxdot(a_ref[...], b_ref[...], dims)
        if has_second:
            prod = prod + _mxdot(a2_ref[...], b2_ref[...], dims)
        acc_ref[...] += prod

        @pl.when(k == nk - 1)
        def _():
            r = acc_ref[...]
            if has_bias:
                r = r + bias_ref[...]
            if has_add:
                r = r + add_ref[...]
            o_ref[...] = r.astype(o_ref.dtype)

    if mode == 'tn':
        a_spec = pl.BlockSpec((tk, tm), lambda i, j, k: (k, i))
    else:
        a_spec = pl.BlockSpec((tm, tk), lambda i, j, k: (i, k))
    if mode == 'nt':
        b_spec = pl.BlockSpec((tn, tk), lambda i, j, k: (j, k))
    else:
        b_spec = pl.BlockSpec((tk, tn), lambda i, j, k: (k, j))
    in_specs, args = [a_spec, b_spec], [a, b]
    if has_second:
        assert second[0].shape == a.shape and second[1].shape == b.shape
        in_specs += [a_spec, b_spec]
        args += list(second)
    if has_bias:
        in_specs.append(pl.BlockSpec((1, tn), lambda i, j, k: (0, j)))
        args.append(bias)
    if has_add:
        in_specs.append(pl.BlockSpec((tm, tn), lambda i, j, k: (i, j)))
        args.append(add)
    return pl.pallas_call(
        body, name=name, grid=(M // tm, N // tn, nk), in_specs=in_specs,
        out_specs=pl.BlockSpec((tm, tn), lambda i, j, k: (i, j)),
        out_shape=jax.ShapeDtypeStruct((M, N), out_dtype),
        scratch_shapes=[pltpu.VMEM((tm, tn), f32)],
        compiler_params=_params(("parallel", "parallel", "arbitrary")),
    )(*args)


def _ffn_in(h2, wg, wu, name):
    (M, K), N = h2.shape, wg.shape[1]
    tm, tn = _tile(M, 1024, LANE), _tile(N, 512, LANE)

    def body(a_ref, g_ref, u_ref, go_ref, uo_ref, act_ref):
        a = a_ref[...]
        g, u = _mxdot(a, g_ref[...]), _mxdot(a, u_ref[...])
        go_ref[...] = g.astype(go_ref.dtype)
        uo_ref[...] = u.astype(uo_ref.dtype)
        act_ref[...] = (jax.nn.silu(g) * u).astype(act_ref.dtype)

    wspec = pl.BlockSpec((K, tn), lambda i, j: (0, j))
    ospec = pl.BlockSpec((tm, tn), lambda i, j: (i, j))
    return pl.pallas_call(
        body, name=name, grid=(M // tm, N // tn), in_specs=[pl.BlockSpec((tm, K), lambda i, j: (i, 0)), wspec, wspec],
        out_specs=[ospec] * 3, out_shape=[jax.ShapeDtypeStruct((M, N), _MXU)] * 3,
        compiler_params=_params(("parallel", "parallel")),
    )(h2, wg, wu)


def _ffn_dact(dx, wd, gate, up, name):
    (M, K), N = dx.shape, wd.shape[0]
    tm, tn = _tile(M, 512, LANE), _tile(N, 1408, LANE)

    def body(a_ref, w_ref, g_ref, u_ref, dg_ref, du_ref):
        da = _mxdot(a_ref[...], w_ref[...], _NT)
        _, vjp = jax.vjp(lambda gg, uu: jax.nn.silu(gg) * uu, g_ref[...].astype(f32), u_ref[...].astype(f32))
        dg, du = vjp(da)
        dg_ref[...] = dg.astype(dg_ref.dtype)
        du_ref[...] = du.astype(du_ref.dtype)

    ospec = pl.BlockSpec((tm, tn), lambda i, j: (i, j))
    return pl.pallas_call(
        body, name=name, grid=(M // tm, N // tn),
        in_specs=[pl.BlockSpec((tm, K), lambda i, j: (i, 0)), pl.BlockSpec((tn, K), lambda i, j: (j, 0)), ospec, ospec],
        out_specs=[ospec] * 2, out_shape=[jax.ShapeDtypeStruct((M, N), _MXU)] * 2,
        compiler_params=_params(("parallel", "parallel")),
    )(dx, wd, gate, up)


def _pw(fn, slabs, out_dtypes, width, name, consts=()):
    T = slabs[0][0].shape[0]
    tt, cb = _tile(T, 512, 8), _tile(width, 512, LANE)
    nin, ncst = len(slabs), len(consts)

    def body(*refs):
        res = fn(*[r[...] for r in refs[:nin + ncst]])
        if not isinstance(res, (tuple, list)):
            res = (res,)
        for r, o in zip(res, refs[nin + ncst:], strict=True):
            o[...] = r.astype(o.dtype)

    in_specs, args = [], []
    for arr, col0 in slabs:
        assert col0 % cb == 0
        in_specs.append(pl.BlockSpec((tt, cb), functools.partial(lambda t, c, c0: (t, c0 + c), c0=col0 // cb)))
        args.append(arr)
    for cst in consts:
        in_specs.append(pl.BlockSpec((1, cb), lambda t, c: (0, c)))
        args.append(cst)
    outs = pl.pallas_call(
        body, name=name, grid=(T // tt, width // cb), in_specs=in_specs,
        out_specs=[pl.BlockSpec((tt, cb), lambda t, c: (t, c)) for _ in out_dtypes],
        out_shape=[jax.ShapeDtypeStruct((T, width), d) for d in out_dtypes],
        compiler_params=_params(("parallel", "parallel")),
    )(*args)
    return outs[0] if len(out_dtypes) == 1 else outs


def _rmsnorm_fwd(x, g, name):
    T, D = x.shape
    tt = _tile(T, 256, 8)

    def body(x_ref, g_ref, o_ref):
        xv = x_ref[...]
        r = lax.rsqrt(jnp.mean(xv * xv, axis=-1, keepdims=True) + EPS)
        o_ref[...] = (xv * r * g_ref[...]).astype(o_ref.dtype)

    return pl.pallas_call(
        body, name=name, grid=(T // tt,),
        in_specs=[pl.BlockSpec((tt, D), lambda t: (t, 0)), pl.BlockSpec((1, D), lambda t: (0, 0))],
        out_specs=pl.BlockSpec((tt, D), lambda t: (t, 0)),
        out_shape=jax.ShapeDtypeStruct((T, D), _MXU), compiler_params=_params(("parallel",)),
    )(x, g)


def _rmsnorm_bwd(dh, x, g, dx_in, name):
    T, D = x.shape
    tt = _tile(T, 256, 8)

    def body(dh_ref, x_ref, g_ref, dxi_ref, dx_ref, dxb_ref, dg_ref):
        t = pl.program_id(0)
        xv, d = x_ref[...], dh_ref[...]
        r = lax.rsqrt(jnp.mean(xv * xv, axis=-1, keepdims=True) + EPS)
        n = xv * r
        dn = d * g_ref[...]
        dx = dxi_ref[...] + r * (dn - n * jnp.mean(dn * n, axis=-1, keepdims=True))
        dx_ref[...] = dx
        dxb_ref[...] = dx.astype(dxb_ref.dtype)

        @pl.when(t == 0)
        def _():
            dg_ref[...] = jnp.zeros_like(dg_ref)

        dg_ref[...] += jnp.sum(d * n, axis=0, keepdims=True)

    return pl.pallas_call(
        body, name=name, grid=(T // tt,),
        in_specs=[pl.BlockSpec((tt, D), lambda t: (t, 0)), pl.BlockSpec((tt, D), lambda t: (t, 0)),
                  pl.BlockSpec((1, D), lambda t: (0, 0)), pl.BlockSpec((tt, D), lambda t: (t, 0))],
        out_specs=[pl.BlockSpec((tt, D), lambda t: (t, 0)), pl.BlockSpec((tt, D), lambda t: (t, 0)),
                   pl.BlockSpec((1, D), lambda t: (0, 0))],
        out_shape=[jax.ShapeDtypeStruct((T, D), f32), jax.ShapeDtypeStruct((T, D), _MXU), jax.ShapeDtypeStruct((1, D), f32)],
        compiler_params=_params(("arbitrary",)),
    )(dh, x, g, dx_in)


def _final_loss(x, g, tgt, name):
    T, D = x.shape
    tt = _tile(T, 256, 8)

    def body(x_ref, g_ref, t_ref, l_ref, dx_ref, dxb_ref, dg_ref):
        t = pl.program_id(0)
        xv = x_ref[...]
        r = lax.rsqrt(jnp.mean(xv * xv, axis=-1, keepdims=True) + EPS)
        n = xv * r
        e = n * g_ref[...] - t_ref[...]
        part = 0.5 * jnp.sum(jnp.mean(e * e, axis=-1, keepdims=True), axis=0, keepdims=True)
        dy = e * (1.0 / D)
        dn = dy * g_ref[...]
        dx = r * (dn - n * jnp.mean(dn * n, axis=-1, keepdims=True))
        dx_ref[...] = dx
        dxb_ref[...] = dx.astype(dxb_ref.dtype)

        @pl.when(t == 0)
        def _():
            dg_ref[...] = jnp.zeros_like(dg_ref)
            l_ref[...] = jnp.zeros_like(l_ref)

        dg_ref[...] += jnp.sum(dy * n, axis=0, keepdims=True)
        l_ref[...] += jnp.broadcast_to(part, l_ref.shape)

    return pl.pallas_call(
        body, name=name, grid=(T // tt,),
        in_specs=[pl.BlockSpec((tt, D), lambda t: (t, 0)), pl.BlockSpec((1, D), lambda t: (0, 0)),
                  pl.BlockSpec((tt, D), lambda t: (t, 0))],
        out_specs=[pl.BlockSpec((1, LANE), lambda t: (0, 0)), pl.BlockSpec((tt, D), lambda t: (t, 0)),
                   pl.BlockSpec((tt, D), lambda t: (t, 0)), pl.BlockSpec((1, D), lambda t: (0, 0))],
        out_shape=[jax.ShapeDtypeStruct((1, LANE), f32), jax.ShapeDtypeStruct((T, D), f32),
                   jax.ShapeDtypeStruct((T, D), _MXU), jax.ShapeDtypeStruct((1, D), f32)],
        compiler_params=_params(("arbitrary",)),
    )(x, g, tgt)


def _assemble(pieces, name):
    T, W = pieces[0].shape
    n = len(pieces)
    tt = _tile(T, 512, 16)

    def body(*refs):
        o_ref, cs_ref = refs[n], refs[n + 1]
        t, q = pl.program_id(0), pl.program_id(1)

        @pl.when(jnp.logical_and(t == 0, q == 0))
        def _():
            cs_ref[...] = jnp.zeros_like(cs_ref)

        for i in range(n):
            @pl.when(q == i)
            def _(i=i):
                v = refs[i][...]
                o_ref[...] = v
                cs_ref[:, i * W:(i + 1) * W] += jnp.sum(v.astype(f32), axis=0, keepdims=True)

    return pl.pallas_call(
        body, name=name, grid=(T // tt, n), in_specs=[pl.BlockSpec((tt, W), lambda t, q: (t, 0))] * n,
        out_specs=[pl.BlockSpec((tt, W), lambda t, q: (t, q)), pl.BlockSpec((1, n * W), lambda t, q: (0, 0))],
        out_shape=[jax.ShapeDtypeStruct((T, n * W), pieces[0].dtype), jax.ShapeDtypeStruct((1, n * W), f32)],
        compiler_params=_params(("arbitrary", "arbitrary")),
    )(*pieces)


def _shift_down(cur, prev, j):
    if j == 0:
        return cur
    n = cur.shape[0]
    return jnp.where(_rows(cur.shape) < j, pltpu.roll(prev, j, 0), pltpu.roll(cur, j, 0))


def _shift_up(cur, nxt, j):
    if j == 0:
        return cur
    n = cur.shape[0]
    return jnp.where(_rows(cur.shape) >= n - j, pltpu.roll(nxt, n - j, 0), pltpu.roll(cur, n - j, 0))


def _conv_tiles(T, C, K):
    big = K <= 8
    tt, cb = _tile(T, 512 if big else 256, 8), _tile(C, 512 if big else 256, LANE)
    assert tt >= K, (tt, K)
    return tt, cb


def _conv_fwd(x, col0, w, b, C, name):
    T, K = x.shape[0], w.shape[0]
    tt, cb = _conv_tiles(T, C, K)
    c0 = col0 // cb
    assert col0 % cb == 0
    has_b = b is not None

    def body(*refs):
        cur_ref, prev_ref, w_ref = refs[:3]
        o_ref = refs[-1]
        t = pl.program_id(1)
        cur = cur_ref[...]
        prev = jnp.where(t > 0, prev_ref[...], 0.0)
        wv = w_ref[...]
        acc = jnp.zeros_like(cur)
        for k in range(K):
            acc = acc + wv[k:k + 1, :] * _shift_down(cur, prev, K - 1 - k)
        if has_b:
            acc = acc + refs[3][...]
        o_ref[...] = acc

    in_specs = [pl.BlockSpec((tt, cb), lambda c, t: (t, c0 + c)),
                pl.BlockSpec((tt, cb), lambda c, t: (jnp.maximum(t - 1, 0), c0 + c)),
                pl.BlockSpec((K, cb), lambda c, t: (0, c))]
    args = [x, x, w]
    if has_b:
        in_specs.append(pl.BlockSpec((1, cb), lambda c, t: (0, c)))
        args.append(b)
    return pl.pallas_call(
        body, name=name, grid=(C // cb, T // tt), in_specs=in_specs,
        out_specs=pl.BlockSpec((tt, cb), lambda c, t: (t, c)), out_shape=jax.ShapeDtypeStruct((T, C), f32),
        compiler_params=_params(("parallel", "parallel")),
    )(*args)


def _conv_bwd_dx(dy, w, name, out_dtype=f32):
    T, C = dy.shape
    K = w.shape[0]
    tt, cb = _conv_tiles(T, C, K)
    nt = T // tt

    def body(cur_ref, nxt_ref, w_ref, o_ref):
        t = pl.program_id(1)
        cur = cur_ref[...]
        nxt = jnp.where(t < nt - 1, nxt_ref[...], 0.0)
        wv = w_ref[...]
        acc = jnp.zeros_like(cur)
        for k in range(K):
            acc = acc + wv[k:k + 1, :] * _shift_up(cur, nxt, K - 1 - k)
        o_ref[...] = acc.astype(o_ref.dtype)

    return pl.pallas_call(
        body, name=name, grid=(C // cb, nt),
        in_specs=[pl.BlockSpec((tt, cb), lambda c, t: (t, c)),
                  pl.BlockSpec((tt, cb), lambda c, t: (jnp.minimum(t + 1, nt - 1), c)),
                  pl.BlockSpec((K, cb), lambda c, t: (0, c))],
        out_specs=pl.BlockSpec((tt, cb), lambda c, t: (t, c)), out_shape=jax.ShapeDtypeStruct((T, C), out_dtype),
        compiler_params=_params(("parallel", "parallel")),
    )(dy, dy, w)


def _conv_bwd_dw(dy, x, col0, K, name):
    T, C = dy.shape
    tt, cb = _conv_tiles(T, C, K)
    c0 = col0 // cb
    assert col0 % cb == 0

    def body(dy_ref, cur_ref, prev_ref, dw_ref, db_ref):
        t = pl.program_id(1)

        @pl.when(t == 0)
        def _():
            dw_ref[...] = jnp.zeros_like(dw_ref)
            db_ref[...] = jnp.zeros_like(db_ref)

        d = dy_ref[...]
        cur = cur_ref[...]
        prev = jnp.where(t > 0, prev_ref[...], 0.0)
        for k in range(K):
            row = jnp.sum(d * _shift_down(cur, prev, K - 1 - k), axis=0, keepdims=True)
            dw_ref[pl.ds(k, 1), :] = dw_ref[pl.ds(k, 1), :] + row
        db_ref[...] += jnp.sum(d, axis=0, keepdims=True)

    return pl.pallas_call(
        body, name=name, grid=(C // cb, T // tt),
        in_specs=[pl.BlockSpec((tt, cb), lambda c, t: (t, c)),
                  pl.BlockSpec((tt, cb), lambda c, t: (t, c0 + c)),
                  pl.BlockSpec((tt, cb), lambda c, t: (jnp.maximum(t - 1, 0), c0 + c))],
        out_specs=[pl.BlockSpec((K, cb), lambda c, t: (0, c)), pl.BlockSpec((1, cb), lambda c, t: (0, c))],
        out_shape=[jax.ShapeDtypeStruct((K, C), f32), jax.ShapeDtypeStruct((1, C), f32)],
        compiler_params=_params(("parallel", "arbitrary")),
    )(dy, x, x)


def _expm1(z):
    poly = z * (1.0 + z * (0.5 + z * (1.0 / 6.0 + z * (1.0 / 24.0 + z * (1.0 / 120.0)))))
    return jnp.where(jnp.abs(z) < 0.1, poly, jnp.exp(z) - 1.0)


def _lru_pt(xc, rp, ip, lam, first):
    r = jax.nn.sigmoid(rp)
    i = jax.nn.sigmoid(ip)
    log_a = -LRU_C * r * jax.nn.softplus(-lam)
    a = jnp.exp(log_a)
    mult = jnp.sqrt(jnp.maximum(-_expm1(2.0 * log_a), 0.0))
    mult = jnp.where(first, 1.0, mult)
    return a, mult * i * xc


def _first_mask(shape, t):
    return jnp.logical_and(_rows(shape) == 0, t == 0)


def _lru_gates_fwd(xc, wa, ba, wx, bx, lam, name):
    T, W = xc.shape
    hd = W // N_HEADS
    tt = _tile(T, 512, 8)

    def body(xc_ref, wa_ref, ba_ref, wx_ref, bx_ref, lam_ref, a_ref, u_ref):
        t = pl.program_id(1)
        x = xc_ref[...]
        rp = _mxdot(x, wa_ref[...]) + ba_ref[...]
        ip = _mxdot(x, wx_ref[...]) + bx_ref[...]
        a, u = _lru_pt(x, rp, ip, lam_ref[...], _first_mask(x.shape, t))
        a_ref[...] = a
        u_ref[...] = u

    wspec = pl.BlockSpec((None, hd, hd), lambda h, t: (h, 0, 0))
    bspec = pl.BlockSpec((None, 1, hd), lambda h, t: (h, 0, 0))
    tspec = pl.BlockSpec((tt, hd), lambda h, t: (t, h))
    return pl.pallas_call(
        body, name=name, grid=(N_HEADS, T // tt),
        in_specs=[tspec, wspec, bspec, wspec, bspec, pl.BlockSpec((1, hd), lambda h, t: (0, h))],
        out_specs=[tspec, tspec], out_shape=[jax.ShapeDtypeStruct((T, W), f32)] * 2,
        compiler_params=_params(("parallel", "parallel")),
    )(xc, wa, ba, wx, bx, lam)


def _lru_scan_fwd(a, u, gate, gcol0, name):
    T, W = a.shape
    tt, cb = _tile(T, 256, 8), _tile(W, 512, LANE)
    g0 = gcol0 // cb
    assert gcol0 % cb == 0

    def body(a_ref, u_ref, g_ref, h_ref, y_ref, carry_ref):
        t = pl.program_id(1)

        @pl.when(t == 0)
        def _():
            carry_ref[...] = jnp.zeros_like(carry_ref)

        def step(i, h):
            base = pl.multiple_of(i * 8, 8)
            a8, u8 = a_ref[pl.ds(base, 8), :], u_ref[pl.ds(base, 8), :]
            rows = []
            for j in range(8):
                h = a8[j:j + 1, :] * h + u8[j:j + 1, :]
                rows.append(h)
            h_ref[pl.ds(base, 8), :] = jnp.concatenate(rows, axis=0)
            return h

        h_last = lax.fori_loop(0, tt // 8, step, carry_ref[0:1, :])
        carry_ref[...] = jnp.broadcast_to(h_last, carry_ref.shape)
        y_ref[...] = (h_ref[...] * jax.nn.gelu(g_ref[...])).astype(y_ref.dtype)

    tspec = pl.BlockSpec((tt, cb), lambda c, t: (t, c))
    return pl.pallas_call(
        body, name=name, grid=(W // cb, T // tt),
        in_specs=[tspec, tspec, pl.BlockSpec((tt, cb), lambda c, t: (t, g0 + c))],
        out_specs=[tspec, tspec],
        out_shape=[jax.ShapeDtypeStruct((T, W), f32), jax.ShapeDtypeStruct((T, W), _MXU)],
        scratch_shapes=[pltpu.VMEM((8, cb), f32)],
        compiler_params=_params(("parallel", "arbitrary")),
    )(a, u, gate)


def _lru_scan_bwd(dy, dcol0, gate, gcol0, h, a, name):
    T, W = a.shape
    tt, cb = _tile(T, 256, 8), _tile(W, 512, LANE)
    nt = T // tt
    d0, g0 = dcol0 // cb, gcol0 // cb
    assert dcol0 % cb == 0 and gcol0 % cb == 0

    def body(dy_ref, g_ref, h_ref, a_ref, lam_ref, dg_ref, carry_ref, dh_ref):
        t = pl.program_id(1)

        @pl.when(t == 0)
        def _():
            carry_ref[...] = jnp.zeros_like(carry_ref)

        _, vjp = jax.vjp(lambda hh, gg: hh * jax.nn.gelu(gg), h_ref[...], g_ref[...])
        dh, dg = vjp(dy_ref[...])
        dg_ref[...] = dg.astype(dg_ref.dtype)
        dh_ref[...] = dh

        def step(i, c):
            base = pl.multiple_of((tt // 8 - 1 - i) * 8, 8)
            a8, d8 = a_ref[pl.ds(base, 8), :], dh_ref[pl.ds(base, 8), :]
            rows = [None] * 8
            for j in range(7, -1, -1):
                lam = d8[j:j + 1, :] + c
                c = a8[j:j + 1, :] * lam
                rows[j] = lam
            lam_ref[pl.ds(base, 8), :] = jnp.concatenate(rows, axis=0)
            return c

        c_last = lax.fori_loop(0, tt // 8, step, carry_ref[0:1, :])
        carry_ref[...] = jnp.broadcast_to(c_last, carry_ref.shape)

    rev = lambda c, t: (nt - 1 - t, c)
    tspec = pl.BlockSpec((tt, cb), rev)
    return pl.pallas_call(
        body, name=name, grid=(W // cb, nt),
        in_specs=[pl.BlockSpec((tt, cb), lambda c, t: (nt - 1 - t, d0 + c)),
                  pl.BlockSpec((tt, cb), lambda c, t: (nt - 1 - t, g0 + c)), tspec, tspec],
        out_specs=[tspec, tspec], out_shape=[jax.ShapeDtypeStruct((T, W), f32), jax.ShapeDtypeStruct((T, W), _MXU)],
        scratch_shapes=[pltpu.VMEM((8, cb), f32), pltpu.VMEM((tt, cb), f32)],
        compiler_params=_params(("parallel", "arbitrary")),
    )(dy, gate, h, a)


def _lru_gates_bwd(lam_g, h, xc, wa, ba, wx, bx, lam, name):
    T, W = xc.shape
    hd = W // N_HEADS
    tt = _tile(T, 512, 8)

    def body(lg_ref, h_ref, hp_ref, xc_ref, wa_ref, ba_ref, wx_ref, bx_ref, lam_ref,
             dxc_ref, dwa_ref, dba_ref, dwx_ref, dbx_ref, dlam_ref):
        t = pl.program_id(1)

        @pl.when(t == 0)
        def _():
            for r in (dwa_ref, dba_ref, dwx_ref, dbx_ref, dlam_ref):
                r[...] = jnp.zeros_like(r)

        x = xc_ref[...]
        lg = lg_ref[...]
        h_prev = _shift_down(h_ref[...], jnp.where(t > 0, hp_ref[...], 0.0), 1)
        rp = _mxdot(x, wa_ref[...]) + ba_ref[...]
        ip = _mxdot(x, wx_ref[...]) + bx_ref[...]
        first = _first_mask(x.shape, t)
        _, vjp = jax.vjp(lambda xx, r_, i_, l_: _lru_pt(xx, r_, i_, l_, first), x, rp, ip, lam_ref[...])
        dx, drp, dip, dl = vjp((lg * h_prev, lg))
        dxc_ref[...] = dx + _mxdot(drp, wa_ref[...], _NT) + _mxdot(dip, wx_ref[...], _NT)
        dwa_ref[...] += _mxdot(x, drp, _TN)
        dwx_ref[...] += _mxdot(x, dip, _TN)
        dba_ref[...] += jnp.sum(drp, axis=0, keepdims=True)
        dbx_ref[...] += jnp.sum(dip, axis=0, keepdims=True)
        dlam_ref[...] += dl

    wspec = pl.BlockSpec((None, hd, hd), lambda h_, t: (h_, 0, 0))
    bspec = pl.BlockSpec((None, 1, hd), lambda h_, t: (h_, 0, 0))
    tspec = pl.BlockSpec((tt, hd), lambda h_, t: (t, h_))
    pspec = pl.BlockSpec((tt, hd), lambda h_, t: (jnp.maximum(t - 1, 0), h_))
    lspec = pl.BlockSpec((1, hd), lambda h_, t: (0, h_))
    return pl.pallas_call(
        body, name=name, grid=(N_HEADS, T // tt),
        in_specs=[tspec, tspec, pspec, tspec, wspec, bspec, wspec, bspec, lspec],
        out_specs=[tspec, wspec, bspec, wspec, bspec, lspec],
        out_shape=[jax.ShapeDtypeStruct((T, W), f32), jax.ShapeDtypeStruct((N_HEADS, hd, hd), f32),
                   jax.ShapeDtypeStruct((N_HEADS, 1, hd), f32), jax.ShapeDtypeStruct((N_HEADS, hd, hd), f32),
                   jax.ShapeDtypeStruct((N_HEADS, 1, hd), f32), jax.ShapeDtypeStruct((1, W), f32)],
        compiler_params=_params(("parallel", "arbitrary")),
    )(lam_g, h, h, xc, wa, ba, wx, bx, lam)


def _hgrn_pt(z, qp, lb):
    sig = jax.nn.sigmoid(z)
    fg = lb + (1.0 - lb) * sig
    logf = jnp.log(jnp.maximum(fg, F_FLOOR))
    k = (1.0 - lb) * (1.0 - sig)
    return logf, k, jax.nn.silu(qp)


def _hgrn_out(o, gp, ng):
    on = o * lax.rsqrt(jnp.mean(o * o, axis=-1, keepdims=True) + EPS)
    return on * ng * jax.nn.silu(gp)


def _cumsum_rows(x):
    n, row, sh = x.shape[0], _rows(x.shape), 1
    while sh < n:
        x = x + jnp.where(row >= sh, pltpu.roll(x, sh, 0), 0.0)
        sh *= 2
    return x


def _rev_cumsum_rows(x):
    n, row, sh = x.shape[0], _rows(x.shape), 1
    while sh < n:
        x = x + jnp.where(row < n - sh, pltpu.roll(x, n - sh, 0), 0.0)
        sh *= 2
    return x


def _hgrn_fwd(proj, qcol, fcol, vcol, gcol, lb, ng, name):
    T = proj.shape[0]
    W = lb.shape[1]
    hd = W // N_HEADS
    bw = HGRN_HPB * hd
    tt = _tile(T, 256, SUB)
    ns = tt // SUB
    q0, f0, v0, g0 = qcol // bw, fcol // bw, vcol // bw, gcol // bw

    def body(q_ref, f_ref, v_ref, g_ref, lb_ref, ng_ref, y_ref, o_ref, st_ref, s_ref):
        t = pl.program_id(1)

        @pl.when(t == 0)
        def _():
            s_ref[...] = jnp.zeros_like(s_ref)

        row = _rows((SUB, hd))

        def sub(j, carry):
            rs = pl.ds(pl.multiple_of(j * SUB, SUB), SUB)
            for hh in range(HGRN_HPB):
                cs = slice(hh * hd, (hh + 1) * hd)
                logf, k, qf = _hgrn_pt(f_ref[rs, cs], q_ref[rs, cs], lb_ref[:, cs])
                v = v_ref[rs, cs]
                b = _cumsum_rows(logf)
                b_last = b[SUB - 1:SUB, :]
                S = s_ref[hh]
                st_ref[hh, j] = S
                halves = [jnp.zeros((HALF, hd), f32), jnp.zeros((HALF, hd), f32)]
                for r in range(SUB):
                    n = HALF if r < HALF else SUB
                    e = jnp.exp(jnp.minimum(b[r:r + 1, :] - b[:n], 0.0))
                    m = jnp.where(row[:n] <= r, qf[r:r + 1, :] * k[:n] * e, 0.0)
                    p = jnp.sum(m, axis=1, keepdims=True)
                    o_r = jnp.sum(p * v[:n], axis=0, keepdims=True)
                    halves[r // HALF] = jnp.where(row[:HALF] == r % HALF, o_r, halves[r // HALF])
                o_ref[rs, cs] = _mxdot(qf * jnp.exp(b), S, _NT) + jnp.concatenate(halves, axis=0)
                s_ref[hh] = S * jnp.exp(b_last) + _mxdot(v, k * jnp.exp(b_last - b), _TN)
            return carry

        lax.fori_loop(0, ns, sub, 0)
        for hh in range(HGRN_HPB):
            cs = slice(hh * hd, (hh + 1) * hd)
            y_ref[:, cs] = _hgrn_out(o_ref[:, cs], g_ref[:, cs], ng_ref[:, cs]).astype(y_ref.dtype)

    def slab(c0):
        return pl.BlockSpec((tt, bw), functools.partial(lambda h, t, c0: (t, c0 + h), c0=c0))

    hspec = pl.BlockSpec((tt, bw), lambda h, t: (t, h))
    cspec = pl.BlockSpec((1, bw), lambda h, t: (0, h))
    return pl.pallas_call(
        body, name=name, grid=(N_HEADS // HGRN_HPB, T // tt),
        in_specs=[slab(q0), slab(f0), slab(v0), slab(g0), cspec, cspec],
        out_specs=[hspec, hspec, pl.BlockSpec((HGRN_HPB, ns, hd, hd), lambda h, t: (h, t, 0, 0))],
        out_shape=[jax.ShapeDtypeStruct((T, W), _MXU), jax.ShapeDtypeStruct((T, W), f32),
                   jax.ShapeDtypeStruct((N_HEADS, T // SUB, hd, hd), f32)],
        scratch_shapes=[pltpu.VMEM((HGRN_HPB, hd, hd), f32)],
        compiler_params=_params(("parallel", "arbitrary")),
    )(proj, proj, proj, proj, lb, ng)


def _hgrn_bwd(dy, dcol, proj, qcol, fcol, vcol, gcol, lb, ng, o, states, name):
    T = proj.shape[0]
    W = lb.shape[1]
    hd = W // N_HEADS
    bw = HGRN_HPB * hd
    tt = _tile(T, 256, SUB)
    ns, nt = tt // SUB, T // tt
    q0, f0, v0, g0, d0 = qcol // bw, fcol // bw, vcol // bw, gcol // bw, dcol // bw

    def body(dy_ref, q_ref, f_ref, v_ref, g_ref, lb_ref, ng_ref, o_ref, st_ref,
             dq_ref, df_ref, dv_ref, dg_ref, dlb_ref, dng_ref, ds_ref, do_ref):
        t = pl.program_id(1)

        @pl.when(t == 0)
        def _():
            ds_ref[...] = jnp.zeros_like(ds_ref)
            dlb_ref[...] = jnp.zeros_like(dlb_ref)
            dng_ref[...] = jnp.zeros_like(dng_ref)

        for hh in range(HGRN_HPB):
            cs = slice(hh * hd, (hh + 1) * hd)
            _, vjp_out = jax.vjp(_hgrn_out, o_ref[:, cs], g_ref[:, cs], ng_ref[:, cs])
            do, dgp, dng = vjp_out(dy_ref[:, cs])
            do_ref[:, cs] = do
            dg_ref[:, cs] = dgp.astype(dg_ref.dtype)
            dng_ref[:, cs] += dng
        row = _rows((SUB, hd))

        def sub(jj, carry):
            j = ns - 1 - jj
            rs = pl.ds(pl.multiple_of(j * SUB, SUB), SUB)
            for hh in range(HGRN_HPB):
                cs = slice(hh * hd, (hh + 1) * hd)
                (logf, k, qf), vjp_pt = jax.vjp(_hgrn_pt, f_ref[rs, cs], q_ref[rs, cs], lb_ref[:, cs])
                v = v_ref[rs, cs]
                dO = do_ref[rs, cs]
                b = _cumsum_rows(logf)
                b_last = b[SUB - 1:SUB, :]
                S = st_ref[hh, j]
                dS = ds_ref[hh]
                eb = jnp.exp(b)
                kd = jnp.exp(b_last - b)
                d = jnp.exp(b_last)
                qe, ke = qf * eb, k * kd
                dqe = _mxdot(dO, S, _NN)
                dke = _mxdot(v, dS, _NN)
                dv = _mxdot(ke, dS, _NT)
                dd = jnp.sum(dS * S, axis=0, keepdims=True)
                ds_ref[hh] = dS * d + _mxdot(dO, qe, _TN)
                dq_h = [jnp.zeros((HALF, hd), f32), jnp.zeros((HALF, hd), f32)]
                dk_acc = {HALF: jnp.zeros((HALF, hd), f32), SUB: jnp.zeros((SUB, hd), f32)}
                dv_acc = {HALF: jnp.zeros((HALF, hd), f32), SUB: jnp.zeros((SUB, hd), f32)}
                for r in range(SUB):
                    n = HALF if r < HALF else SUB
                    em = jnp.where(row[:n] <= r, jnp.exp(jnp.minimum(b[r:r + 1, :] - b[:n], 0.0)), 0.0)
                    ke_r = k[:n] * em
                    qr, dor = qf[r:r + 1, :], dO[r:r + 1, :]
                    p = jnp.sum(qr * ke_r, axis=1, keepdims=True)
                    dv_acc[n] = dv_acc[n] + p * dor
                    dp = jnp.sum(dor * v[:n], axis=1, keepdims=True)
                    dq_r = jnp.sum(dp * ke_r, axis=0, keepdims=True)
                    dq_h[r // HALF] = jnp.where(row[:HALF] == r % HALF, dq_r, dq_h[r // HALF])
                    dk_acc[n] = dk_acc[n] + dp * (qr * em)
                pad = jnp.zeros((SUB - HALF, hd), f32)
                dq_i = jnp.concatenate(dq_h, axis=0)
                dk_i = dk_acc[SUB] + jnp.concatenate([dk_acc[HALF], pad], axis=0)
                dv = dv + dv_acc[SUB] + jnp.concatenate([dv_acc[HALF], pad], axis=0)
                dqf = dqe * eb + dq_i
                dk = dke * kd + dk_i
                dke_ke = dke * ke
                db = dqe * qe - dke_ke + qf * dq_i - k * dk_i
                db_last = jnp.sum(dke_ke, axis=0, keepdims=True) + dd * d
                db = db + jnp.where(row == SUB - 1, db_last, 0.0)
                dz, dqp, dlb = vjp_pt((_rev_cumsum_rows(db), dk, dqf))
                dq_ref[rs, cs] = dqp.astype(dq_ref.dtype)
                df_ref[rs, cs] = dz.astype(df_ref.dtype)
                dv_ref[rs, cs] = dv.astype(dv_ref.dtype)
                dlb_ref[:, cs] += dlb
            return carry

        lax.fori_loop(0, ns, sub, 0)

    def slab(c0):
        return pl.BlockSpec((tt, bw), functools.partial(lambda h, t, c0: (nt - 1 - t, c0 + h), c0=c0))

    hspec = pl.BlockSpec((tt, bw), lambda h, t: (nt - 1 - t, h))
    cspec = pl.BlockSpec((1, bw), lambda h, t: (0, h))
    return pl.pallas_call(
        body, name=name, grid=(N_HEADS // HGRN_HPB, nt),
        in_specs=[slab(d0), slab(q0), slab(f0), slab(v0), slab(g0), cspec, cspec, hspec,
                  pl.BlockSpec((HGRN_HPB, ns, hd, hd), lambda h, t: (h, nt - 1 - t, 0, 0))],
        out_specs=[hspec, hspec, hspec, hspec, cspec, cspec],
        out_shape=[jax.ShapeDtypeStruct((T, W), _MXU)] * 4 + [jax.ShapeDtypeStruct((1, W), f32)] * 2,
        scratch_shapes=[pltpu.VMEM((HGRN_HPB, hd, hd), f32), pltpu.VMEM((tt, bw), f32)],
        compiler_params=_params(("parallel", "arbitrary")),
    )(dy, proj, proj, proj, proj, lb, ng, o, states)


def _lower_bounds(logits, name):
    def fn(lg):
        sm = jax.nn.softmax(lg, axis=0)
        run, rows_ = None, []
        for j in range(lg.shape[0]):
            run = sm[j:j + 1, :] if run is None else run + sm[j:j + 1, :]
            rows_.append(run - sm[0:1, :])
        return jnp.concatenate(rows_, axis=0)
    return fn


def _lb_fwd(logits, name):
    fn = _lower_bounds(logits, name)

    def body(l_ref, o_ref):
        o_ref[...] = fn(l_ref[...])

    return pl.pallas_call(body, name=name, out_shape=jax.ShapeDtypeStruct(logits.shape, f32))(logits)


def _lb_bwd(logits, dlb, name):
    fn = _lower_bounds(logits, name)

    def body(l_ref, d_ref, o_ref):
        _, vjp = jax.vjp(fn, l_ref[...])
        o_ref[...] = vjp(d_ref[...])[0]

    return pl.pallas_call(body, name=name, out_shape=jax.ShapeDtypeStruct(logits.shape, f32))(logits, dlb)


def _ln_silu(d, g, b):
    mu = jnp.mean(d, axis=-1, keepdims=True)
    xc = d - mu
    y = xc * lax.rsqrt(jnp.mean(xc * xc, axis=-1, keepdims=True) + EPS)
    return jax.nn.silu(y * g + b)


def _ln_fwd(d, g, b, name):
    T, W = d.shape
    tt = _tile(T, 256, 8)

    def body(d_ref, g_ref, b_ref, o_ref):
        o_ref[...] = _ln_silu(d_ref[...], g_ref[...], b_ref[...]).astype(o_ref.dtype)

    return pl.pallas_call(
        body, name=name, grid=(T // tt,),
        in_specs=[pl.BlockSpec((tt, W), lambda t: (t, 0))] + [pl.BlockSpec((1, W), lambda t: (0, 0))] * 2,
        out_specs=pl.BlockSpec((tt, W), lambda t: (t, 0)), out_shape=jax.ShapeDtypeStruct((T, W), _MXU),
        compiler_params=_params(("parallel",)),
    )(d, g, b)


def _ln_bwd(dy, dcol0, d, g, b, name):
    T, W = d.shape
    tt = _tile(T, 256, 8)
    c0 = dcol0 // W
    assert dcol0 % W == 0

    def body(dy_ref, d_ref, g_ref, b_ref, dd_ref, dg_ref, db_ref):
        @pl.when(pl.program_id(0) == 0)
        def _():
            dg_ref[...] = jnp.zeros_like(dg_ref)
            db_ref[...] = jnp.zeros_like(db_ref)

        _, vjp = jax.vjp(_ln_silu, d_ref[...], g_ref[...], b_ref[...])
        dd, dg, db = vjp(dy_ref[...])
        dd_ref[...] = dd
        dg_ref[...] += dg
        db_ref[...] += db

    cspec = pl.BlockSpec((1, W), lambda t: (0, 0))
    return pl.pallas_call(
        body, name=name, grid=(T // tt,),
        in_specs=[pl.BlockSpec((tt, W), lambda t: (t, c0)), pl.BlockSpec((tt, W), lambda t: (t, 0)), cspec, cspec],
        out_specs=[pl.BlockSpec((tt, W), lambda t: (t, 0)), cspec, cspec],
        out_shape=[jax.ShapeDtypeStruct((T, W), f32), jax.ShapeDtypeStruct((1, W), f32), jax.ShapeDtypeStruct((1, W), f32)],
        compiler_params=_params(("arbitrary",)),
    )(dy, d, g, b)


def _adamw(w, m, v, parts, name):
    R, C = w.shape
    tr, tc = _tile(R, 512, 8), _tile(C, 512, LANE)
    npart = len(parts)

    def body(*refs):
        w_ref, m_ref, v_ref = refs[:3]
        g_ref, d_ref, mo_ref, vo_ref = refs[3 + npart:]
        g = refs[3][...].astype(f32)
        for p_ref in refs[4:3 + npart]:
            g = g + p_ref[...].astype(f32)
        mm = ADAM_B1 * m_ref[...] + (1.0 - ADAM_B1) * g
        vv = ADAM_B2 * v_ref[...] + (1.0 - ADAM_B2) * jnp.square(g)
        m_hat = mm / (1.0 - ADAM_B1 ** ADAM_STEP)
        v_hat = vv / (1.0 - ADAM_B2 ** ADAM_STEP)
        g_ref[...] = g
        d_ref[...] = -ADAM_LR * (m_hat / (jnp.sqrt(v_hat) + ADAM_EPS) + ADAM_WD * w_ref[...])
        mo_ref[...] = mm
        vo_ref[...] = vv

    spec = pl.BlockSpec((tr, tc), lambda i, j: (i, j))
    return pl.pallas_call(
        body, name=name, grid=(R // tr, C // tc), in_specs=[spec] * (3 + npart), out_specs=[spec] * 4,
        out_shape=[jax.ShapeDtypeStruct((R, C), f32)] * 4, compiler_params=_params(("parallel", "parallel")),
    )(w, m, v, *parts)


def _adamw_layer(w3, m3, v3, layer, parts, prev, name):
    L, R, C = w3.shape
    tr, tc = _tile(R, 128, 8), _tile(C, 2048, LANE)
    npart = len(parts)

    def body(*refs):
        w_ref, m_ref, v_ref = refs[:3]
        g_ref, d_ref, mo_ref, vo_ref = refs[3 + npart + 4:]
        g = refs[3][...].astype(f32)
        for p_ref in refs[4:3 + npart]:
            g = g + p_ref[...].astype(f32)
        mm = ADAM_B1 * m_ref[...] + (1.0 - ADAM_B1) * g
        vv = ADAM_B2 * v_ref[...] + (1.0 - ADAM_B2) * jnp.square(g)
        m_hat = mm / (1.0 - ADAM_B1 ** ADAM_STEP)
        v_hat = vv / (1.0 - ADAM_B2 ** ADAM_STEP)
        g_ref[...] = g
        d_ref[...] = -ADAM_LR * (m_hat / (jnp.sqrt(v_hat) + ADAM_EPS) + ADAM_WD * w_ref[...])
        mo_ref[...] = mm
        vo_ref[...] = vv

    spec3 = pl.BlockSpec((None, tr, tc), lambda i, j: (layer, i, j))
    spec2 = pl.BlockSpec((tr, tc), lambda i, j: (i, j))
    return pl.pallas_call(
        body, name=name, grid=(R // tr, C // tc), in_specs=[spec3] * 3 + [spec2] * npart + [_ANY] * 4,
        out_specs=[spec3] * 4, out_shape=[jax.ShapeDtypeStruct((L, R, C), f32)] * 4,
        input_output_aliases={3 + npart + k: k for k in range(4)},
        compiler_params=_params(("parallel", "parallel")),
    )(w3, m3, v3, *parts, *prev)


def _sum_own_recv(g, chip, col, recv, name):
    S, R, C = recv.shape
    tr, tc = _tile(R, 256, 8), _tile(C, 2048, LANE)
    nbr, nbc = R // tr, C // tc

    def body(chip_ref, o_ref, r_ref, out_ref):
        acc = o_ref[...].astype(f32)
        for s in range(S):
            acc = acc + r_ref[s].astype(f32)
        out_ref[...] = acc

    if col:
        own_map = lambda i, j, c: (i, c[0] * nbc + j)
    else:
        own_map = lambda i, j, c: (c[0] * nbr + i, j)
    grid_spec = pltpu.PrefetchScalarGridSpec(
        num_scalar_prefetch=1, grid=(nbr, nbc),
        in_specs=[pl.BlockSpec((tr, tc), own_map), pl.BlockSpec((S, tr, tc), lambda i, j, c: (0, i, j))],
        out_specs=pl.BlockSpec((tr, tc), lambda i, j, c: (i, j)))
    return pl.pallas_call(
        body, name=name, grid_spec=grid_spec, out_shape=jax.ShapeDtypeStruct((R, C), f32),
        compiler_params=_params(("parallel", "parallel")),
    )(chip, g, recv)


def _sum_slots(recv, name):
    S, R, C = recv.shape
    tr, tc = _tile(R, 512, 8), _tile(C, 512, LANE)

    def body(r_ref, o_ref):
        acc = r_ref[0].astype(f32)
        for s in range(1, S):
            acc = acc + r_ref[s].astype(f32)
        o_ref[...] = acc

    return pl.pallas_call(
        body, name=name, grid=(R // tr, C // tc), in_specs=[pl.BlockSpec((S, tr, tc), lambda i, j: (0, i, j))],
        out_specs=pl.BlockSpec((tr, tc), lambda i, j: (i, j)), out_shape=jax.ShapeDtypeStruct((R, C), f32),
        compiler_params=_params(("parallel", "parallel")),
    )(recv)


_CHIP_FLIPS = ((1, 0), (0, 1), (1, 1))
_ANY = pl.BlockSpec(memory_space=pl.ANY)


def _me():
    return lax.axis_index("x"), lax.axis_index("y"), lax.axis_index("c")


def _allgather_chips(local, col, name):
    L, R, C = local.shape
    out_shape = (L, R, 4 * C) if col else (L, 4 * R, C)

    def body(loc_ref, out_ref, send_sems, recv_sems, local_sem):
        x, y, c = _me()

        def block(px, py):
            s = 2 * px + py
            if col:
                return out_ref.at[:, :, pl.ds(s * C, C)]
            return out_ref.at[:, pl.ds(s * R, R), :]

        mine = pltpu.make_async_copy(loc_ref, block(x, y), local_sem)
        mine.start()
        sends = []
        for j, (fx, fy) in enumerate(_CHIP_FLIPS):
            cp = pltpu.make_async_remote_copy(src_ref=loc_ref, dst_ref=block(x, y), send_sem=send_sems.at[j],
                                              recv_sem=recv_sems.at[j], device_id=(x ^ fx, y ^ fy, c), device_id_type=MESH)
            cp.start()
            sends.append(cp)
        for j, (fx, fy) in enumerate(_CHIP_FLIPS):
            pltpu.make_async_remote_copy(src_ref=loc_ref, dst_ref=block(x ^ fx, y ^ fy), send_sem=send_sems.at[j],
                                         recv_sem=recv_sems.at[j], device_id=(x ^ fx, y ^ fy, c),
                                         device_id_type=MESH).wait_recv()
        for cp in sends:
            cp.wait_send()
        mine.wait()

    return pl.pallas_call(
        body, name=name, in_specs=[_ANY], out_specs=_ANY, out_shape=jax.ShapeDtypeStruct(out_shape, local.dtype),
        scratch_shapes=[pltpu.SemaphoreType.DMA((3,)), pltpu.SemaphoreType.DMA((3,)), pltpu.SemaphoreType.DMA],
    )(local)


_HBM = pl.BlockSpec(memory_space=pltpu.HBM)
_SEM = pl.BlockSpec(memory_space=pltpu.SEMAPHORE)
_EFFECT = pltpu.SideEffectType.DATAFLOW_SIDE_EFFECTING


def _chip_block(ref, s, col, n):
    return ref.at[:, pl.ds(s * n, n)] if col else ref.at[pl.ds(s * n, n), :]


_NSEM = {'gather': 7, 'relay': 6, 'scatter': 6, 'swap': 2, 'all': 15}


def _half_block(ref, s, col, n, half):
    rows = (ref.shape[0] if col else n) // 2
    if col:
        return ref.at[pl.ds(half * rows, rows), pl.ds(s * n, n)]
    return ref.at[pl.ds(s * n + half * rows, rows), :]


def _xchg_copies(kind, src_ref, land_ref, col, sems):
    x, y, c = _me()
    if kind == 'swap':
        return [pltpu.make_async_remote_copy(src_ref=src_ref, dst_ref=land_ref, send_sem=sems[0], recv_sem=sems[1],
                                             device_id=(x, y, 1 - c), device_id_type=MESH)]
    if kind == 'all':
        me_slot = land_ref.at[4 * x + 2 * y + c]
        cps = [pltpu.make_async_copy(src_ref, me_slot, sems[14])]
        for m in range(1, 8):
            cps.append(pltpu.make_async_remote_copy(
                src_ref=src_ref, dst_ref=me_slot, send_sem=sems[2 * m - 2], recv_sem=sems[2 * m - 1],
                device_id=(x ^ (m >> 2), y ^ ((m >> 1) & 1), c ^ (m & 1)), device_id_type=MESH))
        return cps
    cps = []
    if kind == 'gather':
        n = src_ref.shape[1] if col else src_ref.shape[0]
        cps.append(pltpu.make_async_copy(src_ref, _chip_block(land_ref, 2 * x + y, col, n), sems[6]))
    for j, (fx, fy) in enumerate(_CHIP_FLIPS):
        px, py = x ^ fx, y ^ fy
        peer = (px, py, c)
        if kind == 'gather':
            rows = src_ref.shape[0] // 2
            src, dst = src_ref.at[pl.ds(c * rows, rows), :], _half_block(land_ref, 2 * x + y, col, n, c)
        elif kind == 'relay':
            n = land_ref.shape[1] // 4 if col else land_ref.shape[0] // 4
            src = dst = _half_block(land_ref, 2 * px + py, col, n, c)
            peer = (x, y, 1 - c)
        else:
            n = land_ref.shape[2] if col else land_ref.shape[1]
            src, dst = _chip_block(src_ref, 2 * px + py, col, n), land_ref.at[j]
        cps.append(pltpu.make_async_remote_copy(src_ref=src, dst_ref=dst, send_sem=sems[2 * j], recv_sem=sems[2 * j + 1],
                                                device_id=peer, device_id_type=MESH))
    return cps


def _xchg_start(kind, srcs, lands, cols, after, name):
    n = len(lands)
    arrs = (list(srcs) if srcs is not None else []) + list(lands)
    na = len(arrs)
    per = _NSEM[kind]
    nsem = per * n

    def body(*refs):
        land_refs = refs[na - n:na]
        src_refs = refs[:n] if srcs is not None else land_refs
        sems = refs[na + 1:na + 1 + nsem]
        token = refs[-1]
        for i in range(n):
            for cp in _xchg_copies(kind, src_refs[i], land_refs[i], cols[i], sems[per * i:per * i + per]):
                cp.start()
        token[...] = jnp.zeros_like(token)

    hbm = lambda a: pltpu.HBM(a.shape, a.dtype)
    outs = pl.pallas_call(
        body, name=name,
        out_shape=tuple([pltpu.SemaphoreType.DMA(())] * nsem + [hbm(a) for a in arrs] + [jax.ShapeDtypeStruct((8, LANE), f32)]),
        in_specs=tuple([_HBM] * na + [_ANY]),
        out_specs=tuple([_SEM] * nsem + [_HBM] * na + [pl.BlockSpec(memory_space=pltpu.VMEM)]),
        input_output_aliases={i: nsem + i for i in range(na)},
        compiler_params=pltpu.CompilerParams(has_side_effects=_EFFECT),
    )(*[pltpu.with_memory_space_constraint(a, pltpu.HBM) for a in arrs], after)
    thru = outs[nsem:nsem + na]
    return outs[:nsem], (thru[:n] if srcs is not None else None), thru[na - n:], outs[-1]


def _xchg_wait(kind, started, cols, after, name):
    sems, srcs, lands, _ = started
    n = len(lands)
    arrs = (list(srcs) if srcs is not None else []) + list(lands)
    na = len(arrs)
    per = _NSEM[kind]
    nsem = per * n

    def body(*refs):
        land_refs = refs[na - n:na]
        src_refs = refs[:n] if srcs is not None else land_refs
        sem_refs = refs[na:na + nsem]
        for i in range(n):
            for cp in _xchg_copies(kind, src_refs[i], land_refs[i], cols[i], sem_refs[per * i:per * i + per]):
                if cp.is_remote:
                    cp.wait_send()
                    cp.wait_recv()
                else:
                    cp.wait()

    hbm = lambda a: pltpu.HBM(a.shape, a.dtype)
    outs = pl.pallas_call(
        body, name=name, out_shape=tuple(hbm(a) for a in arrs),
        in_specs=tuple([_HBM] * na + [_SEM] * nsem + [_ANY]), out_specs=tuple([_HBM] * na),
        input_output_aliases={i: i for i in range(na)},
        compiler_params=pltpu.CompilerParams(has_side_effects=_EFFECT),
    )(*arrs, *sems, after)
    return (outs[:n] if srcs is not None else None), outs[na - n:]


_PACK_ROWS = 512


def _pack(arrs):
    flat = jnp.concatenate([a.reshape(-1).astype(f32) for a in arrs])
    n = flat.shape[0]
    rows = -(-n // (_PACK_ROWS * LANE)) * _PACK_ROWS
    return jnp.pad(flat, (0, rows * LANE - n)).reshape(rows, LANE)


def _unpack(buf, shapes):
    flat, outs, off = buf.reshape(-1), [], 0
    for s in shapes:
        n = 1
        for d_ in s:
            n *= d_
        outs.append(flat[off:off + n].reshape(s))
        off += n
    return outs


def _step(p):
    x0 = p['x'][0]
    tgt = p['loss_target'][0]
    T, D = x0.shape
    W = D // 2
    depth = p['ln_mix_g'].shape[0]
    chip = 2 * lax.axis_index("x") + lax.axis_index("y")
    chip1 = jnp.reshape(chip, (1,)).astype(jnp.int32)

    def group_names(layer, grp):
        if grp == 'ffn':
            return [('ffn_w_gate', layer), ('ffn_w_up', layer), ('ffn_w_down', layer)]
        pre = 'ev' if layer % 2 == 0 else 'od'
        return [(pre + '_w_in', layer // 2), (pre + '_w_out', layer // 2)]

    def gather_start(layer, grp, after):
        srcs, lands, cols = [], [], []
        for n, l in group_names(layer, grp):
            loc = p[n][l].astype(_MXU)
            R, C = loc.shape
            col = _BIG_COL[n]
            land = lax.empty((R, 4 * C) if col else (4 * R, C), _MXU)
            srcs.append(loc)
            lands.append(land)
            cols.append(col)
        return _xchg_start('gather', srcs, lands, cols, after, name=f"ag_start_{grp}{layer}"), cols

    def gather_relay(started, layer, grp, after):
        st, cols = started
        _, lands = _xchg_wait('gather', st, cols, after, name=f"ag_wait_{grp}{layer}")
        return _xchg_start('relay', None, lands, cols, chip1, name=f"ag_relay_{grp}{layer}"), cols

    def gather_finish(relayed, layer, grp, after):
        st, cols = relayed
        _, lands = _xchg_wait('relay', st, cols, after, name=f"ag_done_{grp}{layer}")
        return dict(zip([n for n, _ in group_names(layer, grp)], lands))

    groups = [(layer, grp) for layer in range(depth) for grp in ('mix', 'ffn')]
    pending, relayed = {}, {}
    loose = []

    def take_weights(gi, after):
        wts = gather_finish(relayed.pop(groups[gi]), *groups[gi], after)
        if gi + 2 < len(groups):
            pending[groups[gi + 2]] = gather_start(*groups[gi + 2], next(iter(wts.values())))
            loose.append(pending[groups[gi + 2]][0][3])
        return wts

    def advance(gi, after):
        if gi + 1 < len(groups):
            relayed[groups[gi + 1]] = gather_relay(pending.pop(groups[gi + 1]), *groups[gi + 1], after)
            loose.append(relayed[groups[gi + 1]][0][3])

    def tied(a):
        a = a.reshape(1, -1)
        while loose:
            a = a + loose.pop()[0:1, 0:1]
        return a

    small_rows = []
    cl = W // 4
    for n in _SMALL_SHARDED:
        small_rows.append(p[n].reshape(-1, cl))
    srows = [a.shape[0] for a in small_rows]
    spack = jnp.concatenate(small_rows, axis=0)
    spad = -(-spack.shape[0] // 8) * 8
    spack = jnp.pad(spack, ((0, spad - spack.shape[0]), (0, 0)))
    sfull = _allgather_chips(spack[None], False, name="ag_small")[0].reshape(4, spad, cl)
    pending[groups[0]] = gather_start(*groups[0], sfull)
    pending[groups[1]] = gather_start(*groups[1], pending[groups[0]][0][3])
    relayed[groups[0]] = gather_relay(pending.pop(groups[0]), *groups[0], pending[groups[1]][0][3])
    loose.append(relayed[groups[0]][0][3])
    small = {}
    off = 0
    for n, r in zip(_SMALL_SHARDED, srows):
        blk = sfull[:, off:off + r, :]
        lead = p[n].shape[:-1]
        q = p[n].shape[-1] // cl
        blk = blk.reshape((4,) + lead + (q, cl))
        blk = jnp.moveaxis(blk, 0, len(lead))
        small[n] = blk.reshape(lead + (4 * q * cl,))
        off += r

    lbs = _lb_fwd(p['hgrn_lb_logits'], name="lb_fwd")

    def row(a, tok=None):
        a = a.reshape(1, -1)
        return a if tok is None else a + tok[0:1, 0:1]

    saved = []
    full = {}
    x = x0
    for layer in range(depth):
        j = layer // 2
        s = {'x_in': x}
        wts = take_weights(2 * layer, x)
        full[layer] = wts
        h = _rmsnorm_fwd(x, tied(p['ln_mix_g'][layer]), name="rms_fwd")
        s['h'] = h
        if layer % 2 == 0:
            proj = _matmul(h, wts['ev_w_in'], 'nn', f32, "mm_ev_in", bias=row(p['ev_b_in'][j]))
            xc = _conv_fwd(proj, 0, small['lru_conv_w'][j], row(p['lru_conv_b'][j]), W, name="lru_conv_fwd")
            ba, bx = p['lru_ba'][j][:, None, :], p['lru_bx'][j][:, None, :]
            a, u = _lru_gates_fwd(xc, p['lru_wa'][j], ba, p['lru_wx'][j], bx, row(p['lru_lambda'][j]), name="lru_gates_fwd")
            hl, y_a = _lru_scan_fwd(a, u, proj, W, name="lru_scan_fwd")
            y_b, o, states = _hgrn_fwd(proj, 2 * W, 3 * W, 4 * W, 5 * W, row(lbs[j]), row(p['hgrn_norm_g'][j]),
                                       name="hgrn_fwd")
            s.update(proj=proj, xc=xc, a=a, hl=hl, o=o, states=states)
            ycat = jnp.concatenate([y_a, y_b], axis=1)
            w_out = wts['ev_w_out']
        else:
            proj = _matmul(h, wts['od_w_in'], 'nn', f32, "mm_od_in", bias=row(small['od_b_in'][j]))
            pp = _pw(lambda a_, b_: a_ * b_, [(proj, W), (proj, 2 * W)], [f32], W, name="sc_mul")
            cp = _conv_fwd(pp, 0, small['sc_conv_w'][j], None, W, name="sc_conv_fwd")
            y_c = _pw(lambda a_, b_: a_ * b_, [(proj, 0), (cp, 0)], [_MXU], W, name="sc_out")
            glu = _pw(lambda a_, b_: a_ * jax.nn.sigmoid(b_), [(proj, 3 * W), (proj, 4 * W)], [f32], W, name="cf_glu")
            dcv = _conv_fwd(glu, 0, small['cf_conv_w'][j], row(small['cf_conv_b'][j]), W, name="cf_conv_fwd")
            y_d = _ln_fwd(dcv, row(small['cf_ln_g'][j]), row(small['cf_ln_b'][j]), name="cf_ln_fwd")
            s.update(proj=proj, pp=pp, cp=cp, glu=glu, dcv=dcv)
            ycat = jnp.concatenate([y_c, y_d], axis=1)
            w_out = wts['od_w_out']
        s['ycat'] = ycat
        advance(2 * layer, ycat)
        x = _matmul(ycat, w_out, 'nn', f32, "mm_mix_out", add=x, bias=tied(jnp.zeros((D,), f32)))
        s['x_mid'] = x
        wts = take_weights(2 * layer + 1, x)
        full[layer].update(wts)
        h2 = _rmsnorm_fwd(x, tied(p['ln_ffn_g'][layer]), name="rms_fwd")
        gate, up, act = _ffn_in(h2, wts['ffn_w_gate'], wts['ffn_w_up'], name="ffn_in")
        advance(2 * layer + 1, act)
        x = _matmul(act, wts['ffn_w_down'], 'nn', f32, "mm_ffn_out", add=x, tn=1024, tk=1408,
                    bias=tied(jnp.zeros((D,), f32)))
        s.update(h2=h2, gate=gate, up=up, act=act)
        saved.append(s)

    loss_b, dx, dxb, dg_final = _final_loss(x, tied(p['ln_final_g']), tgt, name="final_loss")

    scat = {}
    tok = None

    def scatter_start(layer, grp, grads):
        srcs, lands, cols = [], [], []
        for (n, l), g in zip(group_names(layer, grp), grads, strict=True):
            R, C = p[n].shape[1:]
            srcs.append(g)
            lands.append(lax.empty((3, R, C), _WIRE))
            cols.append(_BIG_COL[n])
        st = _xchg_start('scatter', srcs, lands, cols, chip1, name=f"rs_start_{grp}{layer}")
        scat[(layer, grp)] = (st, cols)
        return st[3]

    gs = {n: [None] * p[n].shape[0] for n in _IN_NAMES[1:] if n not in _BIG and n != 'ln_final_g'}
    for layer in reversed(range(depth)):
        j = layer // 2
        s = saved[layer]
        F = s['gate'].shape[1]
        wts = full[layer]
        dgate, dup = _ffn_dact(dxb, wts['ffn_w_down'], s['gate'], s['up'], name="ffn_dact")
        g_down = _matmul(s['act'], dxb, 'tn', _WIRE, "mm_dw_down", tm=1408, tn=1024, tk=1024)
        g_gate = _matmul(s['h2'], dgate, 'tn', _WIRE, "mm_dw_in", tm=1024, tn=1408, tk=1024)
        g_up = _matmul(s['h2'], dup, 'tn', _WIRE, "mm_dw_in", tm=1024, tn=1408, tk=1024)
        tok = scatter_start(layer, 'ffn', [g_gate, g_up, g_down])
        dh2 = _matmul(dgate, wts['ffn_w_gate'], 'nt', f32, "mm_ffn_dh", tn=2048, tk=512, second=(dup, wts['ffn_w_up']))
        dx, dxb, gs['ln_ffn_g'][layer] = _rmsnorm_bwd(dh2, s['x_mid'], row(p['ln_ffn_g'][layer], tok), dx, name="rms_bwd")
        if layer % 2 == 0:
            w_out, w_in, n_out, n_in = wts['ev_w_out'], wts['ev_w_in'], 'ev_w_out', 'ev_w_in'
        else:
            w_out, w_in, n_out, n_in = wts['od_w_out'], wts['od_w_in'], 'od_w_out', 'od_w_in'
        dycat = _matmul(dxb, w_out, 'nt', f32, "mm_mix_dy")
        g_out = _matmul(s['ycat'], dxb, 'tn', _WIRE, "mm_dw_out", tm=1024, tn=1024, tk=1024)
        proj = s['proj']
        if layer % 2 == 0:
            ba, bx = p['lru_ba'][j][:, None, :], p['lru_bx'][j][:, None, :]
            lam_g, dgate_a = _lru_scan_bwd(dycat, 0, proj, W, s['hl'], s['a'], name="lru_scan_bwd")
            dxc, dwa, dba, dwx, dbx, dlam = _lru_gates_bwd(lam_g, s['hl'], s['xc'], p['lru_wa'][j], ba, p['lru_wx'][j], bx,
                                                           row(p['lru_lambda'][j]), name="lru_gates_bwd")
            dxa = _conv_bwd_dx(dxc, small['lru_conv_w'][j], name="lru_conv_dx", out_dtype=_MXU)
            dcw, dcb = _conv_bwd_dw(dxc, proj, 0, small['lru_conv_w'][j].shape[0], name="lru_conv_dw")
            dq, df, dv, dgp, dlb, dng = _hgrn_bwd(dycat, W, proj, 2 * W, 3 * W, 4 * W, 5 * W, row(lbs[j]),
                                                  row(p['hgrn_norm_g'][j]), s['o'], s['states'], name="hgrn_bwd")
            gs['lru_wa'][j], gs['lru_ba'][j], gs['lru_wx'][j], gs['lru_bx'][j] = dwa, dba[:, 0, :], dwx, dbx[:, 0, :]
            gs['lru_lambda'][j], gs['lru_conv_w'][j], gs['lru_conv_b'][j] = dlam[0], dcw, dcb[0]
            gs['hgrn_lb_logits'][j], gs['hgrn_norm_g'][j] = dlb[0], dng[0]
            dproj, dbias = _assemble([dxa, dgate_a, dq, df, dv, dgp], name="dproj_ev")
        else:
            def sc_bwd1(dy_, cp_, sb_):
                return dy_ * cp_, dy_ * sb_

            dsb, dcp = _pw(sc_bwd1, [(dycat, 0), (s['cp'], 0), (proj, 0)], [_MXU, f32], W, name="sc_bwd1")
            dpp = _conv_bwd_dx(dcp, small['sc_conv_w'][j], name="sc_conv_dx")
            dscw, _ = _conv_bwd_dw(dcp, s['pp'], 0, small['sc_conv_w'][j].shape[0], name="sc_conv_dw")

            def sc_bwd2(dp_, sc_, sv_):
                return dp_ * sv_, dp_ * sc_

            dsc, dsv = _pw(sc_bwd2, [(dpp, 0), (proj, W), (proj, 2 * W)], [_MXU, _MXU], W, name="sc_bwd2")
            dd, dlg, dlbeta = _ln_bwd(dycat, W, s['dcv'], row(small['cf_ln_g'][j]), row(small['cf_ln_b'][j]),
                                      name="cf_ln_bwd")
            dglu = _conv_bwd_dx(dd, small['cf_conv_w'][j], name="cf_conv_dx")
            dcfw, dcfb = _conv_bwd_dw(dd, s['glu'], 0, small['cf_conv_w'][j].shape[0], name="cf_conv_dw")

            def glu_bwd(dg_, cu_, cg_):
                _, vjp = jax.vjp(lambda a_, b_: a_ * jax.nn.sigmoid(b_), cu_, cg_)
                return vjp(dg_)

            dcu, dcg = _pw(glu_bwd, [(dglu, 0), (proj, 3 * W), (proj, 4 * W)], [_MXU, _MXU], W, name="cf_glu_bwd")
            gs['sc_conv_w'][j], gs['cf_conv_w'][j], gs['cf_conv_b'][j] = dscw, dcfw, dcfb[0]
            gs['cf_ln_g'][j], gs['cf_ln_b'][j] = dlg[0], dlbeta[0]
            dproj, dbias = _assemble([dsb, dsc, dsv, dcu, dcg], name="dproj_od")
        bname = 'ev_b_in' if layer % 2 == 0 else 'od_b_in'
        gs[bname][j] = dbias[0]
        g_in = _matmul(s['h'], dproj, 'tn', _WIRE, "mm_dw_" + n_in, tm=1024, tn=1536, tk=512)
        tok = scatter_start(layer, 'mix', [g_in, g_out])
        dh = _matmul(dproj, w_in, 'nt', f32, "mm_dh_" + n_in)
        dx, dxb, gs['ln_mix_g'][layer] = _rmsnorm_bwd(dh, s['x_in'], row(p['ln_mix_g'][layer], tok), dx, name="rms_bwd")

    g_small = {n: jnp.stack(v_) for n, v_ in gs.items()}
    g_small['ln_mix_g'] = g_small['ln_mix_g'][:, 0, :]
    g_small['ln_ffn_g'] = g_small['ln_ffn_g'][:, 0, :]
    g_small['ln_final_g'] = dg_final[0]
    g_small['hgrn_lb_logits'] = _lb_bwd(p['hgrn_lb_logits'], g_small['hgrn_lb_logits'], name="lb_bwd")
    small_names = _SMALL_REPL + _SMALL_SHARDED
    pack = _pack([g_small[n] for n in small_names])
    small_xchg = _xchg_start('all', [pack], [lax.empty((8,) + pack.shape, f32)], [None], chip1, name="small_grads_start")
    outs = {}

    acc = {n: tuple(lax.empty(p[n].shape, f32) for _ in range(4)) for n in _BIG}
    after_tail = small_xchg[3]

    def finish(swap, after):
        st, names, tag = swap
        parts, others = _xchg_wait('swap', st, [None] * len(names), after, name=f"swap_wait_{tag}")
        for (n, l), part, other in zip(names, parts, others, strict=True):
            acc[n] = tuple(_adamw_layer(p[n], p['m_' + n], p['v_' + n], l, [part, other], acc[n], name=f"adamw_{n}"))

    prev_swap = None
    for layer, grp in reversed(groups):
        st, cols = scat.pop((layer, grp))
        srcs, recvs = _xchg_wait('scatter', st, cols, after_tail, name=f"rs_wait_{grp}{layer}")
        names = group_names(layer, grp)
        parts = [_sum_own_recv(g, chip1, col, recv, name=f"sum_{n}")
                 for (n, l), g, recv, col in zip(names, srcs, recvs, cols, strict=True)]
        st = _xchg_start('swap', parts, [lax.empty(a.shape, f32) for a in parts], [None] * len(parts), recvs[0],
                         name=f"swap_start_{grp}{layer}")
        if prev_swap is not None:
            finish(prev_swap, st[3])
        prev_swap = (st, names, f"{grp}{layer}")
    finish(prev_swap, prev_swap[0][3])
    for n in _BIG:
        for kind, buf in zip(('grad', 'delta', 'new_m', 'new_v'), acc[n]):
            outs[kind + '_' + n] = buf

    _, (recv8,) = _xchg_wait('all', small_xchg, [None], acc[_BIG[0]][0], name="small_grads_wait")
    tot = _sum_slots(recv8, name="sum_small_grads")
    g_tot = dict(zip(small_names, _unpack(tot, [g_small[n].shape for n in small_names])))
    for n in _SMALL_SHARDED:
        lead = p[n].shape[:-1]
        q = p[n].shape[-1] // cl
        blk = g_tot[n].reshape(lead + (4, q * cl))
        g_tot[n] = lax.dynamic_index_in_dim(blk, chip, axis=len(lead), keepdims=False)
    shapes = [p[n].shape for n in small_names]
    res = _adamw(_pack([p[n] for n in small_names]), _pack([p['m_' + n] for n in small_names]),
                 _pack([p['v_' + n] for n in small_names]), [_pack([g_tot[n] for n in small_names])], name="adamw_small")
    for kind, buf in zip(('grad', 'delta', 'new_m', 'new_v'), res):
        for n, a in zip(small_names, _unpack(buf, shapes)):
            outs[kind + '_' + n] = a

    loss = lax.psum(loss_b[0, 0], ("x", "y", "c"))
    weights = _IN_NAMES[1:]
    return (loss, dx[None], *[outs['grad_' + n] for n in weights], *[outs['delta_' + n] for n in weights],
            *[outs['new_m_' + n] for n in weights], *[outs['new_v_' + n] for n in weights])


def kernel(x, ln_mix_g, ln_ffn_g, ln_final_g, ev_w_in, ev_b_in, lru_conv_w, lru_conv_b, lru_wa, lru_ba, lru_wx, lru_bx, lru_lambda, hgrn_lb_logits, hgrn_norm_g, ev_w_out, od_w_in, od_b_in, sc_conv_w, cf_conv_w, cf_conv_b, cf_ln_g, cf_ln_b, od_w_out, ffn_w_gate, ffn_w_up, ffn_w_down, loss_target, m_ln_mix_g, m_ln_ffn_g, m_ln_final_g, m_ev_w_in, m_ev_b_in, m_lru_conv_w, m_lru_conv_b, m_lru_wa, m_lru_ba, m_lru_wx, m_lru_bx, m_lru_lambda, m_hgrn_lb_logits, m_hgrn_norm_g, m_ev_w_out, m_od_w_in, m_od_b_in, m_sc_conv_w, m_cf_conv_w, m_cf_conv_b, m_cf_ln_g, m_cf_ln_b, m_od_w_out, m_ffn_w_gate, m_ffn_w_up, m_ffn_w_down, v_ln_mix_g, v_ln_ffn_g, v_ln_final_g, v_ev_w_in, v_ev_b_in, v_lru_conv_w, v_lru_conv_b, v_lru_wa, v_lru_ba, v_lru_wx, v_lru_bx, v_lru_lambda, v_hgrn_lb_logits, v_hgrn_norm_g, v_ev_w_out, v_od_w_in, v_od_b_in, v_sc_conv_w, v_cf_conv_w, v_cf_conv_b, v_cf_ln_g, v_cf_ln_b, v_od_w_out, v_ffn_w_gate, v_ffn_w_up, v_ffn_w_down):
    vals = locals()
    p = {n: vals[n] for n in _IN_NAMES + ['loss_target']}
    for n in _IN_NAMES[1:]:
        p['m_' + n] = vals['m_' + n]
        p['v_' + n] = vals['v_' + n]
    return _step(p)
```
